```python
import math
import jax, jax.numpy as jnp
from jax import lax
import numpy as np

D_MODEL = 2048
BATCH = 8
SEQ = 4096
DEPTH = 4

D_MIX = D_MODEL
D_POOL = D_MIX // 2
POOL_WINDOWS = (2, 4, 8, 16)
N_POOL_GROUPS = len(POOL_WINDOWS)
POOL_C = D_POOL // N_POOL_GROUPS
N_HEADS = 8
NOPE_DIM = 128
ROPE_DIM = 64
V_DIM = 128
D_ATT = N_HEADS * V_DIM
Q_LORA = D_MODEL // 4
KV_LORA = D_MODEL // 4
ROPE_THETA = 10000.0
Q_BLOCK = 128
D_IN = D_POOL + Q_LORA + KV_LORA + ROPE_DIM
D_FF = 5632
CONV_W = 3
PLE_DIM = 256
EPS = 1e-6

kernel_name = 'hybrid_pool_mla_convffn_ple'


def rms_norm(x, g):
    xf = x.astype(jnp.float32)
    y = xf * lax.rsqrt(jnp.mean(xf * xf, axis=-1, keepdims=True) + EPS)
    return (y * g.astype(jnp.float32)).astype(x.dtype)


def rope_tables(positions):
    inv_freq = 1.0 / (ROPE_THETA ** (jnp.arange(0, ROPE_DIM, 2, dtype=jnp.float32) / ROPE_DIM))
    ang = positions.astype(jnp.float32)[..., None] * inv_freq
    return jnp.cos(ang), jnp.sin(ang)


def apply_rope(x, cos, sin):
    half = x.shape[-1] // 2
    xf = x.astype(jnp.float32)
    x1, x2 = xf[..., :half], xf[..., half:]
    return jnp.concatenate([x1 * cos - x2 * sin, x2 * cos + x1 * sin], axis=-1).astype(x.dtype)


def multiscale_pool(u, pool_w, pool_scale):
    B, S, _ = u.shape
    uf = u.astype(jnp.float32)
    cs = jnp.cumsum(uf, axis=1)
    t = jnp.arange(S)
    outs = []
    for g, w in enumerate(POOL_WINDOWS):
        c = cs[..., g * POOL_C:(g + 1) * POOL_C]
        lag = jnp.pad(c, ((0, 0), (w, 0), (0, 0)))[:, :S]
        cnt = jnp.minimum(t + 1, w).astype(jnp.float32)[None, :, None]
        outs.append((c - lag) / cnt)
    pooled = jnp.stack(outs, axis=2)
    diff = (pooled - uf.reshape(B, S, N_POOL_GROUPS, POOL_C)).astype(u.dtype)
    y = jnp.einsum('bsgc,gcd->bsgd', diff, pool_w).reshape(B, S, D_POOL)
    return y * pool_scale


def latent_attention(c_q, c_kv, k_rope_raw, cos, sin, q_norm_g, w_uq, kv_norm_g, w_ukv):
    B, S, _ = c_q.shape
    q = (rms_norm(c_q, q_norm_g) @ w_uq).reshape(B, S, N_HEADS, NOPE_DIM + ROPE_DIM)
    q_nope = q[..., :NOPE_DIM]
    q_rope = apply_rope(q[..., NOPE_DIM:], cos[:, :, None], sin[:, :, None])
    kv = (rms_norm(c_kv, kv_norm_g) @ w_ukv).reshape(B, S, N_HEADS, NOPE_DIM + V_DIM)
    k_nope, v = kv[..., :NOPE_DIM], kv[..., NOPE_DIM:]
    k_rope = apply_rope(k_rope_raw, cos, sin)
    scale = 1.0 / math.sqrt(NOPE_DIM + ROPE_DIM)
    nb = S // Q_BLOCK
    qn = q_nope.reshape(B, nb, Q_BLOCK, N_HEADS, NOPE_DIM).transpose(1, 0, 2, 3, 4)
    qr = q_rope.reshape(B, nb, Q_BLOCK, N_HEADS, ROPE_DIM).transpose(1, 0, 2, 3, 4)
    k_pos = jnp.arange(S)

    def block(args):
        qn_b, qr_b, bi = args
        s = (jnp.einsum('bqhd,bkhd->bhqk', qn_b, k_nope).astype(jnp.float32)
             + jnp.einsum('bqhr,bkr->bhqk', qr_b, k_rope).astype(jnp.float32)) * scale
        q_pos = bi * Q_BLOCK + jnp.arange(Q_BLOCK)
        s = jnp.where(q_pos[:, None] >= k_pos[None, :], s, -jnp.inf)
        pr = jax.nn.softmax(s, axis=-1).astype(v.dtype)
        return jnp.einsum('bhqk,bkhd->bqhd', pr, v)

    out = lax.map(block, (qn, qr, jnp.arange(nb)))
    return out.transpose(1, 0, 2, 3, 4).reshape(B, S, D_ATT)


def causal_depthwise_conv(x, w, b):
    C = x.shape[-1]
    y = lax.conv_general_dilated(x, w[:, None, :].astype(x.dtype), window_strides=(1,),
                                 padding=[(CONV_W - 1, 0)],
                                 dimension_numbers=('NWC', 'WIO', 'NWC'),
                                 feature_group_count=C)
    return y + b


def conv_gated_ffn(x, w_up, conv_w, conv_b, w_down):
    gu = x @ w_up
    gate, up = gu[..., :D_FF], gu[..., D_FF:]
    gate = causal_depthwise_conv(gate, conv_w, conv_b)
    return (jax.nn.silu(gate) * up) @ w_down


def _fwd_setup_inputs(seed: int = 0) -> dict:
    key = jax.random.key(seed)
    ks = jax.random.split(key, 24)
    f32 = jnp.float32

    def nrm(k, shape, fan_in):
        return jax.random.normal(k, shape, f32) * (fan_in ** -0.5)

    def gain(k, shape):
        return 1.0 + 0.02 * jax.random.normal(k, shape, f32)

    return {
        'x': jax.random.normal(ks[0], (BATCH, SEQ, D_MODEL), f32),
        'p': jax.random.normal(ks[1], (DEPTH, BATCH, SEQ, PLE_DIM), f32),
        'positions': jnp.broadcast_to(jnp.arange(SEQ, dtype=jnp.int32), (BATCH, SEQ)),
        'norm_mix_g': gain(ks[2], (DEPTH, D_MODEL)),
        'w_in': nrm(ks[3], (DEPTH, D_MODEL, D_IN), D_MODEL),
        'pool_w': nrm(ks[4], (DEPTH, N_POOL_GROUPS, POOL_C, POOL_C), POOL_C),
        'pool_scale': gain(ks[5], (DEPTH, D_POOL)),
        'q_norm_g': gain(ks[6], (DEPTH, Q_LORA)),
        'w_uq': nrm(ks[7], (DEPTH, Q_LORA, N_HEADS * (NOPE_DIM + ROPE_DIM)), Q_LORA),
        'kv_norm_g': gain(ks[8], (DEPTH, KV_LORA)),
        'w_ukv': nrm(ks[9], (DEPTH, KV_LORA, N_HEADS * (NOPE_DIM + V_DIM)), KV_LORA),
        'w_out': nrm(ks[10], (DEPTH, D_MIX, D_MODEL), D_MIX),
        'norm_ffn_g': gain(ks[11], (DEPTH, D_MODEL)),
        'w_up': nrm(ks[12], (DEPTH, D_MODEL, 2 * D_FF), D_MODEL),
        'conv_w': nrm(ks[13], (DEPTH, CONV_W, D_FF), CONV_W),
        'conv_b': 0.01 * jax.random.normal(ks[14], (DEPTH, D_FF), f32),
        'w_down': nrm(ks[15], (DEPTH, D_FF, D_MODEL), D_FF),
        'norm_ple_g': gain(ks[16], (DEPTH, D_MODEL)),
        'w_ple': nrm(ks[17], (DEPTH, PLE_DIM, D_MODEL), PLE_DIM),
        'w_ple_gate': nrm(ks[18], (DEPTH, D_MODEL, D_MODEL), D_MODEL),
        'final_norm_g': gain(ks[19], (D_MODEL,)),
    }


def _fwd_reference(x, p, positions, norm_mix_g, w_in, pool_w, pool_scale, q_norm_g, w_uq,
              kv_norm_g, w_ukv, w_out, norm_ffn_g, w_up, conv_w, conv_b, w_down,
              norm_ple_g, w_ple, w_ple_gate, final_norm_g):
    cos, sin = rope_tables(positions)
    o1 = D_POOL
    o2 = o1 + Q_LORA
    o3 = o2 + KV_LORA
    h = x
    for i in range(DEPTH):
        u = rms_norm(h, norm_mix_g[i]) @ w_in[i]
        y_pool = multiscale_pool(u[..., :o1], pool_w[i], pool_scale[i])
        y_att = latent_attention(u[..., o1:o2], u[..., o2:o3], u[..., o3:], cos, sin,
                                 q_norm_g[i], w_uq[i], kv_norm_g[i], w_ukv[i])
        h = h + jnp.concatenate([y_pool, y_att], axis=-1) @ w_out[i]
        h = h + conv_gated_ffn(rms_norm(h, norm_ffn_g[i]), w_up[i], conv_w[i], conv_b[i], w_down[i])
        gate = jax.nn.sigmoid(rms_norm(h, norm_ple_g[i]) @ w_ple_gate[i])
        h = h + (p[i] @ w_ple[i]) * gate
    return rms_norm(h, final_norm_g)


import jax as _jax
import jax.numpy as _jnp

TWIN_FORMAT = 'train_step'
FWD_PARAMS = ['x', 'p', 'positions', 'norm_mix_g', 'w_in', 'pool_w', 'pool_scale', 'q_norm_g', 'w_uq', 'kv_norm_g', 'w_ukv', 'w_out', 'norm_ffn_g', 'w_up', 'conv_w', 'conv_b', 'w_down', 'norm_ple_g', 'w_ple', 'w_ple_gate', 'final_norm_g']
TWIN_WEIGHTS = ['norm_mix_g', 'w_in', 'pool_w', 'pool_scale', 'q_norm_g', 'w_uq', 'kv_norm_g', 'w_ukv', 'w_out', 'norm_ffn_g', 'w_up', 'conv_w', 'conv_b', 'w_down', 'norm_ple_g', 'w_ple', 'w_ple_gate', 'final_norm_g']
TWIN_DIFF_INPUT = 'x'
TWIN_INPUTS = ['x', 'p', 'positions', 'norm_mix_g', 'w_in', 'pool_w', 'pool_scale', 'q_norm_g', 'w_uq', 'kv_norm_g', 'w_ukv', 'w_out', 'norm_ffn_g', 'w_up', 'conv_w', 'conv_b', 'w_down', 'norm_ple_g', 'w_ple', 'w_ple_gate', 'final_norm_g', 'loss_target', 'm_norm_mix_g', 'm_w_in', 'm_pool_w', 'm_pool_scale', 'm_q_norm_g', 'm_w_uq', 'm_kv_norm_g', 'm_w_ukv', 'm_w_out', 'm_norm_ffn_g', 'm_w_up', 'm_conv_w', 'm_conv_b', 'm_w_down', 'm_norm_ple_g', 'm_w_ple', 'm_w_ple_gate', 'm_final_norm_g', 'v_norm_mix_g', 'v_w_in', 'v_pool_w', 'v_pool_scale', 'v_q_norm_g', 'v_w_uq', 'v_kv_norm_g', 'v_w_ukv', 'v_w_out', 'v_norm_ffn_g', 'v_w_up', 'v_conv_w', 'v_conv_b', 'v_w_down', 'v_norm_ple_g', 'v_w_ple', 'v_w_ple_gate', 'v_final_norm_g']
TWIN_OUTPUTS = ['loss', 'grad_x', 'grad_norm_mix_g', 'grad_w_in', 'grad_pool_w', 'grad_pool_scale', 'grad_q_norm_g', 'grad_w_uq', 'grad_kv_norm_g', 'grad_w_ukv', 'grad_w_out', 'grad_norm_ffn_g', 'grad_w_up', 'grad_conv_w', 'grad_conv_b', 'grad_w_down', 'grad_norm_ple_g', 'grad_w_ple', 'grad_w_ple_gate', 'grad_final_norm_g', 'delta_norm_mix_g', 'delta_w_in', 'delta_pool_w', 'delta_pool_scale', 'delta_q_norm_g', 'delta_w_uq', 'delta_kv_norm_g', 'delta_w_ukv', 'delta_w_out', 'delta_norm_ffn_g', 'delta_w_up', 'delta_conv_w', 'delta_conv_b', 'delta_w_down', 'delta_norm_ple_g', 'delta_w_ple', 'delta_w_ple_gate', 'delta_final_norm_g', 'new_m_norm_mix_g', 'new_m_w_in', 'new_m_pool_w', 'new_m_pool_scale', 'new_m_q_norm_g', 'new_m_w_uq', 'new_m_kv_norm_g', 'new_m_w_ukv', 'new_m_w_out', 'new_m_norm_ffn_g', 'new_m_w_up', 'new_m_conv_w', 'new_m_conv_b', 'new_m_w_down', 'new_m_norm_ple_g', 'new_m_w_ple', 'new_m_w_ple_gate', 'new_m_final_norm_g', 'new_v_norm_mix_g', 'new_v_w_in', 'new_v_pool_w', 'new_v_pool_scale', 'new_v_q_norm_g', 'new_v_w_uq', 'new_v_kv_norm_g', 'new_v_w_ukv', 'new_v_w_out', 'new_v_norm_ffn_g', 'new_v_w_up', 'new_v_conv_w', 'new_v_conv_b', 'new_v_w_down', 'new_v_norm_ple_g', 'new_v_w_ple', 'new_v_w_ple_gate', 'new_v_final_norm_g']
TWIN_LEAF_KINDS = {'loss': 'loss', 'grad_x': 'grad_x', 'grad_norm_mix_g': 'grad_w', 'grad_w_in': 'grad_w', 'grad_pool_w': 'grad_w', 'grad_pool_scale': 'grad_w', 'grad_q_norm_g': 'grad_w', 'grad_w_uq': 'grad_w', 'grad_kv_norm_g': 'grad_w', 'grad_w_ukv': 'grad_w', 'grad_w_out': 'grad_w', 'grad_norm_ffn_g': 'grad_w', 'grad_w_up': 'grad_w', 'grad_conv_w': 'grad_w', 'grad_conv_b': 'grad_w', 'grad_w_down': 'grad_w', 'grad_norm_ple_g': 'grad_w', 'grad_w_ple': 'grad_w', 'grad_w_ple_gate': 'grad_w', 'grad_final_norm_g': 'grad_w', 'delta_norm_mix_g': 'delta_w', 'delta_w_in': 'delta_w', 'delta_pool_w': 'delta_w', 'delta_pool_scale': 'delta_w', 'delta_q_norm_g': 'delta_w', 'delta_w_uq': 'delta_w', 'delta_kv_norm_g': 'delta_w', 'delta_w_ukv': 'delta_w', 'delta_w_out': 'delta_w', 'delta_norm_ffn_g': 'delta_w', 'delta_w_up': 'delta_w', 'delta_conv_w': 'delta_w', 'delta_conv_b': 'delta_w', 'delta_w_down': 'delta_w', 'delta_norm_ple_g': 'delta_w', 'delta_w_ple': 'delta_w', 'delta_w_ple_gate': 'delta_w', 'delta_final_norm_g': 'delta_w', 'new_m_norm_mix_g': 'new_m', 'new_m_w_in': 'new_m', 'new_m_pool_w': 'new_m', 'new_m_pool_scale': 'new_m', 'new_m_q_norm_g': 'new_m', 'new_m_w_uq': 'new_m', 'new_m_kv_norm_g': 'new_m', 'new_m_w_ukv': 'new_m', 'new_m_w_out': 'new_m', 'new_m_norm_ffn_g': 'new_m', 'new_m_w_up': 'new_m', 'new_m_conv_w': 'new_m', 'new_m_conv_b': 'new_m', 'new_m_w_down': 'new_m', 'new_m_norm_ple_g': 'new_m', 'new_m_w_ple': 'new_m', 'new_m_w_ple_gate': 'new_m', 'new_m_final_norm_g': 'new_m', 'new_v_norm_mix_g': 'new_v', 'new_v_w_in': 'new_v', 'new_v_pool_w': 'new_v', 'new_v_pool_scale': 'new_v', 'new_v_q_norm_g': 'new_v', 'new_v_w_uq': 'new_v', 'new_v_kv_norm_g': 'new_v', 'new_v_w_ukv': 'new_v', 'new_v_w_out': 'new_v', 'new_v_norm_ffn_g': 'new_v', 'new_v_w_up': 'new_v', 'new_v_conv_w': 'new_v', 'new_v_conv_b': 'new_v', 'new_v_w_down': 'new_v', 'new_v_norm_ple_g': 'new_v', 'new_v_w_ple': 'new_v', 'new_v_w_ple_gate': 'new_v', 'new_v_final_norm_g': 'new_v'}


def _forward(args):
    return _fwd_reference(*[args[k] for k in FWD_PARAMS])


def _output_shape():
    def fwd():
        inp = _fwd_setup_inputs(0)
        return _fwd_reference(*[inp[k] for k in FWD_PARAMS])
    out = _jax.eval_shape(fwd)
    return out.shape, out.dtype

N_MICROBATCH = 1
ADAM_LR = 0.001
ADAM_B1 = 0.9
ADAM_B2 = 0.999
ADAM_EPS = 1e-08
ADAM_WD = 0.01
ADAM_STEP = 10
PER_EXAMPLE_BATCH_AXIS = {'x': 0, 'p': 1, 'positions': 0, 'loss_target': 0}
SHARED_INPUTS = []
_WEIGHT_DTYPES = {'norm_mix_g': _jnp.float32, 'w_in': _jnp.float32, 'pool_w': _jnp.float32, 'pool_scale': _jnp.float32, 'q_norm_g': _jnp.float32, 'w_uq': _jnp.float32, 'kv_norm_g': _jnp.float32, 'w_ukv': _jnp.float32, 'w_out': _jnp.float32, 'norm_ffn_g': _jnp.float32, 'w_up': _jnp.float32, 'conv_w': _jnp.float32, 'conv_b': _jnp.float32, 'w_down': _jnp.float32, 'norm_ple_g': _jnp.float32, 'w_ple': _jnp.float32, 'w_ple_gate': _jnp.float32, 'final_norm_g': _jnp.float32}
MOMENT_SCALE = {'norm_mix_g': 4.045348e-02, 'w_in': 3.955604e-02, 'pool_w': 5.367519e-02, 'pool_scale': 5.325971e-02, 'q_norm_g': 1.546876e-02, 'w_uq': 8.776711e-03, 'kv_norm_g': 2.317024e-02, 'w_ukv': 1.186860e-02, 'w_out': 3.871782e-02, 'norm_ffn_g': 4.579324e-02, 'w_up': 1.952764e-02, 'conv_w': 1.965540e-02, 'conv_b': 1.862798e-02, 'w_down': 3.191901e-02, 'norm_ple_g': 1.106672e-02, 'w_ple': 2.765561e-02, 'w_ple_gate': 1.079105e-02, 'final_norm_g': 1.600755e+01}


def _to_microbatches(a, axis):
    t = _jnp.moveaxis(a, axis, 0)
    t = t.reshape((N_MICROBATCH, t.shape[0] // N_MICROBATCH) + t.shape[1:])
    return _jnp.moveaxis(t, 1, axis + 1)


def setup_inputs(seed: int = 0) -> dict:
    inp = _fwd_setup_inputs(seed)
    key = _jax.random.fold_in(_jax.random.key(seed), 7919)
    shape, _ = _output_shape()
    out = dict(inp)
    out["loss_target"] = _jax.random.normal(_jax.random.fold_in(key, 0), shape, _jnp.float32)
    for i, name in enumerate(TWIN_WEIGHTS):
        w = inp[name].astype(_jnp.float32)
        if MOMENT_SCALE is None:
            s = _jnp.sqrt(_jnp.mean(_jnp.square(w)) + 1e-30)
        else:
            s = MOMENT_SCALE[name]
        km, kv = _jax.random.split(_jax.random.fold_in(key, i + 1))
        out[name] = w
        out["m_" + name] = s * _jax.random.normal(km, w.shape, _jnp.float32)
        out["v_" + name] = (s * s) * _jax.random.uniform(kv, w.shape, _jnp.float32, 0.5, 1.5)
    if N_MICROBATCH > 1:
        for name, axis in PER_EXAMPLE_BATCH_AXIS.items():
            out[name] = _to_microbatches(out[name], axis)
    return {'x': out['x'], 'p': out['p'], 'positions': out['positions'], 'norm_mix_g': out['norm_mix_g'], 'w_in': out['w_in'], 'pool_w': out['pool_w'], 'pool_scale': out['pool_scale'], 'q_norm_g': out['q_norm_g'], 'w_uq': out['w_uq'], 'kv_norm_g': out['kv_norm_g'], 'w_ukv': out['w_ukv'], 'w_out': out['w_out'], 'norm_ffn_g': out['norm_ffn_g'], 'w_up': out['w_up'], 'conv_w': out['conv_w'], 'conv_b': out['conv_b'], 'w_down': out['w_down'], 'norm_ple_g': out['norm_ple_g'], 'w_ple': out['w_ple'], 'w_ple_gate': out['w_ple_gate'], 'final_norm_g': out['final_norm_g'], 'loss_target': out['loss_target'], 'm_norm_mix_g': out['m_norm_mix_g'], 'm_w_in': out['m_w_in'], 'm_pool_w': out['m_pool_w'], 'm_pool_scale': out['m_pool_scale'], 'm_q_norm_g': out['m_q_norm_g'], 'm_w_uq': out['m_w_uq'], 'm_kv_norm_g': out['m_kv_norm_g'], 'm_w_ukv': out['m_w_ukv'], 'm_w_out': out['m_w_out'], 'm_norm_ffn_g': out['m_norm_ffn_g'], 'm_w_up': out['m_w_up'], 'm_conv_w': out['m_conv_w'], 'm_conv_b': out['m_conv_b'], 'm_w_down': out['m_w_down'], 'm_norm_ple_g': out['m_norm_ple_g'], 'm_w_ple': out['m_w_ple'], 'm_w_ple_gate': out['m_w_ple_gate'], 'm_final_norm_g': out['m_final_norm_g'], 'v_norm_mix_g': out['v_norm_mix_g'], 'v_w_in': out['v_w_in'], 'v_pool_w': out['v_pool_w'], 'v_pool_scale': out['v_pool_scale'], 'v_q_norm_g': out['v_q_norm_g'], 'v_w_uq': out['v_w_uq'], 'v_kv_norm_g': out['v_kv_norm_g'], 'v_w_ukv': out['v_w_ukv'], 'v_w_out': out['v_w_out'], 'v_norm_ffn_g': out['v_norm_ffn_g'], 'v_w_up': out['v_w_up'], 'v_conv_w': out['v_conv_w'], 'v_conv_b': out['v_conv_b'], 'v_w_down': out['v_w_down'], 'v_norm_ple_g': out['v_norm_ple_g'], 'v_w_ple': out['v_w_ple'], 'v_w_ple_gate': out['v_w_ple_gate'], 'v_final_norm_g': out['v_final_norm_g']}


def _loss(weights, diff, rest, loss_target):
    with _jax.named_scope("forward"):
        args = {**rest, TWIN_DIFF_INPUT: diff, **{k: w.astype(_WEIGHT_DTYPES[k]) for k, w in weights.items()}}
        y = _forward(args)
    with _jax.named_scope("loss_head"):
        err = _jnp.square(y.astype(_jnp.float32) - loss_target)
        return 0.5 * _jnp.sum(_jnp.mean(err, axis=-1)) if err.ndim else 0.5 * err


def _adamw(w, g, m, v):
    m = ADAM_B1 * m + (1.0 - ADAM_B1) * g
    v = ADAM_B2 * v + (1.0 - ADAM_B2) * _jnp.square(g)
    m_hat = m / (1.0 - ADAM_B1 ** ADAM_STEP)
    v_hat = v / (1.0 - ADAM_B2 ** ADAM_STEP)
    delta = -ADAM_LR * (m_hat / (_jnp.sqrt(v_hat) + ADAM_EPS) + ADAM_WD * w)
    return delta, m, v


def reference(x, p, positions, norm_mix_g, w_in, pool_w, pool_scale, q_norm_g, w_uq, kv_norm_g, w_ukv, w_out, norm_ffn_g, w_up, conv_w, conv_b, w_down, norm_ple_g, w_ple, w_ple_gate, final_norm_g, loss_target, m_norm_mix_g, m_w_in, m_pool_w, m_pool_scale, m_q_norm_g, m_w_uq, m_kv_norm_g, m_w_ukv, m_w_out, m_norm_ffn_g, m_w_up, m_conv_w, m_conv_b, m_w_down, m_norm_ple_g, m_w_ple, m_w_ple_gate, m_final_norm_g, v_norm_mix_g, v_w_in, v_pool_w, v_pool_scale, v_q_norm_g, v_w_uq, v_kv_norm_g, v_w_ukv, v_w_out, v_norm_ffn_g, v_w_up, v_conv_w, v_conv_b, v_w_down, v_norm_ple_g, v_w_ple, v_w_ple_gate, v_final_norm_g):
    given = dict(x=x, p=p, positions=positions, norm_mix_g=norm_mix_g, w_in=w_in, pool_w=pool_w, pool_scale=pool_scale, q_norm_g=q_norm_g, w_uq=w_uq, kv_norm_g=kv_norm_g, w_ukv=w_ukv, w_out=w_out, norm_ffn_g=norm_ffn_g, w_up=w_up, conv_w=conv_w, conv_b=conv_b, w_down=w_down, norm_ple_g=norm_ple_g, w_ple=w_ple, w_ple_gate=w_ple_gate, final_norm_g=final_norm_g, loss_target=loss_target, m_norm_mix_g=m_norm_mix_g, m_w_in=m_w_in, m_pool_w=m_pool_w, m_pool_scale=m_pool_scale, m_q_norm_g=m_q_norm_g, m_w_uq=m_w_uq, m_kv_norm_g=m_kv_norm_g, m_w_ukv=m_w_ukv, m_w_out=m_w_out, m_norm_ffn_g=m_norm_ffn_g, m_w_up=m_w_up, m_conv_w=m_conv_w, m_conv_b=m_conv_b, m_w_down=m_w_down, m_norm_ple_g=m_norm_ple_g, m_w_ple=m_w_ple, m_w_ple_gate=m_w_ple_gate, m_final_norm_g=m_final_norm_g, v_norm_mix_g=v_norm_mix_g, v_w_in=v_w_in, v_pool_w=v_pool_w, v_pool_scale=v_pool_scale, v_q_norm_g=v_q_norm_g, v_w_uq=v_w_uq, v_kv_norm_g=v_kv_norm_g, v_w_ukv=v_w_ukv, v_w_out=v_w_out, v_norm_ffn_g=v_norm_ffn_g, v_w_up=v_w_up, v_conv_w=v_conv_w, v_conv_b=v_conv_b, v_w_down=v_w_down, v_norm_ple_g=v_norm_ple_g, v_w_ple=v_w_ple, v_w_ple_gate=v_w_ple_gate, v_final_norm_g=v_final_norm_g)
    weights = {n: given[n] for n in TWIN_WEIGHTS}
    shared = {n: given[n] for n in SHARED_INPUTS}
    per_example = {n: given[n] for n in ['x', 'p', 'positions']}
    grad_fn = _jax.value_and_grad(_loss, argnums=(0, 1))

    def one_microbatch(ex, loss_target):
        ex = dict(ex)
        diff = ex.pop(TWIN_DIFF_INPUT)
        return grad_fn(weights, diff, {**shared, **ex}, loss_target)

    if N_MICROBATCH == 1:
        loss, (grad_w, grad_x) = one_microbatch(per_example, given["loss_target"])
    else:
        def body(carry, xs):
            loss_sum, grad_sum = carry
            l_k, (gw_k, gx_k) = one_microbatch(xs[0], xs[1])
            with _jax.named_scope("update"):
                return (loss_sum + l_k, _jax.tree.map(_jnp.add, grad_sum, gw_k)), gx_k

        init = (_jnp.zeros((), _jnp.float32), _jax.tree.map(_jnp.zeros_like, weights))
        (loss, grad_w), grad_x = _jax.lax.scan(body, init, (per_example, given["loss_target"]))
    with _jax.named_scope("update"):
        delta_w, new_m, new_v = {}, {}, {}
        for n in TWIN_WEIGHTS:
            delta_w[n], new_m[n], new_v[n] = _adamw(weights[n], grad_w[n], given["m_" + n], given["v_" + n])
    return (loss, grad_x, *[grad_w[n] for n in TWIN_WEIGHTS], *[delta_w[n] for n in TWIN_WEIGHTS],
            *[new_m[n] for n in TWIN_WEIGHTS], *[new_v[n] for n in TWIN_WEIGHTS])
```

```python
import functools
import math

import jax
import jax.numpy as jnp
from jax import lax
from jax.experimental import pallas as pl
from jax.experimental.pallas import tpu as pltpu

F32 = jnp.float32
BF16 = jnp.bfloat16

NOPE_DIM = 128
ROPE_DIM = 64
V_DIM = 128
LANES = 128
SUBLANES_BF16 = 16
ROPE_THETA = 10000.0
EPS = 1e-6
POOL_WINDOWS = (2, 4, 8, 16)
POOL_HALO = 16
CONV_TAPS = 3
CONV_HALO = 8
ADAM_LR = 0.001
ADAM_B1 = 0.9
ADAM_B2 = 0.999
ADAM_EPS = 1e-08
ADAM_WD = 0.01
ADAM_STEP = 10
NEG_BIG = -1e30
V7X_VMEM_BYTES = 64 * 2 ** 20
N_SHARDS = 4
PACK_ALIGN = 2 * SUBLANES_BF16 * LANES

TILES = dict(row=256, att=512, mm_m=1024, mm_n=1024, mm_k=2048, ffn_c=512, pack_rows=2048)

SHARDED = ("w_in", "pool_w", "w_uq", "w_ukv", "w_out", "w_up", "conv_w", "w_down", "w_ple", "w_ple_gate")
COL_SHARDED = ("w_in", "w_uq", "w_ukv", "w_up", "conv_w", "w_ple")
ROW_SHARDED = ("w_out", "w_down", "w_ple_gate")
WEIGHTS = ("norm_mix_g", "w_in", "pool_w", "pool_scale", "q_norm_g", "w_uq", "kv_norm_g", "w_ukv", "w_out",
           "norm_ffn_g", "w_up", "conv_w", "conv_b", "w_down", "norm_ple_g", "w_ple", "w_ple_gate", "final_norm_g")
MESH = pl.DeviceIdType.MESH


def _nbytes(shape, dtype):
    return math.prod(shape) * jnp.dtype(dtype).itemsize


def _params(sem, need_bytes):
    limit = min(V7X_VMEM_BYTES - (8 << 20), max(32 << 20, int(need_bytes)))
    return pltpu.CompilerParams(dimension_semantics=sem, vmem_limit_bytes=limit)


def _tile(n, want, mult=8):
    if n <= want:
        return n
    for t in range(want - want % mult, 0, -mult):
        if n % t == 0:
            return t
    return n


def _sigmoid(x):
    return 1.0 / (1.0 + jnp.exp(-x))


def _rstd(x):
    return lax.rsqrt(jnp.mean(x * x, axis=-1, keepdims=True) + EPS)


_DOT_DIMS = {"nn": (((1,), (0,)), ((), ())), "nt": (((1,), (1,)), ((), ())), "tn": (((0,), (0,)), ((), ()))}


def _matmul(a, b, mode, out_dtype, name, res=None, tm=None, tn=None, tk=None):
    if mode == "nn":
        (M, K), N = a.shape, b.shape[1]
    elif mode == "nt":
        (M, K), N = a.shape, b.shape[0]
    else:
        (K, M), N = a.shape, b.shape[1]
    tm = _tile(M, tm or TILES["mm_m"], LANES)
    tn = _tile(N, tn or TILES["mm_n"], LANES)
    tk = _tile(K, tk or TILES["mm_k"], LANES)
    nk = K // tk
    has_res = res is not None
    dims = _DOT_DIMS[mode]

    def body(*refs):
        a_ref, b_ref = refs[0], refs[1]
        o_ref = refs[2 + has_res]
        part = lax.dot_general(a_ref[...].astype(BF16), b_ref[...].astype(BF16), dims, preferred_element_type=F32)

        def finish(acc):
            if has_res:
                acc = acc + refs[2][...]
            o_ref[...] = acc.astype(o_ref.dtype)

        if nk == 1:
            finish(part)
        else:
            acc_ref = refs[3 + has_res]
            k = pl.program_id(2)

            @pl.when(k == 0)
            def _():
                acc_ref[...] = part

            @pl.when(k > 0)
            def _():
                acc_ref[...] += part

            @pl.when(k == nk - 1)
            def _():
                finish(acc_ref[...])

    if mode == "nn":
        a_spec, b_spec = pl.BlockSpec((tm, tk), lambda i, j, k: (i, k)), pl.BlockSpec((tk, tn), lambda i, j, k: (k, j))
    elif mode == "nt":
        a_spec, b_spec = pl.BlockSpec((tm, tk), lambda i, j, k: (i, k)), pl.BlockSpec((tn, tk), lambda i, j, k: (j, k))
    else:
        a_spec, b_spec = pl.BlockSpec((tk, tm), lambda i, j, k: (k, i)), pl.BlockSpec((tk, tn), lambda i, j, k: (k, j))
    o_spec = pl.BlockSpec((tm, tn), lambda i, j, k: (i, j))
    in_specs, args = [a_spec, b_spec], [a, b]
    need = 2 * (_nbytes((tm, tk), a.dtype) + _nbytes((tk, tn), b.dtype) + _nbytes((tm, tn), out_dtype)) + 2 * _nbytes((tm, tn), F32)
    if has_res:
        in_specs.append(o_spec)
        args.append(res)
        need += 2 * _nbytes((tm, tn), res.dtype)
    scratch = [pltpu.VMEM((tm, tn), F32)] if nk > 1 else []
    return pl.pallas_call(
        body, name=name, grid=(M // tm, N // tn, nk), in_specs=in_specs, out_specs=o_spec,
        out_shape=jax.ShapeDtypeStruct((M, N), out_dtype), scratch_shapes=scratch,
        compiler_params=_params(("parallel", "parallel", "arbitrary"), need + (4 << 20)),
    )(*args)


def _rms_fwd(x, g, name):
    T, D = x.shape
    tt = _tile(T, TILES["row"])

    def body(x_ref, g_ref, o_ref):
        xv = x_ref[...]
        o_ref[...] = (xv * _rstd(xv) * g_ref[...]).astype(o_ref.dtype)

    row = pl.BlockSpec((tt, D), lambda i: (i, 0))
    return pl.pallas_call(
        body, name=name, grid=(T // tt,), in_specs=[row, pl.BlockSpec((1, D), lambda i: (0, 0))], out_specs=row,
        out_shape=jax.ShapeDtypeStruct((T, D), BF16), compiler_params=_params(("parallel",), 8 * _nbytes((tt, D), F32)),
    )(x, g.reshape(1, D))


def _rms_bwd(dn, x, g, dres, name):
    T, D = x.shape
    tt = _tile(T, TILES["row"])

    def body(dn_ref, x_ref, g_ref, dres_ref, dx_ref, dxb_ref, dg_ref):
        i = pl.program_id(0)
        xv = x_ref[...]
        r = _rstd(xv)
        xh = xv * r
        dnv = dn_ref[...].astype(F32)
        dxh = dnv * g_ref[...]
        tot = dres_ref[...] + r * (dxh - xh * jnp.mean(dxh * xh, axis=-1, keepdims=True))
        dx_ref[...] = tot
        dxb_ref[...] = tot.astype(BF16)
        part = jnp.sum(dnv * xh, axis=0, keepdims=True)

        @pl.when(i == 0)
        def _():
            dg_ref[...] = part

        @pl.when(i > 0)
        def _():
            dg_ref[...] += part

    row = pl.BlockSpec((tt, D), lambda i: (i, 0))
    vec = pl.BlockSpec((1, D), lambda i: (0, 0))
    return pl.pallas_call(
        body, name=name, grid=(T // tt,), in_specs=[row, row, vec, row], out_specs=[row, row, vec],
        out_shape=[jax.ShapeDtypeStruct((T, D), F32), jax.ShapeDtypeStruct((T, D), BF16), jax.ShapeDtypeStruct((1, D), F32)],
        compiler_params=_params(("arbitrary",), 16 * _nbytes((tt, D), F32)),
    )(dn, x, g.reshape(1, D), dres)


def _final_loss(h, target, g):
    T, D = h.shape
    tt = _tile(T, TILES["row"])

    def body(h_ref, t_ref, g_ref, dx_ref, dxb_ref, dg_ref, loss_ref):
        i = pl.program_id(0)
        xv = h_ref[...]
        r = _rstd(xv)
        xh = xv * r
        gv = g_ref[...]
        err = xh * gv - t_ref[...]
        lpart = 0.5 * jnp.sum(jnp.mean(err * err, axis=-1, keepdims=True), axis=0, keepdims=True)
        dy = err * (1.0 / D)
        dxh = dy * gv
        dx = r * (dxh - xh * jnp.mean(dxh * xh, axis=-1, keepdims=True))
        dx_ref[...] = dx
        dxb_ref[...] = dx.astype(BF16)
        gpart = jnp.sum(dy * xh, axis=0, keepdims=True)
        lrow = jnp.broadcast_to(lpart, (1, LANES))

        @pl.when(i == 0)
        def _():
            dg_ref[...] = gpart
            loss_ref[...] = lrow

        @pl.when(i > 0)
        def _():
            dg_ref[...] += gpart
            loss_ref[...] += lrow

    row = pl.BlockSpec((tt, D), lambda i: (i, 0))
    vec = pl.BlockSpec((1, D), lambda i: (0, 0))
    return pl.pallas_call(
        body, name="final_loss", grid=(T // tt,), in_specs=[row, row, vec],
        out_specs=[row, row, vec, pl.BlockSpec((1, LANES), lambda i: (0, 0))],
        out_shape=[jax.ShapeDtypeStruct((T, D), F32), jax.ShapeDtypeStruct((T, D), BF16),
                   jax.ShapeDtypeStruct((1, D), F32), jax.ShapeDtypeStruct((1, LANES), F32)],
        compiler_params=_params(("arbitrary",), 16 * _nbytes((tt, D), F32)),
    )(h, target, g.reshape(1, D))


def _rope_tables(pos_col, inv_lane):
    T = pos_col.shape[0]
    tt = _tile(T, TILES["row"])

    def body(p_ref, f_ref, c_ref, s1_ref, s2_ref):
        ang = p_ref[...] * f_ref[...]
        lane = lax.broadcasted_iota(jnp.int32, ang.shape, 1)
        half = ROPE_DIM // 2
        cs, sn = jnp.cos(ang), jnp.sin(ang)
        c_ref[...] = jnp.where(lane < ROPE_DIM, cs, 0.0)
        s1_ref[...] = jnp.where(lane < half, -sn, 0.0)
        s2_ref[...] = jnp.where((lane >= half) & (lane < ROPE_DIM), sn, 0.0)

    tab = pl.BlockSpec((tt, LANES), lambda i: (i, 0))
    shp = jax.ShapeDtypeStruct((T, LANES), F32)
    return pl.pallas_call(
        body, name="rope_tables", grid=(T // tt,),
        in_specs=[pl.BlockSpec((tt, 1), lambda i: (i, 0)), pl.BlockSpec((1, LANES), lambda i: (0, 0))],
        out_specs=[tab, tab, tab], out_shape=[shp, shp, shp],
        compiler_params=_params(("parallel",), 32 * _nbytes((tt, LANES), F32)),
    )(pos_col, inv_lane)


def _rope(x, c, s1, s2):
    return x * c + pltpu.roll(x, LANES - ROPE_DIM // 2, 1) * s1 + pltpu.roll(x, ROPE_DIM // 2, 1) * s2


def _rope_t(d, c, s1, s2):
    return d * c + pltpu.roll(d * s1, ROPE_DIM // 2, 1) + pltpu.roll(d * s2, LANES - ROPE_DIM // 2, 1)


def _window_sum(xe, w, forward):
    n = xe.shape[0]
    s, sh = xe, 1
    while sh < w:
        s = s + pltpu.roll(s, (n - sh) if forward else sh, 0)
        sh *= 2
    return s


def _post_u(u, gq, gkv, tabs, dims):
    T, Dp = u.shape
    P, QL, KL, C = dims["P"], dims["QL"], dims["KL"], dims["C"]
    tt = _tile(T, TILES["row"], POOL_HALO)
    hb = tt // POOL_HALO

    def body(u_ref, halo_ref, gq_ref, gkv_ref, c_ref, s1_ref, s2_ref, diff_ref, cq_ref, ckv_ref, kr_ref):
        i = pl.program_id(0)
        t = i * tt + lax.broadcasted_iota(jnp.int32, (tt, 1), 0)
        halo = jnp.where(i > 0, halo_ref[...], 0.0)
        for gi, w in enumerate(POOL_WINDOWS):
            cols = slice(gi * C, (gi + 1) * C)
            xg = u_ref[:, cols]
            s = _window_sum(jnp.concatenate([halo[:, cols], xg], axis=0), w, False)[POOL_HALO:]
            cnt = jnp.minimum(t + 1, w).astype(F32)
            diff_ref[:, cols] = (s / cnt - xg).astype(BF16)
        cq = u_ref[:, P:P + QL]
        cq_ref[...] = (cq * _rstd(cq) * gq_ref[...]).astype(BF16)
        ckv = u_ref[:, P + QL:P + QL + KL]
        ckv_ref[...] = (ckv * _rstd(ckv) * gkv_ref[...]).astype(BF16)
        kr_ref[...] = _rope(u_ref[:, P + QL + KL:], c_ref[...], s1_ref[...], s2_ref[...]).astype(BF16)

    def row(w):
        return pl.BlockSpec((tt, w), lambda i: (i, 0))

    def vec(w):
        return pl.BlockSpec((1, w), lambda i: (0, 0))

    return pl.pallas_call(
        body, name="post_u", grid=(T // tt,),
        in_specs=[row(Dp), pl.BlockSpec((POOL_HALO, P), lambda i: (jnp.maximum(i * hb - 1, 0), 0)),
                  vec(QL), vec(KL), row(LANES), row(LANES), row(LANES)],
        out_specs=[row(P), row(QL), row(KL), row(LANES)],
        out_shape=[jax.ShapeDtypeStruct((T, P), BF16), jax.ShapeDtypeStruct((T, QL), BF16),
                   jax.ShapeDtypeStruct((T, KL), BF16), jax.ShapeDtypeStruct((T, LANES), BF16)],
        compiler_params=_params(("parallel",), 10 * _nbytes((tt, Dp), F32)),
    )(u, u, gq.reshape(1, QL), gkv.reshape(1, KL), *tabs)


def _pre_u_bwd(u, d_cqn, d_ckvn, d_diff, dkr, gq, gkv, tabs, dims):
    T, Dp = u.shape
    P, QL, KL, C, H = dims["P"], dims["QL"], dims["KL"], dims["C"], dims["H"]
    tt = _tile(T, TILES["row"], POOL_HALO)
    hb = tt // POOL_HALO
    n_t = T // tt

    def norm_bwd(xv, dn, gv):
        r = _rstd(xv)
        xh = xv * r
        dxh = dn * gv
        return r * (dxh - xh * jnp.mean(dxh * xh, axis=-1, keepdims=True)), jnp.sum(dn * xh, axis=0, keepdims=True)

    def body(u_ref, dcq_ref, dckv_ref, dd_ref, ddn_ref, dkr_ref, gq_ref, gkv_ref, c_ref, s1_ref, s2_ref,
             du_ref, dgq_ref, dgkv_ref):
        i = pl.program_id(0)
        t = i * tt + lax.broadcasted_iota(jnp.int32, (tt, 1), 0)
        nxt = jnp.where(i < n_t - 1, ddn_ref[...].astype(F32), 0.0)
        for gi, w in enumerate(POOL_WINDOWS):
            cols = slice(gi * C, (gi + 1) * C)
            dd = dd_ref[:, cols].astype(F32)
            e = dd / jnp.minimum(t + 1, w).astype(F32)
            s = _window_sum(jnp.concatenate([e, nxt[:, cols] / float(w)], axis=0), w, True)[:tt]
            du_ref[:, cols] = (s - dd).astype(BF16)
        dq, pq = norm_bwd(u_ref[:, P:P + QL], dcq_ref[...], gq_ref[...])
        du_ref[:, P:P + QL] = dq.astype(BF16)
        dkv, pkv = norm_bwd(u_ref[:, P + QL:P + QL + KL], dckv_ref[...], gkv_ref[...])
        du_ref[:, P + QL:P + QL + KL] = dkv.astype(BF16)
        dk = dkr_ref[0]
        for hh in range(1, H):
            dk = dk + dkr_ref[hh]
        du_ref[:, P + QL + KL:] = _rope_t(dk, c_ref[...], s1_ref[...], s2_ref[...]).astype(BF16)

        @pl.when(i == 0)
        def _():
            dgq_ref[...] = pq
            dgkv_ref[...] = pkv

        @pl.when(i > 0)
        def _():
            dgq_ref[...] += pq
            dgkv_ref[...] += pkv

    def row(w):
        return pl.BlockSpec((tt, w), lambda i: (i, 0))

    def vec(w):
        return pl.BlockSpec((1, w), lambda i: (0, 0))

    return pl.pallas_call(
        body, name="pre_u_bwd", grid=(n_t,),
        in_specs=[row(Dp), row(QL), row(KL), row(P),
                  pl.BlockSpec((POOL_HALO, P), lambda i: (jnp.minimum((i + 1) * hb, T // POOL_HALO - 1), 0)),
                  pl.BlockSpec((H, tt, LANES), lambda i: (0, i, 0)), vec(QL), vec(KL), row(LANES), row(LANES), row(LANES)],
        out_specs=[row(Dp), vec(QL), vec(KL)],
        out_shape=[jax.ShapeDtypeStruct((T, Dp), BF16), jax.ShapeDtypeStruct((1, QL), F32), jax.ShapeDtypeStruct((1, KL), F32)],
        compiler_params=_params(("arbitrary",), 12 * _nbytes((tt, Dp), F32)),
    )(u, d_cqn, d_ckvn, d_diff, d_diff, dkr, gq.reshape(1, QL), gkv.reshape(1, KL), *tabs)


def _pool_fwd(diff, pw, ps, dims):
    T, P = diff.shape
    G, C = len(POOL_WINDOWS), dims["C"]
    tt = _tile(T, TILES["row"])

    def body(d_ref, w_ref, s_ref, o_ref):
        for gi in range(G):
            cols = slice(gi * C, (gi + 1) * C)
            y = jnp.dot(d_ref[:, cols], w_ref[gi], preferred_element_type=F32)
            o_ref[:, cols] = (y * s_ref[:, cols]).astype(BF16)

    row = pl.BlockSpec((tt, P), lambda i: (i, 0))
    return pl.pallas_call(
        body, name="pool_fwd", grid=(T // tt,),
        in_specs=[row, pl.BlockSpec((G, C, C), lambda i: (0, 0, 0)), pl.BlockSpec((1, P), lambda i: (0, 0))],
        out_specs=row, out_shape=jax.ShapeDtypeStruct((T, P), BF16),
        compiler_params=_params(("parallel",), 8 * _nbytes((tt, P), F32)),
    )(diff, pw, ps.reshape(1, P))


def _pool_bwd(dmix, diff, pw, ps, dims):
    T, P = diff.shape
    G, C = len(POOL_WINDOWS), dims["C"]
    tt = _tile(T, TILES["row"])

    def body(dy_ref, d_ref, w_ref, s_ref, dd_ref, dw_ref, ds_ref):
        i = pl.program_id(0)

        @pl.when(i == 0)
        def _():
            dw_ref[...] = jnp.zeros_like(dw_ref)
            ds_ref[...] = jnp.zeros_like(ds_ref)

        for gi in range(G):
            cols = slice(gi * C, (gi + 1) * C)
            dy = dy_ref[:, cols].astype(F32)
            d = d_ref[:, cols]
            w = w_ref[gi]
            ypre = jnp.dot(d, w, preferred_element_type=F32)
            ds_ref[:, cols] += jnp.sum(dy * ypre, axis=0, keepdims=True)
            dyp = (dy * s_ref[:, cols]).astype(BF16)
            dd_ref[:, cols] = lax.dot_general(dyp, w, _DOT_DIMS["nt"], preferred_element_type=F32).astype(BF16)
            dw_ref[gi] += lax.dot_general(d, dyp, _DOT_DIMS["tn"], preferred_element_type=F32)

    row = pl.BlockSpec((tt, P), lambda i: (i, 0))
    wsp = pl.BlockSpec((G, C, C), lambda i: (0, 0, 0))
    vec = pl.BlockSpec((1, P), lambda i: (0, 0))
    return pl.pallas_call(
        body, name="pool_bwd", grid=(T // tt,), in_specs=[row, row, wsp, vec], out_specs=[row, wsp, vec],
        out_shape=[jax.ShapeDtypeStruct((T, P), BF16), jax.ShapeDtypeStruct((G, C, C), F32), jax.ShapeDtypeStruct((1, P), F32)],
        compiler_params=_params(("arbitrary",), 10 * _nbytes((tt, P), F32)),
    )(dmix, diff, pw, ps.reshape(1, P))


def _q_rope(qp, tabs, dims):
    T, W = qp.shape
    H = dims["H"]
    tt = _tile(T, TILES["row"])

    def body(q_ref, c_ref, s1_ref, s2_ref, o_ref):
        o_ref[:, :H * LANES] = q_ref[:, :H * LANES].astype(BF16)
        c, s1, s2 = c_ref[...], s1_ref[...], s2_ref[...]
        for hh in range(H, 2 * H):
            cols = slice(hh * LANES, (hh + 1) * LANES)
            o_ref[:, cols] = _rope(q_ref[:, cols], c, s1, s2).astype(BF16)

    row = pl.BlockSpec((tt, W), lambda i: (i, 0))
    tab = pl.BlockSpec((tt, LANES), lambda i: (i, 0))
    return pl.pallas_call(
        body, name="q_rope", grid=(T // tt,), in_specs=[row, tab, tab, tab], out_specs=row,
        out_shape=jax.ShapeDtypeStruct((T, W), BF16), compiler_params=_params(("parallel",), 8 * _nbytes((tt, W), F32)),
    )(qp, *tabs)


def _scores(qn_ref, qr_ref, kn_ref, kr_ref, iq, ik, t, scale):
    q = jnp.concatenate([qn_ref[...], qr_ref[...]], axis=1)
    k = jnp.concatenate([kn_ref[...], kr_ref[...]], axis=1)
    s = lax.dot_general(q, k, _DOT_DIMS["nt"], preferred_element_type=F32) * scale
    rows = iq * t + lax.broadcasted_iota(jnp.int32, (t, t), 0)
    cols = ik * t + lax.broadcasted_iota(jnp.int32, (t, t), 1)
    return q, k, jnp.where(rows >= cols, s, NEG_BIG)


def _flash_fwd(q_att, kv, kr, dims):
    T = q_att.shape[0]
    H = dims["H"]
    t = _tile(T, TILES["att"])
    n = T // t
    scale = 1.0 / math.sqrt(NOPE_DIM + ROPE_DIM)

    def body(qn_ref, qr_ref, kn_ref, v_ref, kr_ref, o_ref, lse_ref, m_ref, l_ref, acc_ref):
        i, j = pl.program_id(1), pl.program_id(2)

        @pl.when(j == 0)
        def _():
            m_ref[...] = jnp.full_like(m_ref, NEG_BIG)
            l_ref[...] = jnp.zeros_like(l_ref)
            acc_ref[...] = jnp.zeros_like(acc_ref)

        @pl.when(j <= i)
        def _():
            _, _, s = _scores(qn_ref, qr_ref, kn_ref, kr_ref, i, j, t, scale)
            m_prev = m_ref[...]
            m_new = jnp.maximum(m_prev, jnp.max(s, axis=1, keepdims=True))
            alpha = jnp.exp(m_prev - m_new)
            p = jnp.exp(s - m_new[:, :1])
            l_ref[...] = alpha * l_ref[...] + jnp.sum(p, axis=1, keepdims=True)
            acc_ref[...] = alpha * acc_ref[...] + jnp.dot(p.astype(BF16), v_ref[...], preferred_element_type=F32)
            m_ref[...] = m_new

        @pl.when(j == i)
        def _():
            o_ref[...] = (acc_ref[...] / l_ref[...]).astype(BF16)
            lse_ref[...] = m_ref[...] + jnp.log(l_ref[...])

    blk = (t, LANES)
    return pl.pallas_call(
        body, name="flash_fwd", grid=(H, n, n),
        in_specs=[pl.BlockSpec(blk, lambda h, i, j: (i, h)), pl.BlockSpec(blk, lambda h, i, j: (i, H + h)),
                  pl.BlockSpec(blk, lambda h, i, j: (jnp.minimum(j, i), h)),
                  pl.BlockSpec(blk, lambda h, i, j: (jnp.minimum(j, i), H + h)),
                  pl.BlockSpec(blk, lambda h, i, j: (jnp.minimum(j, i), 0))],
        out_specs=[pl.BlockSpec(blk, lambda h, i, j: (i, h)), pl.BlockSpec(blk, lambda h, i, j: (i, h))],
        out_shape=[jax.ShapeDtypeStruct((T, H * LANES), BF16), jax.ShapeDtypeStruct((T, H * LANES), F32)],
        scratch_shapes=[pltpu.VMEM(blk, F32), pltpu.VMEM(blk, F32), pltpu.VMEM(blk, F32)],
        compiler_params=_params(("parallel", "parallel", "arbitrary"), 8 * _nbytes((t, t), F32) + (8 << 20)),
    )(q_att, q_att, kv, kv, kr)


def _flash_bwd(q_att, kv, kr, o, lse, dmix, dims):
    T = q_att.shape[0]
    H = dims["H"]
    ob = dims["P"] // LANES
    t = _tile(T, TILES["att"])
    n = T // t
    scale = 1.0 / math.sqrt(NOPE_DIM + ROPE_DIM)

    def body(qn_ref, qr_ref, kn_ref, v_ref, kr_ref, o_ref, lse_ref, do_ref,
             dq_ref, dkn_ref, dv_ref, dkr_ref, dk_acc, dv_acc):
        j, i = pl.program_id(1), pl.program_id(2)

        @pl.when((j == 0) & (i == 0))
        def _():
            dq_ref[...] = jnp.zeros_like(dq_ref)

        @pl.when(i == 0)
        def _():
            dk_acc[...] = jnp.zeros_like(dk_acc)
            dv_acc[...] = jnp.zeros_like(dv_acc)

        @pl.when(i >= j)
        def _():
            q, k, s = _scores(qn_ref, qr_ref, kn_ref, kr_ref, i, j, t, scale)
            p = jnp.exp(s - lse_ref[:, :1])
            do = do_ref[...]
            delta = jnp.sum(do.astype(F32) * o_ref[...].astype(F32), axis=1, keepdims=True)
            dv_acc[...] += lax.dot_general(p.astype(BF16), do, _DOT_DIMS["tn"], preferred_element_type=F32)
            dp = lax.dot_general(do, v_ref[...], _DOT_DIMS["nt"], preferred_element_type=F32)
            ds = (p * (dp - delta) * scale).astype(BF16)
            dk_acc[...] += lax.dot_general(ds, q, _DOT_DIMS["tn"], preferred_element_type=F32)
            rows = pl.ds(pl.multiple_of(i * t, t), t)
            dq_ref[rows, :] += jnp.dot(ds, k, preferred_element_type=F32)

        @pl.when(i == n - 1)
        def _():
            dkn_ref[...] = dk_acc[:, :LANES].astype(BF16)
            dkr_ref[...] = dk_acc[:, LANES:]
            dv_ref[...] = dv_acc[...].astype(BF16)

    blk = (t, LANES)

    def qi(j, i):
        return jnp.maximum(i, j)

    return pl.pallas_call(
        body, name="flash_bwd", grid=(H, n, n),
        in_specs=[pl.BlockSpec(blk, lambda h, j, i: (qi(j, i), h)), pl.BlockSpec(blk, lambda h, j, i: (qi(j, i), H + h)),
                  pl.BlockSpec(blk, lambda h, j, i: (j, h)), pl.BlockSpec(blk, lambda h, j, i: (j, H + h)),
                  pl.BlockSpec(blk, lambda h, j, i: (j, 0)),
                  pl.BlockSpec(blk, lambda h, j, i: (qi(j, i), h)), pl.BlockSpec(blk, lambda h, j, i: (qi(j, i), h)),
                  pl.BlockSpec(blk, lambda h, j, i: (qi(j, i), ob + h))],
        out_specs=[pl.BlockSpec((None, T, 2 * LANES), lambda h, j, i: (h, 0, 0)),
                   pl.BlockSpec(blk, lambda h, j, i: (j, h)), pl.BlockSpec(blk, lambda h, j, i: (j, h)),
                   pl.BlockSpec((None, t, LANES), lambda h, j, i: (h, j, 0))],
        out_shape=[jax.ShapeDtypeStruct((H, T, 2 * LANES), F32), jax.ShapeDtypeStruct((T, H * LANES), BF16),
                   jax.ShapeDtypeStruct((T, H * LANES), BF16), jax.ShapeDtypeStruct((H, T, LANES), F32)],
        scratch_shapes=[pltpu.VMEM((t, 2 * LANES), F32), pltpu.VMEM(blk, F32)],
        compiler_params=_params(("parallel", "arbitrary", "arbitrary"),
                                12 * _nbytes((t, t), F32) + 2 * _nbytes((T, 2 * LANES), F32) + (8 << 20)),
    )(q_att, q_att, kv, kv, kr, o, lse, dmix)


def _dq_post(dq, tabs, dims):
    H, T, _ = dq.shape
    tt = _tile(T, TILES["row"])

    def body(dq_ref, c_ref, s1_ref, s2_ref, o_ref):
        c, s1, s2 = c_ref[...], s1_ref[...], s2_ref[...]
        for hh in range(H):
            o_ref[:, hh * LANES:(hh + 1) * LANES] = dq_ref[hh, :, :LANES].astype(BF16)
            o_ref[:, (H + hh) * LANES:(H + hh + 1) * LANES] = _rope_t(dq_ref[hh, :, LANES:], c, s1, s2).astype(BF16)

    tab = pl.BlockSpec((tt, LANES), lambda i: (i, 0))
    return pl.pallas_call(
        body, name="dq_post", grid=(T // tt,),
        in_specs=[pl.BlockSpec((H, tt, 2 * LANES), lambda i: (0, i, 0)), tab, tab, tab],
        out_specs=pl.BlockSpec((tt, 2 * H * LANES), lambda i: (i, 0)),
        out_shape=jax.ShapeDtypeStruct((T, 2 * H * LANES), BF16),
        compiler_params=_params(("parallel",), 8 * _nbytes((tt, 2 * H * LANES), F32)),
    )(dq, *tabs)


def _conv3(ge, cw, n):
    return cw[2:3] * ge + cw[1:2] * pltpu.roll(ge, 1, 0) + cw[0:1] * pltpu.roll(ge, 2, 0) + cw[3:4]


def _ffn_fwd(gate, up, cw8):
    T, F = gate.shape
    tt = _tile(T, TILES["row"])
    tc = _tile(F, TILES["ffn_c"], LANES)
    hb = tt // CONV_HALO

    def body(g_ref, gp_ref, u_ref, cw_ref, a_ref):
        it = pl.program_id(1)
        prev = jnp.where(it > 0, gp_ref[...].astype(F32), 0.0)
        ge = jnp.concatenate([prev, g_ref[...].astype(F32)], axis=0)
        gc = _conv3(ge, cw_ref[...], tt + CONV_HALO)[CONV_HALO:]
        a_ref[...] = (gc * _sigmoid(gc) * u_ref[...].astype(F32)).astype(BF16)

    blk = pl.BlockSpec((tt, tc), lambda jc, it: (it, jc))
    return pl.pallas_call(
        body, name="ffn_fwd", grid=(F // tc, T // tt),
        in_specs=[blk, pl.BlockSpec((CONV_HALO, tc), lambda jc, it: (jnp.maximum(it * hb - 1, 0), jc)), blk,
                  pl.BlockSpec((8, tc), lambda jc, it: (0, jc))],
        out_specs=blk, out_shape=jax.ShapeDtypeStruct((T, F), BF16),
        compiler_params=_params(("parallel", "parallel"), 16 * _nbytes((tt, tc), F32)),
    )(gate, gate, up, cw8)


def _ffn_bwd(da, gate, up, cw8):
    T, F = gate.shape
    tt = _tile(T, TILES["row"])
    tc = _tile(F, TILES["ffn_c"], LANES)
    hb = tt // CONV_HALO
    n_t = T // tt
    n = tt + 2 * CONV_HALO

    def body(da_ref, dan_ref, g_ref, gp_ref, gn_ref, u_ref, un_ref, cw_ref, dg_ref, du_ref, dcw_ref):
        it = pl.program_id(1)
        first, last = it == 0, it == n_t - 1
        cw = cw_ref[...]
        zeros = jnp.zeros((CONV_HALO, tc), F32)
        ge = jnp.concatenate([jnp.where(first, 0.0, gp_ref[...].astype(F32)), g_ref[...].astype(F32),
                              gn_ref[...].astype(F32)], axis=0)
        dae = jnp.concatenate([zeros, da_ref[...].astype(F32), jnp.where(last, 0.0, dan_ref[...].astype(F32))], axis=0)
        ue = jnp.concatenate([zeros, u_ref[...].astype(F32), un_ref[...].astype(F32)], axis=0)
        g1, g2 = pltpu.roll(ge, 1, 0), pltpu.roll(ge, 2, 0)
        gc = cw[2:3] * ge + cw[1:2] * g1 + cw[0:1] * g2 + cw[3:4]
        sg = _sigmoid(gc)
        dgc = dae * ue * (sg * (1.0 + gc * (1.0 - sg)))
        du_ref[...] = (dae * gc * sg)[CONV_HALO:CONV_HALO + tt].astype(BF16)
        dgp = cw[2:3] * dgc + cw[1:2] * pltpu.roll(dgc, n - 1, 0) + cw[0:1] * pltpu.roll(dgc, n - 2, 0)
        dg_ref[...] = dgp[CONV_HALO:CONV_HALO + tt].astype(BF16)
        mid = slice(CONV_HALO, CONV_HALO + tt)
        d_mid = dgc[mid]
        part = jnp.concatenate([jnp.sum(d_mid * g2[mid], axis=0, keepdims=True), jnp.sum(d_mid * g1[mid], axis=0, keepdims=True),
                                jnp.sum(d_mid * ge[mid], axis=0, keepdims=True), jnp.sum(d_mid, axis=0, keepdims=True),
                                jnp.zeros((4, tc), F32)], axis=0)

        @pl.when(first)
        def _():
            dcw_ref[...] = part

        @pl.when(it > 0)
        def _():
            dcw_ref[...] += part

    blk = pl.BlockSpec((tt, tc), lambda jc, it: (it, jc))
    prv = pl.BlockSpec((CONV_HALO, tc), lambda jc, it: (jnp.maximum(it * hb - 1, 0), jc))
    nxt = pl.BlockSpec((CONV_HALO, tc), lambda jc, it: (jnp.minimum((it + 1) * hb, T // CONV_HALO - 1), jc))
    cws = pl.BlockSpec((8, tc), lambda jc, it: (0, jc))
    return pl.pallas_call(
        body, name="ffn_bwd", grid=(F // tc, n_t), in_specs=[blk, nxt, blk, prv, nxt, blk, nxt, cws],
        out_specs=[blk, blk, cws],
        out_shape=[jax.ShapeDtypeStruct((T, F), BF16), jax.ShapeDtypeStruct((T, F), BF16), jax.ShapeDtypeStruct((8, F), F32)],
        compiler_params=_params(("parallel", "arbitrary"), 32 * _nbytes((tt, tc), F32)),
    )(da, da, gate, gate, gate, up, up, cw8)


def _ple_fwd(h2, gl, pe):
    T, D = h2.shape
    tt = _tile(T, TILES["row"])

    def body(h_ref, gl_ref, pe_ref, o_ref):
        o_ref[...] = h_ref[...] + pe_ref[...] * _sigmoid(gl_ref[...])

    row = pl.BlockSpec((tt, D), lambda i: (i, 0))
    return pl.pallas_call(
        body, name="ple_fwd", grid=(T // tt,), in_specs=[row, row, row], out_specs=row,
        out_shape=jax.ShapeDtypeStruct((T, D), F32), compiler_params=_params(("parallel",), 12 * _nbytes((tt, D), F32)),
    )(h2, gl, pe)


def _ple_bwd(dh, gl, pe):
    T, D = dh.shape
    tt = _tile(T, TILES["row"])

    def body(dh_ref, gl_ref, pe_ref, dpe_ref, dgl_ref):
        d = dh_ref[...]
        sg = _sigmoid(gl_ref[...])
        dpe_ref[...] = (d * sg).astype(BF16)
        dgl_ref[...] = (d * pe_ref[...] * (sg * (1.0 - sg))).astype(BF16)

    row = pl.BlockSpec((tt, D), lambda i: (i, 0))
    return pl.pallas_call(
        body, name="ple_bwd", grid=(T // tt,), in_specs=[row, row, row], out_specs=[row, row],
        out_shape=[jax.ShapeDtypeStruct((T, D), BF16), jax.ShapeDtypeStruct((T, D), BF16)],
        compiler_params=_params(("parallel",), 12 * _nbytes((tt, D), F32)),
    )(dh, gl, pe)


def _adamw(w, g, m, v, name):
    shape = w.shape
    cols = shape[-1]
    rows = math.prod(shape[:-1]) if len(shape) > 1 else 1
    w2, g2, m2, v2 = (a.reshape(rows, cols) for a in (w, g, m, v))
    tr = _tile(rows, max(8, (1 << 20) // (cols * 4)))
    c1 = 1.0 - ADAM_B1 ** ADAM_STEP
    c2 = 1.0 - ADAM_B2 ** ADAM_STEP

    def body(w_ref, g_ref, m_ref, v_ref, d_ref, mo_ref, vo_ref):
        gv = g_ref[...]
        mn = ADAM_B1 * m_ref[...] + (1.0 - ADAM_B1) * gv
        vn = ADAM_B2 * v_ref[...] + (1.0 - ADAM_B2) * (gv * gv)
        mo_ref[...] = mn
        vo_ref[...] = vn
        d_ref[...] = -ADAM_LR * ((mn / c1) / (jnp.sqrt(vn / c2) + ADAM_EPS) + ADAM_WD * w_ref[...])

    blk = pl.BlockSpec((tr, cols), lambda i: (i, 0))
    shp = jax.ShapeDtypeStruct((rows, cols), F32)
    outs = pl.pallas_call(
        body, name=name, grid=(rows // tr,), in_specs=[blk] * 4, out_specs=[blk] * 3, out_shape=[shp] * 3,
        compiler_params=_params(("parallel",), 16 * _nbytes((tr, cols), F32)),
    )(w2, g2, m2, v2)
    return tuple(o.reshape(shape) for o in outs)


_HBM = pl.BlockSpec(memory_space=pltpu.HBM)


def _place():
    x, y, c = lax.axis_index("x"), lax.axis_index("y"), lax.axis_index("c")
    return x, y, c, [(1 - x, y), (x, 1 - y), (1 - x, 1 - y)]


def _remote(src, dst, send_sems, recv_sems, k, to):
    return pltpu.make_async_remote_copy(src_ref=src, dst_ref=dst, send_sem=send_sems.at[k], recv_sem=recv_sems.at[k],
                                        device_id=to, device_id_type=MESH)


def _all_gather(flat):
    def body(x_ref, o_ref, send_sems, recv_sems, local_sem):
        x, y, c, chips = _place()
        me, sib = 2 * x + y, (x, y, 1 - c)
        mine = pltpu.make_async_copy(x_ref, o_ref.at[me], local_sem)
        mine.start()
        first = [_remote(x_ref.at[c], o_ref.at[me, c], send_sems, recv_sems, k, (cx, cy, c)) for k, (cx, cy) in enumerate(chips)]
        for cp in first:
            cp.start()
        passed = []
        for k, (cx, cy) in enumerate(chips):
            got = o_ref.at[2 * cx + cy, c]
            _remote(got, got, send_sems, recv_sems, k, (cx, cy, c)).wait_recv()
            fwd = _remote(got, got, send_sems, recv_sems, 3 + k, sib)
            fwd.start()
            passed.append(fwd)
        for k, (cx, cy) in enumerate(chips):
            got = o_ref.at[2 * cx + cy, 1 - c]
            _remote(got, got, send_sems, recv_sems, 3 + k, sib).wait_recv()
        for cp in first + passed:
            cp.wait_send()
        mine.wait()

    return pl.pallas_call(
        body, name="all_gather", in_specs=[_HBM], out_specs=_HBM,
        out_shape=jax.ShapeDtypeStruct((N_SHARDS,) + flat.shape, flat.dtype),
        scratch_shapes=[pltpu.SemaphoreType.DMA((6,)), pltpu.SemaphoreType.DMA((6,)), pltpu.SemaphoreType.DMA],
    )(flat)


def _sibling_exchange(g):
    def body(g_ref, o_ref, send_sems, recv_sems):
        x, y, c, _ = _place()
        sib = (x, y, 1 - c)
        sends = [_remote(g_ref.at[s, 1 - c], o_ref.at[s], send_sems, recv_sems, s, sib) for s in range(N_SHARDS)]
        for cp in sends:
            cp.start()
        for s in range(N_SHARDS):
            _remote(g_ref.at[s, c], o_ref.at[s], send_sems, recv_sems, s, sib).wait_recv()
        for cp in sends:
            cp.wait_send()

    return pl.pallas_call(
        body, name="rs_sibling", in_specs=[_HBM], out_specs=_HBM,
        out_shape=jax.ShapeDtypeStruct((N_SHARDS,) + g.shape[2:], g.dtype),
        scratch_shapes=[pltpu.SemaphoreType.DMA((N_SHARDS,)), pltpu.SemaphoreType.DMA((N_SHARDS,))],
    )(g)


def _chip_exchange(cs):
    def body(s_ref, o_ref, send_sems, recv_sems):
        x, y, c, chips = _place()
        sends = [_remote(s_ref.at[2 * cx + cy], o_ref.at[k], send_sems, recv_sems, k, (cx, cy, c)) for k, (cx, cy) in enumerate(chips)]
        for cp in sends:
            cp.start()
        for k in range(3):
            _remote(s_ref.at[0], o_ref.at[k], send_sems, recv_sems, k, (x, y, c)).wait_recv()
        for cp in sends:
            cp.wait_send()

    return pl.pallas_call(
        body, name="rs_chips", in_specs=[_HBM], out_specs=_HBM,
        out_shape=jax.ShapeDtypeStruct((3,) + cs.shape[1:], cs.dtype),
        scratch_shapes=[pltpu.SemaphoreType.DMA((3,)), pltpu.SemaphoreType.DMA((3,))],
    )(cs)


def _sibling_share(red):
    def body(r_ref, o_ref, send_sems, recv_sems, local_sem):
        x, y, c, _ = _place()
        sib = (x, y, 1 - c)
        mine = pltpu.make_async_copy(r_ref, o_ref.at[c], local_sem)
        mine.start()
        send = _remote(r_ref, o_ref.at[c], send_sems, recv_sems, 0, sib)
        send.start()
        _remote(r_ref, o_ref.at[1 - c], send_sems, recv_sems, 0, sib).wait_recv()
        send.wait_send()
        mine.wait()

    return pl.pallas_call(
        body, name="rs_share", in_specs=[_HBM], out_specs=_HBM,
        out_shape=jax.ShapeDtypeStruct((2,) + red.shape, red.dtype),
        scratch_shapes=[pltpu.SemaphoreType.DMA((1,)), pltpu.SemaphoreType.DMA((1,)), pltpu.SemaphoreType.DMA],
    )(red)


def _add_sibling(g, sib_in, c_idx):
    S, _, R, _ = g.shape
    tr = _tile(R, TILES["pack_rows"], SUBLANES_BF16)

    def body(c_ref, a_ref, b_ref, o_ref):
        o_ref[...] = (a_ref[...].astype(F32) + b_ref[...].astype(F32)).astype(o_ref.dtype)

    return pl.pallas_call(
        body, name="rs_add_sibling",
        grid_spec=pltpu.PrefetchScalarGridSpec(
            num_scalar_prefetch=1, grid=(S, R // tr),
            in_specs=[pl.BlockSpec((None, None, tr, LANES), lambda s, r, c: (s, c[0], r, 0)),
                      pl.BlockSpec((None, tr, LANES), lambda s, r, c: (s, r, 0))],
            out_specs=pl.BlockSpec((None, tr, LANES), lambda s, r, c: (s, r, 0))),
        out_shape=jax.ShapeDtypeStruct((S, R, LANES), g.dtype),
        compiler_params=_params(("parallel", "parallel"), 16 * _nbytes((tr, LANES), F32)),
    )(c_idx, g, sib_in)


def _add_chips(cs, got, me_idx):
    S, R, _ = cs.shape
    tr = _tile(R, TILES["pack_rows"], SUBLANES_BF16)

    def body(me_ref, a_ref, b_ref, o_ref):
        acc = a_ref[...].astype(F32)
        for k in range(3):
            acc = acc + b_ref[k].astype(F32)
        o_ref[...] = acc

    return pl.pallas_call(
        body, name="rs_add_chips",
        grid_spec=pltpu.PrefetchScalarGridSpec(
            num_scalar_prefetch=1, grid=(R // tr,),
            in_specs=[pl.BlockSpec((None, tr, LANES), lambda r, me: (me[0], r, 0)),
                      pl.BlockSpec((3, tr, LANES), lambda r, me: (0, r, 0))],
            out_specs=pl.BlockSpec((tr, LANES), lambda r, me: (r, 0))),
        out_shape=jax.ShapeDtypeStruct((R, LANES), F32),
        compiler_params=_params(("parallel",), 24 * _nbytes((tr, LANES), F32)),
    )(me_idx, cs, got)


def _reduce_scatter(g, c_idx, me_idx):
    chip_sum = _add_sibling(g, _sibling_exchange(g), c_idx)
    red = _add_chips(chip_sum, _chip_exchange(chip_sum), me_idx)
    return _sibling_share(red)


def _all_reduce_small(v):
    R = v.shape[0]

    def body(v_ref, o_ref, buf, send_sems, recv_sems):
        x, y, c, _ = _place()
        me = 4 * x + 2 * y + c
        buf[me] = v_ref[...]
        sends = []
        for k in range(1, 8):
            px = 1 - x if k & 4 else x
            py = 1 - y if k & 2 else y
            pc = 1 - c if k & 1 else c
            sends.append(_remote(v_ref, buf.at[me], send_sems, recv_sems, k - 1, (px, py, pc)))
        for cp in sends:
            cp.start()
        for k in range(1, 8):
            px = 1 - x if k & 4 else x
            py = 1 - y if k & 2 else y
            pc = 1 - c if k & 1 else c
            _remote(v_ref, buf.at[4 * px + 2 * py + pc], send_sems, recv_sems, k - 1, (px, py, pc)).wait_recv()
        for cp in sends:
            cp.wait_send()
        acc = buf[0]
        for d in range(1, 8):
            acc = acc + buf[d]
        o_ref[...] = acc

    vm = pl.BlockSpec(memory_space=pltpu.VMEM)
    return pl.pallas_call(
        body, name="all_reduce_small", in_specs=[vm], out_specs=vm, out_shape=jax.ShapeDtypeStruct(v.shape, F32),
        scratch_shapes=[pltpu.VMEM((8, R, LANES), F32), pltpu.SemaphoreType.DMA((7,)), pltpu.SemaphoreType.DMA((7,))],
    )(v)


def _pad_to(a, n):
    return a if a.shape[0] == n else jnp.pad(a, (0, n - a.shape[0]))


def _piece_len(shape):
    return -(-math.prod(shape) // PACK_ALIGN) * PACK_ALIGN


def _pack(pieces, dtype):
    flat = jnp.concatenate([_pad_to(a.reshape(-1).astype(dtype), _piece_len(a.shape)) for a in pieces])
    return flat.reshape(2, -1, LANES)


def _unpack(flat, shapes, lead):
    flat = flat.reshape(lead + (-1,))
    out, off = [], 0
    for shp in shapes:
        out.append(flat[..., off:off + math.prod(shp)].reshape(lead + tuple(shp)))
        off += _piece_len(shp)
    return out


def _join(name, a):
    if name in COL_SHARDED:
        return a.transpose(1, 0, 2).reshape(a.shape[1], -1)
    if name in ROW_SHARDED:
        return a.reshape(-1, a.shape[-1])
    return a.transpose(1, 0, 2, 3).reshape(a.shape[1], -1, a.shape[-1])


def _split(name, a):
    if name in COL_SHARDED:
        return a.reshape(a.shape[0], N_SHARDS, -1).transpose(1, 0, 2)
    if name in ROW_SHARDED:
        return a.reshape(N_SHARDS, -1, a.shape[-1])
    return a.reshape(a.shape[0], N_SHARDS, -1, a.shape[-1]).transpose(1, 0, 2, 3)


def _heads_split(w, H, first, second, pad_second):
    K = w.shape[0]
    w3 = w.reshape(K, H, first + second)
    b = w3[:, :, first:]
    if pad_second > second:
        b = jnp.pad(b, ((0, 0), (0, 0), (0, pad_second - second)))
    return jnp.concatenate([w3[:, :, :first].reshape(K, -1), b.reshape(K, -1)], axis=1)


def _heads_merge(w, H, first, second, pad_second):
    K = w.shape[0]
    a = w[:, :H * first].reshape(K, H, first)
    b = w[:, H * first:].reshape(K, H, pad_second)[:, :, :second]
    return jnp.concatenate([a, b], axis=2).reshape(K, -1)


def kernel(x, p, positions, norm_mix_g, w_in, pool_w, pool_scale, q_norm_g, w_uq, kv_norm_g, w_ukv, w_out, norm_ffn_g, w_up, conv_w, conv_b, w_down, norm_ple_g, w_ple, w_ple_gate, final_norm_g, loss_target, m_norm_mix_g, m_w_in, m_pool_w, m_pool_scale, m_q_norm_g, m_w_uq, m_kv_norm_g, m_w_ukv, m_w_out, m_norm_ffn_g, m_w_up, m_conv_w, m_conv_b, m_w_down, m_norm_ple_g, m_w_ple, m_w_ple_gate, m_final_norm_g, v_norm_mix_g, v_w_in, v_pool_w, v_pool_scale, v_q_norm_g, v_w_uq, v_kv_norm_g, v_w_ukv, v_w_out, v_norm_ffn_g, v_w_up, v_conv_w, v_conv_b, v_w_down, v_norm_ple_g, v_w_ple, v_w_ple_gate, v_final_norm_g):
    W = dict(norm_mix_g=norm_mix_g, w_in=w_in, pool_w=pool_w, pool_scale=pool_scale, q_norm_g=q_norm_g, w_uq=w_uq,
             kv_norm_g=kv_norm_g, w_ukv=w_ukv, w_out=w_out, norm_ffn_g=norm_ffn_g, w_up=w_up, conv_w=conv_w, conv_b=conv_b,
             w_down=w_down, norm_ple_g=norm_ple_g, w_ple=w_ple, w_ple_gate=w_ple_gate, final_norm_g=final_norm_g)
    M1 = dict(norm_mix_g=m_norm_mix_g, w_in=m_w_in, pool_w=m_pool_w, pool_scale=m_pool_scale, q_norm_g=m_q_norm_g, w_uq=m_w_uq,
              kv_norm_g=m_kv_norm_g, w_ukv=m_w_ukv, w_out=m_w_out, norm_ffn_g=m_norm_ffn_g, w_up=m_w_up, conv_w=m_conv_w,
              conv_b=m_conv_b, w_down=m_w_down, norm_ple_g=m_norm_ple_g, w_ple=m_w_ple, w_ple_gate=m_w_ple_gate,
              final_norm_g=m_final_norm_g)
    M2 = dict(norm_mix_g=v_norm_mix_g, w_in=v_w_in, pool_w=v_pool_w, pool_scale=v_pool_scale, q_norm_g=v_q_norm_g, w_uq=v_w_uq,
              kv_norm_g=v_kv_norm_g, w_ukv=v_w_ukv, w_out=v_w_out, norm_ffn_g=v_norm_ffn_g, w_up=v_w_up, conv_w=v_conv_w,
              conv_b=v_conv_b, w_down=v_w_down, norm_ple_g=v_norm_ple_g, w_ple=v_w_ple, w_ple_gate=v_w_ple_gate,
              final_norm_g=v_final_norm_g)

    _, T, D = x.shape
    L = p.shape[0]
    P, QL, KL, F = pool_scale.shape[-1], q_norm_g.shape[-1], kv_norm_g.shape[-1], conv_b.shape[-1]
    C = pool_w.shape[-1]
    H = (D - P) // V_DIM
    d_in = P + QL + KL + ROPE_DIM
    dims = dict(P=P, QL=QL, KL=KL, C=C, H=H)
    shard_shapes = [W[n].shape[1:] for n in SHARDED]

    xi, yi, ci = lax.axis_index("x"), lax.axis_index("y"), lax.axis_index("c")
    c_idx = jnp.reshape(ci, (1,)).astype(jnp.int32)
    me_idx = jnp.reshape(2 * xi + yi, (1,)).astype(jnp.int32)

    inv_freq = 1.0 / (ROPE_THETA ** (jnp.arange(0, ROPE_DIM, 2, dtype=F32) / ROPE_DIM))
    inv_lane = jnp.concatenate([inv_freq, inv_freq, jnp.zeros((LANES - ROPE_DIM,), F32)]).reshape(1, LANES)
    tabs = _rope_tables(positions.reshape(T, 1).astype(F32), inv_lane)

    def gathered(l):
        full = dict(zip(SHARDED, _unpack(_all_gather(_pack([W[n][l] for n in SHARDED], BF16)), shard_shapes, (N_SHARDS,))))
        full = {n: _join(n, a) for n, a in full.items()}
        full["w_in"] = jnp.pad(full["w_in"], ((0, 0), (0, LANES - ROPE_DIM)))
        full["w_uq"] = _heads_split(full["w_uq"], H, NOPE_DIM, ROPE_DIM, LANES)
        full["w_ukv"] = _heads_split(full["w_ukv"], H, NOPE_DIM, V_DIM, V_DIM)
        full["w_gate"], full["w_upp"] = full["w_up"][:, :F], full["w_up"][:, F:]
        full["cw8"] = jnp.concatenate([full["conv_w"].astype(F32), conv_b[l][None], jnp.zeros((4, F), F32)], axis=0)
        return full

    FW = [gathered(l) for l in range(L)]

    h = x[0]
    saved = []
    for l in range(L):
        fw = FW[l]
        s = dict(h0=h)
        s["n1"] = _rms_fwd(h, norm_mix_g[l], "norm_mix")
        s["u"] = _matmul(s["n1"], fw["w_in"], "nn", F32, "mm_in", tm=512, tn=d_in + LANES - ROPE_DIM)
        s["diff"], s["cqn"], s["ckvn"], s["kr"] = _post_u(s["u"], q_norm_g[l], kv_norm_g[l], tabs, dims)
        s["q"] = _q_rope(_matmul(s["cqn"], fw["w_uq"], "nn", F32, "mm_uq", tn=2 * H * LANES), tabs, dims)
        s["kv"] = _matmul(s["ckvn"], fw["w_ukv"], "nn", BF16, "mm_ukv", tn=2 * H * LANES)
        s["o"], s["lse"] = _flash_fwd(s["q"], s["kv"], s["kr"], dims)
        s["mix"] = jnp.concatenate([_pool_fwd(s["diff"], fw["pool_w"], pool_scale[l], dims), s["o"]], axis=1)
        s["h1"] = _matmul(s["mix"], fw["w_out"], "nn", F32, "mm_out", res=h, tm=512)
        s["n2"] = _rms_fwd(s["h1"], norm_ffn_g[l], "norm_ffn")
        s["gate"] = _matmul(s["n2"], fw["w_gate"], "nn", BF16, "mm_gate")
        s["up"] = _matmul(s["n2"], fw["w_upp"], "nn", BF16, "mm_up")
        s["a"] = _ffn_fwd(s["gate"], s["up"], fw["cw8"])
        s["h2"] = _matmul(s["a"], fw["w_down"], "nn", F32, "mm_down", res=s["h1"], tm=512, tk=F // 2 if F % (2 * LANES) == 0 else F)
        s["n3"] = _rms_fwd(s["h2"], norm_ple_g[l], "norm_ple")
        s["gl"] = _matmul(s["n3"], fw["w_ple_gate"], "nn", F32, "mm_ple_gate", tm=512)
        s["pe"] = _matmul(p[l, 0], fw["w_ple"], "nn", F32, "mm_ple", tn=D)
        h = _ple_fwd(s["h2"], s["gl"], s["pe"])
        saved.append(s)

    dh, dhb, dg_final, loss_part = _final_loss(h, loss_target[0], final_norm_g)
    loss = lax.psum(loss_part[0, 0], ("x", "y", "c"))

    small = {}
    reduced = [None] * L
    for l in reversed(range(L)):
        fw, s = FW[l], saved[l]
        gw = {}
        dpe, dgl = _ple_bwd(dh, s["gl"], s["pe"])
        gw["w_ple"] = _matmul(p[l, 0], dpe, "tn", BF16, "dw_ple", tm=512)
        gw["w_ple_gate"] = _matmul(s["n3"], dgl, "tn", BF16, "dw_ple_gate")
        dn3 = _matmul(dgl, fw["w_ple_gate"], "nt", F32, "dx_ple_gate", tm=512)
        dh, dhb, small["norm_ple_g", l] = _rms_bwd(dn3, s["h2"], norm_ple_g[l], dh, "norm_ple_bwd")

        da = _matmul(dhb, fw["w_down"], "nt", BF16, "dx_down", tn=F // 4 if F % (4 * LANES) == 0 else F)
        gw["w_down"] = _matmul(s["a"], dhb, "tn", BF16, "dw_down")
        dgate, dup, dcw = _ffn_bwd(da, s["gate"], s["up"], fw["cw8"])
        gw["conv_w"], small["conv_b", l] = dcw[:CONV_TAPS], dcw[CONV_TAPS:CONV_TAPS + 1]
        gw["w_up"] = jnp.concatenate([_matmul(s["n2"], dgate, "tn", BF16, "dw_gate"), _matmul(s["n2"], dup, "tn", BF16, "dw_up")], axis=1)
        dn2 = _matmul(dgate, fw["w_gate"], "nt", F32, "dx_gate", tm=512, tn=D, tk=1024)
        dn2 = _matmul(dup, fw["w_upp"], "nt", F32, "dx_up", res=dn2, tm=512, tn=D, tk=1024)
        dh, dhb, small["norm_ffn_g", l] = _rms_bwd(dn2, s["h1"], norm_ffn_g[l], dh, "norm_ffn_bwd")

        dmix = _matmul(dhb, fw["w_out"], "nt", BF16, "dx_out")
        gw["w_out"] = _matmul(s["mix"], dhb, "tn", BF16, "dw_out")
        ddiff, gw["pool_w"], small["pool_scale", l] = _pool_bwd(dmix, s["diff"], fw["pool_w"], pool_scale[l], dims)
        dq, dkn, dv, dkr = _flash_bwd(s["q"], s["kv"], s["kr"], s["o"], s["lse"], dmix, dims)
        dqb = _dq_post(dq, tabs, dims)
        dkv = jnp.concatenate([dkn, dv], axis=1)
        gw["w_uq"] = _heads_merge(_matmul(s["cqn"], dqb, "tn", BF16, "dw_uq", tn=2 * H * LANES), H, NOPE_DIM, ROPE_DIM, LANES)
        gw["w_ukv"] = _heads_merge(_matmul(s["ckvn"], dkv, "tn", BF16, "dw_ukv", tn=2 * H * LANES), H, NOPE_DIM, V_DIM, V_DIM)
        dcqn = _matmul(dqb, fw["w_uq"], "nt", F32, "dx_uq")
        dckvn = _matmul(dkv, fw["w_ukv"], "nt", F32, "dx_ukv")
        du, small["q_norm_g", l], small["kv_norm_g", l] = _pre_u_bwd(s["u"], dcqn, dckvn, ddiff, dkr, q_norm_g[l], kv_norm_g[l], tabs, dims)
        gw["w_in"] = _matmul(s["n1"], du, "tn", BF16, "dw_in", tm=512, tn=du.shape[1])[:, :d_in]
        dn1 = _matmul(du, fw["w_in"], "nt", F32, "dx_in", tm=512, tk=du.shape[1])
        dh, dhb, small["norm_mix_g", l] = _rms_bwd(dn1, s["h0"], norm_mix_g[l], dh, "norm_mix_bwd")

        packed = jnp.stack([_pack([_split(n, gw[n])[sh] for n in SHARDED], BF16) for sh in range(N_SHARDS)])
        reduced[l] = _reduce_scatter(packed, c_idx, me_idx)

    grads = {}
    per_layer = [_unpack(reduced[l], shard_shapes, ()) for l in range(L)]
    for k, n in enumerate(SHARDED):
        grads[n] = jnp.stack([per_layer[l][k] for l in range(L)])

    small_names = ("norm_mix_g", "pool_scale", "q_norm_g", "kv_norm_g", "norm_ffn_g", "conv_b", "norm_ple_g")
    parts = [small[n, l].reshape(-1) for n in small_names for l in range(L)] + [dg_final.reshape(-1)]
    sizes = [a.shape[0] for a in parts]
    total = sum(sizes)
    padded = -(-total // (8 * LANES)) * (8 * LANES)
    summed = _all_reduce_small(_pad_to(jnp.concatenate(parts), padded).reshape(-1, LANES)).reshape(-1)
    off = 0
    for n in small_names:
        rows = []
        for l in range(L):
            rows.append(summed[off:off + W[n].shape[-1]])
            off += W[n].shape[-1]
        grads[n] = jnp.stack(rows)
    grads["final_norm_g"] = summed[off:off + D]

    deltas, new_m, new_v = {}, {}, {}
    for n in WEIGHTS:
        deltas[n], new_m[n], new_v[n] = _adamw(W[n], grads[n], M1[n], M2[n], "adamw_" + n)

    return (loss, dh[None], *[grads[n] for n in WEIGHTS], *[deltas[n] for n in WEIGHTS],
            *[new_m[n] for n in WEIGHTS], *[new_v[n] for n in WEIGHTS])
```

```python
import functools
import math

import jax
import jax.numpy as jnp
from jax import lax
from jax.experimental import pallas as pl
from jax.experimental.pallas import tpu as pltpu

F32 = jnp.float32
BF16 = jnp.bfloat16

NOPE_DIM = 128
ROPE_DIM = 64
V_DIM = 128
LANES = 128
SUBLANES_BF16 = 16
ROPE_THETA = 10000.0
EPS = 1e-6
POOL_WINDOWS = (2, 4, 8, 16)
POOL_HALO = 16
CONV_TAPS = 3
CONV_HALO = 8
ADAM_LR = 0.001
ADAM_B1 = 0.9
ADAM_B2 = 0.999
ADAM_EPS = 1e-08
ADAM_WD = 0.01
ADAM_STEP = 10
NEG_BIG = -1e30
V7X_VMEM_BYTES = 64 * 2 ** 20
N_SHARDS = 4
PACK_ALIGN = 2 * SUBLANES_BF16 * LANES

TILES = dict(row=256, att=512, mm_m=1024, mm_n=1024, mm_k=2048, ffn_c=512, add_bytes=1 << 20)

BIG = ("w_in", "w_out", "w_up", "w_down", "w_ple_gate")
MISC = ("w_uq", "w_ukv", "w_ple", "pool_w")
COL_SHARDED = ("w_in", "w_uq", "w_ukv", "w_up", "conv_w", "w_ple")
ROW_SHARDED = ("w_out", "w_down", "w_ple_gate")
WEIGHTS = ("norm_mix_g", "w_in", "pool_w", "pool_scale", "q_norm_g", "w_uq", "kv_norm_g", "w_ukv", "w_out",
           "norm_ffn_g", "w_up", "conv_w", "conv_b", "w_down", "norm_ple_g", "w_ple", "w_ple_gate", "final_norm_g")
MESH = pl.DeviceIdType.MESH


def _nbytes(shape, dtype):
    return math.prod(shape) * jnp.dtype(dtype).itemsize


def _params(sem, need_bytes):
    limit = min(V7X_VMEM_BYTES - (8 << 20), max(32 << 20, int(need_bytes)))
    return pltpu.CompilerParams(dimension_semantics=sem, vmem_limit_bytes=limit)


def _tile(n, want, mult=8):
    if n <= want:
        return n
    for t in range(want - want % mult, 0, -mult):
        if n % t == 0:
            return t
    return n


def _sigmoid(x):
    return 1.0 / (1.0 + jnp.exp(-x))


def _rstd(x):
    return lax.rsqrt(jnp.mean(x * x, axis=-1, keepdims=True) + EPS)


_DOT_DIMS = {"nn": (((1,), (0,)), ((), ())), "nt": (((1,), (1,)), ((), ())), "tn": (((0,), (0,)), ((), ()))}


def _matmul(a, b, mode, out_dtype, name, res=None, tm=None, tn=None, tk=None, b_shards=None, out_shards=None, carry=None):
    if mode == "nn":
        (M, K), N = a.shape, b.shape[-1] * (b_shards[1] if b_shards else 1)
    elif mode == "nt":
        (M, K), N = a.shape, b.shape[-2]
    else:
        (K, M), N = a.shape, b.shape[1]
    per = b.shape[-1] if b_shards else (out_shards[2] if out_shards else None)
    tm = _tile(M, tm or TILES["mm_m"], LANES)
    tn = _tile(per if (per and mode != "nt") else N, tn or TILES["mm_n"], LANES)
    tk = _tile(per if (per and mode == "nt") else K, tk or TILES["mm_k"], LANES)
    nk = K // tk
    has_res = res is not None
    has_carry = carry is not None
    dims = _DOT_DIMS[mode]

    def body(*refs):
        a_ref, b_ref = refs[0], refs[1]
        o_ref = refs[2 + has_res + has_carry]
        part = lax.dot_general(a_ref[...].astype(BF16), b_ref[...].astype(BF16), dims, preferred_element_type=F32)

        def finish(acc):
            if has_res:
                acc = acc + refs[2][...]
            o_ref[...] = acc.astype(o_ref.dtype)

        if nk == 1:
            finish(part)
        else:
            acc_ref = refs[3 + has_res + has_carry]
            k = pl.program_id(2)

            @pl.when(k == 0)
            def _():
                acc_ref[...] = part

            @pl.when(k > 0)
            def _():
                acc_ref[...] += part

            @pl.when(k == nk - 1)
            def _():
                finish(acc_ref[...])

    if mode == "nn":
        a_spec, b_spec = pl.BlockSpec((tm, tk), lambda i, j, k: (i, k)), pl.BlockSpec((tk, tn), lambda i, j, k: (k, j))
    elif mode == "nt":
        a_spec, b_spec = pl.BlockSpec((tm, tk), lambda i, j, k: (i, k)), pl.BlockSpec((tn, tk), lambda i, j, k: (j, k))
    else:
        a_spec, b_spec = pl.BlockSpec((tk, tm), lambda i, j, k: (k, i)), pl.BlockSpec((tk, tn), lambda i, j, k: (k, j))
    o_spec = pl.BlockSpec((tm, tn), lambda i, j, k: (i, j))
    out_shape = jax.ShapeDtypeStruct((M, N), out_dtype)
    if b_shards:
        first = b_shards[0]
        if mode == "nn":
            nps = per // tn
            b_spec = pl.BlockSpec((None, tk, tn), lambda i, j, k: (first + j // nps, k, j % nps))
        else:
            kps = per // tk
            b_spec = pl.BlockSpec((None, tn, tk), lambda i, j, k: (first + k // kps, j, k % kps))
    if out_shards:
        ofirst, nps_o = out_shards[0], per // tn
        o_spec_out = pl.BlockSpec((None, tm, tn), lambda i, j, k: (ofirst + j // nps_o, i, j % nps_o))
        out_shape = jax.ShapeDtypeStruct((out_shards[1], M, per), out_dtype)
    else:
        o_spec_out = o_spec
    in_specs, args = [a_spec, b_spec], [a, b]
    need = 2 * (_nbytes((tm, tk), a.dtype) + _nbytes((tk, tn), b.dtype) + _nbytes((tm, tn), out_dtype)) + 2 * _nbytes((tm, tn), F32)
    if has_res:
        in_specs.append(o_spec)
        args.append(res)
        need += 2 * _nbytes((tm, tn), res.dtype)
    aliases = {}
    if has_carry:
        aliases = {len(args): 0}
        in_specs.append(pl.BlockSpec(memory_space=pl.ANY))
        args.append(carry)
    scratch = [pltpu.VMEM((tm, tn), F32)] if nk > 1 else []
    return pl.pallas_call(
        body, name=name, grid=(M // tm, N // tn, nk), in_specs=in_specs, out_specs=o_spec_out,
        out_shape=out_shape, scratch_shapes=scratch, input_output_aliases=aliases,
        compiler_params=_params(("parallel", "parallel", "arbitrary"), need + (4 << 20)),
    )(*args)


def _rms_fwd(x, g, name):
    T, D = x.shape
    tt = _tile(T, TILES["row"])

    def body(x_ref, g_ref, o_ref):
        xv = x_ref[...]
        o_ref[...] = (xv * _rstd(xv) * g_ref[...]).astype(o_ref.dtype)

    row = pl.BlockSpec((tt, D), lambda i: (i, 0))
    return pl.pallas_call(
        body, name=name, grid=(T // tt,), in_specs=[row, pl.BlockSpec((1, D), lambda i: (0, 0))], out_specs=row,
        out_shape=jax.ShapeDtypeStruct((T, D), BF16), compiler_params=_params(("parallel",), 8 * _nbytes((tt, D), F32)),
    )(x, g.reshape(1, D))


def _rms_bwd(dn, x, g, dres, name):
    T, D = x.shape
    tt = _tile(T, TILES["row"])

    def body(dn_ref, x_ref, g_ref, dres_ref, dx_ref, dxb_ref, dg_ref):
        i = pl.program_id(0)
        xv = x_ref[...]
        r = _rstd(xv)
        xh = xv * r
        dnv = dn_ref[...].astype(F32)
        dxh = dnv * g_ref[...]
        tot = dres_ref[...] + r * (dxh - xh * jnp.mean(dxh * xh, axis=-1, keepdims=True))
        dx_ref[...] = tot
        dxb_ref[...] = tot.astype(BF16)
        part = jnp.sum(dnv * xh, axis=0, keepdims=True)

        @pl.when(i == 0)
        def _():
            dg_ref[...] = part

        @pl.when(i > 0)
        def _():
            dg_ref[...] += part

    row = pl.BlockSpec((tt, D), lambda i: (i, 0))
    vec = pl.BlockSpec((1, D), lambda i: (0, 0))
    return pl.pallas_call(
        body, name=name, grid=(T // tt,), in_specs=[row, row, vec, row], out_specs=[row, row, vec],
        out_shape=[jax.ShapeDtypeStruct((T, D), F32), jax.ShapeDtypeStruct((T, D), BF16), jax.ShapeDtypeStruct((1, D), F32)],
        compiler_params=_params(("arbitrary",), 16 * _nbytes((tt, D), F32)),
    )(dn, x, g.reshape(1, D), dres)


def _final_loss(h, target, g):
    T, D = h.shape
    tt = _tile(T, TILES["row"])

    def body(h_ref, t_ref, g_ref, dx_ref, dxb_ref, dg_ref, loss_ref):
        i = pl.program_id(0)
        xv = h_ref[...]
        r = _rstd(xv)
        xh = xv * r
        gv = g_ref[...]
        err = xh * gv - t_ref[...]
        lpart = 0.5 * jnp.sum(jnp.mean(err * err, axis=-1, keepdims=True), axis=0, keepdims=True)
        dy = err * (1.0 / D)
        dxh = dy * gv
        dx = r * (dxh - xh * jnp.mean(dxh * xh, axis=-1, keepdims=True))
        dx_ref[...] = dx
        dxb_ref[...] = dx.astype(BF16)
        gpart = jnp.sum(dy * xh, axis=0, keepdims=True)
        lrow = jnp.broadcast_to(lpart, (1, LANES))

        @pl.when(i == 0)
        def _():
            dg_ref[...] = gpart
            loss_ref[...] = lrow

        @pl.when(i > 0)
        def _():
            dg_ref[...] += gpart
            loss_ref[...] += lrow

    row = pl.BlockSpec((tt, D), lambda i: (i, 0))
    vec = pl.BlockSpec((1, D), lambda i: (0, 0))
    return pl.pallas_call(
        body, name="final_loss", grid=(T // tt,), in_specs=[row, row, vec],
        out_specs=[row, row, vec, pl.BlockSpec((1, LANES), lambda i: (0, 0))],
        out_shape=[jax.ShapeDtypeStruct((T, D), F32), jax.ShapeDtypeStruct((T, D), BF16),
                   jax.ShapeDtypeStruct((1, D), F32), jax.ShapeDtypeStruct((1, LANES), F32)],
        compiler_params=_params(("arbitrary",), 16 * _nbytes((tt, D), F32)),
    )(h, target, g.reshape(1, D))


def _rope_tables(pos_col, inv_lane):
    T = pos_col.shape[0]
    tt = _tile(T, TILES["row"])

    def body(p_ref, f_ref, c_ref, s1_ref, s2_ref):
        ang = p_ref[...] * f_ref[...]
        lane = lax.broadcasted_iota(jnp.int32, ang.shape, 1)
        half = ROPE_DIM // 2
        cs, sn = jnp.cos(ang), jnp.sin(ang)
        c_ref[...] = jnp.where(lane < ROPE_DIM, cs, 0.0)
        s1_ref[...] = jnp.where(lane < half, -sn, 0.0)
        s2_ref[...] = jnp.where((lane >= half) & (lane < ROPE_DIM), sn, 0.0)

    tab = pl.BlockSpec((tt, LANES), lambda i: (i, 0))
    shp = jax.ShapeDtypeStruct((T, LANES), F32)
    return pl.pallas_call(
        body, name="rope_tables", grid=(T // tt,),
        in_specs=[pl.BlockSpec((tt, 1), lambda i: (i, 0)), pl.BlockSpec((1, LANES), lambda i: (0, 0))],
        out_specs=[tab, tab, tab], out_shape=[shp, shp, shp],
        compiler_params=_params(("parallel",), 32 * _nbytes((tt, LANES), F32)),
    )(pos_col, inv_lane)


def _rope(x, c, s1, s2):
    return x * c + pltpu.roll(x, LANES - ROPE_DIM // 2, 1) * s1 + pltpu.roll(x, ROPE_DIM // 2, 1) * s2


def _rope_t(d, c, s1, s2):
    return d * c + pltpu.roll(d * s1, ROPE_DIM // 2, 1) + pltpu.roll(d * s2, LANES - ROPE_DIM // 2, 1)


def _window_sum(xe, w, forward):
    n = xe.shape[0]
    s, sh = xe, 1
    while sh < w:
        s = s + pltpu.roll(s, (n - sh) if forward else sh, 0)
        sh *= 2
    return s


def _post_u(u, gq, gkv, tabs, dims):
    T, Dp = u.shape
    P, QL, KL, C = dims["P"], dims["QL"], dims["KL"], dims["C"]
    tt = _tile(T, TILES["row"], POOL_HALO)
    hb = tt // POOL_HALO

    def body(u_ref, halo_ref, gq_ref, gkv_ref, c_ref, s1_ref, s2_ref, diff_ref, cq_ref, ckv_ref, kr_ref):
        i = pl.program_id(0)
        t = i * tt + lax.broadcasted_iota(jnp.int32, (tt, 1), 0)
        halo = jnp.where(i > 0, halo_ref[...], 0.0)
        for gi, w in enumerate(POOL_WINDOWS):
            cols = slice(gi * C, (gi + 1) * C)
            xg = u_ref[:, cols]
            s = _window_sum(jnp.concatenate([halo[:, cols], xg], axis=0), w, False)[POOL_HALO:]
            cnt = jnp.minimum(t + 1, w).astype(F32)
            diff_ref[:, cols] = (s / cnt - xg).astype(BF16)
        cq = u_ref[:, P:P + QL]
        cq_ref[...] = (cq * _rstd(cq) * gq_ref[...]).astype(BF16)
        ckv = u_ref[:, P + QL:P + QL + KL]
        ckv_ref[...] = (ckv * _rstd(ckv) * gkv_ref[...]).astype(BF16)
        kr_ref[...] = _rope(u_ref[:, P + QL + KL:], c_ref[...], s1_ref[...], s2_ref[...]).astype(BF16)

    def row(w):
        return pl.BlockSpec((tt, w), lambda i: (i, 0))

    def vec(w):
        return pl.BlockSpec((1, w), lambda i: (0, 0))

    return pl.pallas_call(
        body, name="post_u", grid=(T // tt,),
        in_specs=[row(Dp), pl.BlockSpec((POOL_HALO, P), lambda i: (jnp.maximum(i * hb - 1, 0), 0)),
                  vec(QL), vec(KL), row(LANES), row(LANES), row(LANES)],
        out_specs=[row(P), row(QL), row(KL), row(LANES)],
        out_shape=[jax.ShapeDtypeStruct((T, P), BF16), jax.ShapeDtypeStruct((T, QL), BF16),
                   jax.ShapeDtypeStruct((T, KL), BF16), jax.ShapeDtypeStruct((T, LANES), BF16)],
        compiler_params=_params(("parallel",), 10 * _nbytes((tt, Dp), F32)),
    )(u, u, gq.reshape(1, QL), gkv.reshape(1, KL), *tabs)


def _pre_u_bwd(u, d_cqn, d_ckvn, d_diff, dkr, gq, gkv, tabs, dims):
    T, Dp = u.shape
    P, QL, KL, C, H = dims["P"], dims["QL"], dims["KL"], dims["C"], dims["H"]
    tt = _tile(T, TILES["row"], POOL_HALO)
    hb = tt // POOL_HALO
    n_t = T // tt

    def norm_bwd(xv, dn, gv):
        r = _rstd(xv)
        xh = xv * r
        dxh = dn * gv
        return r * (dxh - xh * jnp.mean(dxh * xh, axis=-1, keepdims=True)), jnp.sum(dn * xh, axis=0, keepdims=True)

    def body(u_ref, dcq_ref, dckv_ref, dd_ref, ddn_ref, dkr_ref, gq_ref, gkv_ref, c_ref, s1_ref, s2_ref,
             du_ref, dgq_ref, dgkv_ref):
        i = pl.program_id(0)
        t = i * tt + lax.broadcasted_iota(jnp.int32, (tt, 1), 0)
        nxt = jnp.where(i < n_t - 1, ddn_ref[...].astype(F32), 0.0)
        for gi, w in enumerate(POOL_WINDOWS):
            cols = slice(gi * C, (gi + 1) * C)
            dd = dd_ref[:, cols].astype(F32)
            e = dd / jnp.minimum(t + 1, w).astype(F32)
            s = _window_sum(jnp.concatenate([e, nxt[:, cols] / float(w)], axis=0), w, True)[:tt]
            du_ref[:, cols] = (s - dd).astype(BF16)
        dq, pq = norm_bwd(u_ref[:, P:P + QL], dcq_ref[...], gq_ref[...])
        du_ref[:, P:P + QL] = dq.astype(BF16)
        dkv, pkv = norm_bwd(u_ref[:, P + QL:P + QL + KL], dckv_ref[...], gkv_ref[...])
        du_ref[:, P + QL:P + QL + KL] = dkv.astype(BF16)
        dk = dkr_ref[0]
        for hh in range(1, H):
            dk = dk + dkr_ref[hh]
        du_ref[:, P + QL + KL:] = _rope_t(dk, c_ref[...], s1_ref[...], s2_ref[...]).astype(BF16)

        @pl.when(i == 0)
        def _():
            dgq_ref[...] = pq
            dgkv_ref[...] = pkv

        @pl.when(i > 0)
        def _():
            dgq_ref[...] += pq
            dgkv_ref[...] += pkv

    def row(w):
        return pl.BlockSpec((tt, w), lambda i: (i, 0))

    def vec(w):
        return pl.BlockSpec((1, w), lambda i: (0, 0))

    return pl.pallas_call(
        body, name="pre_u_bwd", grid=(n_t,),
        in_specs=[row(Dp), row(QL), row(KL), row(P),
                  pl.BlockSpec((POOL_HALO, P), lambda i: (jnp.minimum((i + 1) * hb, T // POOL_HALO - 1), 0)),
                  pl.BlockSpec((H, tt, LANES), lambda i: (0, i, 0)), vec(QL), vec(KL), row(LANES), row(LANES), row(LANES)],
        out_specs=[row(Dp), vec(QL), vec(KL)],
        out_shape=[jax.ShapeDtypeStruct((T, Dp), BF16), jax.ShapeDtypeStruct((1, QL), F32), jax.ShapeDtypeStruct((1, KL), F32)],
        compiler_params=_params(("arbitrary",), 12 * _nbytes((tt, Dp), F32)),
    )(u, d_cqn, d_ckvn, d_diff, d_diff, dkr, gq.reshape(1, QL), gkv.reshape(1, KL), *tabs)


def _pool_fwd(diff, pw, ps, dims):
    T, P = diff.shape
    G, C = len(POOL_WINDOWS), dims["C"]
    tt = _tile(T, TILES["row"])

    def body(d_ref, w_ref, s_ref, o_ref):
        for gi in range(G):
            cols = slice(gi * C, (gi + 1) * C)
            y = jnp.dot(d_ref[:, cols], w_ref[gi], preferred_element_type=F32)
            o_ref[:, cols] = (y * s_ref[:, cols]).astype(BF16)

    row = pl.BlockSpec((tt, P), lambda i: (i, 0))
    return pl.pallas_call(
        body, name="pool_fwd", grid=(T // tt,),
        in_specs=[row, pl.BlockSpec((G, C, C), lambda i: (0, 0, 0)), pl.BlockSpec((1, P), lambda i: (0, 0))],
        out_specs=row, out_shape=jax.ShapeDtypeStruct((T, P), BF16),
        compiler_params=_params(("parallel",), 8 * _nbytes((tt, P), F32)),
    )(diff, pw, ps.reshape(1, P))


def _pool_bwd(dmix, diff, pw, ps, dims):
    T, P = diff.shape
    G, C = len(POOL_WINDOWS), dims["C"]
    tt = _tile(T, TILES["row"])

    def body(dy_ref, d_ref, w_ref, s_ref, dd_ref, dw_ref, ds_ref):
        i = pl.program_id(0)

        @pl.when(i == 0)
        def _():
            dw_ref[...] = jnp.zeros_like(dw_ref)
            ds_ref[...] = jnp.zeros_like(ds_ref)

        for gi in range(G):
            cols = slice(gi * C, (gi + 1) * C)
            dy = dy_ref[:, cols].astype(F32)
            d = d_ref[:, cols]
            w = w_ref[gi]
            ypre = jnp.dot(d, w, preferred_element_type=F32)
            ds_ref[:, cols] += jnp.sum(dy * ypre, axis=0, keepdims=True)
            dyp = (dy * s_ref[:, cols]).astype(BF16)
            dd_ref[:, cols] = lax.dot_general(dyp, w, _DOT_DIMS["nt"], preferred_element_type=F32).astype(BF16)
            dw_ref[gi] += lax.dot_general(d, dyp, _DOT_DIMS["tn"], preferred_element_type=F32)

    row = pl.BlockSpec((tt, P), lambda i: (i, 0))
    wsp = pl.BlockSpec((G, C, C), lambda i: (0, 0, 0))
    vec = pl.BlockSpec((1, P), lambda i: (0, 0))
    return pl.pallas_call(
        body, name="pool_bwd", grid=(T // tt,), in_specs=[row, row, wsp, vec], out_specs=[row, wsp, vec],
        out_shape=[jax.ShapeDtypeStruct((T, P), BF16), jax.ShapeDtypeStruct((G, C, C), F32), jax.ShapeDtypeStruct((1, P), F32)],
        compiler_params=_params(("arbitrary",), 10 * _nbytes((tt, P), F32)),
    )(dmix, diff, pw, ps.reshape(1, P))


def _q_rope(qp, tabs, dims):
    T, W = qp.shape
    H = dims["H"]
    tt = _tile(T, TILES["row"])

    def body(q_ref, c_ref, s1_ref, s2_ref, o_ref):
        o_ref[:, :H * LANES] = q_ref[:, :H * LANES].astype(BF16)
        c, s1, s2 = c_ref[...], s1_ref[...], s2_ref[...]
        for hh in range(H, 2 * H):
            cols = slice(hh * LANES, (hh + 1) * LANES)
            o_ref[:, cols] = _rope(q_ref[:, cols], c, s1, s2).astype(BF16)

    row = pl.BlockSpec((tt, W), lambda i: (i, 0))
    tab = pl.BlockSpec((tt, LANES), lambda i: (i, 0))
    return pl.pallas_call(
        body, name="q_rope", grid=(T // tt,), in_specs=[row, tab, tab, tab], out_specs=row,
        out_shape=jax.ShapeDtypeStruct((T, W), BF16), compiler_params=_params(("parallel",), 8 * _nbytes((tt, W), F32)),
    )(qp, *tabs)


def _scores(qn_ref, qr_ref, kn_ref, kr_ref, t, scale, diagonal):
    q = jnp.concatenate([qn_ref[...], qr_ref[...]], axis=1)
    k = jnp.concatenate([kn_ref[...], kr_ref[...]], axis=1)
    s = lax.dot_general(q, k, _DOT_DIMS["nt"], preferred_element_type=F32) * scale
    if diagonal:
        s = jnp.where(lax.broadcasted_iota(jnp.int32, (t, t), 0) >= lax.broadcasted_iota(jnp.int32, (t, t), 1), s, NEG_BIG)
    return q, k, s


def _flash_fwd(q_att, kv, kr, dims):
    T = q_att.shape[0]
    H = dims["H"]
    t = _tile(T, TILES["att"])
    n = T // t
    scale = 1.0 / math.sqrt(NOPE_DIM + ROPE_DIM)

    def body(qn_ref, qr_ref, kn_ref, v_ref, kr_ref, o_ref, lse_ref, m_ref, l_ref, acc_ref):
        i, j = pl.program_id(1), pl.program_id(2)

        @pl.when(j == 0)
        def _():
            m_ref[...] = jnp.full_like(m_ref, NEG_BIG)
            l_ref[...] = jnp.zeros_like(l_ref)
            acc_ref[...] = jnp.zeros_like(acc_ref)

        def step(diagonal):
            _, _, s = _scores(qn_ref, qr_ref, kn_ref, kr_ref, t, scale, diagonal)
            m_prev = m_ref[...]
            m_new = jnp.maximum(m_prev, jnp.max(s, axis=1, keepdims=True))
            alpha = jnp.exp(m_prev - m_new)
            p = jnp.exp(s - m_new[:, :1])
            l_ref[...] = alpha * l_ref[...] + jnp.sum(p, axis=1, keepdims=True)
            acc_ref[...] = alpha * acc_ref[...] + jnp.dot(p.astype(BF16), v_ref[...], preferred_element_type=F32)
            m_ref[...] = m_new

        @pl.when(j < i)
        def _():
            step(False)

        @pl.when(j == i)
        def _():
            step(True)
            o_ref[...] = (acc_ref[...] / l_ref[...]).astype(BF16)
            lse_ref[...] = m_ref[...] + jnp.log(l_ref[...])

    blk = (t, LANES)
    return pl.pallas_call(
        body, name="flash_fwd", grid=(H, n, n),
        in_specs=[pl.BlockSpec(blk, lambda h, i, j: (i, h)), pl.BlockSpec(blk, lambda h, i, j: (i, H + h)),
                  pl.BlockSpec(blk, lambda h, i, j: (jnp.minimum(j, i), h)),
                  pl.BlockSpec(blk, lambda h, i, j: (jnp.minimum(j, i), H + h)),
                  pl.BlockSpec(blk, lambda h, i, j: (jnp.minimum(j, i), 0))],
        out_specs=[pl.BlockSpec(blk, lambda h, i, j: (i, h)), pl.BlockSpec(blk, lambda h, i, j: (i, h))],
        out_shape=[jax.ShapeDtypeStruct((T, H * LANES), BF16), jax.ShapeDtypeStruct((T, H * LANES), F32)],
        scratch_shapes=[pltpu.VMEM(blk, F32), pltpu.VMEM(blk, F32), pltpu.VMEM(blk, F32)],
        compiler_params=_params(("parallel", "parallel", "arbitrary"), 8 * _nbytes((t, t), F32) + (8 << 20)),
    )(q_att, q_att, kv, kv, kr)


def _flash_bwd(q_att, kv, kr, o, lse, dmix, dims):
    T = q_att.shape[0]
    H = dims["H"]
    ob = dims["P"] // LANES
    t = _tile(T, TILES["att"])
    n = T // t
    scale = 1.0 / math.sqrt(NOPE_DIM + ROPE_DIM)

    def body(qn_ref, qr_ref, kn_ref, v_ref, kr_ref, o_ref, lse_ref, do_ref,
             dq_ref, dkn_ref, dv_ref, dkr_ref, dk_acc, dv_acc):
        j, i = pl.program_id(1), pl.program_id(2)

        @pl.when((j == 0) & (i == 0))
        def _():
            dq_ref[...] = jnp.zeros_like(dq_ref)

        @pl.when(i == 0)
        def _():
            dk_acc[...] = jnp.zeros_like(dk_acc)
            dv_acc[...] = jnp.zeros_like(dv_acc)

        def step(diagonal):
            q, k, s = _scores(qn_ref, qr_ref, kn_ref, kr_ref, t, scale, diagonal)
            p = jnp.exp(s - lse_ref[:, :1])
            do = do_ref[...]
            delta = jnp.sum(do.astype(F32) * o_ref[...].astype(F32), axis=1, keepdims=True)
            dv_acc[...] += lax.dot_general(p.astype(BF16), do, _DOT_DIMS["tn"], preferred_element_type=F32)
            dp = lax.dot_general(do, v_ref[...], _DOT_DIMS["nt"], preferred_element_type=F32)
            ds = (p * (dp - delta) * scale).astype(BF16)
            dk_acc[...] += lax.dot_general(ds, q, _DOT_DIMS["tn"], preferred_element_type=F32)
            rows = pl.ds(pl.multiple_of(i * t, t), t)
            dq_ref[rows, :] += jnp.dot(ds, k, preferred_element_type=F32)

        @pl.when(i > j)
        def _():
            step(False)

        @pl.when(i == j)
        def _():
            step(True)

        @pl.when(i == n - 1)
        def _():
            dkn_ref[...] = dk_acc[:, :LANES].astype(BF16)
            dkr_ref[...] = dk_acc[:, LANES:]
            dv_ref[...] = dv_acc[...].astype(BF16)

    blk = (t, LANES)

    def qi(j, i):
        return jnp.maximum(i, j)

    return pl.pallas_call(
        body, name="flash_bwd", grid=(H, n, n),
        in_specs=[pl.BlockSpec(blk, lambda h, j, i: (qi(j, i), h)), pl.BlockSpec(blk, lambda h, j, i: (qi(j, i), H + h)),
                  pl.BlockSpec(blk, lambda h, j, i: (j, h)), pl.BlockSpec(blk, lambda h, j, i: (j, H + h)),
                  pl.BlockSpec(blk, lambda h, j, i: (j, 0)),
                  pl.BlockSpec(blk, lambda h, j, i: (qi(j, i), h)), pl.BlockSpec(blk, lambda h, j, i: (qi(j, i), h)),
                  pl.BlockSpec(blk, lambda h, j, i: (qi(j, i), ob + h))],
        out_specs=[pl.BlockSpec((None, T, 2 * LANES), lambda h, j, i: (h, 0, 0)),
                   pl.BlockSpec(blk, lambda h, j, i: (j, h)), pl.BlockSpec(blk, lambda h, j, i: (j, h)),
                   pl.BlockSpec((None, t, LANES), lambda h, j, i: (h, j, 0))],
        out_shape=[jax.ShapeDtypeStruct((H, T, 2 * LANES), F32), jax.ShapeDtypeStruct((T, H * LANES), BF16),
                   jax.ShapeDtypeStruct((T, H * LANES), BF16), jax.ShapeDtypeStruct((H, T, LANES), F32)],
        scratch_shapes=[pltpu.VMEM((t, 2 * LANES), F32), pltpu.VMEM(blk, F32)],
        compiler_params=_params(("parallel", "arbitrary", "arbitrary"),
                                12 * _nbytes((t, t), F32) + 2 * _nbytes((T, 2 * LANES), F32) + (8 << 20)),
    )(q_att, q_att, kv, kv, kr, o, lse, dmix)


def _dq_post(dq, tabs, dims):
    H, T, _ = dq.shape
    tt = _tile(T, TILES["row"])

    def body(dq_ref, c_ref, s1_ref, s2_ref, o_ref):
        c, s1, s2 = c_ref[...], s1_ref[...], s2_ref[...]
        for hh in range(H):
            o_ref[:, hh * LANES:(hh + 1) * LANES] = dq_ref[hh, :, :LANES].astype(BF16)
            o_ref[:, (H + hh) * LANES:(H + hh + 1) * LANES] = _rope_t(dq_ref[hh, :, LANES:], c, s1, s2).astype(BF16)

    tab = pl.BlockSpec((tt, LANES), lambda i: (i, 0))
    return pl.pallas_call(
        body, name="dq_post", grid=(T // tt,),
        in_specs=[pl.BlockSpec((H, tt, 2 * LANES), lambda i: (0, i, 0)), tab, tab, tab],
        out_specs=pl.BlockSpec((tt, 2 * H * LANES), lambda i: (i, 0)),
        out_shape=jax.ShapeDtypeStruct((T, 2 * H * LANES), BF16),
        compiler_params=_params(("parallel",), 8 * _nbytes((tt, 2 * H * LANES), F32)),
    )(dq, *tabs)


def _conv3(ge, cw, n):
    return cw[2:3] * ge + cw[1:2] * pltpu.roll(ge, 1, 0) + cw[0:1] * pltpu.roll(ge, 2, 0) + cw[3:4]


def _ffn_fwd(gate, up, cw8):
    T, F = gate.shape
    tt = _tile(T, TILES["row"])
    tc = _tile(F, TILES["ffn_c"], LANES)
    hb = tt // CONV_HALO

    def body(g_ref, gp_ref, u_ref, cw_ref, a_ref):
        it = pl.program_id(1)
        prev = jnp.where(it > 0, gp_ref[...].astype(F32), 0.0)
        ge = jnp.concatenate([prev, g_ref[...].astype(F32)], axis=0)
        gc = _conv3(ge, cw_ref[...], tt + CONV_HALO)[CONV_HALO:]
        a_ref[...] = (gc * _sigmoid(gc) * u_ref[...].astype(F32)).astype(BF16)

    blk = pl.BlockSpec((tt, tc), lambda jc, it: (it, jc))
    return pl.pallas_call(
        body, name="ffn_fwd", grid=(F // tc, T // tt),
        in_specs=[blk, pl.BlockSpec((CONV_HALO, tc), lambda jc, it: (jnp.maximum(it * hb - 1, 0), jc)), blk,
                  pl.BlockSpec((8, tc), lambda jc, it: (0, jc))],
        out_specs=blk, out_shape=jax.ShapeDtypeStruct((T, F), BF16),
        compiler_params=_params(("parallel", "parallel"), 16 * _nbytes((tt, tc), F32)),
    )(gate, gate, up, cw8)


def _ffn_bwd(da, gate, up, cw8):
    T, F = gate.shape
    tt = _tile(T, TILES["row"])
    tc = _tile(F, TILES["ffn_c"], LANES)
    hb = tt // CONV_HALO
    n_t = T // tt
    n = tt + 2 * CONV_HALO

    def body(da_ref, dan_ref, g_ref, gp_ref, gn_ref, u_ref, un_ref, cw_ref, dg_ref, du_ref, dcw_ref):
        it = pl.program_id(1)
        first, last = it == 0, it == n_t - 1
        cw = cw_ref[...]
        zeros = jnp.zeros((CONV_HALO, tc), F32)
        ge = jnp.concatenate([jnp.where(first, 0.0, gp_ref[...].astype(F32)), g_ref[...].astype(F32),
                              gn_ref[...].astype(F32)], axis=0)
        dae = jnp.concatenate([zeros, da_ref[...].astype(F32), jnp.where(last, 0.0, dan_ref[...].astype(F32))], axis=0)
        ue = jnp.concatenate([zeros, u_ref[...].astype(F32), un_ref[...].astype(F32)], axis=0)
        g1, g2 = pltpu.roll(ge, 1, 0), pltpu.roll(ge, 2, 0)
        gc = cw[2:3] * ge + cw[1:2] * g1 + cw[0:1] * g2 + cw[3:4]
        sg = _sigmoid(gc)
        dgc = dae * ue * (sg * (1.0 + gc * (1.0 - sg)))
        du_ref[...] = (dae * gc * sg)[CONV_HALO:CONV_HALO + tt].astype(BF16)
        dgp = cw[2:3] * dgc + cw[1:2] * pltpu.roll(dgc, n - 1, 0) + cw[0:1] * pltpu.roll(dgc, n - 2, 0)
        dg_ref[...] = dgp[CONV_HALO:CONV_HALO + tt].astype(BF16)
        mid = slice(CONV_HALO, CONV_HALO + tt)
        d_mid = dgc[mid]
        part = jnp.concatenate([jnp.sum(d_mid * g2[mid], axis=0, keepdims=True), jnp.sum(d_mid * g1[mid], axis=0, keepdims=True),
                                jnp.sum(d_mid * ge[mid], axis=0, keepdims=True), jnp.sum(d_mid, axis=0, keepdims=True),
                                jnp.zeros((4, tc), F32)], axis=0)

        @pl.when(first)
        def _():
            dcw_ref[...] = part

        @pl.when(it > 0)
        def _():
            dcw_ref[...] += part

    blk = pl.BlockSpec((tt, tc), lambda jc, it: (it, jc))
    prv = pl.BlockSpec((CONV_HALO, tc), lambda jc, it: (jnp.maximum(it * hb - 1, 0), jc))
    nxt = pl.BlockSpec((CONV_HALO, tc), lambda jc, it: (jnp.minimum((it + 1) * hb, T // CONV_HALO - 1), jc))
    cws = pl.BlockSpec((8, tc), lambda jc, it: (0, jc))
    return pl.pallas_call(
        body, name="ffn_bwd", grid=(F // tc, n_t), in_specs=[blk, nxt, blk, prv, nxt, blk, nxt, cws],
        out_specs=[blk, blk, cws],
        out_shape=[jax.ShapeDtypeStruct((T, F), BF16), jax.ShapeDtypeStruct((T, F), BF16), jax.ShapeDtypeStruct((8, F), F32)],
        compiler_params=_params(("parallel", "arbitrary"), 32 * _nbytes((tt, tc), F32)),
    )(da, da, gate, gate, gate, up, up, cw8)


def _ple_fwd(h2, gl, pe):
    T, D = h2.shape
    tt = _tile(T, TILES["row"])

    def body(h_ref, gl_ref, pe_ref, o_ref):
        o_ref[...] = h_ref[...] + pe_ref[...] * _sigmoid(gl_ref[...])

    row = pl.BlockSpec((tt, D), lambda i: (i, 0))
    return pl.pallas_call(
        body, name="ple_fwd", grid=(T // tt,), in_specs=[row, row, row], out_specs=row,
        out_shape=jax.ShapeDtypeStruct((T, D), F32), compiler_params=_params(("parallel",), 12 * _nbytes((tt, D), F32)),
    )(h2, gl, pe)


def _ple_bwd(dh, gl, pe):
    T, D = dh.shape
    tt = _tile(T, TILES["row"])

    def body(dh_ref, gl_ref, pe_ref, dpe_ref, dgl_ref):
        d = dh_ref[...]
        sg = _sigmoid(gl_ref[...])
        dpe_ref[...] = (d * sg).astype(BF16)
        dgl_ref[...] = (d * pe_ref[...] * (sg * (1.0 - sg))).astype(BF16)

    row = pl.BlockSpec((tt, D), lambda i: (i, 0))
    return pl.pallas_call(
        body, name="ple_bwd", grid=(T // tt,), in_specs=[row, row, row], out_specs=[row, row],
        out_shape=[jax.ShapeDtypeStruct((T, D), BF16), jax.ShapeDtypeStruct((T, D), BF16)],
        compiler_params=_params(("parallel",), 12 * _nbytes((tt, D), F32)),
    )(dh, gl, pe)


def _adamw(w, g, m, v, name):
    shape = w.shape
    cols = shape[-1]
    rows = math.prod(shape[:-1]) if len(shape) > 1 else 1
    w2, g2, m2, v2 = (a.reshape(rows, cols) for a in (w, g, m, v))
    tr = _tile(rows, max(8, (1 << 20) // (cols * 4)))
    c1 = 1.0 - ADAM_B1 ** ADAM_STEP
    c2 = 1.0 - ADAM_B2 ** ADAM_STEP

    def body(w_ref, g_ref, m_ref, v_ref, d_ref, mo_ref, vo_ref):
        gv = g_ref[...]
        mn = ADAM_B1 * m_ref[...] + (1.0 - ADAM_B1) * gv
        vn = ADAM_B2 * v_ref[...] + (1.0 - ADAM_B2) * (gv * gv)
        mo_ref[...] = mn
        vo_ref[...] = vn
        d_ref[...] = -ADAM_LR * ((mn / c1) / (jnp.sqrt(vn / c2) + ADAM_EPS) + ADAM_WD * w_ref[...])

    blk = pl.BlockSpec((tr, cols), lambda i: (i, 0))
    shp = jax.ShapeDtypeStruct((rows, cols), F32)
    outs = pl.pallas_call(
        body, name=name, grid=(rows // tr,), in_specs=[blk] * 4, out_specs=[blk] * 3, out_shape=[shp] * 3,
        compiler_params=_params(("parallel",), 16 * _nbytes((tr, cols), F32)),
    )(w2, g2, m2, v2)
    return tuple(o.reshape(shape) for o in outs)


_HBM = pl.BlockSpec(memory_space=pltpu.HBM)


def _place():
    x, y, c = lax.axis_index("x"), lax.axis_index("y"), lax.axis_index("c")
    return x, y, c, [(1 - x, y), (x, 1 - y), (1 - x, 1 - y)]


def _remote(src, dst, send_sems, recv_sems, k, to):
    return pltpu.make_async_remote_copy(src_ref=src, dst_ref=dst, send_sem=send_sems.at[k], recv_sem=recv_sems.at[k],
                                        device_id=to, device_id_type=MESH)


def _half(ref, lead, h):
    hr = ref.shape[-2] // 2
    return ref.at[(*lead, pl.ds(pl.multiple_of(h * hr, SUBLANES_BF16), hr))]


def _sem_pair(n):
    return [pltpu.SemaphoreType.DMA((n,)), pltpu.SemaphoreType.DMA((n,))]


def _all_gather(arrs, layer):
    n = len(arrs)

    def body(*refs):
        ins, outs, send_sems, recv_sems = refs[:n], refs[n:2 * n], refs[2 * n], refs[2 * n + 1]
        x, y, c, chips = _place()
        me, sib = 2 * x + y, (x, y, 1 - c)

        def cp(a, k, src, dst, to):
            return _remote(src, dst, send_sems, recv_sems, 7 * a + k, to)

        started = []
        for a in range(n):
            sends = [cp(a, 6, ins[a].at[layer], outs[a].at[me], sib)]
            sends += [cp(a, k, _half(ins[a], (layer,), c), _half(outs[a], (me,), c), (cx, cy, c)) for k, (cx, cy) in enumerate(chips)]
            for s in sends:
                s.start()
            started += sends
        for a in range(n):
            for k, (cx, cy) in enumerate(chips):
                got = _half(outs[a], (2 * cx + cy,), c)
                cp(a, k, got, got, (cx, cy, c)).wait_recv()
                fwd = cp(a, 3 + k, got, got, sib)
                fwd.start()
                started.append(fwd)
        for a in range(n):
            cp(a, 6, ins[a].at[layer], outs[a].at[me], sib).wait_recv()
            for k, (cx, cy) in enumerate(chips):
                got = _half(outs[a], (2 * cx + cy,), 1 - c)
                cp(a, 3 + k, got, got, sib).wait_recv()
        for s in started:
            s.wait_send()

    return pl.pallas_call(
        body, name="all_gather", in_specs=[_HBM] * n, out_specs=[_HBM] * n,
        out_shape=[jax.ShapeDtypeStruct((N_SHARDS,) + a.shape[1:], a.dtype) for a in arrs],
        scratch_shapes=_sem_pair(7 * n),
    )(*arrs)


def _sibling_exchange(arrs):
    n = len(arrs)

    def body(*refs):
        ins, outs, send_sems, recv_sems = refs[:n], refs[n:2 * n], refs[2 * n], refs[2 * n + 1]
        x, y, c, _ = _place()
        sib = (x, y, 1 - c)
        sends = [_remote(_half(ins[a], (s,), 1 - c), outs[a].at[s], send_sems, recv_sems, N_SHARDS * a + s, sib)
                 for a in range(n) for s in range(N_SHARDS)]
        for cp in sends:
            cp.start()
        for a in range(n):
            for s in range(N_SHARDS):
                _remote(_half(ins[a], (s,), c), outs[a].at[s], send_sems, recv_sems, N_SHARDS * a + s, sib).wait_recv()
        for cp in sends:
            cp.wait_send()

    return pl.pallas_call(
        body, name="rs_sibling", in_specs=[_HBM] * n, out_specs=[_HBM] * n,
        out_shape=[jax.ShapeDtypeStruct((N_SHARDS, a.shape[1] // 2, a.shape[2]), a.dtype) for a in arrs],
        scratch_shapes=_sem_pair(N_SHARDS * n),
    )(*arrs)


def _chip_exchange(arrs):
    n = len(arrs)

    def body(*refs):
        ins, outs, send_sems, recv_sems = refs[:n], refs[n:2 * n], refs[2 * n], refs[2 * n + 1]
        x, y, c, chips = _place()
        sends = [_remote(ins[a].at[2 * cx + cy], outs[a].at[k], send_sems, recv_sems, 3 * a + k, (cx, cy, c))
                 for a in range(n) for k, (cx, cy) in enumerate(chips)]
        for cp in sends:
            cp.start()
        for a in range(n):
            for k in range(3):
                _remote(ins[a].at[0], outs[a].at[k], send_sems, recv_sems, 3 * a + k, (x, y, c)).wait_recv()
        for cp in sends:
            cp.wait_send()

    return pl.pallas_call(
        body, name="rs_chips", in_specs=[_HBM] * n, out_specs=[_HBM] * n,
        out_shape=[jax.ShapeDtypeStruct((3,) + a.shape[1:], a.dtype) for a in arrs],
        scratch_shapes=_sem_pair(3 * n),
    )(*arrs)


def _sibling_share(arrs):
    n = len(arrs)

    def body(*refs):
        outs, send_sems, recv_sems = refs[n:2 * n], refs[2 * n], refs[2 * n + 1]
        x, y, c, _ = _place()
        sib = (x, y, 1 - c)
        sends = [_remote(outs[a].at[c], outs[a].at[c], send_sems, recv_sems, a, sib) for a in range(n)]
        for cp in sends:
            cp.start()
        for a in range(n):
            _remote(outs[a].at[c], outs[a].at[1 - c], send_sems, recv_sems, a, sib).wait_recv()
        for cp in sends:
            cp.wait_send()

    return pl.pallas_call(
        body, name="rs_share", in_specs=[_HBM] * n, out_specs=[_HBM] * n,
        out_shape=[jax.ShapeDtypeStruct(a.shape, a.dtype) for a in arrs],
        input_output_aliases={a: a for a in range(n)}, scratch_shapes=_sem_pair(n),
    )(*arrs)


def _add_sibling(g, sib_in, place):
    S, rows, cols = g.shape
    hr = rows // 2
    tr = _tile(hr, max(SUBLANES_BF16, TILES["add_bytes"] // (cols * 2)), SUBLANES_BF16)
    nb = hr // tr

    def body(p_ref, a_ref, b_ref, o_ref):
        o_ref[...] = (a_ref[...].astype(F32) + b_ref[...].astype(F32)).astype(o_ref.dtype)

    blk = pl.BlockSpec((None, tr, cols), lambda s, r, p: (s, r, 0))
    return pl.pallas_call(
        body, name="rs_add_sibling",
        grid_spec=pltpu.PrefetchScalarGridSpec(
            num_scalar_prefetch=1, grid=(S, nb),
            in_specs=[pl.BlockSpec((None, tr, cols), lambda s, r, p: (s, p[1] * nb + r, 0)), blk], out_specs=blk),
        out_shape=jax.ShapeDtypeStruct((S, hr, cols), g.dtype),
        compiler_params=_params(("parallel", "parallel"), 16 * _nbytes((tr, cols), F32)),
    )(place, g, sib_in)


def _add_chips(cs, got, place):
    S, r, cols = cs.shape
    tr = _tile(r, max(SUBLANES_BF16, TILES["add_bytes"] // (cols * 2)), SUBLANES_BF16)

    def body(p_ref, a_ref, b_ref, o_ref):
        acc = a_ref[...].astype(F32)
        for k in range(3):
            acc = acc + b_ref[k].astype(F32)
        o_ref[...] = acc

    return pl.pallas_call(
        body, name="rs_add_chips",
        grid_spec=pltpu.PrefetchScalarGridSpec(
            num_scalar_prefetch=1, grid=(r // tr,),
            in_specs=[pl.BlockSpec((None, tr, cols), lambda i, p: (p[0], i, 0)),
                      pl.BlockSpec((3, tr, cols), lambda i, p: (0, i, 0))],
            out_specs=pl.BlockSpec((None, tr, cols), lambda i, p: (p[1], i, 0))),
        out_shape=jax.ShapeDtypeStruct((2, r, cols), F32),
        compiler_params=_params(("parallel",), 24 * _nbytes((tr, cols), F32)),
    )(place, cs, got)


def _reduce_scatter(arrs, place):
    sums = [_add_sibling(g, s, place) for g, s in zip(arrs, _sibling_exchange(arrs))]
    halves = [_add_chips(cs, got, place) for cs, got in zip(sums, _chip_exchange(sums))]
    return [f.reshape(-1, f.shape[-1]) for f in _sibling_share(halves)]


def _all_reduce_small(v):
    R = v.shape[0]

    def body(v_ref, o_ref, buf, send_sems, recv_sems):
        x, y, c, _ = _place()
        me = 4 * x + 2 * y + c
        buf[me] = v_ref[...]
        sends = []
        for k in range(1, 8):
            px = 1 - x if k & 4 else x
            py = 1 - y if k & 2 else y
            pc = 1 - c if k & 1 else c
            sends.append(_remote(v_ref, buf.at[me], send_sems, recv_sems, k - 1, (px, py, pc)))
        for cp in sends:
            cp.start()
        for k in range(1, 8):
            px = 1 - x if k & 4 else x
            py = 1 - y if k & 2 else y
            pc = 1 - c if k & 1 else c
            _remote(v_ref, buf.at[4 * px + 2 * py + pc], send_sems, recv_sems, k - 1, (px, py, pc)).wait_recv()
        for cp in sends:
            cp.wait_send()
        acc = buf[0]
        for d in range(1, 8):
            acc = acc + buf[d]
        o_ref[...] = acc

    vm = pl.BlockSpec(memory_space=pltpu.VMEM)
    return pl.pallas_call(
        body, name="all_reduce_small", in_specs=[vm], out_specs=vm, out_shape=jax.ShapeDtypeStruct(v.shape, F32),
        scratch_shapes=[pltpu.VMEM((8, R, LANES), F32), pltpu.SemaphoreType.DMA((7,)), pltpu.SemaphoreType.DMA((7,))],
    )(v)


def _pad_to(a, n):
    return a if a.shape[0] == n else jnp.pad(a, (0, n - a.shape[0]))


def _piece_len(shape):
    return -(-math.prod(shape) // PACK_ALIGN) * PACK_ALIGN


def _pack(pieces, dtype):
    flat = jnp.concatenate([_pad_to(a.reshape(-1).astype(dtype), _piece_len(a.shape)) for a in pieces])
    return flat.reshape(-1, LANES)


def _unpack(flat, shapes, lead):
    flat = flat.reshape(lead + (-1,))
    out, off = [], 0
    for shp in shapes:
        out.append(flat[..., off:off + math.prod(shp)].reshape(lead + tuple(shp)))
        off += _piece_len(shp)
    return out


def _join(name, a):
    if name in COL_SHARDED:
        return a.transpose(1, 0, 2).reshape(a.shape[1], -1)
    if name in ROW_SHARDED:
        return a.reshape(-1, a.shape[-1])
    return a.transpose(1, 0, 2, 3).reshape(a.shape[1], -1, a.shape[-1])


def _split(name, a):
    if name in COL_SHARDED:
        return a.reshape(a.shape[0], N_SHARDS, -1).transpose(1, 0, 2)
    if name in ROW_SHARDED:
        return a.reshape(N_SHARDS, -1, a.shape[-1])
    return a.reshape(a.shape[0], N_SHARDS, -1, a.shape[-1]).transpose(1, 0, 2, 3)


def _heads_split(w, H, first, second, pad_second):
    K = w.shape[0]
    w3 = w.reshape(K, H, first + second)
    b = w3[:, :, first:]
    if pad_second > second:
        b = jnp.pad(b, ((0, 0), (0, 0), (0, pad_second - second)))
    return jnp.concatenate([w3[:, :, :first].reshape(K, -1), b.reshape(K, -1)], axis=1)


def _heads_merge(w, H, first, second, pad_second):
    K = w.shape[0]
    a = w[:, :H * first].reshape(K, H, first)
    b = w[:, H * first:].reshape(K, H, pad_second)[:, :, :second]
    return jnp.concatenate([a, b], axis=2).reshape(K, -1)


def kernel(x, p, positions, norm_mix_g, w_in, pool_w, pool_scale, q_norm_g, w_uq, kv_norm_g, w_ukv, w_out, norm_ffn_g, w_up, conv_w, conv_b, w_down, norm_ple_g, w_ple, w_ple_gate, final_norm_g, loss_target, m_norm_mix_g, m_w_in, m_pool_w, m_pool_scale, m_q_norm_g, m_w_uq, m_kv_norm_g, m_w_ukv, m_w_out, m_norm_ffn_g, m_w_up, m_conv_w, m_conv_b, m_w_down, m_norm_ple_g, m_w_ple, m_w_ple_gate, m_final_norm_g, v_norm_mix_g, v_w_in, v_pool_w, v_pool_scale, v_q_norm_g, v_w_uq, v_kv_norm_g, v_w_ukv, v_w_out, v_norm_ffn_g, v_w_up, v_conv_w, v_conv_b, v_w_down, v_norm_ple_g, v_w_ple, v_w_ple_gate, v_final_norm_g):
    W = dict(norm_mix_g=norm_mix_g, w_in=w_in, pool_w=pool_w, pool_scale=pool_scale, q_norm_g=q_norm_g, w_uq=w_uq,
             kv_norm_g=kv_norm_g, w_ukv=w_ukv, w_out=w_out, norm_ffn_g=norm_ffn_g, w_up=w_up, conv_w=conv_w, conv_b=conv_b,
             w_down=w_down, norm_ple_g=norm_ple_g, w_ple=w_ple, w_ple_gate=w_ple_gate, final_norm_g=final_norm_g)
    M1 = dict(norm_mix_g=m_norm_mix_g, w_in=m_w_in, pool_w=m_pool_w, pool_scale=m_pool_scale, q_norm_g=m_q_norm_g, w_uq=m_w_uq,
              kv_norm_g=m_kv_norm_g, w_ukv=m_w_ukv, w_out=m_w_out, norm_ffn_g=m_norm_ffn_g, w_up=m_w_up, conv_w=m_conv_w,
              conv_b=m_conv_b, w_down=m_w_down, norm_ple_g=m_norm_ple_g, w_ple=m_w_ple, w_ple_gate=m_w_ple_gate,
              final_norm_g=m_final_norm_g)
    M2 = dict(norm_mix_g=v_norm_mix_g, w_in=v_w_in, pool_w=v_pool_w, pool_scale=v_pool_scale, q_norm_g=v_q_norm_g, w_uq=v_w_uq,
              kv_norm_g=v_kv_norm_g, w_ukv=v_w_ukv, w_out=v_w_out, norm_ffn_g=v_norm_ffn_g, w_up=v_w_up, conv_w=v_conv_w,
              conv_b=v_conv_b, w_down=v_w_down, norm_ple_g=v_norm_ple_g, w_ple=v_w_ple, w_ple_gate=v_w_ple_gate,
              final_norm_g=v_final_norm_g)

    _, T, D = x.shape
    L = p.shape[0]
    P, QL, KL, F = pool_scale.shape[-1], q_norm_g.shape[-1], kv_norm_g.shape[-1], conv_b.shape[-1]
    C = pool_w.shape[-1]
    H = (D - P) // V_DIM
    d_in = P + QL + KL + ROPE_DIM
    dims = dict(P=P, QL=QL, KL=KL, C=C, H=H)
    misc_shapes = [W[n].shape[1:] for n in MISC]
    ns_in, ns_up, ns_conv = w_in.shape[-1], w_up.shape[-1], conv_w.shape[-1]

    xi, yi, ci = lax.axis_index("x"), lax.axis_index("y"), lax.axis_index("c")
    me = 2 * xi + yi
    place = jnp.stack([me, ci]).astype(jnp.int32)

    def all_reduce(parts):
        flat = jnp.concatenate(parts)
        padded = -(-flat.shape[0] // (8 * LANES)) * (8 * LANES)
        return _all_reduce_small(_pad_to(flat, padded).reshape(-1, LANES)).reshape(-1)

    inv_freq = 1.0 / (ROPE_THETA ** (jnp.arange(0, ROPE_DIM, 2, dtype=F32) / ROPE_DIM))
    inv_lane = jnp.concatenate([inv_freq, inv_freq, jnp.zeros((LANES - ROPE_DIM,), F32)]).reshape(1, LANES)
    tabs = _rope_tables(positions.reshape(T, 1).astype(F32), inv_lane)

    local = [W[n].astype(BF16) for n in BIG] + [jnp.stack([_pack([W[n][l] for n in MISC], BF16) for l in range(L)])]
    placed = lax.dynamic_update_slice(jnp.zeros((L, CONV_TAPS, F), F32), conv_w, (0, 0, me * ns_conv))
    conv_full = all_reduce([jnp.where(ci == 0, placed, 0.0).reshape(-1)])[:L * CONV_TAPS * F].reshape(L, CONV_TAPS, F)

    def gathered(l):
        got = _all_gather(local, l)
        g = dict(zip(BIG, got[:-1]))
        misc = {n: _join(n, a) for n, a in zip(MISC, _unpack(got[-1], misc_shapes, (N_SHARDS,)))}
        return dict(
            w_in=jnp.concatenate([g["w_in"][sh] for sh in range(N_SHARDS)] + [jnp.zeros((D, LANES - ROPE_DIM), BF16)], axis=1),
            w_out=g["w_out"].reshape(-1, D), w_down=g["w_down"].reshape(-1, D), w_ple_gate=g["w_ple_gate"].reshape(-1, D),
            w_up=g["w_up"], w_ple=misc["w_ple"], pool_w=misc["pool_w"],
            w_uq=_heads_split(misc["w_uq"], H, NOPE_DIM, ROPE_DIM, LANES),
            w_ukv=_heads_split(misc["w_ukv"], H, NOPE_DIM, V_DIM, V_DIM),
            cw8=jnp.concatenate([conv_full[l], conv_b[l][None], jnp.zeros((4, F), F32)], axis=0))

    FW = [gathered(l) for l in range(L)]
    half_up = (0, N_SHARDS // 2), (N_SHARDS // 2, N_SHARDS // 2)

    h = x[0]
    saved = []
    for l in range(L):
        fw = FW[l]
        s = dict(h0=h)
        s["n1"] = _rms_fwd(h, norm_mix_g[l], "norm_mix")
        s["u"] = _matmul(s["n1"], fw["w_in"], "nn", F32, "mm_in", tm=512, tn=d_in + LANES - ROPE_DIM)
        s["diff"], s["cqn"], s["ckvn"], s["kr"] = _post_u(s["u"], q_norm_g[l], kv_norm_g[l], tabs, dims)
        s["q"] = _q_rope(_matmul(s["cqn"], fw["w_uq"], "nn", F32, "mm_uq", tn=2 * H * LANES), tabs, dims)
        s["kv"] = _matmul(s["ckvn"], fw["w_ukv"], "nn", BF16, "mm_ukv", tn=2 * H * LANES)
        s["o"], s["lse"] = _flash_fwd(s["q"], s["kv"], s["kr"], dims)
        s["mix"] = jnp.concatenate([_pool_fwd(s["diff"], fw["pool_w"], pool_scale[l], dims), s["o"]], axis=1)
        s["h1"] = _matmul(s["mix"], fw["w_out"], "nn", F32, "mm_out", res=h, tm=512)
        s["n2"] = _rms_fwd(s["h1"], norm_ffn_g[l], "norm_ffn")
        s["gate"] = _matmul(s["n2"], fw["w_up"], "nn", BF16, "mm_gate", tm=512, tn=ns_up // 2, b_shards=half_up[0])
        s["up"] = _matmul(s["n2"], fw["w_up"], "nn", BF16, "mm_up", tm=512, tn=ns_up // 2, b_shards=half_up[1])
        s["a"] = _ffn_fwd(s["gate"], s["up"], fw["cw8"])
        s["h2"] = _matmul(s["a"], fw["w_down"], "nn", F32, "mm_down", res=s["h1"], tm=512, tk=F // 2 if F % (2 * LANES) == 0 else F)
        s["n3"] = _rms_fwd(s["h2"], norm_ple_g[l], "norm_ple")
        s["gl"] = _matmul(s["n3"], fw["w_ple_gate"], "nn", F32, "mm_ple_gate", tm=512)
        s["pe"] = _matmul(p[l, 0], fw["w_ple"], "nn", F32, "mm_ple", tn=D)
        h = _ple_fwd(s["h2"], s["gl"], s["pe"])
        saved.append(s)

    dh, dhb, dg_final, loss_part = _final_loss(h, loss_target[0], final_norm_g)
    loss = lax.psum(loss_part[0, 0], ("x", "y", "c"))

    small = {}
    reduced = [None] * L
    for l in reversed(range(L)):
        fw, s = FW[l], saved[l]
        gw = {}
        dpe, dgl = _ple_bwd(dh, s["gl"], s["pe"])
        gw["w_ple"] = _matmul(p[l, 0], dpe, "tn", BF16, "dw_ple", tm=512)
        gw["w_ple_gate"] = _matmul(s["n3"], dgl, "tn", BF16, "dw_ple_gate")
        dn3 = _matmul(dgl, fw["w_ple_gate"], "nt", F32, "dx_ple_gate", tm=512)
        dh, dhb, small["norm_ple_g", l] = _rms_bwd(dn3, s["h2"], norm_ple_g[l], dh, "norm_ple_bwd")

        da = _matmul(dhb, fw["w_down"], "nt", BF16, "dx_down", tn=F // 4 if F % (4 * LANES) == 0 else F)
        gw["w_down"] = _matmul(s["a"], dhb, "tn", BF16, "dw_down")
        dgate, dup, dcw = _ffn_bwd(da, s["gate"], s["up"], fw["cw8"])
        small["conv_w", l], small["conv_b", l] = dcw[:CONV_TAPS], dcw[CONV_TAPS:CONV_TAPS + 1]
        gw["w_up"] = _matmul(s["n2"], dgate, "tn", BF16, "dw_gate", tn=ns_up // 2, out_shards=(half_up[0][0], N_SHARDS, ns_up))
        gw["w_up"] = _matmul(s["n2"], dup, "tn", BF16, "dw_up", tn=ns_up // 2, out_shards=(half_up[1][0], N_SHARDS, ns_up), carry=gw["w_up"])
        dn2 = _matmul(dgate, fw["w_up"], "nt", F32, "dx_gate", tm=512, tn=D, tk=ns_up // 2, b_shards=half_up[0])
        dn2 = _matmul(dup, fw["w_up"], "nt", F32, "dx_up", res=dn2, tm=512, tn=D, tk=ns_up // 2, b_shards=half_up[1])
        dh, dhb, small["norm_ffn_g", l] = _rms_bwd(dn2, s["h1"], norm_ffn_g[l], dh, "norm_ffn_bwd")

        dmix = _matmul(dhb, fw["w_out"], "nt", BF16, "dx_out")
        gw["w_out"] = _matmul(s["mix"], dhb, "tn", BF16, "dw_out")
        ddiff, gw["pool_w"], small["pool_scale", l] = _pool_bwd(dmix, s["diff"], fw["pool_w"], pool_scale[l], dims)
        dq, dkn, dv, dkr = _flash_bwd(s["q"], s["kv"], s["kr"], s["o"], s["lse"], dmix, dims)
        dqb = _dq_post(dq, tabs, dims)
        dkv = jnp.concatenate([dkn, dv], axis=1)
        gw["w_uq"] = _heads_merge(_matmul(s["cqn"], dqb, "tn", BF16, "dw_uq", tn=2 * H * LANES), H, NOPE_DIM, ROPE_DIM, LANES)
        gw["w_ukv"] = _heads_merge(_matmul(s["ckvn"], dkv, "tn", BF16, "dw_ukv", tn=2 * H * LANES), H, NOPE_DIM, V_DIM, V_DIM)
        dcqn = _matmul(dqb, fw["w_uq"], "nt", F32, "dx_uq")
        dckvn = _matmul(dkv, fw["w_ukv"], "nt", F32, "dx_ukv")
        du, small["q_norm_g", l], small["kv_norm_g", l] = _pre_u_bwd(s["u"], dcqn, dckvn, ddiff, dkr, q_norm_g[l], kv_norm_g[l], tabs, dims)
        gw["w_in"] = _matmul(s["n1"], du, "tn", BF16, "dw_in", tm=512, tn=du.shape[1])[:, :d_in]
        dn1 = _matmul(du, fw["w_in"], "nt", F32, "dx_in", tm=512, tk=du.shape[1])
        dh, dhb, small["norm_mix_g", l] = _rms_bwd(dn1, s["h0"], norm_mix_g[l], dh, "norm_mix_bwd")

        split = {n: _split(n, gw[n]) for n in MISC}
        arrs = [jnp.stack([gw["w_in"][:, sh * ns_in:(sh + 1) * ns_in] for sh in range(N_SHARDS)]),
                gw["w_out"].reshape(N_SHARDS, -1, D), gw["w_up"], gw["w_down"].reshape(N_SHARDS, -1, D),
                gw["w_ple_gate"].reshape(N_SHARDS, -1, D),
                jnp.stack([_pack([split[n][sh] for n in MISC], BF16) for sh in range(N_SHARDS)])]
        reduced[l] = _reduce_scatter(arrs, place)

    grads = {n: jnp.stack([reduced[l][k].reshape(W[n].shape[1:]) for l in range(L)]) for k, n in enumerate(BIG)}
    per_layer = [_unpack(reduced[l][-1], misc_shapes, ()) for l in range(L)]
    for k, n in enumerate(MISC):
        grads[n] = jnp.stack([per_layer[l][k] for l in range(L)])

    small_names = ("norm_mix_g", "pool_scale", "q_norm_g", "kv_norm_g", "norm_ffn_g", "conv_b", "norm_ple_g", "conv_w")
    summed = all_reduce([small[n, l].reshape(-1) for n in small_names for l in range(L)] + [dg_final.reshape(-1)])
    off = 0
    for n in small_names:
        size = CONV_TAPS * F if n == "conv_w" else W[n].shape[-1]
        grads[n] = summed[off:off + L * size].reshape((L, CONV_TAPS, F) if n == "conv_w" else (L, size))
        off += L * size
    grads["final_norm_g"] = summed[off:off + D]
    grads["conv_w"] = lax.dynamic_slice(grads["conv_w"], (0, 0, me * ns_conv), (L, CONV_TAPS, ns_conv))

    deltas, new_m, new_v = {}, {}, {}
    for n in WEIGHTS:
        deltas[n], new_m[n], new_v[n] = _adamw(W[n], grads[n], M1[n], M2[n], "adamw_" + n)

    return (loss, dh[None], *[grads[n] for n in WEIGHTS], *[deltas[n] for n in WEIGHTS],
            *[new_m[n] for n in WEIGHTS], *[new_v[n] for n in WEIGHTS])
```

```python
import functools
import math

import jax
import jax.numpy as jnp
from jax import lax
from jax.experimental import pallas as pl
from jax.experimental.pallas import tpu as pltpu

F32 = jnp.float32
BF16 = jnp.bfloat16

NOPE_DIM = 128
ROPE_DIM = 64
V_DIM = 128
LANES = 128
SUBLANES_BF16 = 16
ROPE_THETA = 10000.0
EPS = 1e-6
POOL_WINDOWS = (2, 4, 8, 16)
POOL_HALO = 16
CONV_TAPS = 3
CONV_HALO = 8
ADAM_LR = 0.001
ADAM_B1 = 0.9
ADAM_B2 = 0.999
ADAM_EPS = 1e-08
ADAM_WD = 0.01
ADAM_STEP = 10
NEG_BIG = -1e30
ATT_SCALE = 1.0 / math.sqrt(NOPE_DIM + ROPE_DIM)
V7X_VMEM_BYTES = 64 * 2 ** 20
N_SHARDS = 4
PACK_ALIGN = 2 * SUBLANES_BF16 * LANES

TILES = dict(row=256, att=512, mm_m=1024, mm_n=1024, mm_k=2048, ffn_row=512, ffn_c=512, add_bytes=1 << 20)

BIG = ("w_in", "w_out", "w_up", "w_down", "w_ple_gate")
MISC = ("w_uq", "w_ukv", "w_ple", "pool_w")
COL_SHARDED = ("w_in", "w_uq", "w_ukv", "w_up", "conv_w", "w_ple")
ROW_SHARDED = ("w_out", "w_down", "w_ple_gate")
WEIGHTS = ("norm_mix_g", "w_in", "pool_w", "pool_scale", "q_norm_g", "w_uq", "kv_norm_g", "w_ukv", "w_out",
           "norm_ffn_g", "w_up", "conv_w", "conv_b", "w_down", "norm_ple_g", "w_ple", "w_ple_gate", "final_norm_g")
MESH = pl.DeviceIdType.MESH


def _nbytes(shape, dtype):
    return math.prod(shape) * jnp.dtype(dtype).itemsize


def _params(sem, need_bytes):
    limit = min(V7X_VMEM_BYTES - (8 << 20), max(32 << 20, int(need_bytes)))
    return pltpu.CompilerParams(dimension_semantics=sem, vmem_limit_bytes=limit)


def _tile(n, want, mult=8):
    if n <= want:
        return n
    for t in range(want - want % mult, 0, -mult):
        if n % t == 0:
            return t
    return n


def _sigmoid(x):
    return 1.0 / (1.0 + jnp.exp(-x))


def _rstd(x):
    return lax.rsqrt(jnp.mean(x * x, axis=-1, keepdims=True) + EPS)


_DOT_DIMS = {"nn": (((1,), (0,)), ((), ())), "nt": (((1,), (1,)), ((), ())), "tn": (((0,), (0,)), ((), ()))}


def _matmul(a, b, mode, out_dtype, name, res=None, tm=None, tn=None, tk=None, b_shards=None, out_shards=None, carry=None):
    if mode == "nn":
        (M, K), N = a.shape, b.shape[-1] * (b_shards[1] if b_shards else 1)
    elif mode == "nt":
        (M, K), N = a.shape, b.shape[-2]
    else:
        (K, M), N = a.shape, b.shape[1]
    per = b.shape[-1] if b_shards else (out_shards[2] if out_shards else None)
    tm = _tile(M, tm or TILES["mm_m"], LANES)
    tn = _tile(per if (per and mode != "nt") else N, tn or TILES["mm_n"], LANES)
    tk = _tile(per if (per and mode == "nt") else K, tk or TILES["mm_k"], LANES)
    nk = K // tk
    has_res = res is not None
    has_carry = carry is not None
    dims = _DOT_DIMS[mode]

    def body(*refs):
        a_ref, b_ref = refs[0], refs[1]
        o_ref = refs[2 + has_res + has_carry]
        part = lax.dot_general(a_ref[...].astype(BF16), b_ref[...].astype(BF16), dims, preferred_element_type=F32)

        def finish(acc):
            if has_res:
                acc = acc + refs[2][...]
            o_ref[...] = acc.astype(o_ref.dtype)

        if nk == 1:
            finish(part)
        else:
            acc_ref = refs[3 + has_res + has_carry]
            k = pl.program_id(2)

            @pl.when(k == 0)
            def _():
                acc_ref[...] = part

            @pl.when(k > 0)
            def _():
                acc_ref[...] += part

            @pl.when(k == nk - 1)
            def _():
                finish(acc_ref[...])

    if mode == "nn":
        a_spec, b_spec = pl.BlockSpec((tm, tk), lambda i, j, k: (i, k)), pl.BlockSpec((tk, tn), lambda i, j, k: (k, j))
    elif mode == "nt":
        a_spec, b_spec = pl.BlockSpec((tm, tk), lambda i, j, k: (i, k)), pl.BlockSpec((tn, tk), lambda i, j, k: (j, k))
    else:
        a_spec, b_spec = pl.BlockSpec((tk, tm), lambda i, j, k: (k, i)), pl.BlockSpec((tk, tn), lambda i, j, k: (k, j))
    o_spec = pl.BlockSpec((tm, tn), lambda i, j, k: (i, j))
    out_shape = jax.ShapeDtypeStruct((M, N), out_dtype)
    if b_shards:
        first = b_shards[0]
        if mode == "nn":
            nps = per // tn
            b_spec = pl.BlockSpec((None, tk, tn), lambda i, j, k: (first + j // nps, k, j % nps))
        else:
            kps = per // tk
            b_spec = pl.BlockSpec((None, tn, tk), lambda i, j, k: (first + k // kps, j, k % kps))
    if out_shards:
        ofirst, nps_o = out_shards[0], per // tn
        o_spec_out = pl.BlockSpec((None, tm, tn), lambda i, j, k: (ofirst + j // nps_o, i, j % nps_o))
        out_shape = jax.ShapeDtypeStruct((out_shards[1], M, per), out_dtype)
    else:
        o_spec_out = o_spec
    in_specs, args = [a_spec, b_spec], [a, b]
    need = 2 * (_nbytes((tm, tk), a.dtype) + _nbytes((tk, tn), b.dtype) + _nbytes((tm, tn), out_dtype)) + 2 * _nbytes((tm, tn), F32)
    if has_res:
        in_specs.append(o_spec)
        args.append(res)
        need += 2 * _nbytes((tm, tn), res.dtype)
    aliases = {}
    if has_carry:
        aliases = {len(args): 0}
        in_specs.append(pl.BlockSpec(memory_space=pl.ANY))
        args.append(carry)
    scratch = [pltpu.VMEM((tm, tn), F32)] if nk > 1 else []
    return pl.pallas_call(
        body, name=name, grid=(M // tm, N // tn, nk), in_specs=in_specs, out_specs=o_spec_out,
        out_shape=out_shape, scratch_shapes=scratch, input_output_aliases=aliases,
        compiler_params=_params(("parallel", "parallel", "arbitrary"), need + (4 << 20)),
    )(*args)


def _rms_fwd(x, g, name):
    T, D = x.shape
    tt = _tile(T, TILES["row"])

    def body(x_ref, g_ref, o_ref):
        xv = x_ref[...]
        o_ref[...] = (xv * _rstd(xv) * g_ref[...]).astype(o_ref.dtype)

    row = pl.BlockSpec((tt, D), lambda i: (i, 0))
    return pl.pallas_call(
        body, name=name, grid=(T // tt,), in_specs=[row, pl.BlockSpec((1, D), lambda i: (0, 0))], out_specs=row,
        out_shape=jax.ShapeDtypeStruct((T, D), BF16), compiler_params=_params(("parallel",), 8 * _nbytes((tt, D), F32)),
    )(x, g.reshape(1, D))


def _rms_bwd(dn, x, g, dres, name):
    T, D = x.shape
    tt = _tile(T, TILES["row"])

    def body(dn_ref, x_ref, g_ref, dres_ref, dx_ref, dxb_ref, dg_ref):
        i = pl.program_id(0)
        xv = x_ref[...]
        r = _rstd(xv)
        xh = xv * r
        dnv = dn_ref[...].astype(F32)
        dxh = dnv * g_ref[...]
        tot = dres_ref[...] + r * (dxh - xh * jnp.mean(dxh * xh, axis=-1, keepdims=True))
        dx_ref[...] = tot
        dxb_ref[...] = tot.astype(BF16)
        part = jnp.sum(dnv * xh, axis=0, keepdims=True)

        @pl.when(i == 0)
        def _():
            dg_ref[...] = part

        @pl.when(i > 0)
        def _():
            dg_ref[...] += part

    row = pl.BlockSpec((tt, D), lambda i: (i, 0))
    vec = pl.BlockSpec((1, D), lambda i: (0, 0))
    return pl.pallas_call(
        body, name=name, grid=(T // tt,), in_specs=[row, row, vec, row], out_specs=[row, row, vec],
        out_shape=[jax.ShapeDtypeStruct((T, D), F32), jax.ShapeDtypeStruct((T, D), BF16), jax.ShapeDtypeStruct((1, D), F32)],
        compiler_params=_params(("arbitrary",), 16 * _nbytes((tt, D), F32)),
    )(dn, x, g.reshape(1, D), dres)


def _final_loss(h, target, g):
    T, D = h.shape
    tt = _tile(T, TILES["row"])

    def body(h_ref, t_ref, g_ref, dx_ref, dxb_ref, dg_ref, loss_ref):
        i = pl.program_id(0)
        xv = h_ref[...]
        r = _rstd(xv)
        xh = xv * r
        gv = g_ref[...]
        err = xh * gv - t_ref[...]
        lpart = 0.5 * jnp.sum(jnp.mean(err * err, axis=-1, keepdims=True), axis=0, keepdims=True)
        dy = err * (1.0 / D)
        dxh = dy * gv
        dx = r * (dxh - xh * jnp.mean(dxh * xh, axis=-1, keepdims=True))
        dx_ref[...] = dx
        dxb_ref[...] = dx.astype(BF16)
        gpart = jnp.sum(dy * xh, axis=0, keepdims=True)
        lrow = jnp.broadcast_to(lpart, (1, LANES))

        @pl.when(i == 0)
        def _():
            dg_ref[...] = gpart
            loss_ref[...] = lrow

        @pl.when(i > 0)
        def _():
            dg_ref[...] += gpart
            loss_ref[...] += lrow

    row = pl.BlockSpec((tt, D), lambda i: (i, 0))
    vec = pl.BlockSpec((1, D), lambda i: (0, 0))
    return pl.pallas_call(
        body, name="final_loss", grid=(T // tt,), in_specs=[row, row, vec],
        out_specs=[row, row, vec, pl.BlockSpec((1, LANES), lambda i: (0, 0))],
        out_shape=[jax.ShapeDtypeStruct((T, D), F32), jax.ShapeDtypeStruct((T, D), BF16),
                   jax.ShapeDtypeStruct((1, D), F32), jax.ShapeDtypeStruct((1, LANES), F32)],
        compiler_params=_params(("arbitrary",), 16 * _nbytes((tt, D), F32)),
    )(h, target, g.reshape(1, D))


def _rope_tables(pos_col, inv_lane):
    T = pos_col.shape[0]
    tt = _tile(T, TILES["row"])

    def body(p_ref, f_ref, c_ref, s1_ref, s2_ref):
        ang = p_ref[...] * f_ref[...]
        lane = lax.broadcasted_iota(jnp.int32, ang.shape, 1)
        half = ROPE_DIM // 2
        cs, sn = jnp.cos(ang), jnp.sin(ang)
        c_ref[...] = jnp.where(lane < ROPE_DIM, cs, 0.0)
        s1_ref[...] = jnp.where(lane < half, -sn, 0.0)
        s2_ref[...] = jnp.where((lane >= half) & (lane < ROPE_DIM), sn, 0.0)

    tab = pl.BlockSpec((tt, LANES), lambda i: (i, 0))
    shp = jax.ShapeDtypeStruct((T, LANES), F32)
    return pl.pallas_call(
        body, name="rope_tables", grid=(T // tt,),
        in_specs=[pl.BlockSpec((tt, 1), lambda i: (i, 0)), pl.BlockSpec((1, LANES), lambda i: (0, 0))],
        out_specs=[tab, tab, tab], out_shape=[shp, shp, shp],
        compiler_params=_params(("parallel",), 32 * _nbytes((tt, LANES), F32)),
    )(pos_col, inv_lane)


def _rope(x, c, s1, s2):
    return x * c + pltpu.roll(x, LANES - ROPE_DIM // 2, 1) * s1 + pltpu.roll(x, ROPE_DIM // 2, 1) * s2


def _rope_t(d, c, s1, s2):
    return d * c + pltpu.roll(d * s1, ROPE_DIM // 2, 1) + pltpu.roll(d * s2, LANES - ROPE_DIM // 2, 1)


def _window_sum(xe, w, forward):
    n = xe.shape[0]
    s, sh = xe, 1
    while sh < w:
        s = s + pltpu.roll(s, (n - sh) if forward else sh, 0)
        sh *= 2
    return s


def _post_u(u, gq, gkv, tabs, dims):
    T, Dp = u.shape
    P, QL, KL, C = dims["P"], dims["QL"], dims["KL"], dims["C"]
    tt = _tile(T, TILES["row"], POOL_HALO)
    hb = tt // POOL_HALO

    def body(u_ref, halo_ref, gq_ref, gkv_ref, c_ref, s1_ref, s2_ref, diff_ref, cq_ref, ckv_ref, kr_ref):
        i = pl.program_id(0)
        t = i * tt + lax.broadcasted_iota(jnp.int32, (tt, 1), 0)
        halo = jnp.where(i > 0, halo_ref[...], 0.0)
        for gi, w in enumerate(POOL_WINDOWS):
            cols = slice(gi * C, (gi + 1) * C)
            xg = u_ref[:, cols]
            s = _window_sum(jnp.concatenate([halo[:, cols], xg], axis=0), w, False)[POOL_HALO:]
            cnt = jnp.minimum(t + 1, w).astype(F32)
            diff_ref[:, cols] = (s / cnt - xg).astype(BF16)
        cq = u_ref[:, P:P + QL]
        cq_ref[...] = (cq * _rstd(cq) * gq_ref[...]).astype(BF16)
        ckv = u_ref[:, P + QL:P + QL + KL]
        ckv_ref[...] = (ckv * _rstd(ckv) * gkv_ref[...]).astype(BF16)
        kr_ref[...] = _rope(u_ref[:, P + QL + KL:], c_ref[...], s1_ref[...], s2_ref[...]).astype(BF16)

    def row(w):
        return pl.BlockSpec((tt, w), lambda i: (i, 0))

    def vec(w):
        return pl.BlockSpec((1, w), lambda i: (0, 0))

    return pl.pallas_call(
        body, name="post_u", grid=(T // tt,),
        in_specs=[row(Dp), pl.BlockSpec((POOL_HALO, P), lambda i: (jnp.maximum(i * hb - 1, 0), 0)),
                  vec(QL), vec(KL), row(LANES), row(LANES), row(LANES)],
        out_specs=[row(P), row(QL), row(KL), row(LANES)],
        out_shape=[jax.ShapeDtypeStruct((T, P), BF16), jax.ShapeDtypeStruct((T, QL), BF16),
                   jax.ShapeDtypeStruct((T, KL), BF16), jax.ShapeDtypeStruct((T, LANES), BF16)],
        compiler_params=_params(("parallel",), 10 * _nbytes((tt, Dp), F32)),
    )(u, u, gq.reshape(1, QL), gkv.reshape(1, KL), *tabs)


def _pre_u_bwd(u, d_cqn, d_ckvn, d_diff, dkr, gq, gkv, tabs, dims):
    T, Dp = u.shape
    P, QL, KL, C, H = dims["P"], dims["QL"], dims["KL"], dims["C"], dims["H"]
    tt = _tile(T, TILES["row"], POOL_HALO)
    hb = tt // POOL_HALO
    n_t = T // tt

    def norm_bwd(xv, dn, gv):
        r = _rstd(xv)
        xh = xv * r
        dxh = dn * gv
        return r * (dxh - xh * jnp.mean(dxh * xh, axis=-1, keepdims=True)), jnp.sum(dn * xh, axis=0, keepdims=True)

    def body(u_ref, dcq_ref, dckv_ref, dd_ref, ddn_ref, dkr_ref, gq_ref, gkv_ref, c_ref, s1_ref, s2_ref,
             du_ref, dgq_ref, dgkv_ref):
        i = pl.program_id(0)
        t = i * tt + lax.broadcasted_iota(jnp.int32, (tt, 1), 0)
        nxt = jnp.where(i < n_t - 1, ddn_ref[...].astype(F32), 0.0)
        for gi, w in enumerate(POOL_WINDOWS):
            cols = slice(gi * C, (gi + 1) * C)
            dd = dd_ref[:, cols].astype(F32)
            e = dd / jnp.minimum(t + 1, w).astype(F32)
            s = _window_sum(jnp.concatenate([e, nxt[:, cols] / float(w)], axis=0), w, True)[:tt]
            du_ref[:, cols] = (s - dd).astype(BF16)
        dq, pq = norm_bwd(u_ref[:, P:P + QL], dcq_ref[...], gq_ref[...])
        du_ref[:, P:P + QL] = dq.astype(BF16)
        dkv, pkv = norm_bwd(u_ref[:, P + QL:P + QL + KL], dckv_ref[...], gkv_ref[...])
        du_ref[:, P + QL:P + QL + KL] = dkv.astype(BF16)
        dk = dkr_ref[0]
        for hh in range(1, H):
            dk = dk + dkr_ref[hh]
        du_ref[:, P + QL + KL:] = _rope_t(dk, c_ref[...], s1_ref[...], s2_ref[...]).astype(BF16)

        @pl.when(i == 0)
        def _():
            dgq_ref[...] = pq
            dgkv_ref[...] = pkv

        @pl.when(i > 0)
        def _():
            dgq_ref[...] += pq
            dgkv_ref[...] += pkv

    def row(w):
        return pl.BlockSpec((tt, w), lambda i: (i, 0))

    def vec(w):
        return pl.BlockSpec((1, w), lambda i: (0, 0))

    return pl.pallas_call(
        body, name="pre_u_bwd", grid=(n_t,),
        in_specs=[row(Dp), row(QL), row(KL), row(P),
                  pl.BlockSpec((POOL_HALO, P), lambda i: (jnp.minimum((i + 1) * hb, T // POOL_HALO - 1), 0)),
                  pl.BlockSpec((H, tt, LANES), lambda i: (0, i, 0)), vec(QL), vec(KL), row(LANES), row(LANES), row(LANES)],
        out_specs=[row(Dp), vec(QL), vec(KL)],
        out_shape=[jax.ShapeDtypeStruct((T, Dp), BF16), jax.ShapeDtypeStruct((1, QL), F32), jax.ShapeDtypeStruct((1, KL), F32)],
        compiler_params=_params(("arbitrary",), 12 * _nbytes((tt, Dp), F32)),
    )(u, d_cqn, d_ckvn, d_diff, d_diff, dkr, gq.reshape(1, QL), gkv.reshape(1, KL), *tabs)


def _pool_fwd(diff, pw, ps, dims):
    T, P = diff.shape
    G, C = len(POOL_WINDOWS), dims["C"]
    tt = _tile(T, TILES["row"])

    def body(d_ref, w_ref, s_ref, o_ref):
        for gi in range(G):
            cols = slice(gi * C, (gi + 1) * C)
            y = jnp.dot(d_ref[:, cols], w_ref[gi], preferred_element_type=F32)
            o_ref[:, cols] = (y * s_ref[:, cols]).astype(BF16)

    row = pl.BlockSpec((tt, P), lambda i: (i, 0))
    return pl.pallas_call(
        body, name="pool_fwd", grid=(T // tt,),
        in_specs=[row, pl.BlockSpec((G, C, C), lambda i: (0, 0, 0)), pl.BlockSpec((1, P), lambda i: (0, 0))],
        out_specs=row, out_shape=jax.ShapeDtypeStruct((T, P), BF16),
        compiler_params=_params(("parallel",), 8 * _nbytes((tt, P), F32)),
    )(diff, pw, ps.reshape(1, P))


def _pool_bwd(dmix, diff, pw, ps, dims):
    T, P = diff.shape
    G, C = len(POOL_WINDOWS), dims["C"]
    tt = _tile(T, TILES["row"])

    def body(dy_ref, d_ref, w_ref, s_ref, dd_ref, dw_ref, ds_ref):
        i = pl.program_id(0)

        @pl.when(i == 0)
        def _():
            dw_ref[...] = jnp.zeros_like(dw_ref)
            ds_ref[...] = jnp.zeros_like(ds_ref)

        for gi in range(G):
            cols = slice(gi * C, (gi + 1) * C)
            dy = dy_ref[:, cols].astype(F32)
            d = d_ref[:, cols]
            w = w_ref[gi]
            ypre = jnp.dot(d, w, preferred_element_type=F32)
            ds_ref[:, cols] += jnp.sum(dy * ypre, axis=0, keepdims=True)
            dyp = (dy * s_ref[:, cols]).astype(BF16)
            dd_ref[:, cols] = lax.dot_general(dyp, w, _DOT_DIMS["nt"], preferred_element_type=F32).astype(BF16)
            dw_ref[gi] += lax.dot_general(d, dyp, _DOT_DIMS["tn"], preferred_element_type=F32)

    row = pl.BlockSpec((tt, P), lambda i: (i, 0))
    wsp = pl.BlockSpec((G, C, C), lambda i: (0, 0, 0))
    vec = pl.BlockSpec((1, P), lambda i: (0, 0))
    return pl.pallas_call(
        body, name="pool_bwd", grid=(T // tt,), in_specs=[row, row, wsp, vec], out_specs=[row, wsp, vec],
        out_shape=[jax.ShapeDtypeStruct((T, P), BF16), jax.ShapeDtypeStruct((G, C, C), F32), jax.ShapeDtypeStruct((1, P), F32)],
        compiler_params=_params(("arbitrary",), 10 * _nbytes((tt, P), F32)),
    )(dmix, diff, pw, ps.reshape(1, P))


def _q_rope(qp, tabs, dims):
    T, W = qp.shape
    H = dims["H"]
    tt = _tile(T, TILES["row"])

    def body(q_ref, c_ref, s1_ref, s2_ref, o_ref):
        o_ref[:, :H * LANES] = (q_ref[:, :H * LANES] * ATT_SCALE).astype(BF16)
        c, s1, s2 = c_ref[...], s1_ref[...], s2_ref[...]
        for hh in range(H, 2 * H):
            cols = slice(hh * LANES, (hh + 1) * LANES)
            o_ref[:, cols] = _rope(q_ref[:, cols] * ATT_SCALE, c, s1, s2).astype(BF16)

    row = pl.BlockSpec((tt, W), lambda i: (i, 0))
    tab = pl.BlockSpec((tt, LANES), lambda i: (i, 0))
    return pl.pallas_call(
        body, name="q_rope", grid=(T // tt,), in_specs=[row, tab, tab, tab], out_specs=row,
        out_shape=jax.ShapeDtypeStruct((T, W), BF16), compiler_params=_params(("parallel",), 8 * _nbytes((tt, W), F32)),
    )(qp, *tabs)


def _scores(qn_ref, qr_ref, kn_ref, kr_ref, t, diagonal):
    q = jnp.concatenate([qn_ref[...], qr_ref[...]], axis=1)
    k = jnp.concatenate([kn_ref[...], kr_ref[...]], axis=1)
    s = lax.dot_general(q, k, _DOT_DIMS["nt"], preferred_element_type=F32)
    if diagonal:
        s = jnp.where(lax.broadcasted_iota(jnp.int32, (t, t), 0) >= lax.broadcasted_iota(jnp.int32, (t, t), 1), s, NEG_BIG)
    return q, k, s


class _Side:
    def __init__(self, ins, out_shapes, n_sems, start, finish):
        self.ins, self.out_shapes, self.n_sems, self.start, self.finish = list(ins), list(out_shapes), n_sems, start, finish


def _side_parts(side):
    if side is None:
        return [], [], [], [], ()
    return side.ins, [_HBM] * len(side.ins), side.out_shapes, [_HBM] * len(side.out_shapes), _sem_pair(side.n_sems)


def _flash_fwd(q_att, kv, kr, dims, side=None):
    T = q_att.shape[0]
    H = dims["H"]
    t = _tile(T, TILES["att"])
    n = T // t
    s_ins, s_in_specs, s_shapes, s_out_specs, s_sems = _side_parts(side)
    n_si, n_so = len(s_ins), len(s_shapes)

    def body(*refs):
        qn_ref, qr_ref, kn_ref, v_ref, kr_ref = refs[:5]
        o_ref, lse_ref = refs[5 + n_si:7 + n_si]
        m_ref, l_ref, acc_ref = refs[7 + n_si + n_so:10 + n_si + n_so]
        side_refs = (refs[5:5 + n_si], refs[7 + n_si:7 + n_si + n_so]) + tuple(refs[10 + n_si + n_so:])
        h, i, j = pl.program_id(0), pl.program_id(1), pl.program_id(2)

        if side is not None:
            @pl.when((h == 0) & (i == 0) & (j == 0))
            def _():
                side.start(*side_refs)

        @pl.when(j == 0)
        def _():
            m_ref[...] = jnp.full_like(m_ref, NEG_BIG)
            l_ref[...] = jnp.zeros_like(l_ref)
            acc_ref[...] = jnp.zeros_like(acc_ref)

        def step(diagonal):
            _, _, s = _scores(qn_ref, qr_ref, kn_ref, kr_ref, t, diagonal)
            m_prev = m_ref[...]
            m_new = jnp.maximum(m_prev, jnp.max(s, axis=1, keepdims=True))
            alpha = jnp.exp(m_prev - m_new)
            p = jnp.exp(s - m_new[:, :1])
            l_ref[...] = alpha * l_ref[...] + jnp.sum(p, axis=1, keepdims=True)
            acc_ref[...] = alpha * acc_ref[...] + jnp.dot(p.astype(BF16), v_ref[...], preferred_element_type=F32)
            m_ref[...] = m_new

        @pl.when(j < i)
        def _():
            step(False)

        @pl.when(j == i)
        def _():
            step(True)
            o_ref[...] = (acc_ref[...] / l_ref[...]).astype(BF16)
            lse_ref[...] = m_ref[...] + jnp.log(l_ref[...])

        if side is not None:
            @pl.when((h == H - 1) & (i == n - 1) & (j == n - 1))
            def _():
                side.finish(*side_refs)

    blk = (t, LANES)
    return pl.pallas_call(
        body, name="flash_fwd", grid=(H, n, n),
        in_specs=[pl.BlockSpec(blk, lambda h, i, j: (i, h)), pl.BlockSpec(blk, lambda h, i, j: (i, H + h)),
                  pl.BlockSpec(blk, lambda h, i, j: (jnp.minimum(j, i), h)),
                  pl.BlockSpec(blk, lambda h, i, j: (jnp.minimum(j, i), H + h)),
                  pl.BlockSpec(blk, lambda h, i, j: (jnp.minimum(j, i), 0))] + s_in_specs,
        out_specs=[pl.BlockSpec(blk, lambda h, i, j: (i, h)), pl.BlockSpec(blk, lambda h, i, j: (i, h))] + s_out_specs,
        out_shape=[jax.ShapeDtypeStruct((T, H * LANES), BF16), jax.ShapeDtypeStruct((T, H * LANES), F32)] + s_shapes,
        scratch_shapes=[pltpu.VMEM(blk, F32), pltpu.VMEM(blk, F32), pltpu.VMEM(blk, F32), *s_sems],
        compiler_params=_params(("arbitrary", "arbitrary", "arbitrary"), 8 * _nbytes((t, t), F32) + (8 << 20)),
    )(q_att, q_att, kv, kv, kr, *s_ins)


def _flash_bwd(q_att, kv, kr, o, lse, dmix, dims, side=None):
    T = q_att.shape[0]
    H = dims["H"]
    ob = dims["P"] // LANES
    t = _tile(T, TILES["att"])
    n = T // t
    s_ins, s_in_specs, s_shapes, s_out_specs, s_sems = _side_parts(side)
    n_si, n_so = len(s_ins), len(s_shapes)

    def body(*refs):
        qn_ref, qr_ref, kn_ref, v_ref, kr_ref, o_ref, lse_ref, do_ref = refs[:8]
        dq_ref, dkn_ref, dv_ref, dkr_ref = refs[8 + n_si:12 + n_si]
        dk_acc, dv_acc = refs[12 + n_si + n_so:14 + n_si + n_so]
        side_refs = (refs[8:8 + n_si], refs[12 + n_si:12 + n_si + n_so]) + tuple(refs[14 + n_si + n_so:])
        h, j, i = pl.program_id(0), pl.program_id(1), pl.program_id(2)

        if side is not None:
            @pl.when((h == 0) & (j == 0) & (i == 0))
            def _():
                side.start(*side_refs)

        @pl.when((j == 0) & (i == 0))
        def _():
            dq_ref[...] = jnp.zeros_like(dq_ref)

        @pl.when(i == 0)
        def _():
            dk_acc[...] = jnp.zeros_like(dk_acc)
            dv_acc[...] = jnp.zeros_like(dv_acc)

        def step(diagonal):
            q, k, s = _scores(qn_ref, qr_ref, kn_ref, kr_ref, t, diagonal)
            p = jnp.exp(s - lse_ref[:, :1])
            do = do_ref[...]
            delta = jnp.sum(do.astype(F32) * o_ref[...].astype(F32), axis=1, keepdims=True)
            dv_acc[...] += lax.dot_general(p.astype(BF16), do, _DOT_DIMS["tn"], preferred_element_type=F32)
            dp = lax.dot_general(do, v_ref[...], _DOT_DIMS["nt"], preferred_element_type=F32)
            ds = (p * (dp - delta)).astype(BF16)
            dk_acc[...] += lax.dot_general(ds, q, _DOT_DIMS["tn"], preferred_element_type=F32)
            rows = pl.ds(pl.multiple_of(i * t, t), t)
            dq_ref[rows, :] += jnp.dot(ds, k, preferred_element_type=F32)

        @pl.when(i > j)
        def _():
            step(False)

        @pl.when(i == j)
        def _():
            step(True)

        @pl.when(i == n - 1)
        def _():
            dkn_ref[...] = dk_acc[:, :LANES].astype(BF16)
            dkr_ref[...] = dk_acc[:, LANES:]
            dv_ref[...] = dv_acc[...].astype(BF16)

        if side is not None:
            @pl.when((h == H - 1) & (j == n - 1) & (i == n - 1))
            def _():
                side.finish(*side_refs)

    blk = (t, LANES)

    def qi(j, i):
        return jnp.maximum(i, j)

    return pl.pallas_call(
        body, name="flash_bwd", grid=(H, n, n),
        in_specs=[pl.BlockSpec(blk, lambda h, j, i: (qi(j, i), h)), pl.BlockSpec(blk, lambda h, j, i: (qi(j, i), H + h)),
                  pl.BlockSpec(blk, lambda h, j, i: (j, h)), pl.BlockSpec(blk, lambda h, j, i: (j, H + h)),
                  pl.BlockSpec(blk, lambda h, j, i: (j, 0)),
                  pl.BlockSpec(blk, lambda h, j, i: (qi(j, i), h)), pl.BlockSpec(blk, lambda h, j, i: (qi(j, i), h)),
                  pl.BlockSpec(blk, lambda h, j, i: (qi(j, i), ob + h))] + s_in_specs,
        out_specs=[pl.BlockSpec((None, T, 2 * LANES), lambda h, j, i: (h, 0, 0)),
                   pl.BlockSpec(blk, lambda h, j, i: (j, h)), pl.BlockSpec(blk, lambda h, j, i: (j, h)),
                   pl.BlockSpec((None, t, LANES), lambda h, j, i: (h, j, 0))] + s_out_specs,
        out_shape=[jax.ShapeDtypeStruct((H, T, 2 * LANES), F32), jax.ShapeDtypeStruct((T, H * LANES), BF16),
                   jax.ShapeDtypeStruct((T, H * LANES), BF16), jax.ShapeDtypeStruct((H, T, LANES), F32)] + s_shapes,
        scratch_shapes=[pltpu.VMEM((t, 2 * LANES), F32), pltpu.VMEM(blk, F32), *s_sems],
        compiler_params=_params(("arbitrary", "arbitrary", "arbitrary"),
                                12 * _nbytes((t, t), F32) + 2 * _nbytes((T, 2 * LANES), F32) + (8 << 20)),
    )(q_att, q_att, kv, kv, kr, o, lse, dmix, *s_ins)


def _dq_post(dq, tabs, dims):
    H, T, _ = dq.shape
    tt = _tile(T, TILES["row"])

    def body(dq_ref, c_ref, s1_ref, s2_ref, o_ref):
        c, s1, s2 = c_ref[...], s1_ref[...], s2_ref[...]
        for hh in range(H):
            o_ref[:, hh * LANES:(hh + 1) * LANES] = (dq_ref[hh, :, :LANES] * ATT_SCALE).astype(BF16)
            o_ref[:, (H + hh) * LANES:(H + hh + 1) * LANES] = _rope_t(dq_ref[hh, :, LANES:] * ATT_SCALE, c, s1, s2).astype(BF16)

    tab = pl.BlockSpec((tt, LANES), lambda i: (i, 0))
    return pl.pallas_call(
        body, name="dq_post", grid=(T // tt,),
        in_specs=[pl.BlockSpec((H, tt, 2 * LANES), lambda i: (0, i, 0)), tab, tab, tab],
        out_specs=pl.BlockSpec((tt, 2 * H * LANES), lambda i: (i, 0)),
        out_shape=jax.ShapeDtypeStruct((T, 2 * H * LANES), BF16),
        compiler_params=_params(("parallel",), 8 * _nbytes((tt, 2 * H * LANES), F32)),
    )(dq, *tabs)


def _conv3(ge, cw, n):
    return cw[2:3] * ge + cw[1:2] * pltpu.roll(ge, 1, 0) + cw[0:1] * pltpu.roll(ge, 2, 0) + cw[3:4]


def _ffn_fwd(gate, up, cw8):
    T, F = gate.shape
    tt = _tile(T, TILES["ffn_row"])
    tc = _tile(F, TILES["ffn_c"], LANES)
    hb = tt // CONV_HALO

    def body(g_ref, gp_ref, u_ref, cw_ref, a_ref):
        it = pl.program_id(1)
        prev = jnp.where(it > 0, gp_ref[...].astype(F32), 0.0)
        ge = jnp.concatenate([prev, g_ref[...].astype(F32)], axis=0)
        gc = _conv3(ge, cw_ref[...], tt + CONV_HALO)[CONV_HALO:]
        a_ref[...] = (gc * _sigmoid(gc) * u_ref[...].astype(F32)).astype(BF16)

    blk = pl.BlockSpec((tt, tc), lambda jc, it: (it, jc))
    return pl.pallas_call(
        body, name="ffn_fwd", grid=(F // tc, T // tt),
        in_specs=[blk, pl.BlockSpec((CONV_HALO, tc), lambda jc, it: (jnp.maximum(it * hb - 1, 0), jc)), blk,
                  pl.BlockSpec((8, tc), lambda jc, it: (0, jc))],
        out_specs=blk, out_shape=jax.ShapeDtypeStruct((T, F), BF16),
        compiler_params=_params(("parallel", "parallel"), 16 * _nbytes((tt, tc), F32)),
    )(gate, gate, up, cw8)


def _ffn_bwd(da, gate, up, cw8):
    T, F = gate.shape
    tt = _tile(T, TILES["ffn_row"])
    tc = _tile(F, TILES["ffn_c"], LANES)
    hb = tt // CONV_HALO
    n_t = T // tt
    n = tt + 2 * CONV_HALO

    def body(da_ref, dan_ref, g_ref, gp_ref, gn_ref, u_ref, un_ref, cw_ref, dg_ref, du_ref, dcw_ref):
        it = pl.program_id(1)
        first, last = it == 0, it == n_t - 1
        cw = cw_ref[...]
        zeros = jnp.zeros((CONV_HALO, tc), F32)
        ge = jnp.concatenate([jnp.where(first, 0.0, gp_ref[...].astype(F32)), g_ref[...].astype(F32),
                              gn_ref[...].astype(F32)], axis=0)
        dae = jnp.concatenate([zeros, da_ref[...].astype(F32), jnp.where(last, 0.0, dan_ref[...].astype(F32))], axis=0)
        ue = jnp.concatenate([zeros, u_ref[...].astype(F32), un_ref[...].astype(F32)], axis=0)
        g1, g2 = pltpu.roll(ge, 1, 0), pltpu.roll(ge, 2, 0)
        gc = cw[2:3] * ge + cw[1:2] * g1 + cw[0:1] * g2 + cw[3:4]
        sg = _sigmoid(gc)
        dgc = dae * ue * (sg * (1.0 + gc * (1.0 - sg)))
        du_ref[...] = (dae * gc * sg)[CONV_HALO:CONV_HALO + tt].astype(BF16)
        dgp = cw[2:3] * dgc + cw[1:2] * pltpu.roll(dgc, n - 1, 0) + cw[0:1] * pltpu.roll(dgc, n - 2, 0)
        dg_ref[...] = dgp[CONV_HALO:CONV_HALO + tt].astype(BF16)
        mid = slice(CONV_HALO, CONV_HALO + tt)
        d_mid = dgc[mid]
        part = jnp.concatenate([jnp.sum(d_mid * g2[mid], axis=0, keepdims=True), jnp.sum(d_mid * g1[mid], axis=0, keepdims=True),
                                jnp.sum(d_mid * ge[mid], axis=0, keepdims=True), jnp.sum(d_mid, axis=0, keepdims=True),
                                jnp.zeros((4, tc), F32)], axis=0)

        @pl.when(first)
        def _():
            dcw_ref[...] = part

        @pl.when(it > 0)
        def _():
            dcw_ref[...] += part

    blk = pl.BlockSpec((tt, tc), lambda jc, it: (it, jc))
    prv = pl.BlockSpec((CONV_HALO, tc), lambda jc, it: (jnp.maximum(it * hb - 1, 0), jc))
    nxt = pl.BlockSpec((CONV_HALO, tc), lambda jc, it: (jnp.minimum((it + 1) * hb, T // CONV_HALO - 1), jc))
    cws = pl.BlockSpec((8, tc), lambda jc, it: (0, jc))
    return pl.pallas_call(
        body, name="ffn_bwd", grid=(F // tc, n_t), in_specs=[blk, nxt, blk, prv, nxt, blk, nxt, cws],
        out_specs=[blk, blk, cws],
        out_shape=[jax.ShapeDtypeStruct((T, F), BF16), jax.ShapeDtypeStruct((T, F), BF16), jax.ShapeDtypeStruct((8, F), F32)],
        compiler_params=_params(("parallel", "arbitrary"), 32 * _nbytes((tt, tc), F32)),
    )(da, da, gate, gate, gate, up, up, cw8)


def _ple_fwd(h2, gl, pe):
    T, D = h2.shape
    tt = _tile(T, TILES["row"])

    def body(h_ref, gl_ref, pe_ref, o_ref):
        o_ref[...] = h_ref[...] + pe_ref[...] * _sigmoid(gl_ref[...])

    row = pl.BlockSpec((tt, D), lambda i: (i, 0))
    return pl.pallas_call(
        body, name="ple_fwd", grid=(T // tt,), in_specs=[row, row, row], out_specs=row,
        out_shape=jax.ShapeDtypeStruct((T, D), F32), compiler_params=_params(("parallel",), 12 * _nbytes((tt, D), F32)),
    )(h2, gl, pe)


def _ple_bwd(dh, gl, pe):
    T, D = dh.shape
    tt = _tile(T, TILES["row"])

    def body(dh_ref, gl_ref, pe_ref, dpe_ref, dgl_ref):
        d = dh_ref[...]
        sg = _sigmoid(gl_ref[...])
        dpe_ref[...] = (d * sg).astype(BF16)
        dgl_ref[...] = (d * pe_ref[...] * (sg * (1.0 - sg))).astype(BF16)

    row = pl.BlockSpec((tt, D), lambda i: (i, 0))
    return pl.pallas_call(
        body, name="ple_bwd", grid=(T // tt,), in_specs=[row, row, row], out_specs=[row, row],
        out_shape=[jax.ShapeDtypeStruct((T, D), BF16), jax.ShapeDtypeStruct((T, D), BF16)],
        compiler_params=_params(("parallel",), 12 * _nbytes((tt, D), F32)),
    )(dh, gl, pe)


def _adamw(w, g, m, v, name):
    shape = w.shape
    cols = shape[-1]
    rows = math.prod(shape[:-1]) if len(shape) > 1 else 1
    w2, g2, m2, v2 = (a.reshape(rows, cols) for a in (w, g, m, v))
    tr = _tile(rows, max(8, (1 << 20) // (cols * 4)))
    c1 = 1.0 - ADAM_B1 ** ADAM_STEP
    c2 = 1.0 - ADAM_B2 ** ADAM_STEP

    def body(w_ref, g_ref, m_ref, v_ref, d_ref, mo_ref, vo_ref):
        gv = g_ref[...]
        mn = ADAM_B1 * m_ref[...] + (1.0 - ADAM_B1) * gv
        vn = ADAM_B2 * v_ref[...] + (1.0 - ADAM_B2) * (gv * gv)
        mo_ref[...] = mn
        vo_ref[...] = vn
        d_ref[...] = -ADAM_LR * ((mn / c1) / (jnp.sqrt(vn / c2) + ADAM_EPS) + ADAM_WD * w_ref[...])

    blk = pl.BlockSpec((tr, cols), lambda i: (i, 0))
    shp = jax.ShapeDtypeStruct((rows, cols), F32)
    outs = pl.pallas_call(
        body, name=name, grid=(rows // tr,), in_specs=[blk] * 4, out_specs=[blk] * 3, out_shape=[shp] * 3,
        compiler_params=_params(("parallel",), 16 * _nbytes((tr, cols), F32)),
    )(w2, g2, m2, v2)
    return tuple(o.reshape(shape) for o in outs)


_HBM = pl.BlockSpec(memory_space=pltpu.HBM)


def _place():
    x, y, c = lax.axis_index("x"), lax.axis_index("y"), lax.axis_index("c")
    return x, y, c, [(1 - x, y), (x, 1 - y), (1 - x, 1 - y)]


def _remote(src, dst, send_sems, recv_sems, k, to):
    return pltpu.make_async_remote_copy(src_ref=src, dst_ref=dst, send_sem=send_sems.at[k], recv_sem=recv_sems.at[k],
                                        device_id=to, device_id_type=MESH)


def _half(ref, lead, h):
    hr = ref.shape[-2] // 2
    return ref.at[(*lead, pl.ds(pl.multiple_of(h * hr, SUBLANES_BF16), hr))]


def _sem_pair(n):
    return [pltpu.SemaphoreType.DMA((n,)), pltpu.SemaphoreType.DMA((n,))]


def _run_side(side, name):
    n_in, n_out = len(side.ins), len(side.out_shapes)

    def body(*refs):
        parts = (refs[:n_in], refs[n_in:n_in + n_out]) + tuple(refs[n_in + n_out:])
        side.start(*parts)
        side.finish(*parts)

    return pl.pallas_call(
        body, name=name, in_specs=[_HBM] * n_in, out_specs=[_HBM] * n_out, out_shape=side.out_shapes,
        scratch_shapes=_sem_pair(side.n_sems),
    )(*side.ins)


def _gather_side(arrs, layer):
    n = len(arrs)

    def copies(ins, outs, send_sems, recv_sems, arriving):
        x, y, c, chips = _place()
        me, sib = 2 * x + y, (x, y, 1 - c)
        out = []
        for a in range(n):
            out.append(_remote(ins[a].at[layer], outs[a].at[me], send_sems, recv_sems, 4 * a + 3, sib))
            for k, (cx, cy) in enumerate(chips):
                slot = 2 * cx + cy if arriving else me
                out.append(_remote(_half(ins[a], (layer,), c), _half(outs[a], (slot,), c), send_sems, recv_sems, 4 * a + k, (cx, cy, c)))
        return out

    def start(ins, outs, send_sems, recv_sems):
        for cp in copies(ins, outs, send_sems, recv_sems, False):
            cp.start()

    def finish(ins, outs, send_sems, recv_sems):
        for cp in copies(ins, outs, send_sems, recv_sems, True):
            cp.wait_recv()
        for cp in copies(ins, outs, send_sems, recv_sems, False):
            cp.wait_send()

    return _Side(arrs, [jax.ShapeDtypeStruct((N_SHARDS,) + a.shape[1:], a.dtype) for a in arrs], 4 * n, start, finish)


def _gather_forward(arrs):
    n = len(arrs)

    def body(*refs):
        outs, send_sems, recv_sems = refs[n:2 * n], refs[2 * n], refs[2 * n + 1]
        x, y, c, chips = _place()
        sib = (x, y, 1 - c)
        sends = []
        for a in range(n):
            for k, (cx, cy) in enumerate(chips):
                got = _half(outs[a], (2 * cx + cy,), c)
                sends.append(_remote(got, got, send_sems, recv_sems, 3 * a + k, sib))
        for cp in sends:
            cp.start()
        for a in range(n):
            for k, (cx, cy) in enumerate(chips):
                got = _half(outs[a], (2 * cx + cy,), 1 - c)
                _remote(got, got, send_sems, recv_sems, 3 * a + k, sib).wait_recv()
        for cp in sends:
            cp.wait_send()

    return pl.pallas_call(
        body, name="gather_forward", in_specs=[_HBM] * n, out_specs=[_HBM] * n,
        out_shape=[jax.ShapeDtypeStruct(a.shape, a.dtype) for a in arrs],
        input_output_aliases={a: a for a in range(n)}, scratch_shapes=_sem_pair(3 * n),
    )(*arrs)


def _sibling_exchange(arrs):
    n = len(arrs)

    def body(*refs):
        ins, outs, send_sems, recv_sems = refs[:n], refs[n:2 * n], refs[2 * n], refs[2 * n + 1]
        x, y, c, _ = _place()
        sib = (x, y, 1 - c)
        sends = [_remote(_half(ins[a], (s,), 1 - c), outs[a].at[s], send_sems, recv_sems, N_SHARDS * a + s, sib)
                 for a in range(n) for s in range(N_SHARDS)]
        for cp in sends:
            cp.start()
        for a in range(n):
            for s in range(N_SHARDS):
                _remote(_half(ins[a], (s,), c), outs[a].at[s], send_sems, recv_sems, N_SHARDS * a + s, sib).wait_recv()
        for cp in sends:
            cp.wait_send()

    return pl.pallas_call(
        body, name="rs_sibling", in_specs=[_HBM] * n, out_specs=[_HBM] * n,
        out_shape=[jax.ShapeDtypeStruct((N_SHARDS, a.shape[1] // 2, a.shape[2]), a.dtype) for a in arrs],
        scratch_shapes=_sem_pair(N_SHARDS * n),
    )(*arrs)


def _chip_side(arrs):
    n = len(arrs)

    def copies(ins, outs, send_sems, recv_sems):
        x, y, c, chips = _place()
        return [_remote(ins[a].at[2 * cx + cy], outs[a].at[k], send_sems, recv_sems, 3 * a + k, (cx, cy, c))
                for a in range(n) for k, (cx, cy) in enumerate(chips)]

    def start(ins, outs, send_sems, recv_sems):
        for cp in copies(ins, outs, send_sems, recv_sems):
            cp.start()

    def finish(ins, outs, send_sems, recv_sems):
        for cp in copies(ins, outs, send_sems, recv_sems):
            cp.wait_recv()
        for cp in copies(ins, outs, send_sems, recv_sems):
            cp.wait_send()

    return _Side(arrs, [jax.ShapeDtypeStruct((3,) + a.shape[1:], a.dtype) for a in arrs], 3 * n, start, finish)


def _sibling_share(arrs):
    n = len(arrs)

    def body(*refs):
        outs, send_sems, recv_sems = refs[n:2 * n], refs[2 * n], refs[2 * n + 1]
        x, y, c, _ = _place()
        sib = (x, y, 1 - c)
        sends = [_remote(outs[a].at[c], outs[a].at[c], send_sems, recv_sems, a, sib) for a in range(n)]
        for cp in sends:
            cp.start()
        for a in range(n):
            _remote(outs[a].at[c], outs[a].at[1 - c], send_sems, recv_sems, a, sib).wait_recv()
        for cp in sends:
            cp.wait_send()

    return pl.pallas_call(
        body, name="rs_share", in_specs=[_HBM] * n, out_specs=[_HBM] * n,
        out_shape=[jax.ShapeDtypeStruct(a.shape, a.dtype) for a in arrs],
        input_output_aliases={a: a for a in range(n)}, scratch_shapes=_sem_pair(n),
    )(*arrs)


def _add_sibling(g, sib_in, place):
    S, rows, cols = g.shape
    hr = rows // 2
    tr = _tile(hr, max(SUBLANES_BF16, TILES["add_bytes"] // (cols * 2)), SUBLANES_BF16)
    nb = hr // tr

    def body(p_ref, a_ref, b_ref, o_ref):
        o_ref[...] = (a_ref[...].astype(F32) + b_ref[...].astype(F32)).astype(o_ref.dtype)

    blk = pl.BlockSpec((None, tr, cols), lambda s, r, p: (s, r, 0))
    return pl.pallas_call(
        body, name="rs_add_sibling",
        grid_spec=pltpu.PrefetchScalarGridSpec(
            num_scalar_prefetch=1, grid=(S, nb),
            in_specs=[pl.BlockSpec((None, tr, cols), lambda s, r, p: (s, p[1] * nb + r, 0)), blk], out_specs=blk),
        out_shape=jax.ShapeDtypeStruct((S, hr, cols), g.dtype),
        compiler_params=_params(("parallel", "parallel"), 16 * _nbytes((tr, cols), F32)),
    )(place, g, sib_in)


def _add_chips(cs, got, place):
    S, r, cols = cs.shape
    tr = _tile(r, max(SUBLANES_BF16, TILES["add_bytes"] // (cols * 2)), SUBLANES_BF16)

    def body(p_ref, a_ref, b_ref, o_ref):
        acc = a_ref[...].astype(F32)
        for k in range(3):
            acc = acc + b_ref[k].astype(F32)
        o_ref[...] = acc

    return pl.pallas_call(
        body, name="rs_add_chips",
        grid_spec=pltpu.PrefetchScalarGridSpec(
            num_scalar_prefetch=1, grid=(r // tr,),
            in_specs=[pl.BlockSpec((None, tr, cols), lambda i, p: (p[0], i, 0)),
                      pl.BlockSpec((3, tr, cols), lambda i, p: (0, i, 0))],
            out_specs=pl.BlockSpec((None, tr, cols), lambda i, p: (p[1], i, 0))),
        out_shape=jax.ShapeDtypeStruct((2, r, cols), F32),
        compiler_params=_params(("parallel",), 24 * _nbytes((tr, cols), F32)),
    )(place, cs, got)


def _reduce_begin(arrs, place):
    return [_add_sibling(g, s, place) for g, s in zip(arrs, _sibling_exchange(arrs))]


def _reduce_end(sums, got, place):
    halves = [_add_chips(cs, g, place) for cs, g in zip(sums, got)]
    return [f.reshape(-1, f.shape[-1]) for f in _sibling_share(halves)]


def _all_reduce_small(v):
    R = v.shape[0]

    def body(v_ref, o_ref, buf, send_sems, recv_sems):
        x, y, c, _ = _place()
        me = 4 * x + 2 * y + c
        buf[me] = v_ref[...]
        sends = []
        for k in range(1, 8):
            px = 1 - x if k & 4 else x
            py = 1 - y if k & 2 else y
            pc = 1 - c if k & 1 else c
            sends.append(_remote(v_ref, buf.at[me], send_sems, recv_sems, k - 1, (px, py, pc)))
        for cp in sends:
            cp.start()
        for k in range(1, 8):
            px = 1 - x if k & 4 else x
            py = 1 - y if k & 2 else y
            pc = 1 - c if k & 1 else c
            _remote(v_ref, buf.at[4 * px + 2 * py + pc], send_sems, recv_sems, k - 1, (px, py, pc)).wait_recv()
        for cp in sends:
            cp.wait_send()
        acc = buf[0]
        for d in range(1, 8):
            acc = acc + buf[d]
        o_ref[...] = acc

    vm = pl.BlockSpec(memory_space=pltpu.VMEM)
    return pl.pallas_call(
        body, name="all_reduce_small", in_specs=[vm], out_specs=vm, out_shape=jax.ShapeDtypeStruct(v.shape, F32),
        scratch_shapes=[pltpu.VMEM((8, R, LANES), F32), pltpu.SemaphoreType.DMA((7,)), pltpu.SemaphoreType.DMA((7,))],
    )(v)


def _pad_to(a, n):
    return a if a.shape[0] == n else jnp.pad(a, (0, n - a.shape[0]))


def _piece_len(shape):
    return -(-math.prod(shape) // PACK_ALIGN) * PACK_ALIGN


def _pack(pieces, dtype):
    flat = jnp.concatenate([_pad_to(a.reshape(-1).astype(dtype), _piece_len(a.shape)) for a in pieces])
    return flat.reshape(-1, LANES)


def _unpack(flat, shapes, lead):
    flat = flat.reshape(lead + (-1,))
    out, off = [], 0
    for shp in shapes:
        out.append(flat[..., off:off + math.prod(shp)].reshape(lead + tuple(shp)))
        off += _piece_len(shp)
    return out


def _join(name, a):
    if name in COL_SHARDED:
        return a.transpose(1, 0, 2).reshape(a.shape[1], -1)
    if name in ROW_SHARDED:
        return a.reshape(-1, a.shape[-1])
    return a.transpose(1, 0, 2, 3).reshape(a.shape[1], -1, a.shape[-1])


def _split(name, a):
    if name in COL_SHARDED:
        return a.reshape(a.shape[0], N_SHARDS, -1).transpose(1, 0, 2)
    if name in ROW_SHARDED:
        return a.reshape(N_SHARDS, -1, a.shape[-1])
    return a.reshape(a.shape[0], N_SHARDS, -1, a.shape[-1]).transpose(1, 0, 2, 3)


def _heads_split(w, H, first, second, pad_second):
    K = w.shape[0]
    w3 = w.reshape(K, H, first + second)
    b = w3[:, :, first:]
    if pad_second > second:
        b = jnp.pad(b, ((0, 0), (0, 0), (0, pad_second - second)))
    return jnp.concatenate([w3[:, :, :first].reshape(K, -1), b.reshape(K, -1)], axis=1)


def _heads_merge(w, H, first, second, pad_second):
    K = w.shape[0]
    a = w[:, :H * first].reshape(K, H, first)
    b = w[:, H * first:].reshape(K, H, pad_second)[:, :, :second]
    return jnp.concatenate([a, b], axis=2).reshape(K, -1)


def kernel(x, p, positions, norm_mix_g, w_in, pool_w, pool_scale, q_norm_g, w_uq, kv_norm_g, w_ukv, w_out, norm_ffn_g, w_up, conv_w, conv_b, w_down, norm_ple_g, w_ple, w_ple_gate, final_norm_g, loss_target, m_norm_mix_g, m_w_in, m_pool_w, m_pool_scale, m_q_norm_g, m_w_uq, m_kv_norm_g, m_w_ukv, m_w_out, m_norm_ffn_g, m_w_up, m_conv_w, m_conv_b, m_w_down, m_norm_ple_g, m_w_ple, m_w_ple_gate, m_final_norm_g, v_norm_mix_g, v_w_in, v_pool_w, v_pool_scale, v_q_norm_g, v_w_uq, v_kv_norm_g, v_w_ukv, v_w_out, v_norm_ffn_g, v_w_up, v_conv_w, v_conv_b, v_w_down, v_norm_ple_g, v_w_ple, v_w_ple_gate, v_final_norm_g):
    W = dict(norm_mix_g=norm_mix_g, w_in=w_in, pool_w=pool_w, pool_scale=pool_scale, q_norm_g=q_norm_g, w_uq=w_uq,
             kv_norm_g=kv_norm_g, w_ukv=w_ukv, w_out=w_out, norm_ffn_g=norm_ffn_g, w_up=w_up, conv_w=conv_w, conv_b=conv_b,
             w_down=w_down, norm_ple_g=norm_ple_g, w_ple=w_ple, w_ple_gate=w_ple_gate, final_norm_g=final_norm_g)
    M1 = dict(norm_mix_g=m_norm_mix_g, w_in=m_w_in, pool_w=m_pool_w, pool_scale=m_pool_scale, q_norm_g=m_q_norm_g, w_uq=m_w_uq,
              kv_norm_g=m_kv_norm_g, w_ukv=m_w_ukv, w_out=m_w_out, norm_ffn_g=m_norm_ffn_g, w_up=m_w_up, conv_w=m_conv_w,
              conv_b=m_conv_b, w_down=m_w_down, norm_ple_g=m_norm_ple_g, w_ple=m_w_ple, w_ple_gate=m_w_ple_gate,
              final_norm_g=m_final_norm_g)
    M2 = dict(norm_mix_g=v_norm_mix_g, w_in=v_w_in, pool_w=v_pool_w, pool_scale=v_pool_scale, q_norm_g=v_q_norm_g, w_uq=v_w_uq,
              kv_norm_g=v_kv_norm_g, w_ukv=v_w_ukv, w_out=v_w_out, norm_ffn_g=v_norm_ffn_g, w_up=v_w_up, conv_w=v_conv_w,
              conv_b=v_conv_b, w_down=v_w_down, norm_ple_g=v_norm_ple_g, w_ple=v_w_ple, w_ple_gate=v_w_ple_gate,
              final_norm_g=v_final_norm_g)

    _, T, D = x.shape
    L = p.shape[0]
    P, QL, KL, F = pool_scale.shape[-1], q_norm_g.shape[-1], kv_norm_g.shape[-1], conv_b.shape[-1]
    C = pool_w.shape[-1]
    H = (D - P) // V_DIM
    d_in = P + QL + KL + ROPE_DIM
    dims = dict(P=P, QL=QL, KL=KL, C=C, H=H)
    misc_shapes = [W[n].shape[1:] for n in MISC]
    ns_in, ns_up, ns_conv = w_in.shape[-1], w_up.shape[-1], conv_w.shape[-1]

    xi, yi, ci = lax.axis_index("x"), lax.axis_index("y"), lax.axis_index("c")
    me = 2 * xi + yi
    place = jnp.stack([me, ci]).astype(jnp.int32)

    def all_reduce(parts):
        flat = jnp.concatenate(parts)
        padded = -(-flat.shape[0] // (8 * LANES)) * (8 * LANES)
        return _all_reduce_small(_pad_to(flat, padded).reshape(-1, LANES)).reshape(-1)

    inv_freq = 1.0 / (ROPE_THETA ** (jnp.arange(0, ROPE_DIM, 2, dtype=F32) / ROPE_DIM))
    inv_lane = jnp.concatenate([inv_freq, inv_freq, jnp.zeros((LANES - ROPE_DIM,), F32)]).reshape(1, LANES)
    tabs = _rope_tables(positions.reshape(T, 1).astype(F32), inv_lane)

    local = [W[n].astype(BF16) for n in BIG] + [jnp.stack([_pack([W[n][l] for n in MISC], BF16) for l in range(L)])]
    placed = lax.dynamic_update_slice(jnp.zeros((L, CONV_TAPS, F), F32), conv_w, (0, 0, me * ns_conv))
    conv_full = all_reduce([jnp.where(ci == 0, placed, 0.0).reshape(-1)])[:L * CONV_TAPS * F].reshape(L, CONV_TAPS, F)

    def layout(got, l):
        g = dict(zip(BIG, got[:-1]))
        misc = {n: _join(n, a) for n, a in zip(MISC, _unpack(got[-1], misc_shapes, (N_SHARDS,)))}
        return dict(
            w_in=jnp.concatenate([g["w_in"][sh] for sh in range(N_SHARDS)] + [jnp.zeros((D, LANES - ROPE_DIM), BF16)], axis=1),
            w_out=g["w_out"].reshape(-1, D), w_down=g["w_down"].reshape(-1, D), w_ple_gate=g["w_ple_gate"].reshape(-1, D),
            w_up=g["w_up"], w_ple=misc["w_ple"], pool_w=misc["pool_w"],
            w_uq=_heads_split(misc["w_uq"], H, NOPE_DIM, ROPE_DIM, LANES),
            w_ukv=_heads_split(misc["w_ukv"], H, NOPE_DIM, V_DIM, V_DIM),
            cw8=jnp.concatenate([conv_full[l], conv_b[l][None], jnp.zeros((4, F), F32)], axis=0))

    half_up = (0, N_SHARDS // 2), (N_SHARDS // 2, N_SHARDS // 2)

    h = x[0]
    saved, FW = [], []
    arriving = _gather_forward(_run_side(_gather_side(local, 0), "all_gather"))
    for l in range(L):
        fw = layout(arriving, l)
        FW.append(fw)
        s = dict(h0=h)
        s["n1"] = _rms_fwd(h, norm_mix_g[l], "norm_mix")
        s["u"] = _matmul(s["n1"], fw["w_in"], "nn", F32, "mm_in", tm=512, tn=d_in + LANES - ROPE_DIM)
        s["diff"], s["cqn"], s["ckvn"], s["kr"] = _post_u(s["u"], q_norm_g[l], kv_norm_g[l], tabs, dims)
        s["q"] = _q_rope(_matmul(s["cqn"], fw["w_uq"], "nn", F32, "mm_uq", tn=2 * H * LANES), tabs, dims)
        s["kv"] = _matmul(s["ckvn"], fw["w_ukv"], "nn", BF16, "mm_ukv", tn=2 * H * LANES)
        s["o"], s["lse"], *landed = _flash_fwd(s["q"], s["kv"], s["kr"], dims, _gather_side(local, l + 1) if l + 1 < L else None)
        if landed:
            arriving = _gather_forward(landed)
        s["mix"] = jnp.concatenate([_pool_fwd(s["diff"], fw["pool_w"], pool_scale[l], dims), s["o"]], axis=1)
        s["h1"] = _matmul(s["mix"], fw["w_out"], "nn", F32, "mm_out", res=h, tm=512)
        s["n2"] = _rms_fwd(s["h1"], norm_ffn_g[l], "norm_ffn")
        s["gate"] = _matmul(s["n2"], fw["w_up"], "nn", BF16, "mm_gate", tm=512, tn=ns_up // 2, b_shards=half_up[0])
        s["up"] = _matmul(s["n2"], fw["w_up"], "nn", BF16, "mm_up", tm=512, tn=ns_up // 2, b_shards=half_up[1])
        s["a"] = _ffn_fwd(s["gate"], s["up"], fw["cw8"])
        s["h2"] = _matmul(s["a"], fw["w_down"], "nn", F32, "mm_down", res=s["h1"], tm=512, tk=F // 2 if F % (2 * LANES) == 0 else F)
        s["n3"] = _rms_fwd(s["h2"], norm_ple_g[l], "norm_ple")
        s["gl"] = _matmul(s["n3"], fw["w_ple_gate"], "nn", F32, "mm_ple_gate", tm=512)
        s["pe"] = _matmul(p[l, 0], fw["w_ple"], "nn", F32, "mm_ple", tn=D)
        h = _ple_fwd(s["h2"], s["gl"], s["pe"])
        saved.append(s)

    dh, dhb, dg_final, loss_part = _final_loss(h, loss_target[0], final_norm_g)
    loss = lax.psum(loss_part[0, 0], ("x", "y", "c"))

    small = {}
    reduced = [None] * L
    waiting = None
    for l in reversed(range(L)):
        fw, s = FW[l], saved[l]
        gw = {}
        dpe, dgl = _ple_bwd(dh, s["gl"], s["pe"])
        gw["w_ple"] = _matmul(p[l, 0], dpe, "tn", BF16, "dw_ple", tm=512)
        gw["w_ple_gate"] = _matmul(s["n3"], dgl, "tn", BF16, "dw_ple_gate")
        dn3 = _matmul(dgl, fw["w_ple_gate"], "nt", F32, "dx_ple_gate", tm=512)
        dh, dhb, small["norm_ple_g", l] = _rms_bwd(dn3, s["h2"], norm_ple_g[l], dh, "norm_ple_bwd")

        da = _matmul(dhb, fw["w_down"], "nt", BF16, "dx_down", tn=F // 4 if F % (4 * LANES) == 0 else F)
        gw["w_down"] = _matmul(s["a"], dhb, "tn", BF16, "dw_down")
        dgate, dup, dcw = _ffn_bwd(da, s["gate"], s["up"], fw["cw8"])
        small["conv_w", l], small["conv_b", l] = dcw[:CONV_TAPS], dcw[CONV_TAPS:CONV_TAPS + 1]
        gw["w_up"] = _matmul(s["n2"], dgate, "tn", BF16, "dw_gate", tn=ns_up // 2, out_shards=(half_up[0][0], N_SHARDS, ns_up))
        gw["w_up"] = _matmul(s["n2"], dup, "tn", BF16, "dw_up", tn=ns_up // 2, out_shards=(half_up[1][0], N_SHARDS, ns_up), carry=gw["w_up"])
        dn2 = _matmul(dgate, fw["w_up"], "nt", F32, "dx_gate", tm=512, tn=D, tk=ns_up // 2, b_shards=half_up[0])
        dn2 = _matmul(dup, fw["w_up"], "nt", F32, "dx_up", res=dn2, tm=512, tn=D, tk=ns_up // 2, b_shards=half_up[1])
        dh, dhb, small["norm_ffn_g", l] = _rms_bwd(dn2, s["h1"], norm_ffn_g[l], dh, "norm_ffn_bwd")

        dmix = _matmul(dhb, fw["w_out"], "nt", BF16, "dx_out")
        gw["w_out"] = _matmul(s["mix"], dhb, "tn", BF16, "dw_out")
        ddiff, gw["pool_w"], small["pool_scale", l] = _pool_bwd(dmix, s["diff"], fw["pool_w"], pool_scale[l], dims)
        dq, dkn, dv, dkr, *got = _flash_bwd(s["q"], s["kv"], s["kr"], s["o"], s["lse"], dmix, dims,
                                            _chip_side(waiting) if waiting else None)
        if got:
            reduced[l + 1] = _reduce_end(waiting, got, place)
        dqb = _dq_post(dq, tabs, dims)
        dkv = jnp.concatenate([dkn, dv], axis=1)
        gw["w_uq"] = _heads_merge(_matmul(s["cqn"], dqb, "tn", BF16, "dw_uq", tn=2 * H * LANES), H, NOPE_DIM, ROPE_DIM, LANES)
        gw["w_ukv"] = _heads_merge(_matmul(s["ckvn"], dkv, "tn", BF16, "dw_ukv", tn=2 * H * LANES), H, NOPE_DIM, V_DIM, V_DIM)
        dcqn = _matmul(dqb, fw["w_uq"], "nt", F32, "dx_uq")
        dckvn = _matmul(dkv, fw["w_ukv"], "nt", F32, "dx_ukv")
        du, small["q_norm_g", l], small["kv_norm_g", l] = _pre_u_bwd(s["u"], dcqn, dckvn, ddiff, dkr, q_norm_g[l], kv_norm_g[l], tabs, dims)
        gw["w_in"] = _matmul(s["n1"], du, "tn", BF16, "dw_in", tm=512, tn=du.shape[1])[:, :d_in]
        dn1 = _matmul(du, fw["w_in"], "nt", F32, "dx_in", tm=512, tk=du.shape[1])
        dh, dhb, small["norm_mix_g", l] = _rms_bwd(dn1, s["h0"], norm_mix_g[l], dh, "norm_mix_bwd")

        split = {n: _split(n, gw[n]) for n in MISC}
        arrs = [jnp.stack([gw["w_in"][:, sh * ns_in:(sh + 1) * ns_in] for sh in range(N_SHARDS)]),
                gw["w_out"].reshape(N_SHARDS, -1, D), gw["w_up"], gw["w_down"].reshape(N_SHARDS, -1, D),
                gw["w_ple_gate"].reshape(N_SHARDS, -1, D),
                jnp.stack([_pack([split[n][sh] for n in MISC], BF16) for sh in range(N_SHARDS)])]
        waiting = _reduce_begin(arrs, place)
    reduced[0] = _reduce_end(waiting, _run_side(_chip_side(waiting), "rs_chips"), place)

    grads = {n: jnp.stack([reduced[l][k].reshape(W[n].shape[1:]) for l in range(L)]) for k, n in enumerate(BIG)}
    per_layer = [_unpack(reduced[l][-1], misc_shapes, ()) for l in range(L)]
    for k, n in enumerate(MISC):
        grads[n] = jnp.stack([per_layer[l][k] for l in range(L)])

    small_names = ("norm_mix_g", "pool_scale", "q_norm_g", "kv_norm_g", "norm_ffn_g", "conv_b", "norm_ple_g", "conv_w")
    summed = all_reduce([small[n, l].reshape(-1) for n in small_names for l in range(L)] + [dg_final.reshape(-1)])
    off = 0
    for n in small_names:
        size = CONV_TAPS * F if n == "conv_w" else W[n].shape[-1]
        grads[n] = summed[off:off + L * size].reshape((L, CONV_TAPS, F) if n == "conv_w" else (L, size))
        off += L * size
    grads["final_norm_g"] = summed[off:off + D]
    grads["conv_w"] = lax.dynamic_slice(grads["conv_w"], (0, 0, me * ns_conv), (L, CONV_TAPS, ns_conv))

    deltas, new_m, new_v = {}, {}, {}
    for n in WEIGHTS:
        deltas[n], new_m[n], new_v[n] = _adamw(W[n], grads[n], M1[n], M2[n], "adamw_" + n)

    return (loss, dh[None], *[grads[n] for n in WEIGHTS], *[deltas[n] for n in WEIGHTS],
            *[new_m[n] for n in WEIGHTS], *[new_v[n] for n in WEIGHTS])
```

```python
import functools
import math

import jax
import jax.numpy as jnp
from jax import lax
from jax.experimental import pallas as pl
from jax.experimental.pallas import tpu as pltpu

F32 = jnp.float32
BF16 = jnp.bfloat16

NOPE_DIM = 128
ROPE_DIM = 64
V_DIM = 128
LANES = 128
SUBLANES_BF16 = 16
ROPE_THETA = 10000.0
EPS = 1e-6
POOL_WINDOWS = (2, 4, 8, 16)
POOL_HALO = 16
CONV_TAPS = 3
CONV_HALO = 8
ADAM_LR = 0.001
ADAM_B1 = 0.9
ADAM_B2 = 0.999
ADAM_EPS = 1e-08
ADAM_WD = 0.01
ADAM_STEP = 10
NEG_BIG = -1e30
ATT_SCALE = 1.0 / math.sqrt(NOPE_DIM + ROPE_DIM)
V7X_VMEM_BYTES = 64 * 2 ** 20
N_SHARDS = 4
PACK_ALIGN = 2 * SUBLANES_BF16 * LANES

TILES = dict(row=256, att=512, mm_m=1024, mm_n=1024, mm_k=2048, ffn_row=512, ffn_c=512, add_bytes=1 << 20)

BIG = ("w_in", "w_out", "w_up", "w_down", "w_ple_gate")
MISC = ("w_uq", "w_ukv", "w_ple", "pool_w")
COL_SHARDED = ("w_in", "w_uq", "w_ukv", "w_up", "conv_w", "w_ple")
ROW_SHARDED = ("w_out", "w_down", "w_ple_gate")
WEIGHTS = ("norm_mix_g", "w_in", "pool_w", "pool_scale", "q_norm_g", "w_uq", "kv_norm_g", "w_ukv", "w_out",
           "norm_ffn_g", "w_up", "conv_w", "conv_b", "w_down", "norm_ple_g", "w_ple", "w_ple_gate", "final_norm_g")
MESH = pl.DeviceIdType.MESH


def _nbytes(shape, dtype):
    return math.prod(shape) * jnp.dtype(dtype).itemsize


def _params(sem, need_bytes):
    limit = min(V7X_VMEM_BYTES - (8 << 20), max(32 << 20, int(need_bytes)))
    return pltpu.CompilerParams(dimension_semantics=sem, vmem_limit_bytes=limit)


def _tile(n, want, mult=8):
    if n <= want:
        return n
    for t in range(want - want % mult, 0, -mult):
        if n % t == 0:
            return t
    return n


def _sigmoid(x):
    return 1.0 / (1.0 + jnp.exp(-x))


def _rstd(x):
    return lax.rsqrt(jnp.mean(x * x, axis=-1, keepdims=True) + EPS)


_DOT_DIMS = {"nn": (((1,), (0,)), ((), ())), "nt": (((1,), (1,)), ((), ())), "tn": (((0,), (0,)), ((), ()))}


def _matmul(a, b, mode, out_dtype, name, res=None, tm=None, tn=None, tk=None, b_shards=None, out_shards=None, carry=None,
            side=None):
    if mode == "nn":
        (M, K), N = a.shape, b.shape[-1] * (b_shards[1] if b_shards else 1)
    elif mode == "nt":
        (M, K), N = a.shape, b.shape[-2]
    else:
        (K, M), N = a.shape, b.shape[1]
    per = b.shape[-1] if b_shards else (out_shards[2] if out_shards else None)
    tm = _tile(M, tm or TILES["mm_m"], LANES)
    tn = _tile(per if (per and mode != "nt") else N, tn or TILES["mm_n"], LANES)
    tk = _tile(per if (per and mode == "nt") else K, tk or TILES["mm_k"], LANES)
    nk = K // tk
    has_res = res is not None
    has_carry = carry is not None
    dims = _DOT_DIMS[mode]

    def body(*refs):
        a_ref, b_ref = refs[0], refs[1]
        o_ref = refs[2 + has_res + has_carry]
        part = lax.dot_general(a_ref[...].astype(BF16), b_ref[...].astype(BF16), dims, preferred_element_type=F32)

        def finish(acc):
            if has_res:
                acc = acc + refs[2][...]
            o_ref[...] = acc.astype(o_ref.dtype)

        if nk == 1:
            finish(part)
        else:
            acc_ref = refs[3 + has_res + has_carry]
            k = pl.program_id(2)

            @pl.when(k == 0)
            def _():
                acc_ref[...] = part

            @pl.when(k > 0)
            def _():
                acc_ref[...] += part

            @pl.when(k == nk - 1)
            def _():
                finish(acc_ref[...])

    if mode == "nn":
        a_spec, b_spec = pl.BlockSpec((tm, tk), lambda i, j, k: (i, k)), pl.BlockSpec((tk, tn), lambda i, j, k: (k, j))
    elif mode == "nt":
        a_spec, b_spec = pl.BlockSpec((tm, tk), lambda i, j, k: (i, k)), pl.BlockSpec((tn, tk), lambda i, j, k: (j, k))
    else:
        a_spec, b_spec = pl.BlockSpec((tk, tm), lambda i, j, k: (k, i)), pl.BlockSpec((tk, tn), lambda i, j, k: (k, j))
    o_spec = pl.BlockSpec((tm, tn), lambda i, j, k: (i, j))
    out_shape = jax.ShapeDtypeStruct((M, N), out_dtype)
    if b_shards:
        first = b_shards[0]
        if mode == "nn":
            nps = per // tn
            b_spec = pl.BlockSpec((None, tk, tn), lambda i, j, k: (first + j // nps, k, j % nps))
        else:
            kps = per // tk
            b_spec = pl.BlockSpec((None, tn, tk), lambda i, j, k: (first + k // kps, j, k % kps))
    if out_shards:
        ofirst, nps_o = out_shards[0], per // tn
        o_spec_out = pl.BlockSpec((None, tm, tn), lambda i, j, k: (ofirst + j // nps_o, i, j % nps_o))
        out_shape = jax.ShapeDtypeStruct((out_shards[1], M, per), out_dtype)
    else:
        o_spec_out = o_spec
    in_specs, args = [a_spec, b_spec], [a, b]
    need = 2 * (_nbytes((tm, tk), a.dtype) + _nbytes((tk, tn), b.dtype) + _nbytes((tm, tn), out_dtype)) + 2 * _nbytes((tm, tn), F32)
    if has_res:
        in_specs.append(o_spec)
        args.append(res)
        need += 2 * _nbytes((tm, tn), res.dtype)
    aliases = {}
    if has_carry:
        aliases = {len(args): 0}
        in_specs.append(pl.BlockSpec(memory_space=pl.ANY))
        args.append(carry)
    scratch = [pltpu.VMEM((tm, tn), F32)] if nk > 1 else []
    grid = (M // tm, N // tn, nk)
    params = _params(("parallel", "parallel", "arbitrary") if side is None else ("arbitrary",) * 3, need + (4 << 20))
    if side is None:
        return pl.pallas_call(body, name=name, grid=grid, in_specs=in_specs, out_specs=o_spec_out, out_shape=out_shape,
                              scratch_shapes=scratch, input_output_aliases=aliases, compiler_params=params)(*args)
    body, s_ins, s_in_specs, s_shapes, s_out_specs, s_sems, s_aliases = _attach(side, body, len(args), 1, grid)
    return pl.pallas_call(
        body, name=name, grid=grid, in_specs=in_specs + s_in_specs, out_specs=[o_spec_out] + s_out_specs,
        out_shape=[out_shape] + s_shapes, scratch_shapes=scratch + s_sems, input_output_aliases={**aliases, **s_aliases},
        compiler_params=params,
    )(*args, *s_ins)


def _rms_fwd(x, g, name):
    T, D = x.shape
    tt = _tile(T, TILES["row"])

    def body(x_ref, g_ref, o_ref):
        xv = x_ref[...]
        o_ref[...] = (xv * _rstd(xv) * g_ref[...]).astype(o_ref.dtype)

    row = pl.BlockSpec((tt, D), lambda i: (i, 0))
    return pl.pallas_call(
        body, name=name, grid=(T // tt,), in_specs=[row, pl.BlockSpec((1, D), lambda i: (0, 0))], out_specs=row,
        out_shape=jax.ShapeDtypeStruct((T, D), BF16), compiler_params=_params(("parallel",), 8 * _nbytes((tt, D), F32)),
    )(x, g.reshape(1, D))


def _rms_bwd(dn, x, g, dres, name):
    T, D = x.shape
    tt = _tile(T, TILES["row"])

    def body(dn_ref, x_ref, g_ref, dres_ref, dx_ref, dxb_ref, dg_ref):
        i = pl.program_id(0)
        xv = x_ref[...]
        r = _rstd(xv)
        xh = xv * r
        dnv = dn_ref[...].astype(F32)
        dxh = dnv * g_ref[...]
        tot = dres_ref[...] + r * (dxh - xh * jnp.mean(dxh * xh, axis=-1, keepdims=True))
        dx_ref[...] = tot
        dxb_ref[...] = tot.astype(BF16)
        part = jnp.sum(dnv * xh, axis=0, keepdims=True)

        @pl.when(i == 0)
        def _():
            dg_ref[...] = part

        @pl.when(i > 0)
        def _():
            dg_ref[...] += part

    row = pl.BlockSpec((tt, D), lambda i: (i, 0))
    vec = pl.BlockSpec((1, D), lambda i: (0, 0))
    return pl.pallas_call(
        body, name=name, grid=(T // tt,), in_specs=[row, row, vec, row], out_specs=[row, row, vec],
        out_shape=[jax.ShapeDtypeStruct((T, D), F32), jax.ShapeDtypeStruct((T, D), BF16), jax.ShapeDtypeStruct((1, D), F32)],
        compiler_params=_params(("arbitrary",), 16 * _nbytes((tt, D), F32)),
    )(dn, x, g.reshape(1, D), dres)


def _final_loss(h, target, g):
    T, D = h.shape
    tt = _tile(T, TILES["row"])

    def body(h_ref, t_ref, g_ref, dx_ref, dxb_ref, dg_ref, loss_ref):
        i = pl.program_id(0)
        xv = h_ref[...]
        r = _rstd(xv)
        xh = xv * r
        gv = g_ref[...]
        err = xh * gv - t_ref[...]
        lpart = 0.5 * jnp.sum(jnp.mean(err * err, axis=-1, keepdims=True), axis=0, keepdims=True)
        dy = err * (1.0 / D)
        dxh = dy * gv
        dx = r * (dxh - xh * jnp.mean(dxh * xh, axis=-1, keepdims=True))
        dx_ref[...] = dx
        dxb_ref[...] = dx.astype(BF16)
        gpart = jnp.sum(dy * xh, axis=0, keepdims=True)
        lrow = jnp.broadcast_to(lpart, (1, LANES))

        @pl.when(i == 0)
        def _():
            dg_ref[...] = gpart
            loss_ref[...] = lrow

        @pl.when(i > 0)
        def _():
            dg_ref[...] += gpart
            loss_ref[...] += lrow

    row = pl.BlockSpec((tt, D), lambda i: (i, 0))
    vec = pl.BlockSpec((1, D), lambda i: (0, 0))
    return pl.pallas_call(
        body, name="final_loss", grid=(T // tt,), in_specs=[row, row, vec],
        out_specs=[row, row, vec, pl.BlockSpec((1, LANES), lambda i: (0, 0))],
        out_shape=[jax.ShapeDtypeStruct((T, D), F32), jax.ShapeDtypeStruct((T, D), BF16),
                   jax.ShapeDtypeStruct((1, D), F32), jax.ShapeDtypeStruct((1, LANES), F32)],
        compiler_params=_params(("arbitrary",), 16 * _nbytes((tt, D), F32)),
    )(h, target, g.reshape(1, D))


def _rope_tables(pos_col, inv_lane):
    T = pos_col.shape[0]
    tt = _tile(T, TILES["row"])

    def body(p_ref, f_ref, c_ref, s1_ref, s2_ref):
        ang = p_ref[...] * f_ref[...]
        lane = lax.broadcasted_iota(jnp.int32, ang.shape, 1)
        half = ROPE_DIM // 2
        cs, sn = jnp.cos(ang), jnp.sin(ang)
        c_ref[...] = jnp.where(lane < ROPE_DIM, cs, 0.0)
        s1_ref[...] = jnp.where(lane < half, -sn, 0.0)
        s2_ref[...] = jnp.where((lane >= half) & (lane < ROPE_DIM), sn, 0.0)

    tab = pl.BlockSpec((tt, LANES), lambda i: (i, 0))
    shp = jax.ShapeDtypeStruct((T, LANES), F32)
    return pl.pallas_call(
        body, name="rope_tables", grid=(T // tt,),
        in_specs=[pl.BlockSpec((tt, 1), lambda i: (i, 0)), pl.BlockSpec((1, LANES), lambda i: (0, 0))],
        out_specs=[tab, tab, tab], out_shape=[shp, shp, shp],
        compiler_params=_params(("parallel",), 32 * _nbytes((tt, LANES), F32)),
    )(pos_col, inv_lane)


def _rope(x, c, s1, s2):
    return x * c + pltpu.roll(x, LANES - ROPE_DIM // 2, 1) * s1 + pltpu.roll(x, ROPE_DIM // 2, 1) * s2


def _rope_t(d, c, s1, s2):
    return d * c + pltpu.roll(d * s1, ROPE_DIM // 2, 1) + pltpu.roll(d * s2, LANES - ROPE_DIM // 2, 1)


def _window_sum(xe, w, forward):
    n = xe.shape[0]
    s, sh = xe, 1
    while sh < w:
        s = s + pltpu.roll(s, (n - sh) if forward else sh, 0)
        sh *= 2
    return s


def _post_u(u, gq, gkv, tabs, dims):
    T, Dp = u.shape
    P, QL, KL, C = dims["P"], dims["QL"], dims["KL"], dims["C"]
    tt = _tile(T, TILES["row"], POOL_HALO)
    hb = tt // POOL_HALO

    def body(u_ref, halo_ref, gq_ref, gkv_ref, c_ref, s1_ref, s2_ref, diff_ref, cq_ref, ckv_ref, kr_ref):
        i = pl.program_id(0)
        t = i * tt + lax.broadcasted_iota(jnp.int32, (tt, 1), 0)
        halo = jnp.where(i > 0, halo_ref[...], 0.0)
        for gi, w in enumerate(POOL_WINDOWS):
            cols = slice(gi * C, (gi + 1) * C)
            xg = u_ref[:, cols]
            s = _window_sum(jnp.concatenate([halo[:, cols], xg], axis=0), w, False)[POOL_HALO:]
            cnt = jnp.minimum(t + 1, w).astype(F32)
            diff_ref[:, cols] = (s / cnt - xg).astype(BF16)
        cq = u_ref[:, P:P + QL]
        cq_ref[...] = (cq * _rstd(cq) * gq_ref[...]).astype(BF16)
        ckv = u_ref[:, P + QL:P + QL + KL]
        ckv_ref[...] = (ckv * _rstd(ckv) * gkv_ref[...]).astype(BF16)
        kr_ref[...] = _rope(u_ref[:, P + QL + KL:], c_ref[...], s1_ref[...], s2_ref[...]).astype(BF16)

    def row(w):
        return pl.BlockSpec((tt, w), lambda i: (i, 0))

    def vec(w):
        return pl.BlockSpec((1, w), lambda i: (0, 0))

    return pl.pallas_call(
        body, name="post_u", grid=(T // tt,),
        in_specs=[row(Dp), pl.BlockSpec((POOL_HALO, P), lambda i: (jnp.maximum(i * hb - 1, 0), 0)),
                  vec(QL), vec(KL), row(LANES), row(LANES), row(LANES)],
        out_specs=[row(P), row(QL), row(KL), row(LANES)],
        out_shape=[jax.ShapeDtypeStruct((T, P), BF16), jax.ShapeDtypeStruct((T, QL), BF16),
                   jax.ShapeDtypeStruct((T, KL), BF16), jax.ShapeDtypeStruct((T, LANES), BF16)],
        compiler_params=_params(("parallel",), 10 * _nbytes((tt, Dp), F32)),
    )(u, u, gq.reshape(1, QL), gkv.reshape(1, KL), *tabs)


def _pre_u_bwd(u, d_cqn, d_ckvn, d_diff, dkr, gq, gkv, tabs, dims):
    T, Dp = u.shape
    P, QL, KL, C, H = dims["P"], dims["QL"], dims["KL"], dims["C"], dims["H"]
    tt = _tile(T, TILES["row"], POOL_HALO)
    hb = tt // POOL_HALO
    n_t = T // tt

    def norm_bwd(xv, dn, gv):
        r = _rstd(xv)
        xh = xv * r
        dxh = dn * gv
        return r * (dxh - xh * jnp.mean(dxh * xh, axis=-1, keepdims=True)), jnp.sum(dn * xh, axis=0, keepdims=True)

    def body(u_ref, dcq_ref, dckv_ref, dd_ref, ddn_ref, dkr_ref, gq_ref, gkv_ref, c_ref, s1_ref, s2_ref,
             du_ref, dgq_ref, dgkv_ref):
        i = pl.program_id(0)
        t = i * tt + lax.broadcasted_iota(jnp.int32, (tt, 1), 0)
        nxt = jnp.where(i < n_t - 1, ddn_ref[...].astype(F32), 0.0)
        for gi, w in enumerate(POOL_WINDOWS):
            cols = slice(gi * C, (gi + 1) * C)
            dd = dd_ref[:, cols].astype(F32)
            e = dd / jnp.minimum(t + 1, w).astype(F32)
            s = _window_sum(jnp.concatenate([e, nxt[:, cols] / float(w)], axis=0), w, True)[:tt]
            du_ref[:, cols] = (s - dd).astype(BF16)
        dq, pq = norm_bwd(u_ref[:, P:P + QL], dcq_ref[...], gq_ref[...])
        du_ref[:, P:P + QL] = dq.astype(BF16)
        dkv, pkv = norm_bwd(u_ref[:, P + QL:P + QL + KL], dckv_ref[...], gkv_ref[...])
        du_ref[:, P + QL:P + QL + KL] = dkv.astype(BF16)
        dk = dkr_ref[0]
        for hh in range(1, H):
            dk = dk + dkr_ref[hh]
        du_ref[:, P + QL + KL:] = _rope_t(dk, c_ref[...], s1_ref[...], s2_ref[...]).astype(BF16)

        @pl.when(i == 0)
        def _():
            dgq_ref[...] = pq
            dgkv_ref[...] = pkv

        @pl.when(i > 0)
        def _():
            dgq_ref[...] += pq
            dgkv_ref[...] += pkv

    def row(w):
        return pl.BlockSpec((tt, w), lambda i: (i, 0))

    def vec(w):
        return pl.BlockSpec((1, w), lambda i: (0, 0))

    return pl.pallas_call(
        body, name="pre_u_bwd", grid=(n_t,),
        in_specs=[row(Dp), row(QL), row(KL), row(P),
                  pl.BlockSpec((POOL_HALO, P), lambda i: (jnp.minimum((i + 1) * hb, T // POOL_HALO - 1), 0)),
                  pl.BlockSpec((H, tt, LANES), lambda i: (0, i, 0)), vec(QL), vec(KL), row(LANES), row(LANES), row(LANES)],
        out_specs=[row(Dp), vec(QL), vec(KL)],
        out_shape=[jax.ShapeDtypeStruct((T, Dp), BF16), jax.ShapeDtypeStruct((1, QL), F32), jax.ShapeDtypeStruct((1, KL), F32)],
        compiler_params=_params(("arbitrary",), 12 * _nbytes((tt, Dp), F32)),
    )(u, d_cqn, d_ckvn, d_diff, d_diff, dkr, gq.reshape(1, QL), gkv.reshape(1, KL), *tabs)


def _pool_fwd(diff, pw, ps, dims):
    T, P = diff.shape
    G, C = len(POOL_WINDOWS), dims["C"]
    tt = _tile(T, TILES["row"])

    def body(d_ref, w_ref, s_ref, o_ref):
        for gi in range(G):
            cols = slice(gi * C, (gi + 1) * C)
            y = jnp.dot(d_ref[:, cols], w_ref[gi], preferred_element_type=F32)
            o_ref[:, cols] = (y * s_ref[:, cols]).astype(BF16)

    row = pl.BlockSpec((tt, P), lambda i: (i, 0))
    return pl.pallas_call(
        body, name="pool_fwd", grid=(T // tt,),
        in_specs=[row, pl.BlockSpec((G, C, C), lambda i: (0, 0, 0)), pl.BlockSpec((1, P), lambda i: (0, 0))],
        out_specs=row, out_shape=jax.ShapeDtypeStruct((T, P), BF16),
        compiler_params=_params(("parallel",), 8 * _nbytes((tt, P), F32)),
    )(diff, pw, ps.reshape(1, P))


def _pool_bwd(dmix, diff, pw, ps, dims):
    T, P = diff.shape
    G, C = len(POOL_WINDOWS), dims["C"]
    tt = _tile(T, TILES["row"])

    def body(dy_ref, d_ref, w_ref, s_ref, dd_ref, dw_ref, ds_ref):
        i = pl.program_id(0)

        @pl.when(i == 0)
        def _():
            dw_ref[...] = jnp.zeros_like(dw_ref)
            ds_ref[...] = jnp.zeros_like(ds_ref)

        for gi in range(G):
            cols = slice(gi * C, (gi + 1) * C)
            dy = dy_ref[:, cols].astype(F32)
            d = d_ref[:, cols]
            w = w_ref[gi]
            ypre = jnp.dot(d, w, preferred_element_type=F32)
            ds_ref[:, cols] += jnp.sum(dy * ypre, axis=0, keepdims=True)
            dyp = (dy * s_ref[:, cols]).astype(BF16)
            dd_ref[:, cols] = lax.dot_general(dyp, w, _DOT_DIMS["nt"], preferred_element_type=F32).astype(BF16)
            dw_ref[gi] += lax.dot_general(d, dyp, _DOT_DIMS["tn"], preferred_element_type=F32)

    row = pl.BlockSpec((tt, P), lambda i: (i, 0))
    wsp = pl.BlockSpec((G, C, C), lambda i: (0, 0, 0))
    vec = pl.BlockSpec((1, P), lambda i: (0, 0))
    return pl.pallas_call(
        body, name="pool_bwd", grid=(T // tt,), in_specs=[row, row, wsp, vec], out_specs=[row, wsp, vec],
        out_shape=[jax.ShapeDtypeStruct((T, P), BF16), jax.ShapeDtypeStruct((G, C, C), F32), jax.ShapeDtypeStruct((1, P), F32)],
        compiler_params=_params(("arbitrary",), 10 * _nbytes((tt, P), F32)),
    )(dmix, diff, pw, ps.reshape(1, P))


def _q_rope(qp, tabs, dims):
    T, W = qp.shape
    H = dims["H"]
    tt = _tile(T, TILES["row"])

    def body(q_ref, c_ref, s1_ref, s2_ref, o_ref):
        o_ref[:, :H * LANES] = (q_ref[:, :H * LANES] * ATT_SCALE).astype(BF16)
        c, s1, s2 = c_ref[...], s1_ref[...], s2_ref[...]
        for hh in range(H, 2 * H):
            cols = slice(hh * LANES, (hh + 1) * LANES)
            o_ref[:, cols] = _rope(q_ref[:, cols] * ATT_SCALE, c, s1, s2).astype(BF16)

    row = pl.BlockSpec((tt, W), lambda i: (i, 0))
    tab = pl.BlockSpec((tt, LANES), lambda i: (i, 0))
    return pl.pallas_call(
        body, name="q_rope", grid=(T // tt,), in_specs=[row, tab, tab, tab], out_specs=row,
        out_shape=jax.ShapeDtypeStruct((T, W), BF16), compiler_params=_params(("parallel",), 8 * _nbytes((tt, W), F32)),
    )(qp, *tabs)


def _scores(qn_ref, qr_ref, kn_ref, kr_ref, t, diagonal):
    q = jnp.concatenate([qn_ref[...], qr_ref[...]], axis=1)
    k = jnp.concatenate([kn_ref[...], kr_ref[...]], axis=1)
    s = lax.dot_general(q, k, _DOT_DIMS["nt"], preferred_element_type=F32)
    if diagonal:
        s = jnp.where(lax.broadcasted_iota(jnp.int32, (t, t), 0) >= lax.broadcasted_iota(jnp.int32, (t, t), 1), s, NEG_BIG)
    return q, k, s


class _Side:
    def __init__(self, ins, out_shapes, n_sems, start, finish, aliases=None):
        self.ins, self.out_shapes, self.n_sems, self.start, self.finish = list(ins), list(out_shapes), n_sems, start, finish
        self.aliases = dict(aliases or {})


def _attach(side, body, n_in, n_out, grid):
    if side is None:
        return body, [], [], [], [], [], {}
    n_si, n_so = len(side.ins), len(side.out_shapes)

    def carrying(*refs):
        outs_at = n_in + n_si
        main = refs[:n_in] + refs[outs_at:outs_at + n_out] + refs[outs_at + n_out + n_so:len(refs) - 2]
        parts = (refs[n_in:outs_at], refs[outs_at + n_out:outs_at + n_out + n_so], refs[-2], refs[-1])
        ids = [pl.program_id(d) for d in range(len(grid))]
        first = functools.reduce(lambda u, v: u & v, [i == 0 for i in ids])
        last = functools.reduce(lambda u, v: u & v, [i == g - 1 for i, g in zip(ids, grid)])

        @pl.when(first)
        def _():
            side.start(*parts)

        body(*main)

        @pl.when(last)
        def _():
            side.finish(*parts)

    aliases = {n_in + i: n_out + o for i, o in side.aliases.items()}
    return carrying, side.ins, [_HBM] * n_si, side.out_shapes, [_HBM] * n_so, _sem_pair(side.n_sems), aliases


def _flash_fwd(q_att, kv, kr, dims, side=None):
    T = q_att.shape[0]
    H = dims["H"]
    t = _tile(T, TILES["att"])
    n = T // t

    def body(qn_ref, qr_ref, kn_ref, v_ref, kr_ref, o_ref, lse_ref, m_ref, l_ref, acc_ref):
        i, j = pl.program_id(1), pl.program_id(2)

        @pl.when(j == 0)
        def _():
            m_ref[...] = jnp.full_like(m_ref, NEG_BIG)
            l_ref[...] = jnp.zeros_like(l_ref)
            acc_ref[...] = jnp.zeros_like(acc_ref)

        def step(diagonal):
            _, _, s = _scores(qn_ref, qr_ref, kn_ref, kr_ref, t, diagonal)
            m_prev = m_ref[...]
            m_new = jnp.maximum(m_prev, jnp.max(s, axis=1, keepdims=True))
            alpha = jnp.exp(m_prev - m_new)
            p = jnp.exp(s - m_new[:, :1])
            l_ref[...] = alpha * l_ref[...] + jnp.sum(p, axis=1, keepdims=True)
            acc_ref[...] = alpha * acc_ref[...] + jnp.dot(p.astype(BF16), v_ref[...], preferred_element_type=F32)
            m_ref[...] = m_new

        @pl.when(j < i)
        def _():
            step(False)

        @pl.when(j == i)
        def _():
            step(True)
            o_ref[...] = (acc_ref[...] / l_ref[...]).astype(BF16)
            lse_ref[...] = m_ref[...] + jnp.log(l_ref[...])

    blk = (t, LANES)
    grid = (H, n, n)
    body, s_ins, s_in_specs, s_shapes, s_out_specs, s_sems, aliases = _attach(side, body, 5, 2, grid)
    return pl.pallas_call(
        body, name="flash_fwd", grid=grid,
        in_specs=[pl.BlockSpec(blk, lambda h, i, j: (i, h)), pl.BlockSpec(blk, lambda h, i, j: (i, H + h)),
                  pl.BlockSpec(blk, lambda h, i, j: (jnp.minimum(j, i), h)),
                  pl.BlockSpec(blk, lambda h, i, j: (jnp.minimum(j, i), H + h)),
                  pl.BlockSpec(blk, lambda h, i, j: (jnp.minimum(j, i), 0))] + s_in_specs,
        out_specs=[pl.BlockSpec(blk, lambda h, i, j: (i, h)), pl.BlockSpec(blk, lambda h, i, j: (i, h))] + s_out_specs,
        out_shape=[jax.ShapeDtypeStruct((T, H * LANES), BF16), jax.ShapeDtypeStruct((T, H * LANES), F32)] + s_shapes,
        scratch_shapes=[pltpu.VMEM(blk, F32), pltpu.VMEM(blk, F32), pltpu.VMEM(blk, F32), *s_sems],
        input_output_aliases=aliases,
        compiler_params=_params(("arbitrary", "arbitrary", "arbitrary"), 8 * _nbytes((t, t), F32) + (8 << 20)),
    )(q_att, q_att, kv, kv, kr, *s_ins)


def _flash_bwd(q_att, kv, kr, o, lse, dmix, dims, side=None):
    T = q_att.shape[0]
    H = dims["H"]
    ob = dims["P"] // LANES
    t = _tile(T, TILES["att"])
    n = T // t

    def body(qn_ref, qr_ref, kn_ref, v_ref, kr_ref, o_ref, lse_ref, do_ref,
             dq_ref, dkn_ref, dv_ref, dkr_ref, dk_acc, dv_acc):
        j, i = pl.program_id(1), pl.program_id(2)

        @pl.when((j == 0) & (i == 0))
        def _():
            dq_ref[...] = jnp.zeros_like(dq_ref)

        @pl.when(i == 0)
        def _():
            dk_acc[...] = jnp.zeros_like(dk_acc)
            dv_acc[...] = jnp.zeros_like(dv_acc)

        def step(diagonal):
            q, k, s = _scores(qn_ref, qr_ref, kn_ref, kr_ref, t, diagonal)
            p = jnp.exp(s - lse_ref[:, :1])
            do = do_ref[...]
            delta = jnp.sum(do.astype(F32) * o_ref[...].astype(F32), axis=1, keepdims=True)
            dv_acc[...] += lax.dot_general(p.astype(BF16), do, _DOT_DIMS["tn"], preferred_element_type=F32)
            dp = lax.dot_general(do, v_ref[...], _DOT_DIMS["nt"], preferred_element_type=F32)
            ds = (p * (dp - delta)).astype(BF16)
            dk_acc[...] += lax.dot_general(ds, q, _DOT_DIMS["tn"], preferred_element_type=F32)
            rows = pl.ds(pl.multiple_of(i * t, t), t)
            dq_ref[rows, :] += jnp.dot(ds, k, preferred_element_type=F32)

        @pl.when(i > j)
        def _():
            step(False)

        @pl.when(i == j)
        def _():
            step(True)

        @pl.when(i == n - 1)
        def _():
            dkn_ref[...] = dk_acc[:, :LANES].astype(BF16)
            dkr_ref[...] = dk_acc[:, LANES:]
            dv_ref[...] = dv_acc[...].astype(BF16)

    blk = (t, LANES)
    grid = (H, n, n)
    body, s_ins, s_in_specs, s_shapes, s_out_specs, s_sems, aliases = _attach(side, body, 8, 4, grid)

    def qi(j, i):
        return jnp.maximum(i, j)

    return pl.pallas_call(
        body, name="flash_bwd", grid=grid, input_output_aliases=aliases,
        in_specs=[pl.BlockSpec(blk, lambda h, j, i: (qi(j, i), h)), pl.BlockSpec(blk, lambda h, j, i: (qi(j, i), H + h)),
                  pl.BlockSpec(blk, lambda h, j, i: (j, h)), pl.BlockSpec(blk, lambda h, j, i: (j, H + h)),
                  pl.BlockSpec(blk, lambda h, j, i: (j, 0)),
                  pl.BlockSpec(blk, lambda h, j, i: (qi(j, i), h)), pl.BlockSpec(blk, lambda h, j, i: (qi(j, i), h)),
                  pl.BlockSpec(blk, lambda h, j, i: (qi(j, i), ob + h))] + s_in_specs,
        out_specs=[pl.BlockSpec((None, T, 2 * LANES), lambda h, j, i: (h, 0, 0)),
                   pl.BlockSpec(blk, lambda h, j, i: (j, h)), pl.BlockSpec(blk, lambda h, j, i: (j, h)),
                   pl.BlockSpec((None, t, LANES), lambda h, j, i: (h, j, 0))] + s_out_specs,
        out_shape=[jax.ShapeDtypeStruct((H, T, 2 * LANES), F32), jax.ShapeDtypeStruct((T, H * LANES), BF16),
                   jax.ShapeDtypeStruct((T, H * LANES), BF16), jax.ShapeDtypeStruct((H, T, LANES), F32)] + s_shapes,
        scratch_shapes=[pltpu.VMEM((t, 2 * LANES), F32), pltpu.VMEM(blk, F32), *s_sems],
        compiler_params=_params(("arbitrary", "arbitrary", "arbitrary"),
                                12 * _nbytes((t, t), F32) + 2 * _nbytes((T, 2 * LANES), F32) + (8 << 20)),
    )(q_att, q_att, kv, kv, kr, o, lse, dmix, *s_ins)


def _dq_post(dq, tabs, dims):
    H, T, _ = dq.shape
    tt = _tile(T, TILES["row"])

    def body(dq_ref, c_ref, s1_ref, s2_ref, o_ref):
        c, s1, s2 = c_ref[...], s1_ref[...], s2_ref[...]
        for hh in range(H):
            o_ref[:, hh * LANES:(hh + 1) * LANES] = (dq_ref[hh, :, :LANES] * ATT_SCALE).astype(BF16)
            o_ref[:, (H + hh) * LANES:(H + hh + 1) * LANES] = _rope_t(dq_ref[hh, :, LANES:] * ATT_SCALE, c, s1, s2).astype(BF16)

    tab = pl.BlockSpec((tt, LANES), lambda i: (i, 0))
    return pl.pallas_call(
        body, name="dq_post", grid=(T // tt,),
        in_specs=[pl.BlockSpec((H, tt, 2 * LANES), lambda i: (0, i, 0)), tab, tab, tab],
        out_specs=pl.BlockSpec((tt, 2 * H * LANES), lambda i: (i, 0)),
        out_shape=jax.ShapeDtypeStruct((T, 2 * H * LANES), BF16),
        compiler_params=_params(("parallel",), 8 * _nbytes((tt, 2 * H * LANES), F32)),
    )(dq, *tabs)


def _conv3(ge, cw, n):
    return cw[2:3] * ge + cw[1:2] * pltpu.roll(ge, 1, 0) + cw[0:1] * pltpu.roll(ge, 2, 0) + cw[3:4]


def _ffn_fwd(gate, up, cw8):
    T, F = gate.shape
    tt = _tile(T, TILES["ffn_row"])
    tc = _tile(F, TILES["ffn_c"], LANES)
    hb = tt // CONV_HALO

    def body(g_ref, gp_ref, u_ref, cw_ref, a_ref):
        it = pl.program_id(1)
        prev = jnp.where(it > 0, gp_ref[...].astype(F32), 0.0)
        ge = jnp.concatenate([prev, g_ref[...].astype(F32)], axis=0)
        gc = _conv3(ge, cw_ref[...], tt + CONV_HALO)[CONV_HALO:]
        a_ref[...] = (gc * _sigmoid(gc) * u_ref[...].astype(F32)).astype(BF16)

    blk = pl.BlockSpec((tt, tc), lambda jc, it: (it, jc))
    return pl.pallas_call(
        body, name="ffn_fwd", grid=(F // tc, T // tt),
        in_specs=[blk, pl.BlockSpec((CONV_HALO, tc), lambda jc, it: (jnp.maximum(it * hb - 1, 0), jc)), blk,
                  pl.BlockSpec((8, tc), lambda jc, it: (0, jc))],
        out_specs=blk, out_shape=jax.ShapeDtypeStruct((T, F), BF16),
        compiler_params=_params(("parallel", "parallel"), 16 * _nbytes((tt, tc), F32)),
    )(gate, gate, up, cw8)


def _ffn_bwd(da, gate, up, cw8):
    T, F = gate.shape
    tt = _tile(T, TILES["ffn_row"])
    tc = _tile(F, TILES["ffn_c"], LANES)
    hb = tt // CONV_HALO
    n_t = T // tt
    n = tt + 2 * CONV_HALO

    def body(da_ref, dan_ref, g_ref, gp_ref, gn_ref, u_ref, un_ref, cw_ref, dg_ref, du_ref, dcw_ref):
        it = pl.program_id(1)
        first, last = it == 0, it == n_t - 1
        cw = cw_ref[...]
        zeros = jnp.zeros((CONV_HALO, tc), F32)
        ge = jnp.concatenate([jnp.where(first, 0.0, gp_ref[...].astype(F32)), g_ref[...].astype(F32),
                              gn_ref[...].astype(F32)], axis=0)
        dae = jnp.concatenate([zeros, da_ref[...].astype(F32), jnp.where(last, 0.0, dan_ref[...].astype(F32))], axis=0)
        ue = jnp.concatenate([zeros, u_ref[...].astype(F32), un_ref[...].astype(F32)], axis=0)
        g1, g2 = pltpu.roll(ge, 1, 0), pltpu.roll(ge, 2, 0)
        gc = cw[2:3] * ge + cw[1:2] * g1 + cw[0:1] * g2 + cw[3:4]
        sg = _sigmoid(gc)
        dgc = dae * ue * (sg * (1.0 + gc * (1.0 - sg)))
        du_ref[...] = (dae * gc * sg)[CONV_HALO:CONV_HALO + tt].astype(BF16)
        dgp = cw[2:3] * dgc + cw[1:2] * pltpu.roll(dgc, n - 1, 0) + cw[0:1] * pltpu.roll(dgc, n - 2, 0)
        dg_ref[...] = dgp[CONV_HALO:CONV_HALO + tt].astype(BF16)
        mid = slice(CONV_HALO, CONV_HALO + tt)
        d_mid = dgc[mid]
        part = jnp.concatenate([jnp.sum(d_mid * g2[mid], axis=0, keepdims=True), jnp.sum(d_mid * g1[mid], axis=0, keepdims=True),
                                jnp.sum(d_mid * ge[mid], axis=0, keepdims=True), jnp.sum(d_mid, axis=0, keepdims=True),
                                jnp.zeros((4, tc), F32)], axis=0)

        @pl.when(first)
        def _():
            dcw_ref[...] = part

        @pl.when(it > 0)
        def _():
            dcw_ref[...] += part

    blk = pl.BlockSpec((tt, tc), lambda jc, it: (it, jc))
    prv = pl.BlockSpec((CONV_HALO, tc), lambda jc, it: (jnp.maximum(it * hb - 1, 0), jc))
    nxt = pl.BlockSpec((CONV_HALO, tc), lambda jc, it: (jnp.minimum((it + 1) * hb, T // CONV_HALO - 1), jc))
    cws = pl.BlockSpec((8, tc), lambda jc, it: (0, jc))
    return pl.pallas_call(
        body, name="ffn_bwd", grid=(F // tc, n_t), in_specs=[blk, nxt, blk, prv, nxt, blk, nxt, cws],
        out_specs=[blk, blk, cws],
        out_shape=[jax.ShapeDtypeStruct((T, F), BF16), jax.ShapeDtypeStruct((T, F), BF16), jax.ShapeDtypeStruct((8, F), F32)],
        compiler_params=_params(("parallel", "arbitrary"), 32 * _nbytes((tt, tc), F32)),
    )(da, da, gate, gate, gate, up, up, cw8)


def _ple_fwd(h2, gl, pe):
    T, D = h2.shape
    tt = _tile(T, TILES["row"])

    def body(h_ref, gl_ref, pe_ref, o_ref):
        o_ref[...] = h_ref[...] + pe_ref[...] * _sigmoid(gl_ref[...])

    row = pl.BlockSpec((tt, D), lambda i: (i, 0))
    return pl.pallas_call(
        body, name="ple_fwd", grid=(T // tt,), in_specs=[row, row, row], out_specs=row,
        out_shape=jax.ShapeDtypeStruct((T, D), F32), compiler_params=_params(("parallel",), 12 * _nbytes((tt, D), F32)),
    )(h2, gl, pe)


def _ple_bwd(dh, gl, pe):
    T, D = dh.shape
    tt = _tile(T, TILES["row"])

    def body(dh_ref, gl_ref, pe_ref, dpe_ref, dgl_ref):
        d = dh_ref[...]
        sg = _sigmoid(gl_ref[...])
        dpe_ref[...] = (d * sg).astype(BF16)
        dgl_ref[...] = (d * pe_ref[...] * (sg * (1.0 - sg))).astype(BF16)

    row = pl.BlockSpec((tt, D), lambda i: (i, 0))
    return pl.pallas_call(
        body, name="ple_bwd", grid=(T // tt,), in_specs=[row, row, row], out_specs=[row, row],
        out_shape=[jax.ShapeDtypeStruct((T, D), BF16), jax.ShapeDtypeStruct((T, D), BF16)],
        compiler_params=_params(("parallel",), 12 * _nbytes((tt, D), F32)),
    )(dh, gl, pe)


def _adamw(w, g, m, v, name):
    shape = w.shape
    cols = shape[-1]
    rows = math.prod(shape[:-1]) if len(shape) > 1 else 1
    w2, g2, m2, v2 = (a.reshape(rows, cols) for a in (w, g, m, v))
    tr = _tile(rows, max(8, (1 << 20) // (cols * 4)))
    c1 = 1.0 - ADAM_B1 ** ADAM_STEP
    c2 = 1.0 - ADAM_B2 ** ADAM_STEP

    def body(w_ref, g_ref, m_ref, v_ref, d_ref, mo_ref, vo_ref):
        gv = g_ref[...]
        mn = ADAM_B1 * m_ref[...] + (1.0 - ADAM_B1) * gv
        vn = ADAM_B2 * v_ref[...] + (1.0 - ADAM_B2) * (gv * gv)
        mo_ref[...] = mn
        vo_ref[...] = vn
        d_ref[...] = -ADAM_LR * ((mn / c1) / (jnp.sqrt(vn / c2) + ADAM_EPS) + ADAM_WD * w_ref[...])

    blk = pl.BlockSpec((tr, cols), lambda i: (i, 0))
    shp = jax.ShapeDtypeStruct((rows, cols), F32)
    outs = pl.pallas_call(
        body, name=name, grid=(rows // tr,), in_specs=[blk] * 4, out_specs=[blk] * 3, out_shape=[shp] * 3,
        compiler_params=_params(("parallel",), 16 * _nbytes((tr, cols), F32)),
    )(w2, g2, m2, v2)
    return tuple(o.reshape(shape) for o in outs)


_HBM = pl.BlockSpec(memory_space=pltpu.HBM)


def _place():
    x, y, c = lax.axis_index("x"), lax.axis_index("y"), lax.axis_index("c")
    return x, y, c, [(1 - x, y), (x, 1 - y), (1 - x, 1 - y)]


def _remote(src, dst, send_sems, recv_sems, k, to):
    return pltpu.make_async_remote_copy(src_ref=src, dst_ref=dst, send_sem=send_sems.at[k], recv_sem=recv_sems.at[k],
                                        device_id=to, device_id_type=MESH)


def _half(ref, lead, h):
    hr = ref.shape[-2] // 2
    return ref.at[(*lead, pl.ds(pl.multiple_of(h * hr, SUBLANES_BF16), hr))]


def _sem_pair(n):
    return [pltpu.SemaphoreType.DMA((n,)), pltpu.SemaphoreType.DMA((n,))]


def _run_side(side, name):
    n_in, n_out = len(side.ins), len(side.out_shapes)

    def body(*refs):
        parts = (refs[:n_in], refs[n_in:n_in + n_out]) + tuple(refs[n_in + n_out:])
        side.start(*parts)
        side.finish(*parts)

    return pl.pallas_call(
        body, name=name, in_specs=[_HBM] * n_in, out_specs=[_HBM] * n_out, out_shape=side.out_shapes,
        scratch_shapes=_sem_pair(side.n_sems),
    )(*side.ins)


def _whole(arrs, halves):
    return [(a, 0, arr.shape[-2] // (2 if halves else 1)) for a, arr in enumerate(arrs)]


def _plan(arrs, halves, big):
    whole = _whole(arrs, halves)
    q = whole[big][2] // 4
    return [[pc for pc in whole if pc[0] != big] + [(big, 0, q)]] + [[(big, k * q, q)] for k in (1, 2, 3)]


def _ride(fn, n_main, pieces, make, store):
    if pieces is None:
        return fn(None)
    side, touched = make(pieces)
    out = fn(side)
    store.update(zip(touched, out[n_main:]))
    return out[0] if n_main == 1 else out[:n_main]


def _carried(arrs, pieces, prior):
    touched = sorted({a for a, _, _ in pieces})
    pos = {a: i for i, a in enumerate(touched)}
    carried = [a for a in touched if a in prior]
    ins = [arrs[a] for a in touched] + [prior[a] for a in carried]
    return touched, pos, ins, {len(touched) + i: pos[a] for i, a in enumerate(carried)}


def _gather_side(arrs, layer, pieces, prior):
    touched, pos, ins_arrs, aliases = _carried(arrs, pieces, prior)

    def copies(ins, outs, send_sems, recv_sems, arriving):
        x, y, c, chips = _place()
        me, sib = 2 * x + y, (x, y, 1 - c)
        out = []
        for p, (a, r0, nr) in enumerate(pieces):
            src, dst = ins[pos[a]], outs[pos[a]]
            hr = src.shape[-2] // 2
            for hlf in range(2):
                rows = pl.ds(hlf * hr + r0, nr)
                out.append(_remote(src.at[layer, rows], dst.at[me, rows], send_sems, recv_sems, 5 * p + 3 + hlf, sib))
            rows = pl.ds(pl.multiple_of(c * hr + r0, SUBLANES_BF16), nr)
            for k, (cx, cy) in enumerate(chips):
                slot = 2 * cx + cy if arriving else me
                out.append(_remote(src.at[layer, rows], dst.at[slot, rows], send_sems, recv_sems, 5 * p + k, (cx, cy, c)))
        return out

    def start(ins, outs, send_sems, recv_sems):
        for cp in copies(ins, outs, send_sems, recv_sems, False):
            cp.start()

    def finish(ins, outs, send_sems, recv_sems):
        for cp in copies(ins, outs, send_sems, recv_sems, True):
            cp.wait_recv()
        for cp in copies(ins, outs, send_sems, recv_sems, False):
            cp.wait_send()

    shapes = [jax.ShapeDtypeStruct((N_SHARDS,) + arrs[a].shape[1:], arrs[a].dtype) for a in touched]
    return _Side(ins_arrs, shapes, 5 * len(pieces), start, finish, aliases), touched


def _gather_forward(arrs):
    n = len(arrs)

    def body(*refs):
        outs, send_sems, recv_sems = refs[n:2 * n], refs[2 * n], refs[2 * n + 1]
        x, y, c, chips = _place()
        sib = (x, y, 1 - c)
        sends = []
        for a in range(n):
            for k, (cx, cy) in enumerate(chips):
                got = _half(outs[a], (2 * cx + cy,), c)
                sends.append(_remote(got, got, send_sems, recv_sems, 3 * a + k, sib))
        for cp in sends:
            cp.start()
        for a in range(n):
            for k, (cx, cy) in enumerate(chips):
                got = _half(outs[a], (2 * cx + cy,), 1 - c)
                _remote(got, got, send_sems, recv_sems, 3 * a + k, sib).wait_recv()
        for cp in sends:
            cp.wait_send()

    return pl.pallas_call(
        body, name="gather_forward", in_specs=[_HBM] * n, out_specs=[_HBM] * n,
        out_shape=[jax.ShapeDtypeStruct(a.shape, a.dtype) for a in arrs],
        input_output_aliases={a: a for a in range(n)}, scratch_shapes=_sem_pair(3 * n),
    )(*arrs)


def _sibling_exchange(arrs):
    n = len(arrs)

    def body(*refs):
        ins, outs, send_sems, recv_sems = refs[:n], refs[n:2 * n], refs[2 * n], refs[2 * n + 1]
        x, y, c, _ = _place()
        sib = (x, y, 1 - c)
        sends = [_remote(_half(ins[a], (s,), 1 - c), outs[a].at[s], send_sems, recv_sems, N_SHARDS * a + s, sib)
                 for a in range(n) for s in range(N_SHARDS)]
        for cp in sends:
            cp.start()
        for a in range(n):
            for s in range(N_SHARDS):
                _remote(_half(ins[a], (s,), c), outs[a].at[s], send_sems, recv_sems, N_SHARDS * a + s, sib).wait_recv()
        for cp in sends:
            cp.wait_send()

    return pl.pallas_call(
        body, name="rs_sibling", in_specs=[_HBM] * n, out_specs=[_HBM] * n,
        out_shape=[jax.ShapeDtypeStruct((N_SHARDS, a.shape[1] // 2, a.shape[2]), a.dtype) for a in arrs],
        scratch_shapes=_sem_pair(N_SHARDS * n),
    )(*arrs)


def _chip_side(arrs, pieces, prior):
    touched, pos, ins_arrs, aliases = _carried(arrs, pieces, prior)

    def copies(ins, outs, send_sems, recv_sems):
        x, y, c, chips = _place()
        return [_remote(ins[pos[a]].at[2 * cx + cy, pl.ds(r0, nr)], outs[pos[a]].at[k, pl.ds(r0, nr)], send_sems, recv_sems,
                        3 * p + k, (cx, cy, c))
                for p, (a, r0, nr) in enumerate(pieces) for k, (cx, cy) in enumerate(chips)]

    def start(ins, outs, send_sems, recv_sems):
        for cp in copies(ins, outs, send_sems, recv_sems):
            cp.start()

    def finish(ins, outs, send_sems, recv_sems):
        for cp in copies(ins, outs, send_sems, recv_sems):
            cp.wait_recv()
        for cp in copies(ins, outs, send_sems, recv_sems):
            cp.wait_send()

    shapes = [jax.ShapeDtypeStruct((3,) + arrs[a].shape[1:], arrs[a].dtype) for a in touched]
    return _Side(ins_arrs, shapes, 3 * len(pieces), start, finish, aliases), touched


def _sibling_share(arrs):
    n = len(arrs)

    def body(*refs):
        outs, send_sems, recv_sems = refs[n:2 * n], refs[2 * n], refs[2 * n + 1]
        x, y, c, _ = _place()
        sib = (x, y, 1 - c)
        sends = [_remote(outs[a].at[c], outs[a].at[c], send_sems, recv_sems, a, sib) for a in range(n)]
        for cp in sends:
            cp.start()
        for a in range(n):
            _remote(outs[a].at[c], outs[a].at[1 - c], send_sems, recv_sems, a, sib).wait_recv()
        for cp in sends:
            cp.wait_send()

    return pl.pallas_call(
        body, name="rs_share", in_specs=[_HBM] * n, out_specs=[_HBM] * n,
        out_shape=[jax.ShapeDtypeStruct(a.shape, a.dtype) for a in arrs],
        input_output_aliases={a: a for a in range(n)}, scratch_shapes=_sem_pair(n),
    )(*arrs)


def _add_sibling(g, sib_in, place):
    S, rows, cols = g.shape
    hr = rows // 2
    tr = _tile(hr, max(SUBLANES_BF16, TILES["add_bytes"] // (cols * 2)), SUBLANES_BF16)
    nb = hr // tr

    def body(p_ref, a_ref, b_ref, o_ref):
        o_ref[...] = (a_ref[...].astype(F32) + b_ref[...].astype(F32)).astype(o_ref.dtype)

    blk = pl.BlockSpec((None, tr, cols), lambda s, r, p: (s, r, 0))
    return pl.pallas_call(
        body, name="rs_add_sibling",
        grid_spec=pltpu.PrefetchScalarGridSpec(
            num_scalar_prefetch=1, grid=(S, nb),
            in_specs=[pl.BlockSpec((None, tr, cols), lambda s, r, p: (s, p[1] * nb + r, 0)), blk], out_specs=blk),
        out_shape=jax.ShapeDtypeStruct((S, hr, cols), g.dtype),
        compiler_params=_params(("parallel", "parallel"), 16 * _nbytes((tr, cols), F32)),
    )(place, g, sib_in)


def _add_chips(cs, got, place):
    S, r, cols = cs.shape
    tr = _tile(r, max(SUBLANES_BF16, TILES["add_bytes"] // (cols * 2)), SUBLANES_BF16)

    def body(p_ref, a_ref, b_ref, o_ref):
        acc = a_ref[...].astype(F32)
        for k in range(3):
            acc = acc + b_ref[k].astype(F32)
        o_ref[...] = acc

    return pl.pallas_call(
        body, name="rs_add_chips",
        grid_spec=pltpu.PrefetchScalarGridSpec(
            num_scalar_prefetch=1, grid=(r // tr,),
            in_specs=[pl.BlockSpec((None, tr, cols), lambda i, p: (p[0], i, 0)),
                      pl.BlockSpec((3, tr, cols), lambda i, p: (0, i, 0))],
            out_specs=pl.BlockSpec((None, tr, cols), lambda i, p: (p[1], i, 0))),
        out_shape=jax.ShapeDtypeStruct((2, r, cols), F32),
        compiler_params=_params(("parallel",), 24 * _nbytes((tr, cols), F32)),
    )(place, cs, got)


def _reduce_begin(arrs, place):
    return [_add_sibling(g, s, place) for g, s in zip(arrs, _sibling_exchange(arrs))]


def _reduce_end(sums, got, place):
    halves = [_add_chips(cs, g, place) for cs, g in zip(sums, got)]
    return [f.reshape(-1, f.shape[-1]) for f in _sibling_share(halves)]


def _all_reduce_small(v):
    R = v.shape[0]

    def body(v_ref, o_ref, buf, send_sems, recv_sems):
        x, y, c, _ = _place()
        me = 4 * x + 2 * y + c
        buf[me] = v_ref[...]
        sends = []
        for k in range(1, 8):
            px = 1 - x if k & 4 else x
            py = 1 - y if k & 2 else y
            pc = 1 - c if k & 1 else c
            sends.append(_remote(v_ref, buf.at[me], send_sems, recv_sems, k - 1, (px, py, pc)))
        for cp in sends:
            cp.start()
        for k in range(1, 8):
            px = 1 - x if k & 4 else x
            py = 1 - y if k & 2 else y
            pc = 1 - c if k & 1 else c
            _remote(v_ref, buf.at[4 * px + 2 * py + pc], send_sems, recv_sems, k - 1, (px, py, pc)).wait_recv()
        for cp in sends:
            cp.wait_send()
        acc = buf[0]
        for d in range(1, 8):
            acc = acc + buf[d]
        o_ref[...] = acc

    vm = pl.BlockSpec(memory_space=pltpu.VMEM)
    return pl.pallas_call(
        body, name="all_reduce_small", in_specs=[vm], out_specs=vm, out_shape=jax.ShapeDtypeStruct(v.shape, F32),
        scratch_shapes=[pltpu.VMEM((8, R, LANES), F32), pltpu.SemaphoreType.DMA((7,)), pltpu.SemaphoreType.DMA((7,))],
    )(v)


def _pad_to(a, n):
    return a if a.shape[0] == n else jnp.pad(a, (0, n - a.shape[0]))


def _piece_len(shape):
    return -(-math.prod(shape) // PACK_ALIGN) * PACK_ALIGN


def _pack(pieces, dtype):
    flat = jnp.concatenate([_pad_to(a.reshape(-1).astype(dtype), _piece_len(a.shape)) for a in pieces])
    return flat.reshape(-1, LANES)


def _unpack(flat, shapes, lead):
    flat = flat.reshape(lead + (-1,))
    out, off = [], 0
    for shp in shapes:
        out.append(flat[..., off:off + math.prod(shp)].reshape(lead + tuple(shp)))
        off += _piece_len(shp)
    return out


def _join(name, a):
    if name in COL_SHARDED:
        return a.transpose(1, 0, 2).reshape(a.shape[1], -1)
    if name in ROW_SHARDED:
        return a.reshape(-1, a.shape[-1])
    return a.transpose(1, 0, 2, 3).reshape(a.shape[1], -1, a.shape[-1])


def _split(name, a):
    if name in COL_SHARDED:
        return a.reshape(a.shape[0], N_SHARDS, -1).transpose(1, 0, 2)
    if name in ROW_SHARDED:
        return a.reshape(N_SHARDS, -1, a.shape[-1])
    return a.reshape(a.shape[0], N_SHARDS, -1, a.shape[-1]).transpose(1, 0, 2, 3)


def _heads_split(w, H, first, second, pad_second):
    K = w.shape[0]
    w3 = w.reshape(K, H, first + second)
    b = w3[:, :, first:]
    if pad_second > second:
        b = jnp.pad(b, ((0, 0), (0, 0), (0, pad_second - second)))
    return jnp.concatenate([w3[:, :, :first].reshape(K, -1), b.reshape(K, -1)], axis=1)


def _heads_merge(w, H, first, second, pad_second):
    K = w.shape[0]
    a = w[:, :H * first].reshape(K, H, first)
    b = w[:, H * first:].reshape(K, H, pad_second)[:, :, :second]
    return jnp.concatenate([a, b], axis=2).reshape(K, -1)


def kernel(x, p, positions, norm_mix_g, w_in, pool_w, pool_scale, q_norm_g, w_uq, kv_norm_g, w_ukv, w_out, norm_ffn_g, w_up, conv_w, conv_b, w_down, norm_ple_g, w_ple, w_ple_gate, final_norm_g, loss_target, m_norm_mix_g, m_w_in, m_pool_w, m_pool_scale, m_q_norm_g, m_w_uq, m_kv_norm_g, m_w_ukv, m_w_out, m_norm_ffn_g, m_w_up, m_conv_w, m_conv_b, m_w_down, m_norm_ple_g, m_w_ple, m_w_ple_gate, m_final_norm_g, v_norm_mix_g, v_w_in, v_pool_w, v_pool_scale, v_q_norm_g, v_w_uq, v_kv_norm_g, v_w_ukv, v_w_out, v_norm_ffn_g, v_w_up, v_conv_w, v_conv_b, v_w_down, v_norm_ple_g, v_w_ple, v_w_ple_gate, v_final_norm_g):
    W = dict(norm_mix_g=norm_mix_g, w_in=w_in, pool_w=pool_w, pool_scale=pool_scale, q_norm_g=q_norm_g, w_uq=w_uq,
             kv_norm_g=kv_norm_g, w_ukv=w_ukv, w_out=w_out, norm_ffn_g=norm_ffn_g, w_up=w_up, conv_w=conv_w, conv_b=conv_b,
             w_down=w_down, norm_ple_g=norm_ple_g, w_ple=w_ple, w_ple_gate=w_ple_gate, final_norm_g=final_norm_g)
    M1 = dict(norm_mix_g=m_norm_mix_g, w_in=m_w_in, pool_w=m_pool_w, pool_scale=m_pool_scale, q_norm_g=m_q_norm_g, w_uq=m_w_uq,
              kv_norm_g=m_kv_norm_g, w_ukv=m_w_ukv, w_out=m_w_out, norm_ffn_g=m_norm_ffn_g, w_up=m_w_up, conv_w=m_conv_w,
              conv_b=m_conv_b, w_down=m_w_down, norm_ple_g=m_norm_ple_g, w_ple=m_w_ple, w_ple_gate=m_w_ple_gate,
              final_norm_g=m_final_norm_g)
    M2 = dict(norm_mix_g=v_norm_mix_g, w_in=v_w_in, pool_w=v_pool_w, pool_scale=v_pool_scale, q_norm_g=v_q_norm_g, w_uq=v_w_uq,
              kv_norm_g=v_kv_norm_g, w_ukv=v_w_ukv, w_out=v_w_out, norm_ffn_g=v_norm_ffn_g, w_up=v_w_up, conv_w=v_conv_w,
              conv_b=v_conv_b, w_down=v_w_down, norm_ple_g=v_norm_ple_g, w_ple=v_w_ple, w_ple_gate=v_w_ple_gate,
              final_norm_g=v_final_norm_g)

    _, T, D = x.shape
    L = p.shape[0]
    P, QL, KL, F = pool_scale.shape[-1], q_norm_g.shape[-1], kv_norm_g.shape[-1], conv_b.shape[-1]
    C = pool_w.shape[-1]
    H = (D - P) // V_DIM
    d_in = P + QL + KL + ROPE_DIM
    dims = dict(P=P, QL=QL, KL=KL, C=C, H=H)
    misc_shapes = [W[n].shape[1:] for n in MISC]
    ns_in, ns_up, ns_conv = w_in.shape[-1], w_up.shape[-1], conv_w.shape[-1]

    xi, yi, ci = lax.axis_index("x"), lax.axis_index("y"), lax.axis_index("c")
    me = 2 * xi + yi
    place = jnp.stack([me, ci]).astype(jnp.int32)

    def all_reduce(parts):
        flat = jnp.concatenate(parts)
        padded = -(-flat.shape[0] // (8 * LANES)) * (8 * LANES)
        return _all_reduce_small(_pad_to(flat, padded).reshape(-1, LANES)).reshape(-1)

    inv_freq = 1.0 / (ROPE_THETA ** (jnp.arange(0, ROPE_DIM, 2, dtype=F32) / ROPE_DIM))
    inv_lane = jnp.concatenate([inv_freq, inv_freq, jnp.zeros((LANES - ROPE_DIM,), F32)]).reshape(1, LANES)
    tabs = _rope_tables(positions.reshape(T, 1).astype(F32), inv_lane)

    local = [W[n].astype(BF16) for n in BIG] + [jnp.stack([_pack([W[n][l] for n in MISC], BF16) for l in range(L)])]
    placed = lax.dynamic_update_slice(jnp.zeros((L, CONV_TAPS, F), F32), conv_w, (0, 0, me * ns_conv))
    conv_full = all_reduce([jnp.where(ci == 0, placed, 0.0).reshape(-1)])[:L * CONV_TAPS * F].reshape(L, CONV_TAPS, F)

    def layout(got, l):
        g = dict(zip(BIG, got[:-1]))
        misc = {n: _join(n, a) for n, a in zip(MISC, _unpack(got[-1], misc_shapes, (N_SHARDS,)))}
        return dict(
            w_in=jnp.concatenate([g["w_in"][sh] for sh in range(N_SHARDS)] + [jnp.zeros((D, LANES - ROPE_DIM), BF16)], axis=1),
            w_out=g["w_out"].reshape(-1, D), w_down=g["w_down"].reshape(-1, D), w_ple_gate=g["w_ple_gate"].reshape(-1, D),
            w_up=g["w_up"], w_ple=misc["w_ple"], pool_w=misc["pool_w"],
            w_uq=_heads_split(misc["w_uq"], H, NOPE_DIM, ROPE_DIM, LANES),
            w_ukv=_heads_split(misc["w_ukv"], H, NOPE_DIM, V_DIM, V_DIM),
            cw8=jnp.concatenate([conv_full[l], conv_b[l][None], jnp.zeros((4, F), F32)], axis=0))

    half_up = (0, N_SHARDS // 2), (N_SHARDS // 2, N_SHARDS // 2)

    h = x[0]
    saved, FW = [], []
    up_at = BIG.index("w_up")
    arriving = _gather_forward(_run_side(_gather_side(local, 0, _whole(local, True), {})[0], "all_gather"))
    for l in range(L):
        fw = layout(arriving, l)
        FW.append(fw)
        s = dict(h0=h)
        nxt = {}
        parts = _plan(local, True, up_at) if l + 1 < L else [None] * 4

        def gather(pieces):
            return _gather_side(local, l + 1, pieces, nxt)
        s["n1"] = _rms_fwd(h, norm_mix_g[l], "norm_mix")
        s["u"] = _matmul(s["n1"], fw["w_in"], "nn", F32, "mm_in", tm=512, tn=d_in + LANES - ROPE_DIM)
        s["diff"], s["cqn"], s["ckvn"], s["kr"] = _post_u(s["u"], q_norm_g[l], kv_norm_g[l], tabs, dims)
        s["q"] = _q_rope(_matmul(s["cqn"], fw["w_uq"], "nn", F32, "mm_uq", tn=2 * H * LANES), tabs, dims)
        s["kv"] = _matmul(s["ckvn"], fw["w_ukv"], "nn", BF16, "mm_ukv", tn=2 * H * LANES)
        s["o"], s["lse"] = _ride(lambda sd: _flash_fwd(s["q"], s["kv"], s["kr"], dims, sd), 2, parts[0], gather, nxt)
        s["mix"] = jnp.concatenate([_pool_fwd(s["diff"], fw["pool_w"], pool_scale[l], dims), s["o"]], axis=1)
        s["h1"] = _matmul(s["mix"], fw["w_out"], "nn", F32, "mm_out", res=h, tm=512)
        s["n2"] = _rms_fwd(s["h1"], norm_ffn_g[l], "norm_ffn")
        s["gate"] = _ride(lambda sd: _matmul(s["n2"], fw["w_up"], "nn", BF16, "mm_gate", tm=512, tn=ns_up // 2, b_shards=half_up[0],
                                             side=sd), 1, parts[1], gather, nxt)
        s["up"] = _ride(lambda sd: _matmul(s["n2"], fw["w_up"], "nn", BF16, "mm_up", tm=512, tn=ns_up // 2, b_shards=half_up[1],
                                           side=sd), 1, parts[2], gather, nxt)
        s["a"] = _ffn_fwd(s["gate"], s["up"], fw["cw8"])
        s["h2"] = _ride(lambda sd: _matmul(s["a"], fw["w_down"], "nn", F32, "mm_down", res=s["h1"], tm=512,
                                           tk=F // 2 if F % (2 * LANES) == 0 else F, side=sd), 1, parts[3], gather, nxt)
        if nxt:
            arriving = _gather_forward([nxt[a] for a in range(len(local))])
        s["n3"] = _rms_fwd(s["h2"], norm_ple_g[l], "norm_ple")
        s["gl"] = _matmul(s["n3"], fw["w_ple_gate"], "nn", F32, "mm_ple_gate", tm=512)
        s["pe"] = _matmul(p[l, 0], fw["w_ple"], "nn", F32, "mm_ple", tn=D)
        h = _ple_fwd(s["h2"], s["gl"], s["pe"])
        saved.append(s)

    dh, dhb, dg_final, loss_part = _final_loss(h, loss_target[0], final_norm_g)
    loss = lax.psum(loss_part[0, 0], ("x", "y", "c"))

    small = {}
    reduced = [None] * L
    waiting = None
    for l in reversed(range(L)):
        fw, s = FW[l], saved[l]
        gw = {}
        got = {}
        parts = _plan(waiting, False, up_at) if waiting else [None] * 4

        def chips(pieces):
            return _chip_side(waiting, pieces, got)

        dpe, dgl = _ple_bwd(dh, s["gl"], s["pe"])
        gw["w_ple"] = _matmul(p[l, 0], dpe, "tn", BF16, "dw_ple", tm=512)
        gw["w_ple_gate"] = _matmul(s["n3"], dgl, "tn", BF16, "dw_ple_gate")
        dn3 = _matmul(dgl, fw["w_ple_gate"], "nt", F32, "dx_ple_gate", tm=512)
        dh, dhb, small["norm_ple_g", l] = _rms_bwd(dn3, s["h2"], norm_ple_g[l], dh, "norm_ple_bwd")

        da = _matmul(dhb, fw["w_down"], "nt", BF16, "dx_down", tn=F // 4 if F % (4 * LANES) == 0 else F)
        gw["w_down"] = _matmul(s["a"], dhb, "tn", BF16, "dw_down")
        dgate, dup, dcw = _ffn_bwd(da, s["gate"], s["up"], fw["cw8"])
        small["conv_w", l], small["conv_b", l] = dcw[:CONV_TAPS], dcw[CONV_TAPS:CONV_TAPS + 1]
        gw["w_up"] = _ride(lambda sd: _matmul(s["n2"], dgate, "tn", BF16, "dw_gate", tn=ns_up // 2,
                                              out_shards=(half_up[0][0], N_SHARDS, ns_up), side=sd), 1, parts[1], chips, got)
        gw["w_up"] = _ride(lambda sd: _matmul(s["n2"], dup, "tn", BF16, "dw_up", tn=ns_up // 2,
                                              out_shards=(half_up[1][0], N_SHARDS, ns_up), carry=gw["w_up"], side=sd), 1, parts[2], chips, got)
        dn2 = _ride(lambda sd: _matmul(dgate, fw["w_up"], "nt", F32, "dx_gate", tm=512, tn=D, tk=ns_up // 2, b_shards=half_up[0],
                                       side=sd), 1, parts[3], chips, got)
        dn2 = _matmul(dup, fw["w_up"], "nt", F32, "dx_up", res=dn2, tm=512, tn=D, tk=ns_up // 2, b_shards=half_up[1])
        dh, dhb, small["norm_ffn_g", l] = _rms_bwd(dn2, s["h1"], norm_ffn_g[l], dh, "norm_ffn_bwd")

        dmix = _matmul(dhb, fw["w_out"], "nt", BF16, "dx_out")
        gw["w_out"] = _matmul(s["mix"], dhb, "tn", BF16, "dw_out")
        ddiff, gw["pool_w"], small["pool_scale", l] = _pool_bwd(dmix, s["diff"], fw["pool_w"], pool_scale[l], dims)
        dq, dkn, dv, dkr = _ride(lambda sd: _flash_bwd(s["q"], s["kv"], s["kr"], s["o"], s["lse"], dmix, dims, sd), 4, parts[0], chips, got)
        if got:
            reduced[l + 1] = _reduce_end(waiting, [got[a] for a in range(len(waiting))], place)
        dqb = _dq_post(dq, tabs, dims)
        dkv = jnp.concatenate([dkn, dv], axis=1)
        gw["w_uq"] = _heads_merge(_matmul(s["cqn"], dqb, "tn", BF16, "dw_uq", tn=2 * H * LANES), H, NOPE_DIM, ROPE_DIM, LANES)
        gw["w_ukv"] = _heads_merge(_matmul(s["ckvn"], dkv, "tn", BF16, "dw_ukv", tn=2 * H * LANES), H, NOPE_DIM, V_DIM, V_DIM)
        dcqn = _matmul(dqb, fw["w_uq"], "nt", F32, "dx_uq")
        dckvn = _matmul(dkv, fw["w_ukv"], "nt", F32, "dx_ukv")
        du, small["q_norm_g", l], small["kv_norm_g", l] = _pre_u_bwd(s["u"], dcqn, dckvn, ddiff, dkr, q_norm_g[l], kv_norm_g[l], tabs, dims)
        gw["w_in"] = _matmul(s["n1"], du, "tn", BF16, "dw_in", tm=512, tn=du.shape[1])[:, :d_in]
        dn1 = _matmul(du, fw["w_in"], "nt", F32, "dx_in", tm=512, tk=du.shape[1])
        dh, dhb, small["norm_mix_g", l] = _rms_bwd(dn1, s["h0"], norm_mix_g[l], dh, "norm_mix_bwd")

        split = {n: _split(n, gw[n]) for n in MISC}
        arrs = [jnp.stack([gw["w_in"][:, sh * ns_in:(sh + 1) * ns_in] for sh in range(N_SHARDS)]),
                gw["w_out"].reshape(N_SHARDS, -1, D), gw["w_up"], gw["w_down"].reshape(N_SHARDS, -1, D),
                gw["w_ple_gate"].reshape(N_SHARDS, -1, D),
                jnp.stack([_pack([split[n][sh] for n in MISC], BF16) for sh in range(N_SHARDS)])]
        waiting = _reduce_begin(arrs, place)
    reduced[0] = _reduce_end(waiting, _run_side(_chip_side(waiting, _whole(waiting, False), {})[0], "rs_chips"), place)

    grads = {n: jnp.stack([reduced[l][k].reshape(W[n].shape[1:]) for l in range(L)]) for k, n in enumerate(BIG)}
    per_layer = [_unpack(reduced[l][-1], misc_shapes, ()) for l in range(L)]
    for k, n in enumerate(MISC):
        grads[n] = jnp.stack([per_layer[l][k] for l in range(L)])

    small_names = ("norm_mix_g", "pool_scale", "q_norm_g", "kv_norm_g", "norm_ffn_g", "conv_b", "norm_ple_g", "conv_w")
    summed = all_reduce([small[n, l].reshape(-1) for n in small_names for l in range(L)] + [dg_final.reshape(-1)])
    off = 0
    for n in small_names:
        size = CONV_TAPS * F if n == "conv_w" else W[n].shape[-1]
        grads[n] = summed[off:off + L * size].reshape((L, CONV_TAPS, F) if n == "conv_w" else (L, size))
        off += L * size
    grads["final_norm_g"] = summed[off:off + D]
    grads["conv_w"] = lax.dynamic_slice(grads["conv_w"], (0, 0, me * ns_conv), (L, CONV_TAPS, ns_conv))

    deltas, new_m, new_v = {}, {}, {}
    for n in WEIGHTS:
        deltas[n], new_m[n], new_v[n] = _adamw(W[n], grads[n], M1[n], M2[n], "adamw_" + n)

    return (loss, dh[None], *[grads[n] for n in WEIGHTS], *[deltas[n] for n in WEIGHTS],
            *[new_m[n] for n in WEIGHTS], *[new_v[n] for n in WEIGHTS])
```

```python
import functools
import math

import jax
import jax.numpy as jnp
from jax import lax
from jax.experimental import pallas as pl
from jax.experimental.pallas import tpu as pltpu

F32 = jnp.float32
BF16 = jnp.bfloat16

NOPE_DIM = 128
ROPE_DIM = 64
V_DIM = 128
LANES = 128
SUBLANES_BF16 = 16
ROPE_THETA = 10000.0
EPS = 1e-6
POOL_WINDOWS = (2, 4, 8, 16)
POOL_HALO = 16
CONV_TAPS = 3
CONV_HALO = 8
ADAM_LR = 0.001
ADAM_B1 = 0.9
ADAM_B2 = 0.999
ADAM_EPS = 1e-08
ADAM_WD = 0.01
ADAM_STEP = 10
NEG_BIG = -1e30
ATT_SCALE = 1.0 / math.sqrt(NOPE_DIM + ROPE_DIM)
V7X_VMEM_BYTES = 64 * 2 ** 20
N_SHARDS = 4
PACK_ALIGN = 2 * SUBLANES_BF16 * LANES

TILES = dict(row=256, att=512, mm_m=1024, mm_n=1024, mm_k=2048, ffn_row=512, ffn_c=512, add_bytes=1 << 20)

BIG = ("w_in", "w_out", "w_up", "w_down", "w_ple_gate")
MISC = ("w_uq", "w_ukv", "w_ple", "pool_w")
COL_SHARDED = ("w_in", "w_uq", "w_ukv", "w_up", "conv_w", "w_ple")
ROW_SHARDED = ("w_out", "w_down", "w_ple_gate")
WEIGHTS = ("norm_mix_g", "w_in", "pool_w", "pool_scale", "q_norm_g", "w_uq", "kv_norm_g", "w_ukv", "w_out",
           "norm_ffn_g", "w_up", "conv_w", "conv_b", "w_down", "norm_ple_g", "w_ple", "w_ple_gate", "final_norm_g")
MESH = pl.DeviceIdType.MESH


def _nbytes(shape, dtype):
    return math.prod(shape) * jnp.dtype(dtype).itemsize


def _params(sem, need_bytes):
    limit = min(V7X_VMEM_BYTES - (8 << 20), max(32 << 20, int(need_bytes)))
    return pltpu.CompilerParams(dimension_semantics=sem, vmem_limit_bytes=limit)


def _tile(n, want, mult=8):
    if n <= want:
        return n
    for t in range(want - want % mult, 0, -mult):
        if n % t == 0:
            return t
    return n


def _sigmoid(x):
    return 1.0 / (1.0 + jnp.exp(-x))


def _rstd(x):
    return lax.rsqrt(jnp.mean(x * x, axis=-1, keepdims=True) + EPS)


_DOT_DIMS = {"nn": (((1,), (0,)), ((), ())), "nt": (((1,), (1,)), ((), ())), "tn": (((0,), (0,)), ((), ()))}


def _matmul(a, b, mode, out_dtype, name, res=None, tm=None, tn=None, tk=None, b_shards=None, out_shards=None, carry=None,
            side=None):
    if mode == "nn":
        (M, K), N = a.shape, b.shape[-1] * (b_shards[1] if b_shards else 1)
    elif mode == "nt":
        (M, K), N = a.shape, b.shape[-2]
    else:
        (K, M), N = a.shape, b.shape[1]
    per = b.shape[-1] if b_shards else (out_shards[2] if out_shards else None)
    tm = _tile(M, tm or TILES["mm_m"], LANES)
    tn = _tile(per if (per and mode != "nt") else N, tn or TILES["mm_n"], LANES)
    tk = _tile(per if (per and mode == "nt") else K, tk or TILES["mm_k"], LANES)
    nk = K // tk
    has_res = res is not None
    has_carry = carry is not None
    dims = _DOT_DIMS[mode]

    def body(*refs):
        a_ref, b_ref = refs[0], refs[1]
        o_ref = refs[2 + has_res + has_carry]
        part = lax.dot_general(a_ref[...].astype(BF16), b_ref[...].astype(BF16), dims, preferred_element_type=F32)

        def finish(acc):
            if has_res:
                acc = acc + refs[2][...]
            o_ref[...] = acc.astype(o_ref.dtype)

        if nk == 1:
            finish(part)
        else:
            acc_ref = refs[3 + has_res + has_carry]
            k = pl.program_id(2)

            @pl.when(k == 0)
            def _():
                acc_ref[...] = part

            @pl.when(k > 0)
            def _():
                acc_ref[...] += part

            @pl.when(k == nk - 1)
            def _():
                finish(acc_ref[...])

    if mode == "nn":
        a_spec, b_spec = pl.BlockSpec((tm, tk), lambda i, j, k: (i, k)), pl.BlockSpec((tk, tn), lambda i, j, k: (k, j))
    elif mode == "nt":
        a_spec, b_spec = pl.BlockSpec((tm, tk), lambda i, j, k: (i, k)), pl.BlockSpec((tn, tk), lambda i, j, k: (j, k))
    else:
        a_spec, b_spec = pl.BlockSpec((tk, tm), lambda i, j, k: (k, i)), pl.BlockSpec((tk, tn), lambda i, j, k: (k, j))
    o_spec = pl.BlockSpec((tm, tn), lambda i, j, k: (i, j))
    out_shape = jax.ShapeDtypeStruct((M, N), out_dtype)
    if b_shards:
        first = b_shards[0]
        if mode == "nn":
            nps = per // tn
            b_spec = pl.BlockSpec((None, tk, tn), lambda i, j, k: (first + j // nps, k, j % nps))
        else:
            kps = per // tk
            b_spec = pl.BlockSpec((None, tn, tk), lambda i, j, k: (first + k // kps, j, k % kps))
    if out_shards:
        ofirst, nps_o = out_shards[0], per // tn
        o_spec_out = pl.BlockSpec((None, tm, tn), lambda i, j, k: (ofirst + j // nps_o, i, j % nps_o))
        out_shape = jax.ShapeDtypeStruct((out_shards[1], M, per), out_dtype)
    else:
        o_spec_out = o_spec
    in_specs, args = [a_spec, b_spec], [a, b]
    need = 2 * (_nbytes((tm, tk), a.dtype) + _nbytes((tk, tn), b.dtype) + _nbytes((tm, tn), out_dtype)) + 2 * _nbytes((tm, tn), F32)
    if has_res:
        in_specs.append(o_spec)
        args.append(res)
        need += 2 * _nbytes((tm, tn), res.dtype)
    aliases = {}
    if has_carry:
        aliases = {len(args): 0}
        in_specs.append(pl.BlockSpec(memory_space=pl.ANY))
        args.append(carry)
    scratch = [pltpu.VMEM((tm, tn), F32)] if nk > 1 else []
    grid = (M // tm, N // tn, nk)
    params = _params(("parallel", "parallel", "arbitrary") if side is None else ("arbitrary",) * 3, need + (4 << 20))
    if side is None:
        return pl.pallas_call(body, name=name, grid=grid, in_specs=in_specs, out_specs=o_spec_out, out_shape=out_shape,
                              scratch_shapes=scratch, input_output_aliases=aliases, compiler_params=params)(*args)
    body, s_ins, s_in_specs, s_shapes, s_out_specs, s_sems, s_aliases = _attach(side, body, len(args), 1, grid)
    return pl.pallas_call(
        body, name=name, grid=grid, in_specs=in_specs + s_in_specs, out_specs=[o_spec_out] + s_out_specs,
        out_shape=[out_shape] + s_shapes, scratch_shapes=scratch + s_sems, input_output_aliases={**aliases, **s_aliases},
        compiler_params=params,
    )(*args, *s_ins)


def _rms_fwd(x, g, name):
    T, D = x.shape
    tt = _tile(T, TILES["row"])

    def body(x_ref, g_ref, o_ref):
        xv = x_ref[...]
        o_ref[...] = (xv * _rstd(xv) * g_ref[...]).astype(o_ref.dtype)

    row = pl.BlockSpec((tt, D), lambda i: (i, 0))
    return pl.pallas_call(
        body, name=name, grid=(T // tt,), in_specs=[row, pl.BlockSpec((1, D), lambda i: (0, 0))], out_specs=row,
        out_shape=jax.ShapeDtypeStruct((T, D), BF16), compiler_params=_params(("parallel",), 8 * _nbytes((tt, D), F32)),
    )(x, g.reshape(1, D))


def _rms_bwd(dn, x, g, dres, name):
    T, D = x.shape
    tt = _tile(T, TILES["row"])

    def body(dn_ref, x_ref, g_ref, dres_ref, dx_ref, dxb_ref, dg_ref):
        i = pl.program_id(0)
        xv = x_ref[...]
        r = _rstd(xv)
        xh = xv * r
        dnv = dn_ref[...].astype(F32)
        dxh = dnv * g_ref[...]
        tot = dres_ref[...] + r * (dxh - xh * jnp.mean(dxh * xh, axis=-1, keepdims=True))
        dx_ref[...] = tot
        dxb_ref[...] = tot.astype(BF16)
        part = jnp.sum(dnv * xh, axis=0, keepdims=True)

        @pl.when(i == 0)
        def _():
            dg_ref[...] = part

        @pl.when(i > 0)
        def _():
            dg_ref[...] += part

    row = pl.BlockSpec((tt, D), lambda i: (i, 0))
    vec = pl.BlockSpec((1, D), lambda i: (0, 0))
    return pl.pallas_call(
        body, name=name, grid=(T // tt,), in_specs=[row, row, vec, row], out_specs=[row, row, vec],
        out_shape=[jax.ShapeDtypeStruct((T, D), F32), jax.ShapeDtypeStruct((T, D), BF16), jax.ShapeDtypeStruct((1, D), F32)],
        compiler_params=_params(("arbitrary",), 16 * _nbytes((tt, D), F32)),
    )(dn, x, g.reshape(1, D), dres)


def _final_loss(h, target, g):
    T, D = h.shape
    tt = _tile(T, TILES["row"])

    def body(h_ref, t_ref, g_ref, dx_ref, dxb_ref, dg_ref, loss_ref):
        i = pl.program_id(0)
        xv = h_ref[...]
        r = _rstd(xv)
        xh = xv * r
        gv = g_ref[...]
        err = xh * gv - t_ref[...]
        lpart = 0.5 * jnp.sum(jnp.mean(err * err, axis=-1, keepdims=True), axis=0, keepdims=True)
        dy = err * (1.0 / D)
        dxh = dy * gv
        dx = r * (dxh - xh * jnp.mean(dxh * xh, axis=-1, keepdims=True))
        dx_ref[...] = dx
        dxb_ref[...] = dx.astype(BF16)
        gpart = jnp.sum(dy * xh, axis=0, keepdims=True)
        lrow = jnp.broadcast_to(lpart, (1, LANES))

        @pl.when(i == 0)
        def _():
            dg_ref[...] = gpart
            loss_ref[...] = lrow

        @pl.when(i > 0)
        def _():
            dg_ref[...] += gpart
            loss_ref[...] += lrow

    row = pl.BlockSpec((tt, D), lambda i: (i, 0))
    vec = pl.BlockSpec((1, D), lambda i: (0, 0))
    return pl.pallas_call(
        body, name="final_loss", grid=(T // tt,), in_specs=[row, row, vec],
        out_specs=[row, row, vec, pl.BlockSpec((1, LANES), lambda i: (0, 0))],
        out_shape=[jax.ShapeDtypeStruct((T, D), F32), jax.ShapeDtypeStruct((T, D), BF16),
                   jax.ShapeDtypeStruct((1, D), F32), jax.ShapeDtypeStruct((1, LANES), F32)],
        compiler_params=_params(("arbitrary",), 16 * _nbytes((tt, D), F32)),
    )(h, target, g.reshape(1, D))


def _rope_tables(pos_col, inv_lane):
    T = pos_col.shape[0]
    tt = _tile(T, TILES["row"])

    def body(p_ref, f_ref, c_ref, s1_ref, s2_ref):
        ang = p_ref[...] * f_ref[...]
        lane = lax.broadcasted_iota(jnp.int32, ang.shape, 1)
        half = ROPE_DIM // 2
        cs, sn = jnp.cos(ang), jnp.sin(ang)
        c_ref[...] = jnp.where(lane < ROPE_DIM, cs, 0.0)
        s1_ref[...] = jnp.where(lane < half, -sn, 0.0)
        s2_ref[...] = jnp.where((lane >= half) & (lane < ROPE_DIM), sn, 0.0)

    tab = pl.BlockSpec((tt, LANES), lambda i: (i, 0))
    shp = jax.ShapeDtypeStruct((T, LANES), F32)
    return pl.pallas_call(
        body, name="rope_tables", grid=(T // tt,),
        in_specs=[pl.BlockSpec((tt, 1), lambda i: (i, 0)), pl.BlockSpec((1, LANES), lambda i: (0, 0))],
        out_specs=[tab, tab, tab], out_shape=[shp, shp, shp],
        compiler_params=_params(("parallel",), 32 * _nbytes((tt, LANES), F32)),
    )(pos_col, inv_lane)


def _rope(x, c, s1, s2):
    return x * c + pltpu.roll(x, LANES - ROPE_DIM // 2, 1) * s1 + pltpu.roll(x, ROPE_DIM // 2, 1) * s2


def _rope_t(d, c, s1, s2):
    return d * c + pltpu.roll(d * s1, ROPE_DIM // 2, 1) + pltpu.roll(d * s2, LANES - ROPE_DIM // 2, 1)


def _window_sum(xe, w, forward):
    n = xe.shape[0]
    s, sh = xe, 1
    while sh < w:
        s = s + pltpu.roll(s, (n - sh) if forward else sh, 0)
        sh *= 2
    return s


def _post_u(u, gq, gkv, tabs, dims):
    T, Dp = u.shape
    P, QL, KL, C = dims["P"], dims["QL"], dims["KL"], dims["C"]
    tt = _tile(T, TILES["row"], POOL_HALO)
    hb = tt // POOL_HALO

    def body(u_ref, halo_ref, gq_ref, gkv_ref, c_ref, s1_ref, s2_ref, diff_ref, cq_ref, ckv_ref, kr_ref):
        i = pl.program_id(0)
        t = i * tt + lax.broadcasted_iota(jnp.int32, (tt, 1), 0)
        halo = jnp.where(i > 0, halo_ref[...], 0.0)
        for gi, w in enumerate(POOL_WINDOWS):
            cols = slice(gi * C, (gi + 1) * C)
            xg = u_ref[:, cols]
            s = _window_sum(jnp.concatenate([halo[:, cols], xg], axis=0), w, False)[POOL_HALO:]
            cnt = jnp.minimum(t + 1, w).astype(F32)
            diff_ref[:, cols] = (s / cnt - xg).astype(BF16)
        cq = u_ref[:, P:P + QL]
        cq_ref[...] = (cq * _rstd(cq) * gq_ref[...]).astype(BF16)
        ckv = u_ref[:, P + QL:P + QL + KL]
        ckv_ref[...] = (ckv * _rstd(ckv) * gkv_ref[...]).astype(BF16)
        kr_ref[...] = _rope(u_ref[:, P + QL + KL:], c_ref[...], s1_ref[...], s2_ref[...]).astype(BF16)

    def row(w):
        return pl.BlockSpec((tt, w), lambda i: (i, 0))

    def vec(w):
        return pl.BlockSpec((1, w), lambda i: (0, 0))

    return pl.pallas_call(
        body, name="post_u", grid=(T // tt,),
        in_specs=[row(Dp), pl.BlockSpec((POOL_HALO, P), lambda i: (jnp.maximum(i * hb - 1, 0), 0)),
                  vec(QL), vec(KL), row(LANES), row(LANES), row(LANES)],
        out_specs=[row(P), row(QL), row(KL), row(LANES)],
        out_shape=[jax.ShapeDtypeStruct((T, P), BF16), jax.ShapeDtypeStruct((T, QL), BF16),
                   jax.ShapeDtypeStruct((T, KL), BF16), jax.ShapeDtypeStruct((T, LANES), BF16)],
        compiler_params=_params(("parallel",), 10 * _nbytes((tt, Dp), F32)),
    )(u, u, gq.reshape(1, QL), gkv.reshape(1, KL), *tabs)


def _pre_u_bwd(u, d_cqn, d_ckvn, d_diff, dkr, gq, gkv, tabs, dims):
    T, Dp = u.shape
    P, QL, KL, C, H = dims["P"], dims["QL"], dims["KL"], dims["C"], dims["H"]
    tt = _tile(T, TILES["row"], POOL_HALO)
    hb = tt // POOL_HALO
    n_t = T // tt

    def norm_bwd(xv, dn, gv):
        r = _rstd(xv)
        xh = xv * r
        dxh = dn * gv
        return r * (dxh - xh * jnp.mean(dxh * xh, axis=-1, keepdims=True)), jnp.sum(dn * xh, axis=0, keepdims=True)

    def body(u_ref, dcq_ref, dckv_ref, dd_ref, ddn_ref, dkr_ref, gq_ref, gkv_ref, c_ref, s1_ref, s2_ref,
             du_ref, dgq_ref, dgkv_ref):
        i = pl.program_id(0)
        t = i * tt + lax.broadcasted_iota(jnp.int32, (tt, 1), 0)
        nxt = jnp.where(i < n_t - 1, ddn_ref[...].astype(F32), 0.0)
        for gi, w in enumerate(POOL_WINDOWS):
            cols = slice(gi * C, (gi + 1) * C)
            dd = dd_ref[:, cols].astype(F32)
            e = dd / jnp.minimum(t + 1, w).astype(F32)
            s = _window_sum(jnp.concatenate([e, nxt[:, cols] / float(w)], axis=0), w, True)[:tt]
            du_ref[:, cols] = (s - dd).astype(BF16)
        dq, pq = norm_bwd(u_ref[:, P:P + QL], dcq_ref[...], gq_ref[...])
        du_ref[:, P:P + QL] = dq.astype(BF16)
        dkv, pkv = norm_bwd(u_ref[:, P + QL:P + QL + KL], dckv_ref[...], gkv_ref[...])
        du_ref[:, P + QL:P + QL + KL] = dkv.astype(BF16)
        dk = dkr_ref[0]
        for hh in range(1, H):
            dk = dk + dkr_ref[hh]
        du_ref[:, P + QL + KL:] = _rope_t(dk, c_ref[...], s1_ref[...], s2_ref[...]).astype(BF16)

        @pl.when(i == 0)
        def _():
            dgq_ref[...] = pq
            dgkv_ref[...] = pkv

        @pl.when(i > 0)
        def _():
            dgq_ref[...] += pq
            dgkv_ref[...] += pkv

    def row(w):
        return pl.BlockSpec((tt, w), lambda i: (i, 0))

    def vec(w):
        return pl.BlockSpec((1, w), lambda i: (0, 0))

    return pl.pallas_call(
        body, name="pre_u_bwd", grid=(n_t,),
        in_specs=[row(Dp), row(QL), row(KL), row(P),
                  pl.BlockSpec((POOL_HALO, P), lambda i: (jnp.minimum((i + 1) * hb, T // POOL_HALO - 1), 0)),
                  pl.BlockSpec((H, tt, LANES), lambda i: (0, i, 0)), vec(QL), vec(KL), row(LANES), row(LANES), row(LANES)],
        out_specs=[row(Dp), vec(QL), vec(KL)],
        out_shape=[jax.ShapeDtypeStruct((T, Dp), BF16), jax.ShapeDtypeStruct((1, QL), F32), jax.ShapeDtypeStruct((1, KL), F32)],
        compiler_params=_params(("arbitrary",), 12 * _nbytes((tt, Dp), F32)),
    )(u, d_cqn, d_ckvn, d_diff, d_diff, dkr, gq.reshape(1, QL), gkv.reshape(1, KL), *tabs)


def _pool_fwd(diff, pw, ps, dims):
    T, P = diff.shape
    G, C = len(POOL_WINDOWS), dims["C"]
    tt = _tile(T, TILES["row"])

    def body(d_ref, w_ref, s_ref, o_ref):
        for gi in range(G):
            cols = slice(gi * C, (gi + 1) * C)
            y = jnp.dot(d_ref[:, cols], w_ref[gi], preferred_element_type=F32)
            o_ref[:, cols] = (y * s_ref[:, cols]).astype(BF16)

    row = pl.BlockSpec((tt, P), lambda i: (i, 0))
    return pl.pallas_call(
        body, name="pool_fwd", grid=(T // tt,),
        in_specs=[row, pl.BlockSpec((G, C, C), lambda i: (0, 0, 0)), pl.BlockSpec((1, P), lambda i: (0, 0))],
        out_specs=row, out_shape=jax.ShapeDtypeStruct((T, P), BF16),
        compiler_params=_params(("parallel",), 8 * _nbytes((tt, P), F32)),
    )(diff, pw, ps.reshape(1, P))


def _pool_bwd(dmix, diff, pw, ps, dims):
    T, P = diff.shape
    G, C = len(POOL_WINDOWS), dims["C"]
    tt = _tile(T, TILES["row"])

    def body(dy_ref, d_ref, w_ref, s_ref, dd_ref, dw_ref, ds_ref):
        i = pl.program_id(0)

        @pl.when(i == 0)
        def _():
            dw_ref[...] = jnp.zeros_like(dw_ref)
            ds_ref[...] = jnp.zeros_like(ds_ref)

        for gi in range(G):
            cols = slice(gi * C, (gi + 1) * C)
            dy = dy_ref[:, cols].astype(F32)
            d = d_ref[:, cols]
            w = w_ref[gi]
            ypre = jnp.dot(d, w, preferred_element_type=F32)
            ds_ref[:, cols] += jnp.sum(dy * ypre, axis=0, keepdims=True)
            dyp = (dy * s_ref[:, cols]).astype(BF16)
            dd_ref[:, cols] = lax.dot_general(dyp, w, _DOT_DIMS["nt"], preferred_element_type=F32).astype(BF16)
            dw_ref[gi] += lax.dot_general(d, dyp, _DOT_DIMS["tn"], preferred_element_type=F32)

    row = pl.BlockSpec((tt, P), lambda i: (i, 0))
    wsp = pl.BlockSpec((G, C, C), lambda i: (0, 0, 0))
    vec = pl.BlockSpec((1, P), lambda i: (0, 0))
    return pl.pallas_call(
        body, name="pool_bwd", grid=(T // tt,), in_specs=[row, row, wsp, vec], out_specs=[row, wsp, vec],
        out_shape=[jax.ShapeDtypeStruct((T, P), BF16), jax.ShapeDtypeStruct((G, C, C), F32), jax.ShapeDtypeStruct((1, P), F32)],
        compiler_params=_params(("arbitrary",), 10 * _nbytes((tt, P), F32)),
    )(dmix, diff, pw, ps.reshape(1, P))


def _q_rope(qp, tabs, dims):
    T, W = qp.shape
    H = dims["H"]
    tt = _tile(T, TILES["row"])

    def body(q_ref, c_ref, s1_ref, s2_ref, o_ref):
        o_ref[:, :H * LANES] = (q_ref[:, :H * LANES] * ATT_SCALE).astype(BF16)
        c, s1, s2 = c_ref[...], s1_ref[...], s2_ref[...]
        for hh in range(H, 2 * H):
            cols = slice(hh * LANES, (hh + 1) * LANES)
            o_ref[:, cols] = _rope(q_ref[:, cols] * ATT_SCALE, c, s1, s2).astype(BF16)

    row = pl.BlockSpec((tt, W), lambda i: (i, 0))
    tab = pl.BlockSpec((tt, LANES), lambda i: (i, 0))
    return pl.pallas_call(
        body, name="q_rope", grid=(T // tt,), in_specs=[row, tab, tab, tab], out_specs=row,
        out_shape=jax.ShapeDtypeStruct((T, W), BF16), compiler_params=_params(("parallel",), 8 * _nbytes((tt, W), F32)),
    )(qp, *tabs)


def _scores(qn_ref, qr_ref, kn_ref, kr_ref, t, diagonal):
    q = jnp.concatenate([qn_ref[...], qr_ref[...]], axis=1)
    k = jnp.concatenate([kn_ref[...], kr_ref[...]], axis=1)
    s = lax.dot_general(q, k, _DOT_DIMS["nt"], preferred_element_type=F32)
    if diagonal:
        s = jnp.where(lax.broadcasted_iota(jnp.int32, (t, t), 0) >= lax.broadcasted_iota(jnp.int32, (t, t), 1), s, NEG_BIG)
    return q, k, s


class _Side:
    def __init__(self, ins, out_shapes, n_sems, start, finish, aliases=None):
        self.ins, self.out_shapes, self.n_sems, self.start, self.finish = list(ins), list(out_shapes), n_sems, start, finish
        self.aliases = dict(aliases or {})


def _attach(side, body, n_in, n_out, grid):
    if side is None:
        return body, [], [], [], [], [], {}
    n_si, n_so = len(side.ins), len(side.out_shapes)

    def carrying(*refs):
        outs_at = n_in + n_si
        main = refs[:n_in] + refs[outs_at:outs_at + n_out] + refs[outs_at + n_out + n_so:len(refs) - 2]
        parts = (refs[n_in:outs_at], refs[outs_at + n_out:outs_at + n_out + n_so], refs[-2], refs[-1])
        ids = [pl.program_id(d) for d in range(len(grid))]
        first = functools.reduce(lambda u, v: u & v, [i == 0 for i in ids])
        last = functools.reduce(lambda u, v: u & v, [i == g - 1 for i, g in zip(ids, grid)])

        @pl.when(first)
        def _():
            side.start(*parts)

        body(*main)

        @pl.when(last)
        def _():
            side.finish(*parts)

    aliases = {n_in + i: n_out + o for i, o in side.aliases.items()}
    return carrying, side.ins, [_HBM] * n_si, side.out_shapes, [_HBM] * n_so, _sem_pair(side.n_sems), aliases


def _pairs(n, by_query):
    pairs = [(i, j) for i in range(n) for j in range(i + 1)] if by_query else [(i, j) for j in range(n) for i in range(j, n)]
    return jnp.array([p[0] for p in pairs], jnp.int32), jnp.array([p[1] for p in pairs], jnp.int32), len(pairs)


def _flash_fwd(q_att, kv, kr, dims, side=None):
    T = q_att.shape[0]
    H = dims["H"]
    G = 2 if H % 2 == 0 else 1
    t = _tile(T, TILES["att"])
    n = T // t
    it, jt, n_pairs = _pairs(n, True)
    hb = H // G

    def body(it_ref, jt_ref, qn_ref, qr_ref, kn_ref, v_ref, kr_ref, o_ref, lse_ref, m_ref, l_ref, acc_ref):
        step_id = pl.program_id(1)
        i, j = it_ref[step_id], jt_ref[step_id]

        @pl.when(j == 0)
        def _():
            m_ref[...] = jnp.full_like(m_ref, NEG_BIG)
            l_ref[...] = jnp.zeros_like(l_ref)
            acc_ref[...] = jnp.zeros_like(acc_ref)

        def step(diagonal):
            for g in range(G):
                cols = slice(g * LANES, (g + 1) * LANES)
                _, _, s = _scores(qn_ref.at[:, cols], qr_ref.at[:, cols], kn_ref.at[:, cols], kr_ref, t, diagonal)
                m_prev = m_ref[:, cols]
                m_new = jnp.maximum(m_prev, jnp.max(s, axis=1, keepdims=True))
                alpha = jnp.exp(m_prev - m_new)
                p = jnp.exp(s - m_new[:, :1])
                l_ref[:, cols] = alpha * l_ref[:, cols] + jnp.sum(p, axis=1, keepdims=True)
                acc_ref[:, cols] = alpha * acc_ref[:, cols] + jnp.dot(p.astype(BF16), v_ref[:, cols], preferred_element_type=F32)
                m_ref[:, cols] = m_new

        @pl.when(j < i)
        def _():
            step(False)

        @pl.when(j == i)
        def _():
            step(True)
            o_ref[...] = (acc_ref[...] / l_ref[...]).astype(BF16)
            lse_ref[...] = m_ref[...] + jnp.log(l_ref[...])

    blk = (t, G * LANES)
    grid = (hb, n_pairs)
    body, s_ins, s_in_specs, s_shapes, s_out_specs, s_sems, aliases = _attach(side, body, 7, 2, grid)
    return pl.pallas_call(
        body, name="flash_fwd",
        grid_spec=pltpu.PrefetchScalarGridSpec(
            num_scalar_prefetch=2, grid=grid,
            in_specs=[pl.BlockSpec(blk, lambda h, s, it, jt: (it[s], h)), pl.BlockSpec(blk, lambda h, s, it, jt: (it[s], hb + h)),
                      pl.BlockSpec(blk, lambda h, s, it, jt: (jt[s], h)), pl.BlockSpec(blk, lambda h, s, it, jt: (jt[s], hb + h)),
                      pl.BlockSpec((t, LANES), lambda h, s, it, jt: (jt[s], 0))] + s_in_specs,
            out_specs=[pl.BlockSpec(blk, lambda h, s, it, jt: (it[s], h)), pl.BlockSpec(blk, lambda h, s, it, jt: (it[s], h))] + s_out_specs,
            scratch_shapes=[pltpu.VMEM(blk, F32), pltpu.VMEM(blk, F32), pltpu.VMEM(blk, F32), *s_sems]),
        out_shape=[jax.ShapeDtypeStruct((T, H * LANES), BF16), jax.ShapeDtypeStruct((T, H * LANES), F32)] + s_shapes,
        input_output_aliases=aliases,
        compiler_params=_params(("arbitrary", "arbitrary"), 8 * G * _nbytes((t, t), F32) + (8 << 20)),
    )(it, jt, q_att, q_att, kv, kv, kr, *s_ins)


def _flash_bwd(q_att, kv, kr, o, lse, dmix, dims, side=None):
    T = q_att.shape[0]
    H = dims["H"]
    G = 2 if H % 2 == 0 else 1
    ob = dims["P"] // LANES // G
    t = _tile(T, TILES["att"])
    n = T // t
    it, jt, n_pairs = _pairs(n, False)
    hb = H // G

    def body(it_ref, jt_ref, qn_ref, qr_ref, kn_ref, v_ref, kr_ref, o_ref, lse_ref, do_ref,
             dq_ref, dkn_ref, dv_ref, dkr_ref, dk_acc, dv_acc):
        step_id = pl.program_id(1)
        i, j = it_ref[step_id], jt_ref[step_id]

        @pl.when(step_id == 0)
        def _():
            dq_ref[...] = jnp.zeros_like(dq_ref)

        @pl.when(i == j)
        def _():
            dk_acc[...] = jnp.zeros_like(dk_acc)
            dv_acc[...] = jnp.zeros_like(dv_acc)

        def step(diagonal):
            rows = pl.ds(pl.multiple_of(i * t, t), t)
            for g in range(G):
                cols = slice(g * LANES, (g + 1) * LANES)
                q, k, s = _scores(qn_ref.at[:, cols], qr_ref.at[:, cols], kn_ref.at[:, cols], kr_ref, t, diagonal)
                p = jnp.exp(s - lse_ref[:, g * LANES:g * LANES + 1])
                do = do_ref[:, cols]
                delta = jnp.sum(do.astype(F32) * o_ref[:, cols].astype(F32), axis=1, keepdims=True)
                dv_acc[:, cols] += lax.dot_general(p.astype(BF16), do, _DOT_DIMS["tn"], preferred_element_type=F32)
                dp = lax.dot_general(do, v_ref[:, cols], _DOT_DIMS["nt"], preferred_element_type=F32)
                ds = (p * (dp - delta)).astype(BF16)
                dk_acc[g] += lax.dot_general(ds, q, _DOT_DIMS["tn"], preferred_element_type=F32)
                dq_ref[g, rows, :] += jnp.dot(ds, k, preferred_element_type=F32)

        @pl.when(i > j)
        def _():
            step(False)

        @pl.when(i == j)
        def _():
            step(True)

        @pl.when(i == n - 1)
        def _():
            for g in range(G):
                dkn_ref[:, g * LANES:(g + 1) * LANES] = dk_acc[g, :, :LANES].astype(BF16)
                dkr_ref[g] = dk_acc[g, :, LANES:]
            dv_ref[...] = dv_acc[...].astype(BF16)

    blk = (t, G * LANES)
    grid = (hb, n_pairs)
    body, s_ins, s_in_specs, s_shapes, s_out_specs, s_sems, aliases = _attach(side, body, 10, 4, grid)
    return pl.pallas_call(
        body, name="flash_bwd", input_output_aliases=aliases,
        grid_spec=pltpu.PrefetchScalarGridSpec(
            num_scalar_prefetch=2, grid=grid,
            in_specs=[pl.BlockSpec(blk, lambda h, s, it, jt: (it[s], h)), pl.BlockSpec(blk, lambda h, s, it, jt: (it[s], hb + h)),
                      pl.BlockSpec(blk, lambda h, s, it, jt: (jt[s], h)), pl.BlockSpec(blk, lambda h, s, it, jt: (jt[s], hb + h)),
                      pl.BlockSpec((t, LANES), lambda h, s, it, jt: (jt[s], 0)),
                      pl.BlockSpec(blk, lambda h, s, it, jt: (it[s], h)), pl.BlockSpec(blk, lambda h, s, it, jt: (it[s], h)),
                      pl.BlockSpec(blk, lambda h, s, it, jt: (it[s], ob + h))] + s_in_specs,
            out_specs=[pl.BlockSpec((G, T, 2 * LANES), lambda h, s, it, jt: (h, 0, 0)),
                       pl.BlockSpec(blk, lambda h, s, it, jt: (jt[s], h)), pl.BlockSpec(blk, lambda h, s, it, jt: (jt[s], h)),
                       pl.BlockSpec((G, t, LANES), lambda h, s, it, jt: (h, jt[s], 0))] + s_out_specs,
            scratch_shapes=[pltpu.VMEM((G, t, 2 * LANES), F32), pltpu.VMEM(blk, F32), *s_sems]),
        out_shape=[jax.ShapeDtypeStruct((H, T, 2 * LANES), F32), jax.ShapeDtypeStruct((T, H * LANES), BF16),
                   jax.ShapeDtypeStruct((T, H * LANES), BF16), jax.ShapeDtypeStruct((H, T, LANES), F32)] + s_shapes,
        compiler_params=_params(("arbitrary", "arbitrary"),
                                12 * G * _nbytes((t, t), F32) + 2 * G * _nbytes((T, 2 * LANES), F32) + (8 << 20)),
    )(it, jt, q_att, q_att, kv, kv, kr, o, lse, dmix, *s_ins)


def _dq_post(dq, tabs, dims):
    H, T, _ = dq.shape
    tt = _tile(T, TILES["row"])

    def body(dq_ref, c_ref, s1_ref, s2_ref, o_ref):
        c, s1, s2 = c_ref[...], s1_ref[...], s2_ref[...]
        for hh in range(H):
            o_ref[:, hh * LANES:(hh + 1) * LANES] = (dq_ref[hh, :, :LANES] * ATT_SCALE).astype(BF16)
            o_ref[:, (H + hh) * LANES:(H + hh + 1) * LANES] = _rope_t(dq_ref[hh, :, LANES:] * ATT_SCALE, c, s1, s2).astype(BF16)

    tab = pl.BlockSpec((tt, LANES), lambda i: (i, 0))
    return pl.pallas_call(
        body, name="dq_post", grid=(T // tt,),
        in_specs=[pl.BlockSpec((H, tt, 2 * LANES), lambda i: (0, i, 0)), tab, tab, tab],
        out_specs=pl.BlockSpec((tt, 2 * H * LANES), lambda i: (i, 0)),
        out_shape=jax.ShapeDtypeStruct((T, 2 * H * LANES), BF16),
        compiler_params=_params(("parallel",), 8 * _nbytes((tt, 2 * H * LANES), F32)),
    )(dq, *tabs)


def _conv3(ge, cw, n):
    return cw[2:3] * ge + cw[1:2] * pltpu.roll(ge, 1, 0) + cw[0:1] * pltpu.roll(ge, 2, 0) + cw[3:4]


def _ffn_fwd(gate, up, cw8):
    T, F = gate.shape
    tt = _tile(T, TILES["ffn_row"])
    tc = _tile(F, TILES["ffn_c"], LANES)
    hb = tt // CONV_HALO

    def body(g_ref, gp_ref, u_ref, cw_ref, a_ref):
        it = pl.program_id(1)
        prev = jnp.where(it > 0, gp_ref[...].astype(F32), 0.0)
        ge = jnp.concatenate([prev, g_ref[...].astype(F32)], axis=0)
        gc = _conv3(ge, cw_ref[...], tt + CONV_HALO)[CONV_HALO:]
        a_ref[...] = (gc * _sigmoid(gc) * u_ref[...].astype(F32)).astype(BF16)

    blk = pl.BlockSpec((tt, tc), lambda jc, it: (it, jc))
    return pl.pallas_call(
        body, name="ffn_fwd", grid=(F // tc, T // tt),
        in_specs=[blk, pl.BlockSpec((CONV_HALO, tc), lambda jc, it: (jnp.maximum(it * hb - 1, 0), jc)), blk,
                  pl.BlockSpec((8, tc), lambda jc, it: (0, jc))],
        out_specs=blk, out_shape=jax.ShapeDtypeStruct((T, F), BF16),
        compiler_params=_params(("parallel", "parallel"), 16 * _nbytes((tt, tc), F32)),
    )(gate, gate, up, cw8)


def _ffn_bwd(da, gate, up, cw8):
    T, F = gate.shape
    tt = _tile(T, TILES["ffn_row"])
    tc = _tile(F, TILES["ffn_c"], LANES)
    hb = tt // CONV_HALO
    n_t = T // tt
    n = tt + 2 * CONV_HALO

    def body(da_ref, dan_ref, g_ref, gp_ref, gn_ref, u_ref, un_ref, cw_ref, dg_ref, du_ref, dcw_ref):
        it = pl.program_id(1)
        first, last = it == 0, it == n_t - 1
        cw = cw_ref[...]
        zeros = jnp.zeros((CONV_HALO, tc), F32)
        ge = jnp.concatenate([jnp.where(first, 0.0, gp_ref[...].astype(F32)), g_ref[...].astype(F32),
                              gn_ref[...].astype(F32)], axis=0)
        dae = jnp.concatenate([zeros, da_ref[...].astype(F32), jnp.where(last, 0.0, dan_ref[...].astype(F32))], axis=0)
        ue = jnp.concatenate([zeros, u_ref[...].astype(F32), un_ref[...].astype(F32)], axis=0)
        g1, g2 = pltpu.roll(ge, 1, 0), pltpu.roll(ge, 2, 0)
        gc = cw[2:3] * ge + cw[1:2] * g1 + cw[0:1] * g2 + cw[3:4]
        sg = _sigmoid(gc)
        dgc = dae * ue * (sg * (1.0 + gc * (1.0 - sg)))
        du_ref[...] = (dae * gc * sg)[CONV_HALO:CONV_HALO + tt].astype(BF16)
        dgp = cw[2:3] * dgc + cw[1:2] * pltpu.roll(dgc, n - 1, 0) + cw[0:1] * pltpu.roll(dgc, n - 2, 0)
        dg_ref[...] = dgp[CONV_HALO:CONV_HALO + tt].astype(BF16)
        mid = slice(CONV_HALO, CONV_HALO + tt)
        d_mid = dgc[mid]
        part = jnp.concatenate([jnp.sum(d_mid * g2[mid], axis=0, keepdims=True), jnp.sum(d_mid * g1[mid], axis=0, keepdims=True),
                                jnp.sum(d_mid * ge[mid], axis=0, keepdims=True), jnp.sum(d_mid, axis=0, keepdims=True),
                                jnp.zeros((4, tc), F32)], axis=0)

        @pl.when(first)
        def _():
            dcw_ref[...] = part

        @pl.when(it > 0)
        def _():
            dcw_ref[...] += part

    blk = pl.BlockSpec((tt, tc), lambda jc, it: (it, jc))
    prv = pl.BlockSpec((CONV_HALO, tc), lambda jc, it: (jnp.maximum(it * hb - 1, 0), jc))
    nxt = pl.BlockSpec((CONV_HALO, tc), lambda jc, it: (jnp.minimum((it + 1) * hb, T // CONV_HALO - 1), jc))
    cws = pl.BlockSpec((8, tc), lambda jc, it: (0, jc))
    return pl.pallas_call(
        body, name="ffn_bwd", grid=(F // tc, n_t), in_specs=[blk, nxt, blk, prv, nxt, blk, nxt, cws],
        out_specs=[blk, blk, cws],
        out_shape=[jax.ShapeDtypeStruct((T, F), BF16), jax.ShapeDtypeStruct((T, F), BF16), jax.ShapeDtypeStruct((8, F), F32)],
        compiler_params=_params(("parallel", "arbitrary"), 32 * _nbytes((tt, tc), F32)),
    )(da, da, gate, gate, gate, up, up, cw8)


def _ple_fwd(h2, gl, pe):
    T, D = h2.shape
    tt = _tile(T, TILES["row"])

    def body(h_ref, gl_ref, pe_ref, o_ref):
        o_ref[...] = h_ref[...] + pe_ref[...] * _sigmoid(gl_ref[...])

    row = pl.BlockSpec((tt, D), lambda i: (i, 0))
    return pl.pallas_call(
        body, name="ple_fwd", grid=(T // tt,), in_specs=[row, row, row], out_specs=row,
        out_shape=jax.ShapeDtypeStruct((T, D), F32), compiler_params=_params(("parallel",), 12 * _nbytes((tt, D), F32)),
    )(h2, gl, pe)


def _ple_bwd(dh, gl, pe):
    T, D = dh.shape
    tt = _tile(T, TILES["row"])

    def body(dh_ref, gl_ref, pe_ref, dpe_ref, dgl_ref):
        d = dh_ref[...]
        sg = _sigmoid(gl_ref[...])
        dpe_ref[...] = (d * sg).astype(BF16)
        dgl_ref[...] = (d * pe_ref[...] * (sg * (1.0 - sg))).astype(BF16)

    row = pl.BlockSpec((tt, D), lambda i: (i, 0))
    return pl.pallas_call(
        body, name="ple_bwd", grid=(T // tt,), in_specs=[row, row, row], out_specs=[row, row],
        out_shape=[jax.ShapeDtypeStruct((T, D), BF16), jax.ShapeDtypeStruct((T, D), BF16)],
        compiler_params=_params(("parallel",), 12 * _nbytes((tt, D), F32)),
    )(dh, gl, pe)


def _adamw(w, g, m, v, name):
    shape = w.shape
    cols = shape[-1]
    rows = math.prod(shape[:-1]) if len(shape) > 1 else 1
    w2, g2, m2, v2 = (a.reshape(rows, cols) for a in (w, g, m, v))
    tr = _tile(rows, max(8, (1 << 20) // (cols * 4)))
    c1 = 1.0 - ADAM_B1 ** ADAM_STEP
    c2 = 1.0 - ADAM_B2 ** ADAM_STEP

    def body(w_ref, g_ref, m_ref, v_ref, d_ref, mo_ref, vo_ref):
        gv = g_ref[...]
        mn = ADAM_B1 * m_ref[...] + (1.0 - ADAM_B1) * gv
        vn = ADAM_B2 * v_ref[...] + (1.0 - ADAM_B2) * (gv * gv)
        mo_ref[...] = mn
        vo_ref[...] = vn
        d_ref[...] = -ADAM_LR * ((mn / c1) / (jnp.sqrt(vn / c2) + ADAM_EPS) + ADAM_WD * w_ref[...])

    blk = pl.BlockSpec((tr, cols), lambda i: (i, 0))
    shp = jax.ShapeDtypeStruct((rows, cols), F32)
    outs = pl.pallas_call(
        body, name=name, grid=(rows // tr,), in_specs=[blk] * 4, out_specs=[blk] * 3, out_shape=[shp] * 3,
        compiler_params=_params(("parallel",), 16 * _nbytes((tr, cols), F32)),
    )(w2, g2, m2, v2)
    return tuple(o.reshape(shape) for o in outs)


_HBM = pl.BlockSpec(memory_space=pltpu.HBM)


def _place():
    x, y, c = lax.axis_index("x"), lax.axis_index("y"), lax.axis_index("c")
    return x, y, c, [(1 - x, y), (x, 1 - y), (1 - x, 1 - y)]


def _remote(src, dst, send_sems, recv_sems, k, to):
    return pltpu.make_async_remote_copy(src_ref=src, dst_ref=dst, send_sem=send_sems.at[k], recv_sem=recv_sems.at[k],
                                        device_id=to, device_id_type=MESH)


def _half(ref, lead, h):
    hr = ref.shape[-2] // 2
    return ref.at[(*lead, pl.ds(pl.multiple_of(h * hr, SUBLANES_BF16), hr))]


def _sem_pair(n):
    return [pltpu.SemaphoreType.DMA((n,)), pltpu.SemaphoreType.DMA((n,))]


def _run_side(side, name):
    n_in, n_out = len(side.ins), len(side.out_shapes)

    def body(*refs):
        parts = (refs[:n_in], refs[n_in:n_in + n_out]) + tuple(refs[n_in + n_out:])
        side.start(*parts)
        side.finish(*parts)

    return pl.pallas_call(
        body, name=name, in_specs=[_HBM] * n_in, out_specs=[_HBM] * n_out, out_shape=side.out_shapes,
        scratch_shapes=_sem_pair(side.n_sems),
    )(*side.ins)


def _whole(arrs, halves):
    return [(a, 0, arr.shape[-2] // (2 if halves else 1)) for a, arr in enumerate(arrs)]


def _plan(arrs, halves, big):
    whole = _whole(arrs, halves)
    q = whole[big][2] // 4
    return [[pc for pc in whole if pc[0] != big] + [(big, 0, q)]] + [[(big, k * q, q)] for k in (1, 2, 3)]


def _ride(fn, n_main, pieces, make, store):
    if pieces is None:
        return fn(None)
    side, touched = make(pieces)
    out = fn(side)
    store.update(zip(touched, out[n_main:]))
    return out[0] if n_main == 1 else out[:n_main]


def _carried(arrs, pieces, prior):
    touched = sorted({a for a, _, _ in pieces})
    pos = {a: i for i, a in enumerate(touched)}
    carried = [a for a in touched if a in prior]
    ins = [arrs[a] for a in touched] + [prior[a] for a in carried]
    return touched, pos, ins, {len(touched) + i: pos[a] for i, a in enumerate(carried)}


def _gather_side(arrs, layer, pieces, prior):
    touched, pos, ins_arrs, aliases = _carried(arrs, pieces, prior)

    def copies(ins, outs, send_sems, recv_sems, arriving):
        x, y, c, chips = _place()
        me, sib = 2 * x + y, (x, y, 1 - c)
        out = []
        for p, (a, r0, nr) in enumerate(pieces):
            src, dst = ins[pos[a]], outs[pos[a]]
            hr = src.shape[-2] // 2
            for hlf in range(2):
                rows = pl.ds(hlf * hr + r0, nr)
                out.append(_remote(src.at[layer, rows], dst.at[me, rows], send_sems, recv_sems, 5 * p + 3 + hlf, sib))
            rows = pl.ds(pl.multiple_of(c * hr + r0, SUBLANES_BF16), nr)
            for k, (cx, cy) in enumerate(chips):
                slot = 2 * cx + cy if arriving else me
                out.append(_remote(src.at[layer, rows], dst.at[slot, rows], send_sems, recv_sems, 5 * p + k, (cx, cy, c)))
        return out

    def start(ins, outs, send_sems, recv_sems):
        for cp in copies(ins, outs, send_sems, recv_sems, False):
            cp.start()

    def finish(ins, outs, send_sems, recv_sems):
        for cp in copies(ins, outs, send_sems, recv_sems, True):
            cp.wait_recv()
        for cp in copies(ins, outs, send_sems, recv_sems, False):
            cp.wait_send()

    shapes = [jax.ShapeDtypeStruct((N_SHARDS,) + arrs[a].shape[1:], arrs[a].dtype) for a in touched]
    return _Side(ins_arrs, shapes, 5 * len(pieces), start, finish, aliases), touched


def _gather_forward(arrs):
    n = len(arrs)

    def body(*refs):
        outs, send_sems, recv_sems = refs[n:2 * n], refs[2 * n], refs[2 * n + 1]
        x, y, c, chips = _place()
        sib = (x, y, 1 - c)
        sends = []
        for a in range(n):
            for k, (cx, cy) in enumerate(chips):
                got = _half(outs[a], (2 * cx + cy,), c)
                sends.append(_remote(got, got, send_sems, recv_sems, 3 * a + k, sib))
        for cp in sends:
            cp.start()
        for a in range(n):
            for k, (cx, cy) in enumerate(chips):
                got = _half(outs[a], (2 * cx + cy,), 1 - c)
                _remote(got, got, send_sems, recv_sems, 3 * a + k, sib).wait_recv()
        for cp in sends:
            cp.wait_send()

    return pl.pallas_call(
        body, name="gather_forward", in_specs=[_HBM] * n, out_specs=[_HBM] * n,
        out_shape=[jax.ShapeDtypeStruct(a.shape, a.dtype) for a in arrs],
        input_output_aliases={a: a for a in range(n)}, scratch_shapes=_sem_pair(3 * n),
    )(*arrs)


def _sibling_exchange(arrs):
    n = len(arrs)

    def body(*refs):
        ins, outs, send_sems, recv_sems = refs[:n], refs[n:2 * n], refs[2 * n], refs[2 * n + 1]
        x, y, c, _ = _place()
        sib = (x, y, 1 - c)
        sends = [_remote(_half(ins[a], (s,), 1 - c), outs[a].at[s], send_sems, recv_sems, N_SHARDS * a + s, sib)
                 for a in range(n) for s in range(N_SHARDS)]
        for cp in sends:
            cp.start()
        for a in range(n):
            for s in range(N_SHARDS):
                _remote(_half(ins[a], (s,), c), outs[a].at[s], send_sems, recv_sems, N_SHARDS * a + s, sib).wait_recv()
        for cp in sends:
            cp.wait_send()

    return pl.pallas_call(
        body, name="rs_sibling", in_specs=[_HBM] * n, out_specs=[_HBM] * n,
        out_shape=[jax.ShapeDtypeStruct((N_SHARDS, a.shape[1] // 2, a.shape[2]), a.dtype) for a in arrs],
        scratch_shapes=_sem_pair(N_SHARDS * n),
    )(*arrs)


def _chip_side(arrs, pieces, prior):
    touched, pos, ins_arrs, aliases = _carried(arrs, pieces, prior)

    def copies(ins, outs, send_sems, recv_sems):
        x, y, c, chips = _place()
        return [_remote(ins[pos[a]].at[2 * cx + cy, pl.ds(r0, nr)], outs[pos[a]].at[k, pl.ds(r0, nr)], send_sems, recv_sems,
                        3 * p + k, (cx, cy, c))
                for p, (a, r0, nr) in enumerate(pieces) for k, (cx, cy) in enumerate(chips)]

    def start(ins, outs, send_sems, recv_sems):
        for cp in copies(ins, outs, send_sems, recv_sems):
            cp.start()

    def finish(ins, outs, send_sems, recv_sems):
        for cp in copies(ins, outs, send_sems, recv_sems):
            cp.wait_recv()
        for cp in copies(ins, outs, send_sems, recv_sems):
            cp.wait_send()

    shapes = [jax.ShapeDtypeStruct((3,) + arrs[a].shape[1:], arrs[a].dtype) for a in touched]
    return _Side(ins_arrs, shapes, 3 * len(pieces), start, finish, aliases), touched


def _sibling_share(arrs):
    n = len(arrs)

    def body(*refs):
        outs, send_sems, recv_sems = refs[n:2 * n], refs[2 * n], refs[2 * n + 1]
        x, y, c, _ = _place()
        sib = (x, y, 1 - c)
        sends = [_remote(outs[a].at[c], outs[a].at[c], send_sems, recv_sems, a, sib) for a in range(n)]
        for cp in sends:
            cp.start()
        for a in range(n):
            _remote(outs[a].at[c], outs[a].at[1 - c], send_sems, recv_sems, a, sib).wait_recv()
        for cp in sends:
            cp.wait_send()

    return pl.pallas_call(
        body, name="rs_share", in_specs=[_HBM] * n, out_specs=[_HBM] * n,
        out_shape=[jax.ShapeDtypeStruct(a.shape, a.dtype) for a in arrs],
        input_output_aliases={a: a for a in range(n)}, scratch_shapes=_sem_pair(n),
    )(*arrs)


def _add_sibling(g, sib_in, place):
    S, rows, cols = g.shape
    hr = rows // 2
    tr = _tile(hr, max(SUBLANES_BF16, TILES["add_bytes"] // (cols * 2)), SUBLANES_BF16)
    nb = hr // tr

    def body(p_ref, a_ref, b_ref, o_ref):
        o_ref[...] = (a_ref[...].astype(F32) + b_ref[...].astype(F32)).astype(o_ref.dtype)

    blk = pl.BlockSpec((None, tr, cols), lambda s, r, p: (s, r, 0))
    return pl.pallas_call(
        body, name="rs_add_sibling",
        grid_spec=pltpu.PrefetchScalarGridSpec(
            num_scalar_prefetch=1, grid=(S, nb),
            in_specs=[pl.BlockSpec((None, tr, cols), lambda s, r, p: (s, p[1] * nb + r, 0)), blk], out_specs=blk),
        out_shape=jax.ShapeDtypeStruct((S, hr, cols), g.dtype),
        compiler_params=_params(("parallel", "parallel"), 16 * _nbytes((tr, cols), F32)),
    )(place, g, sib_in)


def _add_chips(cs, got, place):
    S, r, cols = cs.shape
    tr = _tile(r, max(SUBLANES_BF16, TILES["add_bytes"] // (cols * 2)), SUBLANES_BF16)

    def body(p_ref, a_ref, b_ref, o_ref):
        acc = a_ref[...].astype(F32)
        for k in range(3):
            acc = acc + b_ref[k].astype(F32)
        o_ref[...] = acc

    return pl.pallas_call(
        body, name="rs_add_chips",
        grid_spec=pltpu.PrefetchScalarGridSpec(
            num_scalar_prefetch=1, grid=(r // tr,),
            in_specs=[pl.BlockSpec((None, tr, cols), lambda i, p: (p[0], i, 0)),
                      pl.BlockSpec((3, tr, cols), lambda i, p: (0, i, 0))],
            out_specs=pl.BlockSpec((None, tr, cols), lambda i, p: (p[1], i, 0))),
        out_shape=jax.ShapeDtypeStruct((2, r, cols), F32),
        compiler_params=_params(("parallel",), 24 * _nbytes((tr, cols), F32)),
    )(place, cs, got)


def _reduce_begin(arrs, place):
    return [_add_sibling(g, s, place) for g, s in zip(arrs, _sibling_exchange(arrs))]


def _reduce_end(sums, got, place):
    halves = [_add_chips(cs, g, place) for cs, g in zip(sums, got)]
    return [f.reshape(-1, f.shape[-1]) for f in _sibling_share(halves)]


def _all_reduce_small(v):
    R = v.shape[0]

    def body(v_ref, o_ref, buf, send_sems, recv_sems):
        x, y, c, _ = _place()
        me = 4 * x + 2 * y + c
        buf[me] = v_ref[...]
        sends = []
        for k in range(1, 8):
            px = 1 - x if k & 4 else x
            py = 1 - y if k & 2 else y
            pc = 1 - c if k & 1 else c
            sends.append(_remote(v_ref, buf.at[me], send_sems, recv_sems, k - 1, (px, py, pc)))
        for cp in sends:
            cp.start()
        for k in range(1, 8):
            px = 1 - x if k & 4 else x
            py = 1 - y if k & 2 else y
            pc = 1 - c if k & 1 else c
            _remote(v_ref, buf.at[4 * px + 2 * py + pc], send_sems, recv_sems, k - 1, (px, py, pc)).wait_recv()
        for cp in sends:
            cp.wait_send()
        acc = buf[0]
        for d in range(1, 8):
            acc = acc + buf[d]
        o_ref[...] = acc

    vm = pl.BlockSpec(memory_space=pltpu.VMEM)
    return pl.pallas_call(
        body, name="all_reduce_small", in_specs=[vm], out_specs=vm, out_shape=jax.ShapeDtypeStruct(v.shape, F32),
        scratch_shapes=[pltpu.VMEM((8, R, LANES), F32), pltpu.SemaphoreType.DMA((7,)), pltpu.SemaphoreType.DMA((7,))],
    )(v)


def _pad_to(a, n):
    return a if a.shape[0] == n else jnp.pad(a, (0, n - a.shape[0]))


def _piece_len(shape):
    return -(-math.prod(shape) // PACK_ALIGN) * PACK_ALIGN


def _pack(pieces, dtype):
    flat = jnp.concatenate([_pad_to(a.reshape(-1).astype(dtype), _piece_len(a.shape)) for a in pieces])
    return flat.reshape(-1, LANES)


def _unpack(flat, shapes, lead):
    flat = flat.reshape(lead + (-1,))
    out, off = [], 0
    for shp in shapes:
        out.append(flat[..., off:off + math.prod(shp)].reshape(lead + tuple(shp)))
        off += _piece_len(shp)
    return out


def _join(name, a):
    if name in COL_SHARDED:
        return a.transpose(1, 0, 2).reshape(a.shape[1], -1)
    if name in ROW_SHARDED:
        return a.reshape(-1, a.shape[-1])
    return a.transpose(1, 0, 2, 3).reshape(a.shape[1], -1, a.shape[-1])


def _split(name, a):
    if name in COL_SHARDED:
        return a.reshape(a.shape[0], N_SHARDS, -1).transpose(1, 0, 2)
    if name in ROW_SHARDED:
        return a.reshape(N_SHARDS, -1, a.shape[-1])
    return a.reshape(a.shape[0], N_SHARDS, -1, a.shape[-1]).transpose(1, 0, 2, 3)


def _heads_split(w, H, first, second, pad_second):
    K = w.shape[0]
    w3 = w.reshape(K, H, first + second)
    b = w3[:, :, first:]
    if pad_second > second:
        b = jnp.pad(b, ((0, 0), (0, 0), (0, pad_second - second)))
    return jnp.concatenate([w3[:, :, :first].reshape(K, -1), b.reshape(K, -1)], axis=1)


def _heads_merge(w, H, first, second, pad_second):
    K = w.shape[0]
    a = w[:, :H * first].reshape(K, H, first)
    b = w[:, H * first:].reshape(K, H, pad_second)[:, :, :second]
    return jnp.concatenate([a, b], axis=2).reshape(K, -1)


def kernel(x, p, positions, norm_mix_g, w_in, pool_w, pool_scale, q_norm_g, w_uq, kv_norm_g, w_ukv, w_out, norm_ffn_g, w_up, conv_w, conv_b, w_down, norm_ple_g, w_ple, w_ple_gate, final_norm_g, loss_target, m_norm_mix_g, m_w_in, m_pool_w, m_pool_scale, m_q_norm_g, m_w_uq, m_kv_norm_g, m_w_ukv, m_w_out, m_norm_ffn_g, m_w_up, m_conv_w, m_conv_b, m_w_down, m_norm_ple_g, m_w_ple, m_w_ple_gate, m_final_norm_g, v_norm_mix_g, v_w_in, v_pool_w, v_pool_scale, v_q_norm_g, v_w_uq, v_kv_norm_g, v_w_ukv, v_w_out, v_norm_ffn_g, v_w_up, v_conv_w, v_conv_b, v_w_down, v_norm_ple_g, v_w_ple, v_w_ple_gate, v_final_norm_g):
    W = dict(norm_mix_g=norm_mix_g, w_in=w_in, pool_w=pool_w, pool_scale=pool_scale, q_norm_g=q_norm_g, w_uq=w_uq,
             kv_norm_g=kv_norm_g, w_ukv=w_ukv, w_out=w_out, norm_ffn_g=norm_ffn_g, w_up=w_up, conv_w=conv_w, conv_b=conv_b,
             w_down=w_down, norm_ple_g=norm_ple_g, w_ple=w_ple, w_ple_gate=w_ple_gate, final_norm_g=final_norm_g)
    M1 = dict(norm_mix_g=m_norm_mix_g, w_in=m_w_in, pool_w=m_pool_w, pool_scale=m_pool_scale, q_norm_g=m_q_norm_g, w_uq=m_w_uq,
              kv_norm_g=m_kv_norm_g, w_ukv=m_w_ukv, w_out=m_w_out, norm_ffn_g=m_norm_ffn_g, w_up=m_w_up, conv_w=m_conv_w,
              conv_b=m_conv_b, w_down=m_w_down, norm_ple_g=m_norm_ple_g, w_ple=m_w_ple, w_ple_gate=m_w_ple_gate,
              final_norm_g=m_final_norm_g)
    M2 = dict(norm_mix_g=v_norm_mix_g, w_in=v_w_in, pool_w=v_pool_w, pool_scale=v_pool_scale, q_norm_g=v_q_norm_g, w_uq=v_w_uq,
              kv_norm_g=v_kv_norm_g, w_ukv=v_w_ukv, w_out=v_w_out, norm_ffn_g=v_norm_ffn_g, w_up=v_w_up, conv_w=v_conv_w,
              conv_b=v_conv_b, w_down=v_w_down, norm_ple_g=v_norm_ple_g, w_ple=v_w_ple, w_ple_gate=v_w_ple_gate,
              final_norm_g=v_final_norm_g)

    _, T, D = x.shape
    L = p.shape[0]
    P, QL, KL, F = pool_scale.shape[-1], q_norm_g.shape[-1], kv_norm_g.shape[-1], conv_b.shape[-1]
    C = pool_w.shape[-1]
    H = (D - P) // V_DIM
    d_in = P + QL + KL + ROPE_DIM
    dims = dict(P=P, QL=QL, KL=KL, C=C, H=H)
    misc_shapes = [W[n].shape[1:] for n in MISC]
    ns_in, ns_up, ns_conv = w_in.shape[-1], w_up.shape[-1], conv_w.shape[-1]

    xi, yi, ci = lax.axis_index("x"), lax.axis_index("y"), lax.axis_index("c")
    me = 2 * xi + yi
    place = jnp.stack([me, ci]).astype(jnp.int32)

    def all_reduce(parts):
        flat = jnp.concatenate(parts)
        padded = -(-flat.shape[0] // (8 * LANES)) * (8 * LANES)
        return _all_reduce_small(_pad_to(flat, padded).reshape(-1, LANES)).reshape(-1)

    inv_freq = 1.0 / (ROPE_THETA ** (jnp.arange(0, ROPE_DIM, 2, dtype=F32) / ROPE_DIM))
    inv_lane = jnp.concatenate([inv_freq, inv_freq, jnp.zeros((LANES - ROPE_DIM,), F32)]).reshape(1, LANES)
    tabs = _rope_tables(positions.reshape(T, 1).astype(F32), inv_lane)

    local = [W[n].astype(BF16) for n in BIG] + [jnp.stack([_pack([W[n][l] for n in MISC], BF16) for l in range(L)])]
    placed = lax.dynamic_update_slice(jnp.zeros((L, CONV_TAPS, F), F32), conv_w, (0, 0, me * ns_conv))
    conv_full = all_reduce([jnp.where(ci == 0, placed, 0.0).reshape(-1)])[:L * CONV_TAPS * F].reshape(L, CONV_TAPS, F)

    def layout(got, l):
        g = dict(zip(BIG, got[:-1]))
        misc = {n: _join(n, a) for n, a in zip(MISC, _unpack(got[-1], misc_shapes, (N_SHARDS,)))}
        return dict(
            w_in=jnp.concatenate([g["w_in"][sh] for sh in range(N_SHARDS)] + [jnp.zeros((D, LANES - ROPE_DIM), BF16)], axis=1),
            w_out=g["w_out"].reshape(-1, D), w_down=g["w_down"].reshape(-1, D), w_ple_gate=g["w_ple_gate"].reshape(-1, D),
            w_up=g["w_up"], w_ple=misc["w_ple"], pool_w=misc["pool_w"],
            w_uq=_heads_split(misc["w_uq"], H, NOPE_DIM, ROPE_DIM, LANES),
            w_ukv=_heads_split(misc["w_ukv"], H, NOPE_DIM, V_DIM, V_DIM),
            cw8=jnp.concatenate([conv_full[l], conv_b[l][None], jnp.zeros((4, F), F32)], axis=0))

    half_up = (0, N_SHARDS // 2), (N_SHARDS // 2, N_SHARDS // 2)

    h = x[0]
    saved, FW = [], []
    up_at = BIG.index("w_up")
    arriving = _gather_forward(_run_side(_gather_side(local, 0, _whole(local, True), {})[0], "all_gather"))
    for l in range(L):
        fw = layout(arriving, l)
        FW.append(fw)
        s = dict(h0=h)
        nxt = {}
        parts = _plan(local, True, up_at) if l + 1 < L else [None] * 4

        def gather(pieces):
            return _gather_side(local, l + 1, pieces, nxt)
        s["n1"] = _rms_fwd(h, norm_mix_g[l], "norm_mix")
        s["u"] = _matmul(s["n1"], fw["w_in"], "nn", F32, "mm_in", tm=512, tn=d_in + LANES - ROPE_DIM)
        s["diff"], s["cqn"], s["ckvn"], s["kr"] = _post_u(s["u"], q_norm_g[l], kv_norm_g[l], tabs, dims)
        s["q"] = _q_rope(_matmul(s["cqn"], fw["w_uq"], "nn", F32, "mm_uq", tn=2 * H * LANES), tabs, dims)
        s["kv"] = _matmul(s["ckvn"], fw["w_ukv"], "nn", BF16, "mm_ukv", tn=2 * H * LANES)
        s["o"], s["lse"] = _ride(lambda sd: _flash_fwd(s["q"], s["kv"], s["kr"], dims, sd), 2, parts[0], gather, nxt)
        s["mix"] = jnp.concatenate([_pool_fwd(s["diff"], fw["pool_w"], pool_scale[l], dims), s["o"]], axis=1)
        s["h1"] = _matmul(s["mix"], fw["w_out"], "nn", F32, "mm_out", res=h, tm=512)
        s["n2"] = _rms_fwd(s["h1"], norm_ffn_g[l], "norm_ffn")
        s["gate"] = _ride(lambda sd: _matmul(s["n2"], fw["w_up"], "nn", BF16, "mm_gate", tm=512, tn=ns_up // 2, b_shards=half_up[0],
                                             side=sd), 1, parts[1], gather, nxt)
        s["up"] = _ride(lambda sd: _matmul(s["n2"], fw["w_up"], "nn", BF16, "mm_up", tm=512, tn=ns_up // 2, b_shards=half_up[1],
                                           side=sd), 1, parts[2], gather, nxt)
        s["a"] = _ffn_fwd(s["gate"], s["up"], fw["cw8"])
        s["h2"] = _ride(lambda sd: _matmul(s["a"], fw["w_down"], "nn", F32, "mm_down", res=s["h1"], tm=512,
                                           tk=F // 2 if F % (2 * LANES) == 0 else F, side=sd), 1, parts[3], gather, nxt)
        if nxt:
            arriving = _gather_forward([nxt[a] for a in range(len(local))])
        s["n3"] = _rms_fwd(s["h2"], norm_ple_g[l], "norm_ple")
        s["gl"] = _matmul(s["n3"], fw["w_ple_gate"], "nn", F32, "mm_ple_gate", tm=512)
        s["pe"] = _matmul(p[l, 0], fw["w_ple"], "nn", F32, "mm_ple", tn=D)
        h = _ple_fwd(s["h2"], s["gl"], s["pe"])
        saved.append(s)

    dh, dhb, dg_final, loss_part = _final_loss(h, loss_target[0], final_norm_g)
    loss = lax.psum(loss_part[0, 0], ("x", "y", "c"))

    small = {}
    reduced = [None] * L
    waiting = None
    for l in reversed(range(L)):
        fw, s = FW[l], saved[l]
        gw = {}
        got = {}
        parts = _plan(waiting, False, up_at) if waiting else [None] * 4

        def chips(pieces):
            return _chip_side(waiting, pieces, got)

        dpe, dgl = _ple_bwd(dh, s["gl"], s["pe"])
        gw["w_ple"] = _matmul(p[l, 0], dpe, "tn", BF16, "dw_ple", tm=512)
        gw["w_ple_gate"] = _matmul(s["n3"], dgl, "tn", BF16, "dw_ple_gate")
        dn3 = _matmul(dgl, fw["w_ple_gate"], "nt", F32, "dx_ple_gate", tm=512)
        dh, dhb, small["norm_ple_g", l] = _rms_bwd(dn3, s["h2"], norm_ple_g[l], dh, "norm_ple_bwd")

        da = _matmul(dhb, fw["w_down"], "nt", BF16, "dx_down", tn=F // 4 if F % (4 * LANES) == 0 else F)
        gw["w_down"] = _matmul(s["a"], dhb, "tn", BF16, "dw_down")
        dgate, dup, dcw = _ffn_bwd(da, s["gate"], s["up"], fw["cw8"])
        small["conv_w", l], small["conv_b", l] = dcw[:CONV_TAPS], dcw[CONV_TAPS:CONV_TAPS + 1]
        gw["w_up"] = _ride(lambda sd: _matmul(s["n2"], dgate, "tn", BF16, "dw_gate", tn=ns_up // 2,
                                              out_shards=(half_up[0][0], N_SHARDS, ns_up), side=sd), 1, parts[1], chips, got)
        gw["w_up"] = _ride(lambda sd: _matmul(s["n2"], dup, "tn", BF16, "dw_up", tn=ns_up // 2,
                                              out_shards=(half_up[1][0], N_SHARDS, ns_up), carry=gw["w_up"], side=sd), 1, parts[2], chips, got)
        dn2 = _ride(lambda sd: _matmul(dgate, fw["w_up"], "nt", F32, "dx_gate", tm=512, tn=D, tk=ns_up // 2, b_shards=half_up[0],
                                       side=sd), 1, parts[3], chips, got)
        dn2 = _matmul(dup, fw["w_up"], "nt", F32, "dx_up", res=dn2, tm=512, tn=D, tk=ns_up // 2, b_shards=half_up[1])
        dh, dhb, small["norm_ffn_g", l] = _rms_bwd(dn2, s["h1"], norm_ffn_g[l], dh, "norm_ffn_bwd")

        dmix = _matmul(dhb, fw["w_out"], "nt", BF16, "dx_out")
        gw["w_out"] = _matmul(s["mix"], dhb, "tn", BF16, "dw_out")
        ddiff, gw["pool_w"], small["pool_scale", l] = _pool_bwd(dmix, s["diff"], fw["pool_w"], pool_scale[l], dims)
        dq, dkn, dv, dkr = _ride(lambda sd: _flash_bwd(s["q"], s["kv"], s["kr"], s["o"], s["lse"], dmix, dims, sd), 4, parts[0], chips, got)
        if got:
            reduced[l + 1] = _reduce_end(waiting, [got[a] for a in range(len(waiting))], place)
        dqb = _dq_post(dq, tabs, dims)
        dkv = jnp.concatenate([dkn, dv], axis=1)
        gw["w_uq"] = _heads_merge(_matmul(s["cqn"], dqb, "tn", BF16, "dw_uq", tn=2 * H * LANES), H, NOPE_DIM, ROPE_DIM, LANES)
        gw["w_ukv"] = _heads_merge(_matmul(s["ckvn"], dkv, "tn", BF16, "dw_ukv", tn=2 * H * LANES), H, NOPE_DIM, V_DIM, V_DIM)
        dcqn = _matmul(dqb, fw["w_uq"], "nt", F32, "dx_uq")
        dckvn = _matmul(dkv, fw["w_ukv"], "nt", F32, "dx_ukv")
        du, small["q_norm_g", l], small["kv_norm_g", l] = _pre_u_bwd(s["u"], dcqn, dckvn, ddiff, dkr, q_norm_g[l], kv_norm_g[l], tabs, dims)
        gw["w_in"] = _matmul(s["n1"], du, "tn", BF16, "dw_in", tm=512, tn=du.shape[1])[:, :d_in]
        dn1 = _matmul(du, fw["w_in"], "nt", F32, "dx_in", tm=512, tk=du.shape[1])
        dh, dhb, small["norm_mix_g", l] = _rms_bwd(dn1, s["h0"], norm_mix_g[l], dh, "norm_mix_bwd")

        split = {n: _split(n, gw[n]) for n in MISC}
        arrs = [jnp.stack([gw["w_in"][:, sh * ns_in:(sh + 1) * ns_in] for sh in range(N_SHARDS)]),
                gw["w_out"].reshape(N_SHARDS, -1, D), gw["w_up"], gw["w_down"].reshape(N_SHARDS, -1, D),
                gw["w_ple_gate"].reshape(N_SHARDS, -1, D),
                jnp.stack([_pack([split[n][sh] for n in MISC], BF16) for sh in range(N_SHARDS)])]
        waiting = _reduce_begin(arrs, place)
    reduced[0] = _reduce_end(waiting, _run_side(_chip_side(waiting, _whole(waiting, False), {})[0], "rs_chips"), place)

    grads = {n: jnp.stack([reduced[l][k].reshape(W[n].shape[1:]) for l in range(L)]) for k, n in enumerate(BIG)}
    per_layer = [_unpack(reduced[l][-1], misc_shapes, ()) for l in range(L)]
    for k, n in enumerate(MISC):
        grads[n] = jnp.stack([per_layer[l][k] for l in range(L)])

    small_names = ("norm_mix_g", "pool_scale", "q_norm_g", "kv_norm_g", "norm_ffn_g", "conv_b", "norm_ple_g", "conv_w")
    summed = all_reduce([small[n, l].reshape(-1) for n in small_names for l in range(L)] + [dg_final.reshape(-1)])
    off = 0
    for n in small_names:
        size = CONV_TAPS * F if n == "conv_w" else W[n].shape[-1]
        grads[n] = summed[off:off + L * size].reshape((L, CONV_TAPS, F) if n == "conv_w" else (L, size))
        off += L * size
    grads["final_norm_g"] = summed[off:off + D]
    grads["conv_w"] = lax.dynamic_slice(grads["conv_w"], (0, 0, me * ns_conv), (L, CONV_TAPS, ns_conv))

    deltas, new_m, new_v = {}, {}, {}
    for n in WEIGHTS:
        deltas[n], new_m[n], new_v[n] = _adamw(W[n], grads[n], M1[n], M2[n], "adamw_" + n)

    return (loss, dh[None], *[grads[n] for n in WEIGHTS], *[deltas[n] for n in WEIGHTS],
            *[new_m[n] for n in WEIGHTS], *[new_v[n] for n in WEIGHTS])
```

```python
import functools
import math

import jax
import jax.numpy as jnp
from jax import lax
from jax.experimental import pallas as pl
from jax.experimental.pallas import tpu as pltpu

F32 = jnp.float32
BF16 = jnp.bfloat16

NOPE_DIM = 128
ROPE_DIM = 64
V_DIM = 128
LANES = 128
SUBLANES_BF16 = 16
ROPE_THETA = 10000.0
EPS = 1e-6
POOL_WINDOWS = (2, 4, 8, 16)
POOL_HALO = 16
CONV_TAPS = 3
CONV_HALO = 8
ADAM_LR = 0.001
ADAM_B1 = 0.9
ADAM_B2 = 0.999
ADAM_EPS = 1e-08
ADAM_WD = 0.01
ADAM_STEP = 10
NEG_BIG = -1e30
ATT_SCALE = 1.0 / math.sqrt(NOPE_DIM + ROPE_DIM)
V7X_VMEM_BYTES = 64 * 2 ** 20
N_SHARDS = 4
PACK_ALIGN = 2 * SUBLANES_BF16 * LANES

TILES = dict(row=256, att=512, mm_m=1024, mm_n=1024, mm_k=2048, ffn_row=512, ffn_c=512, add_bytes=1 << 20)

BIG = ("w_in", "w_out", "w_up", "w_down", "w_ple_gate")
MISC = ("w_uq", "w_ukv", "w_ple", "pool_w")
COL_SHARDED = ("w_in", "w_uq", "w_ukv", "w_up", "conv_w", "w_ple")
ROW_SHARDED = ("w_out", "w_down", "w_ple_gate")
WEIGHTS = ("norm_mix_g", "w_in", "pool_w", "pool_scale", "q_norm_g", "w_uq", "kv_norm_g", "w_ukv", "w_out",
           "norm_ffn_g", "w_up", "conv_w", "conv_b", "w_down", "norm_ple_g", "w_ple", "w_ple_gate", "final_norm_g")
MESH = pl.DeviceIdType.MESH


def _nbytes(shape, dtype):
    return math.prod(shape) * jnp.dtype(dtype).itemsize


def _params(sem, need_bytes):
    limit = min(V7X_VMEM_BYTES - (8 << 20), max(32 << 20, int(need_bytes)))
    return pltpu.CompilerParams(dimension_semantics=sem, vmem_limit_bytes=limit)


def _tile(n, want, mult=8):
    if n <= want:
        return n
    for t in range(want - want % mult, 0, -mult):
        if n % t == 0:
            return t
    return n


def _sigmoid(x):
    return 1.0 / (1.0 + jnp.exp(-x))


def _rstd(x):
    return lax.rsqrt(jnp.mean(x * x, axis=-1, keepdims=True) + EPS)


_DOT_DIMS = {"nn": (((1,), (0,)), ((), ())), "nt": (((1,), (1,)), ((), ())), "tn": (((0,), (0,)), ((), ()))}


def _matmul(a, b, mode, out_dtype, name, res=None, tm=None, tn=None, tk=None, b_shards=None, out_shards=None, carry=None,
            side=None):
    if mode == "nn":
        (M, K), N = a.shape, b.shape[-1] * (b_shards[1] if b_shards else 1)
    elif mode == "nt":
        (M, K), N = a.shape, b.shape[-2]
    else:
        (K, M), N = a.shape, b.shape[1]
    per = b.shape[-1] if b_shards else (out_shards[2] if out_shards else None)
    tm = _tile(M, tm or TILES["mm_m"], LANES)
    tn = _tile(per if (per and mode != "nt") else N, tn or TILES["mm_n"], LANES)
    tk = _tile(per if (per and mode == "nt") else K, tk or TILES["mm_k"], LANES)
    nk = K // tk
    has_res = res is not None
    has_carry = carry is not None
    dims = _DOT_DIMS[mode]

    def body(*refs):
        a_ref, b_ref = refs[0], refs[1]
        o_ref = refs[2 + has_res + has_carry]
        part = lax.dot_general(a_ref[...].astype(BF16), b_ref[...].astype(BF16), dims, preferred_element_type=F32)

        def finish(acc):
            if has_res:
                acc = acc + refs[2][...]
            o_ref[...] = acc.astype(o_ref.dtype)

        if nk == 1:
            finish(part)
        else:
            acc_ref = refs[3 + has_res + has_carry]
            k = pl.program_id(2)

            @pl.when(k == 0)
            def _():
                acc_ref[...] = part

            @pl.when(k > 0)
            def _():
                acc_ref[...] += part

            @pl.when(k == nk - 1)
            def _():
                finish(acc_ref[...])

    if mode == "nn":
        a_spec, b_spec = pl.BlockSpec((tm, tk), lambda i, j, k: (i, k)), pl.BlockSpec((tk, tn), lambda i, j, k: (k, j))
    elif mode == "nt":
        a_spec, b_spec = pl.BlockSpec((tm, tk), lambda i, j, k: (i, k)), pl.BlockSpec((tn, tk), lambda i, j, k: (j, k))
    else:
        a_spec, b_spec = pl.BlockSpec((tk, tm), lambda i, j, k: (k, i)), pl.BlockSpec((tk, tn), lambda i, j, k: (k, j))
    o_spec = pl.BlockSpec((tm, tn), lambda i, j, k: (i, j))
    out_shape = jax.ShapeDtypeStruct((M, N), out_dtype)
    if b_shards:
        first = b_shards[0]
        if mode == "nn":
            nps = per // tn
            b_spec = pl.BlockSpec((None, tk, tn), lambda i, j, k: (first + j // nps, k, j % nps))
        else:
            kps = per // tk
            b_spec = pl.BlockSpec((None, tn, tk), lambda i, j, k: (first + k // kps, j, k % kps))
    if out_shards:
        ofirst, nps_o = out_shards[0], per // tn
        o_spec_out = pl.BlockSpec((None, tm, tn), lambda i, j, k: (ofirst + j // nps_o, i, j % nps_o))
        out_shape = jax.ShapeDtypeStruct((out_shards[1], M, per), out_dtype)
    else:
        o_spec_out = o_spec
    in_specs, args = [a_spec, b_spec], [a, b]
    need = 2 * (_nbytes((tm, tk), a.dtype) + _nbytes((tk, tn), b.dtype) + _nbytes((tm, tn), out_dtype)) + 2 * _nbytes((tm, tn), F32)
    if has_res:
        in_specs.append(o_spec)
        args.append(res)
        need += 2 * _nbytes((tm, tn), res.dtype)
    aliases = {}
    if has_carry:
        aliases = {len(args): 0}
        in_specs.append(pl.BlockSpec(memory_space=pl.ANY))
        args.append(carry)
    scratch = [pltpu.VMEM((tm, tn), F32)] if nk > 1 else []
    grid = (M // tm, N // tn, nk)
    params = _params(("parallel", "parallel", "arbitrary") if side is None else ("arbitrary",) * 3, need + (4 << 20))
    if side is None:
        return pl.pallas_call(body, name=name, grid=grid, in_specs=in_specs, out_specs=o_spec_out, out_shape=out_shape,
                              scratch_shapes=scratch, input_output_aliases=aliases, compiler_params=params)(*args)
    body, s_ins, s_in_specs, s_shapes, s_out_specs, s_sems, s_aliases = _attach(side, body, len(args), 1, grid)
    return pl.pallas_call(
        body, name=name, grid=grid, in_specs=in_specs + s_in_specs, out_specs=[o_spec_out] + s_out_specs,
        out_shape=[out_shape] + s_shapes, scratch_shapes=scratch + s_sems, input_output_aliases={**aliases, **s_aliases},
        compiler_params=params,
    )(*args, *s_ins)


def _rms_fwd(x, g, name):
    T, D = x.shape
    tt = _tile(T, TILES["row"])

    def body(x_ref, g_ref, o_ref):
        xv = x_ref[...]
        o_ref[...] = (xv * _rstd(xv) * g_ref[...]).astype(o_ref.dtype)

    row = pl.BlockSpec((tt, D), lambda i: (i, 0))
    return pl.pallas_call(
        body, name=name, grid=(T // tt,), in_specs=[row, pl.BlockSpec((1, D), lambda i: (0, 0))], out_specs=row,
        out_shape=jax.ShapeDtypeStruct((T, D), BF16), compiler_params=_params(("parallel",), 8 * _nbytes((tt, D), F32)),
    )(x, g.reshape(1, D))


def _rms_bwd(dn, x, g, dres, name):
    T, D = x.shape
    tt = _tile(T, TILES["row"])

    def body(dn_ref, x_ref, g_ref, dres_ref, dx_ref, dxb_ref, dg_ref):
        i = pl.program_id(0)
        xv = x_ref[...]
        r = _rstd(xv)
        xh = xv * r
        dnv = dn_ref[...].astype(F32)
        dxh = dnv * g_ref[...]
        tot = dres_ref[...] + r * (dxh - xh * jnp.mean(dxh * xh, axis=-1, keepdims=True))
        dx_ref[...] = tot
        dxb_ref[...] = tot.astype(BF16)
        part = jnp.sum(dnv * xh, axis=0, keepdims=True)

        @pl.when(i == 0)
        def _():
            dg_ref[...] = part

        @pl.when(i > 0)
        def _():
            dg_ref[...] += part

    row = pl.BlockSpec((tt, D), lambda i: (i, 0))
    vec = pl.BlockSpec((1, D), lambda i: (0, 0))
    return pl.pallas_call(
        body, name=name, grid=(T // tt,), in_specs=[row, row, vec, row], out_specs=[row, row, vec],
        out_shape=[jax.ShapeDtypeStruct((T, D), F32), jax.ShapeDtypeStruct((T, D), BF16), jax.ShapeDtypeStruct((1, D), F32)],
        compiler_params=_params(("arbitrary",), 16 * _nbytes((tt, D), F32)),
    )(dn, x, g.reshape(1, D), dres)


def _final_loss(h, target, g):
    T, D = h.shape
    tt = _tile(T, TILES["row"])

    def body(h_ref, t_ref, g_ref, dx_ref, dxb_ref, dg_ref, loss_ref):
        i = pl.program_id(0)
        xv = h_ref[...]
        r = _rstd(xv)
        xh = xv * r
        gv = g_ref[...]
        err = xh * gv - t_ref[...]
        lpart = 0.5 * jnp.sum(jnp.mean(err * err, axis=-1, keepdims=True), axis=0, keepdims=True)
        dy = err * (1.0 / D)
        dxh = dy * gv
        dx = r * (dxh - xh * jnp.mean(dxh * xh, axis=-1, keepdims=True))
        dx_ref[...] = dx
        dxb_ref[...] = dx.astype(BF16)
        gpart = jnp.sum(dy * xh, axis=0, keepdims=True)
        lrow = jnp.broadcast_to(lpart, (1, LANES))

        @pl.when(i == 0)
        def _():
            dg_ref[...] = gpart
            loss_ref[...] = lrow

        @pl.when(i > 0)
        def _():
            dg_ref[...] += gpart
            loss_ref[...] += lrow

    row = pl.BlockSpec((tt, D), lambda i: (i, 0))
    vec = pl.BlockSpec((1, D), lambda i: (0, 0))
    return pl.pallas_call(
        body, name="final_loss", grid=(T // tt,), in_specs=[row, row, vec],
        out_specs=[row, row, vec, pl.BlockSpec((1, LANES), lambda i: (0, 0))],
        out_shape=[jax.ShapeDtypeStruct((T, D), F32), jax.ShapeDtypeStruct((T, D), BF16),
                   jax.ShapeDtypeStruct((1, D), F32), jax.ShapeDtypeStruct((1, LANES), F32)],
        compiler_params=_params(("arbitrary",), 16 * _nbytes((tt, D), F32)),
    )(h, target, g.reshape(1, D))


def _rope_tables(pos_col, inv_lane):
    T = pos_col.shape[0]
    tt = _tile(T, TILES["row"])

    def body(p_ref, f_ref, c_ref, s1_ref, s2_ref):
        ang = p_ref[...] * f_ref[...]
        lane = lax.broadcasted_iota(jnp.int32, ang.shape, 1)
        half = ROPE_DIM // 2
        cs, sn = jnp.cos(ang), jnp.sin(ang)
        c_ref[...] = jnp.where(lane < ROPE_DIM, cs, 0.0)
        s1_ref[...] = jnp.where(lane < half, -sn, 0.0)
        s2_ref[...] = jnp.where((lane >= half) & (lane < ROPE_DIM), sn, 0.0)

    tab = pl.BlockSpec((tt, LANES), lambda i: (i, 0))
    shp = jax.ShapeDtypeStruct((T, LANES), F32)
    return pl.pallas_call(
        body, name="rope_tables", grid=(T // tt,),
        in_specs=[pl.BlockSpec((tt, 1), lambda i: (i, 0)), pl.BlockSpec((1, LANES), lambda i: (0, 0))],
        out_specs=[tab, tab, tab], out_shape=[shp, shp, shp],
        compiler_params=_params(("parallel",), 32 * _nbytes((tt, LANES), F32)),
    )(pos_col, inv_lane)


def _rope(x, c, s1, s2):
    return x * c + pltpu.roll(x, LANES - ROPE_DIM // 2, 1) * s1 + pltpu.roll(x, ROPE_DIM // 2, 1) * s2


def _rope_t(d, c, s1, s2):
    return d * c + pltpu.roll(d * s1, ROPE_DIM // 2, 1) + pltpu.roll(d * s2, LANES - ROPE_DIM // 2, 1)


def _window_sum(xe, w, forward):
    n = xe.shape[0]
    s, sh = xe, 1
    while sh < w:
        s = s + pltpu.roll(s, (n - sh) if forward else sh, 0)
        sh *= 2
    return s


def _post_u(u, gq, gkv, tabs, dims):
    T, Dp = u.shape
    P, QL, KL, C = dims["P"], dims["QL"], dims["KL"], dims["C"]
    tt = _tile(T, TILES["row"], POOL_HALO)
    hb = tt // POOL_HALO

    def body(u_ref, halo_ref, gq_ref, gkv_ref, c_ref, s1_ref, s2_ref, diff_ref, cq_ref, ckv_ref, kr_ref):
        i = pl.program_id(0)
        t = i * tt + lax.broadcasted_iota(jnp.int32, (tt, 1), 0)
        halo = jnp.where(i > 0, halo_ref[...], 0.0)
        for gi, w in enumerate(POOL_WINDOWS):
            cols = slice(gi * C, (gi + 1) * C)
            xg = u_ref[:, cols]
            s = _window_sum(jnp.concatenate([halo[:, cols], xg], axis=0), w, False)[POOL_HALO:]
            cnt = jnp.minimum(t + 1, w).astype(F32)
            diff_ref[:, cols] = (s / cnt - xg).astype(BF16)
        cq = u_ref[:, P:P + QL]
        cq_ref[...] = (cq * _rstd(cq) * gq_ref[...]).astype(BF16)
        ckv = u_ref[:, P + QL:P + QL + KL]
        ckv_ref[...] = (ckv * _rstd(ckv) * gkv_ref[...]).astype(BF16)
        kr_ref[...] = _rope(u_ref[:, P + QL + KL:], c_ref[...], s1_ref[...], s2_ref[...]).astype(BF16)

    def row(w):
        return pl.BlockSpec((tt, w), lambda i: (i, 0))

    def vec(w):
        return pl.BlockSpec((1, w), lambda i: (0, 0))

    return pl.pallas_call(
        body, name="post_u", grid=(T // tt,),
        in_specs=[row(Dp), pl.BlockSpec((POOL_HALO, P), lambda i: (jnp.maximum(i * hb - 1, 0), 0)),
                  vec(QL), vec(KL), row(LANES), row(LANES), row(LANES)],
        out_specs=[row(P), row(QL), row(KL), row(LANES)],
        out_shape=[jax.ShapeDtypeStruct((T, P), BF16), jax.ShapeDtypeStruct((T, QL), BF16),
                   jax.ShapeDtypeStruct((T, KL), BF16), jax.ShapeDtypeStruct((T, LANES), BF16)],
        compiler_params=_params(("parallel",), 10 * _nbytes((tt, Dp), F32)),
    )(u, u, gq.reshape(1, QL), gkv.reshape(1, KL), *tabs)


def _pre_u_bwd(u, d_cqn, d_ckvn, d_diff, dkr, gq, gkv, tabs, dims):
    T, Dp = u.shape
    P, QL, KL, C, H = dims["P"], dims["QL"], dims["KL"], dims["C"], dims["H"]
    tt = _tile(T, TILES["row"], POOL_HALO)
    hb = tt // POOL_HALO
    n_t = T // tt

    def norm_bwd(xv, dn, gv):
        r = _rstd(xv)
        xh = xv * r
        dxh = dn * gv
        return r * (dxh - xh * jnp.mean(dxh * xh, axis=-1, keepdims=True)), jnp.sum(dn * xh, axis=0, keepdims=True)

    def body(u_ref, dcq_ref, dckv_ref, dd_ref, ddn_ref, dkr_ref, gq_ref, gkv_ref, c_ref, s1_ref, s2_ref,
             du_ref, dgq_ref, dgkv_ref):
        i = pl.program_id(0)
        t = i * tt + lax.broadcasted_iota(jnp.int32, (tt, 1), 0)
        nxt = jnp.where(i < n_t - 1, ddn_ref[...].astype(F32), 0.0)
        for gi, w in enumerate(POOL_WINDOWS):
            cols = slice(gi * C, (gi + 1) * C)
            dd = dd_ref[:, cols].astype(F32)
            e = dd / jnp.minimum(t + 1, w).astype(F32)
            s = _window_sum(jnp.concatenate([e, nxt[:, cols] / float(w)], axis=0), w, True)[:tt]
            du_ref[:, cols] = (s - dd).astype(BF16)
        dq, pq = norm_bwd(u_ref[:, P:P + QL], dcq_ref[...], gq_ref[...])
        du_ref[:, P:P + QL] = dq.astype(BF16)
        dkv, pkv = norm_bwd(u_ref[:, P + QL:P + QL + KL], dckv_ref[...], gkv_ref[...])
        du_ref[:, P + QL:P + QL + KL] = dkv.astype(BF16)
        dk = dkr_ref[0]
        for hh in range(1, H):
            dk = dk + dkr_ref[hh]
        du_ref[:, P + QL + KL:] = _rope_t(dk, c_ref[...], s1_ref[...], s2_ref[...]).astype(BF16)

        @pl.when(i == 0)
        def _():
            dgq_ref[...] = pq
            dgkv_ref[...] = pkv

        @pl.when(i > 0)
        def _():
            dgq_ref[...] += pq
            dgkv_ref[...] += pkv

    def row(w):
        return pl.BlockSpec((tt, w), lambda i: (i, 0))

    def vec(w):
        return pl.BlockSpec((1, w), lambda i: (0, 0))

    return pl.pallas_call(
        body, name="pre_u_bwd", grid=(n_t,),
        in_specs=[row(Dp), row(QL), row(KL), row(P),
                  pl.BlockSpec((POOL_HALO, P), lambda i: (jnp.minimum((i + 1) * hb, T // POOL_HALO - 1), 0)),
                  pl.BlockSpec((H, tt, LANES), lambda i: (0, i, 0)), vec(QL), vec(KL), row(LANES), row(LANES), row(LANES)],
        out_specs=[row(Dp), vec(QL), vec(KL)],
        out_shape=[jax.ShapeDtypeStruct((T, Dp), BF16), jax.ShapeDtypeStruct((1, QL), F32), jax.ShapeDtypeStruct((1, KL), F32)],
        compiler_params=_params(("arbitrary",), 12 * _nbytes((tt, Dp), F32)),
    )(u, d_cqn, d_ckvn, d_diff, d_diff, dkr, gq.reshape(1, QL), gkv.reshape(1, KL), *tabs)


def _pool_fwd(diff, pw, ps, dims):
    T, P = diff.shape
    G, C = len(POOL_WINDOWS), dims["C"]
    tt = _tile(T, TILES["row"])

    def body(d_ref, w_ref, s_ref, o_ref):
        for gi in range(G):
            cols = slice(gi * C, (gi + 1) * C)
            y = jnp.dot(d_ref[:, cols], w_ref[gi], preferred_element_type=F32)
            o_ref[:, cols] = (y * s_ref[:, cols]).astype(BF16)

    row = pl.BlockSpec((tt, P), lambda i: (i, 0))
    return pl.pallas_call(
        body, name="pool_fwd", grid=(T // tt,),
        in_specs=[row, pl.BlockSpec((G, C, C), lambda i: (0, 0, 0)), pl.BlockSpec((1, P), lambda i: (0, 0))],
        out_specs=row, out_shape=jax.ShapeDtypeStruct((T, P), BF16),
        compiler_params=_params(("parallel",), 8 * _nbytes((tt, P), F32)),
    )(diff, pw, ps.reshape(1, P))


def _pool_bwd(dmix, diff, pw, ps, dims):
    T, P = diff.shape
    G, C = len(POOL_WINDOWS), dims["C"]
    tt = _tile(T, TILES["row"])

    def body(dy_ref, d_ref, w_ref, s_ref, dd_ref, dw_ref, ds_ref):
        i = pl.program_id(0)

        @pl.when(i == 0)
        def _():
            dw_ref[...] = jnp.zeros_like(dw_ref)
            ds_ref[...] = jnp.zeros_like(ds_ref)

        for gi in range(G):
            cols = slice(gi * C, (gi + 1) * C)
            dy = dy_ref[:, cols].astype(F32)
            d = d_ref[:, cols]
            w = w_ref[gi]
            ypre = jnp.dot(d, w, preferred_element_type=F32)
            ds_ref[:, cols] += jnp.sum(dy * ypre, axis=0, keepdims=True)
            dyp = (dy * s_ref[:, cols]).astype(BF16)
            dd_ref[:, cols] = lax.dot_general(dyp, w, _DOT_DIMS["nt"], preferred_element_type=F32).astype(BF16)
            dw_ref[gi] += lax.dot_general(d, dyp, _DOT_DIMS["tn"], preferred_element_type=F32)

    row = pl.BlockSpec((tt, P), lambda i: (i, 0))
    wsp = pl.BlockSpec((G, C, C), lambda i: (0, 0, 0))
    vec = pl.BlockSpec((1, P), lambda i: (0, 0))
    return pl.pallas_call(
        body, name="pool_bwd", grid=(T // tt,), in_specs=[row, row, wsp, vec], out_specs=[row, wsp, vec],
        out_shape=[jax.ShapeDtypeStruct((T, P), BF16), jax.ShapeDtypeStruct((G, C, C), F32), jax.ShapeDtypeStruct((1, P), F32)],
        compiler_params=_params(("arbitrary",), 10 * _nbytes((tt, P), F32)),
    )(dmix, diff, pw, ps.reshape(1, P))


def _q_rope(qp, tabs, dims):
    T, W = qp.shape
    H = dims["H"]
    tt = _tile(T, TILES["row"])

    def body(q_ref, c_ref, s1_ref, s2_ref, o_ref):
        o_ref[:, :H * LANES] = (q_ref[:, :H * LANES] * ATT_SCALE).astype(BF16)
        c, s1, s2 = c_ref[...], s1_ref[...], s2_ref[...]
        for hh in range(H, 2 * H):
            cols = slice(hh * LANES, (hh + 1) * LANES)
            o_ref[:, cols] = _rope(q_ref[:, cols] * ATT_SCALE, c, s1, s2).astype(BF16)

    row = pl.BlockSpec((tt, W), lambda i: (i, 0))
    tab = pl.BlockSpec((tt, LANES), lambda i: (i, 0))
    return pl.pallas_call(
        body, name="q_rope", grid=(T // tt,), in_specs=[row, tab, tab, tab], out_specs=row,
        out_shape=jax.ShapeDtypeStruct((T, W), BF16), compiler_params=_params(("parallel",), 8 * _nbytes((tt, W), F32)),
    )(qp, *tabs)


def _scores(qn_ref, qr_ref, kn_ref, kr_ref, t, diagonal):
    q = jnp.concatenate([qn_ref[...], qr_ref[...]], axis=1)
    k = jnp.concatenate([kn_ref[...], kr_ref[...]], axis=1)
    s = lax.dot_general(q, k, _DOT_DIMS["nt"], preferred_element_type=F32)
    if diagonal:
        s = jnp.where(lax.broadcasted_iota(jnp.int32, (t, t), 0) >= lax.broadcasted_iota(jnp.int32, (t, t), 1), s, NEG_BIG)
    return q, k, s


class _Side:
    def __init__(self, ins, out_shapes, n_sems, start, finish, aliases=None):
        self.ins, self.out_shapes, self.n_sems, self.start, self.finish = list(ins), list(out_shapes), n_sems, start, finish
        self.aliases = dict(aliases or {})


def _attach(side, body, n_in, n_out, grid):
    if side is None:
        return body, [], [], [], [], [], {}
    n_si, n_so = len(side.ins), len(side.out_shapes)

    def carrying(*refs):
        outs_at = n_in + n_si
        main = refs[:n_in] + refs[outs_at:outs_at + n_out] + refs[outs_at + n_out + n_so:len(refs) - 2]
        parts = (refs[n_in:outs_at], refs[outs_at + n_out:outs_at + n_out + n_so], refs[-2], refs[-1])
        ids = [pl.program_id(d) for d in range(len(grid))]
        first = functools.reduce(lambda u, v: u & v, [i == 0 for i in ids])
        last = functools.reduce(lambda u, v: u & v, [i == g - 1 for i, g in zip(ids, grid)])

        @pl.when(first)
        def _():
            side.start(*parts)

        body(*main)

        @pl.when(last)
        def _():
            side.finish(*parts)

    aliases = {n_in + i: n_out + o for i, o in side.aliases.items()}
    return carrying, side.ins, [_HBM] * n_si, side.out_shapes, [_HBM] * n_so, _sem_pair(side.n_sems), aliases


def _pairs(n, by_query):
    pairs = [(i, j) for i in range(n) for j in range(i + 1)] if by_query else [(i, j) for j in range(n) for i in range(j, n)]
    return jnp.array([p[0] for p in pairs], jnp.int32), jnp.array([p[1] for p in pairs], jnp.int32), len(pairs)


def _flash_fwd(q_att, kv, kr, dims, side=None):
    T = q_att.shape[0]
    H = dims["H"]
    G = 2 if H % 2 == 0 else 1
    t = _tile(T, TILES["att"])
    n = T // t
    it, jt, n_pairs = _pairs(n, True)
    hb = H // G

    def body(it_ref, jt_ref, qn_ref, qr_ref, kn_ref, v_ref, kr_ref, o_ref, lse_ref, m_ref, l_ref, acc_ref):
        step_id = pl.program_id(1)
        i, j = it_ref[step_id], jt_ref[step_id]

        @pl.when(j == 0)
        def _():
            m_ref[...] = jnp.full_like(m_ref, NEG_BIG)
            l_ref[...] = jnp.zeros_like(l_ref)
            acc_ref[...] = jnp.zeros_like(acc_ref)

        def step(diagonal):
            for g in range(G):
                cols = slice(g * LANES, (g + 1) * LANES)
                _, _, s = _scores(qn_ref.at[:, cols], qr_ref.at[:, cols], kn_ref.at[:, cols], kr_ref, t, diagonal)
                m_prev = m_ref[:, cols]
                m_new = jnp.maximum(m_prev, jnp.max(s, axis=1, keepdims=True))
                alpha = jnp.exp(m_prev - m_new)
                p = jnp.exp(s - m_new[:, :1])
                l_ref[:, cols] = alpha * l_ref[:, cols] + jnp.sum(p, axis=1, keepdims=True)
                acc_ref[:, cols] = alpha * acc_ref[:, cols] + jnp.dot(p.astype(BF16), v_ref[:, cols], preferred_element_type=F32)
                m_ref[:, cols] = m_new

        @pl.when(j < i)
        def _():
            step(False)

        @pl.when(j == i)
        def _():
            step(True)
            o_ref[...] = (acc_ref[...] / l_ref[...]).astype(BF16)
            lse_ref[...] = m_ref[...] + jnp.log(l_ref[...])

    blk = (t, G * LANES)
    grid = (hb, n_pairs)
    body, s_ins, s_in_specs, s_shapes, s_out_specs, s_sems, aliases = _attach(side, body, 7, 2, grid)
    return pl.pallas_call(
        body, name="flash_fwd",
        grid_spec=pltpu.PrefetchScalarGridSpec(
            num_scalar_prefetch=2, grid=grid,
            in_specs=[pl.BlockSpec(blk, lambda h, s, it, jt: (it[s], h)), pl.BlockSpec(blk, lambda h, s, it, jt: (it[s], hb + h)),
                      pl.BlockSpec(blk, lambda h, s, it, jt: (jt[s], h)), pl.BlockSpec(blk, lambda h, s, it, jt: (jt[s], hb + h)),
                      pl.BlockSpec((t, LANES), lambda h, s, it, jt: (jt[s], 0))] + s_in_specs,
            out_specs=[pl.BlockSpec(blk, lambda h, s, it, jt: (it[s], h)), pl.BlockSpec(blk, lambda h, s, it, jt: (it[s], h))] + s_out_specs,
            scratch_shapes=[pltpu.VMEM(blk, F32), pltpu.VMEM(blk, F32), pltpu.VMEM(blk, F32), *s_sems]),
        out_shape=[jax.ShapeDtypeStruct((T, H * LANES), BF16), jax.ShapeDtypeStruct((T, H * LANES), F32)] + s_shapes,
        input_output_aliases=aliases,
        compiler_params=_params(("arbitrary", "arbitrary"), 8 * G * _nbytes((t, t), F32) + (8 << 20)),
    )(it, jt, q_att, q_att, kv, kv, kr, *s_ins)


def _flash_bwd(q_att, kv, kr, o, lse, dmix, dims, side=None):
    T = q_att.shape[0]
    H = dims["H"]
    G = 2 if H % 2 == 0 else 1
    ob = dims["P"] // LANES // G
    t = _tile(T, TILES["att"])
    n = T // t
    it, jt, n_pairs = _pairs(n, False)
    hb = H // G

    def body(it_ref, jt_ref, qn_ref, qr_ref, kn_ref, v_ref, kr_ref, o_ref, lse_ref, do_ref,
             dq_ref, dkn_ref, dv_ref, dkr_ref, dk_acc, dv_acc):
        step_id = pl.program_id(1)
        i, j = it_ref[step_id], jt_ref[step_id]

        @pl.when(step_id == 0)
        def _():
            dq_ref[...] = jnp.zeros_like(dq_ref)

        @pl.when(i == j)
        def _():
            dk_acc[...] = jnp.zeros_like(dk_acc)
            dv_acc[...] = jnp.zeros_like(dv_acc)

        def step(diagonal):
            rows = pl.ds(pl.multiple_of(i * t, t), t)
            for g in range(G):
                cols = slice(g * LANES, (g + 1) * LANES)
                q, k, s = _scores(qn_ref.at[:, cols], qr_ref.at[:, cols], kn_ref.at[:, cols], kr_ref, t, diagonal)
                p = jnp.exp(s - lse_ref[:, g * LANES:g * LANES + 1])
                do = do_ref[:, cols]
                delta = jnp.sum(do.astype(F32) * o_ref[:, cols].astype(F32), axis=1, keepdims=True)
                dv_acc[:, cols] += lax.dot_general(p.astype(BF16), do, _DOT_DIMS["tn"], preferred_element_type=F32)
                dp = lax.dot_general(do, v_ref[:, cols], _DOT_DIMS["nt"], preferred_element_type=F32)
                ds = (p * (dp - delta)).astype(BF16)
                dk_acc[g] += lax.dot_general(ds, q, _DOT_DIMS["tn"], preferred_element_type=F32)
                dq_ref[g, rows, :] += jnp.dot(ds, k, preferred_element_type=F32)

        @pl.when(i > j)
        def _():
            step(False)

        @pl.when(i == j)
        def _():
            step(True)

        @pl.when(i == n - 1)
        def _():
            for g in range(G):
                dkn_ref[:, g * LANES:(g + 1) * LANES] = dk_acc[g, :, :LANES].astype(BF16)
                dkr_ref[g] = dk_acc[g, :, LANES:]
            dv_ref[...] = dv_acc[...].astype(BF16)

    blk = (t, G * LANES)
    grid = (hb, n_pairs)
    body, s_ins, s_in_specs, s_shapes, s_out_specs, s_sems, aliases = _attach(side, body, 10, 4, grid)
    return pl.pallas_call(
        body, name="flash_bwd", input_output_aliases=aliases,
        grid_spec=pltpu.PrefetchScalarGridSpec(
            num_scalar_prefetch=2, grid=grid,
            in_specs=[pl.BlockSpec(blk, lambda h, s, it, jt: (it[s], h)), pl.BlockSpec(blk, lambda h, s, it, jt: (it[s], hb + h)),
                      pl.BlockSpec(blk, lambda h, s, it, jt: (jt[s], h)), pl.BlockSpec(blk, lambda h, s, it, jt: (jt[s], hb + h)),
                      pl.BlockSpec((t, LANES), lambda h, s, it, jt: (jt[s], 0)),
                      pl.BlockSpec(blk, lambda h, s, it, jt: (it[s], h)), pl.BlockSpec(blk, lambda h, s, it, jt: (it[s], h)),
                      pl.BlockSpec(blk, lambda h, s, it, jt: (it[s], ob + h))] + s_in_specs,
            out_specs=[pl.BlockSpec((G, T, 2 * LANES), lambda h, s, it, jt: (h, 0, 0)),
                       pl.BlockSpec(blk, lambda h, s, it, jt: (jt[s], h)), pl.BlockSpec(blk, lambda h, s, it, jt: (jt[s], h)),
                       pl.BlockSpec((G, t, LANES), lambda h, s, it, jt: (h, jt[s], 0))] + s_out_specs,
            scratch_shapes=[pltpu.VMEM((G, t, 2 * LANES), F32), pltpu.VMEM(blk, F32), *s_sems]),
        out_shape=[jax.ShapeDtypeStruct((H, T, 2 * LANES), F32), jax.ShapeDtypeStruct((T, H * LANES), BF16),
                   jax.ShapeDtypeStruct((T, H * LANES), BF16), jax.ShapeDtypeStruct((H, T, LANES), F32)] + s_shapes,
        compiler_params=_params(("arbitrary", "arbitrary"),
                                12 * G * _nbytes((t, t), F32) + 2 * G * _nbytes((T, 2 * LANES), F32) + (8 << 20)),
    )(it, jt, q_att, q_att, kv, kv, kr, o, lse, dmix, *s_ins)


def _dq_post(dq, tabs, dims):
    H, T, _ = dq.shape
    tt = _tile(T, TILES["row"])

    def body(dq_ref, c_ref, s1_ref, s2_ref, o_ref):
        c, s1, s2 = c_ref[...], s1_ref[...], s2_ref[...]
        for hh in range(H):
            o_ref[:, hh * LANES:(hh + 1) * LANES] = (dq_ref[hh, :, :LANES] * ATT_SCALE).astype(BF16)
            o_ref[:, (H + hh) * LANES:(H + hh + 1) * LANES] = _rope_t(dq_ref[hh, :, LANES:] * ATT_SCALE, c, s1, s2).astype(BF16)

    tab = pl.BlockSpec((tt, LANES), lambda i: (i, 0))
    return pl.pallas_call(
        body, name="dq_post", grid=(T // tt,),
        in_specs=[pl.BlockSpec((H, tt, 2 * LANES), lambda i: (0, i, 0)), tab, tab, tab],
        out_specs=pl.BlockSpec((tt, 2 * H * LANES), lambda i: (i, 0)),
        out_shape=jax.ShapeDtypeStruct((T, 2 * H * LANES), BF16),
        compiler_params=_params(("parallel",), 8 * _nbytes((tt, 2 * H * LANES), F32)),
    )(dq, *tabs)


def _conv3(ge, cw, n):
    return cw[2:3] * ge + cw[1:2] * pltpu.roll(ge, 1, 0) + cw[0:1] * pltpu.roll(ge, 2, 0) + cw[3:4]


def _ffn_fwd(gate, up, cw8):
    T, F = gate.shape
    tt = _tile(T, TILES["ffn_row"])
    tc = _tile(F, TILES["ffn_c"], LANES)
    hb = tt // CONV_HALO

    def body(g_ref, gp_ref, u_ref, cw_ref, a_ref):
        it = pl.program_id(1)
        prev = jnp.where(it > 0, gp_ref[...].astype(F32), 0.0)
        ge = jnp.concatenate([prev, g_ref[...].astype(F32)], axis=0)
        gc = _conv3(ge, cw_ref[...], tt + CONV_HALO)[CONV_HALO:]
        a_ref[...] = (gc * _sigmoid(gc) * u_ref[...].astype(F32)).astype(BF16)

    blk = pl.BlockSpec((tt, tc), lambda jc, it: (it, jc))
    return pl.pallas_call(
        body, name="ffn_fwd", grid=(F // tc, T // tt),
        in_specs=[blk, pl.BlockSpec((CONV_HALO, tc), lambda jc, it: (jnp.maximum(it * hb - 1, 0), jc)), blk,
                  pl.BlockSpec((8, tc), lambda jc, it: (0, jc))],
        out_specs=blk, out_shape=jax.ShapeDtypeStruct((T, F), BF16),
        compiler_params=_params(("parallel", "parallel"), 16 * _nbytes((tt, tc), F32)),
    )(gate, gate, up, cw8)


def _ffn_bwd(da, gate, up, cw8):
    T, F = gate.shape
    tt = _tile(T, TILES["ffn_row"])
    tc = _tile(F, TILES["ffn_c"], LANES)
    hb = tt // CONV_HALO
    n_t = T // tt
    n = tt + 2 * CONV_HALO

    def body(da_ref, dan_ref, g_ref, gp_ref, gn_ref, u_ref, un_ref, cw_ref, dg_ref, du_ref, dcw_ref):
        it = pl.program_id(1)
        first, last = it == 0, it == n_t - 1
        cw = cw_ref[...]
        zeros = jnp.zeros((CONV_HALO, tc), F32)
        ge = jnp.concatenate([jnp.where(first, 0.0, gp_ref[...].astype(F32)), g_ref[...].astype(F32),
                              gn_ref[...].astype(F32)], axis=0)
        dae = jnp.concatenate([zeros, da_ref[...].astype(F32), jnp.where(last, 0.0, dan_ref[...].astype(F32))], axis=0)
        ue = jnp.concatenate([zeros, u_ref[...].astype(F32), un_ref[...].astype(F32)], axis=0)
        g1, g2 = pltpu.roll(ge, 1, 0), pltpu.roll(ge, 2, 0)
        gc = cw[2:3] * ge + cw[1:2] * g1 + cw[0:1] * g2 + cw[3:4]
        sg = _sigmoid(gc)
        dgc = dae * ue * (sg * (1.0 + gc * (1.0 - sg)))
        du_ref[...] = (dae * gc * sg)[CONV_HALO:CONV_HALO + tt].astype(BF16)
        dgp = cw[2:3] * dgc + cw[1:2] * pltpu.roll(dgc, n - 1, 0) + cw[0:1] * pltpu.roll(dgc, n - 2, 0)
        dg_ref[...] = dgp[CONV_HALO:CONV_HALO + tt].astype(BF16)
        mid = slice(CONV_HALO, CONV_HALO + tt)
        d_mid = dgc[mid]
        part = jnp.concatenate([jnp.sum(d_mid * g2[mid], axis=0, keepdims=True), jnp.sum(d_mid * g1[mid], axis=0, keepdims=True),
                                jnp.sum(d_mid * ge[mid], axis=0, keepdims=True), jnp.sum(d_mid, axis=0, keepdims=True),
                                jnp.zeros((4, tc), F32)], axis=0)

        @pl.when(first)
        def _():
            dcw_ref[...] = part

        @pl.when(it > 0)
        def _():
            dcw_ref[...] += part

    blk = pl.BlockSpec((tt, tc), lambda jc, it: (it, jc))
    prv = pl.BlockSpec((CONV_HALO, tc), lambda jc, it: (jnp.maximum(it * hb - 1, 0), jc))
    nxt = pl.BlockSpec((CONV_HALO, tc), lambda jc, it: (jnp.minimum((it + 1) * hb, T // CONV_HALO - 1), jc))
    cws = pl.BlockSpec((8, tc), lambda jc, it: (0, jc))
    return pl.pallas_call(
        body, name="ffn_bwd", grid=(F // tc, n_t), in_specs=[blk, nxt, blk, prv, nxt, blk, nxt, cws],
        out_specs=[blk, blk, cws],
        out_shape=[jax.ShapeDtypeStruct((T, F), BF16), jax.ShapeDtypeStruct((T, F), BF16), jax.ShapeDtypeStruct((8, F), F32)],
        compiler_params=_params(("parallel", "arbitrary"), 32 * _nbytes((tt, tc), F32)),
    )(da, da, gate, gate, gate, up, up, cw8)


def _ple_fwd(h2, gl, pe):
    T, D = h2.shape
    tt = _tile(T, TILES["row"])

    def body(h_ref, gl_ref, pe_ref, o_ref):
        o_ref[...] = h_ref[...] + pe_ref[...] * _sigmoid(gl_ref[...])

    row = pl.BlockSpec((tt, D), lambda i: (i, 0))
    return pl.pallas_call(
        body, name="ple_fwd", grid=(T // tt,), in_specs=[row, row, row], out_specs=row,
        out_shape=jax.ShapeDtypeStruct((T, D), F32), compiler_params=_params(("parallel",), 12 * _nbytes((tt, D), F32)),
    )(h2, gl, pe)


def _ple_bwd(dh, gl, pe):
    T, D = dh.shape
    tt = _tile(T, TILES["row"])

    def body(dh_ref, gl_ref, pe_ref, dpe_ref, dgl_ref):
        d = dh_ref[...]
        sg = _sigmoid(gl_ref[...])
        dpe_ref[...] = (d * sg).astype(BF16)
        dgl_ref[...] = (d * pe_ref[...] * (sg * (1.0 - sg))).astype(BF16)

    row = pl.BlockSpec((tt, D), lambda i: (i, 0))
    return pl.pallas_call(
        body, name="ple_bwd", grid=(T // tt,), in_specs=[row, row, row], out_specs=[row, row],
        out_shape=[jax.ShapeDtypeStruct((T, D), BF16), jax.ShapeDtypeStruct((T, D), BF16)],
        compiler_params=_params(("parallel",), 12 * _nbytes((tt, D), F32)),
    )(dh, gl, pe)


def _adamw(w, g, m, v, name):
    shape = w.shape
    cols = shape[-1]
    rows = math.prod(shape[:-1]) if len(shape) > 1 else 1
    w2, g2, m2, v2 = (a.reshape(rows, cols) for a in (w, g, m, v))
    tr = _tile(rows, max(8, (1 << 20) // (cols * 4)))
    c1 = 1.0 - ADAM_B1 ** ADAM_STEP
    c2 = 1.0 - ADAM_B2 ** ADAM_STEP

    def body(w_ref, g_ref, m_ref, v_ref, d_ref, mo_ref, vo_ref):
        gv = g_ref[...]
        mn = ADAM_B1 * m_ref[...] + (1.0 - ADAM_B1) * gv
        vn = ADAM_B2 * v_ref[...] + (1.0 - ADAM_B2) * (gv * gv)
        mo_ref[...] = mn
        vo_ref[...] = vn
        d_ref[...] = -ADAM_LR * ((mn / c1) / (jnp.sqrt(vn / c2) + ADAM_EPS) + ADAM_WD * w_ref[...])

    blk = pl.BlockSpec((tr, cols), lambda i: (i, 0))
    shp = jax.ShapeDtypeStruct((rows, cols), F32)
    outs = pl.pallas_call(
        body, name=name, grid=(rows // tr,), in_specs=[blk] * 4, out_specs=[blk] * 3, out_shape=[shp] * 3,
        compiler_params=_params(("parallel",), 16 * _nbytes((tr, cols), F32)),
    )(w2, g2, m2, v2)
    return tuple(o.reshape(shape) for o in outs)


_HBM = pl.BlockSpec(memory_space=pltpu.HBM)


def _place():
    x, y, c = lax.axis_index("x"), lax.axis_index("y"), lax.axis_index("c")
    return x, y, c, [(1 - x, y), (x, 1 - y), (1 - x, 1 - y)]


def _remote(src, dst, send_sems, recv_sems, k, to):
    return pltpu.make_async_remote_copy(src_ref=src, dst_ref=dst, send_sem=send_sems.at[k], recv_sem=recv_sems.at[k],
                                        device_id=to, device_id_type=MESH)


def _half(ref, lead, h):
    hr = ref.shape[-2] // 2
    return ref.at[(*lead, pl.ds(pl.multiple_of(h * hr, SUBLANES_BF16), hr))]


def _sem_pair(n):
    return [pltpu.SemaphoreType.DMA((n,)), pltpu.SemaphoreType.DMA((n,))]


def _run_side(side, name):
    n_in, n_out = len(side.ins), len(side.out_shapes)

    def body(*refs):
        parts = (refs[:n_in], refs[n_in:n_in + n_out]) + tuple(refs[n_in + n_out:])
        side.start(*parts)
        side.finish(*parts)

    return pl.pallas_call(
        body, name=name, in_specs=[_HBM] * n_in, out_specs=[_HBM] * n_out, out_shape=side.out_shapes,
        scratch_shapes=_sem_pair(side.n_sems), input_output_aliases=side.aliases,
    )(*side.ins)


def _whole(arrs, halves):
    return [(a, 0, arr.shape[-2] // (2 if halves else 1)) for a, arr in enumerate(arrs)]


def _plan(arrs, halves, big):
    whole = _whole(arrs, halves)
    q = whole[big][2] // 4
    return [[pc for pc in whole if pc[0] != big] + [(big, 0, q)]] + [[(big, k * q, q)] for k in (1, 2, 3)]


def _plan_gather(arrs, big, small):
    whole = _whole(arrs, True)
    e = whole[big][2] // 8
    return [[pc for pc in whole if pc[0] not in (big, small)], [(big, 0, 3 * e)], [(big, 3 * e, 3 * e)],
            [(big, 6 * e, 2 * e), whole[small]]]


def _ride(fn, n_main, pieces, make, store):
    if pieces is None:
        return fn(None)
    side, touched = make(pieces)
    out = fn(side)
    store.update(zip(touched, out[n_main:]))
    return out[0] if n_main == 1 else out[:n_main]


def _carried(arrs, pieces, prior):
    touched = sorted({a for a, _, _ in pieces})
    pos = {a: i for i, a in enumerate(touched)}
    carried = [a for a in touched if a in prior]
    ins = [arrs[a] for a in touched] + [prior[a] for a in carried]
    return touched, pos, ins, {len(touched) + i: pos[a] for i, a in enumerate(carried)}


def _gather_side(arrs, layer, pieces, prior):
    touched, pos, ins_arrs, aliases = _carried(arrs, pieces, prior)

    def copies(ins, outs, send_sems, recv_sems, arriving):
        x, y, c, chips = _place()
        me, sib = 2 * x + y, (x, y, 1 - c)
        out = []
        for p, (a, r0, nr) in enumerate(pieces):
            src, dst = ins[pos[a]], outs[pos[a]]
            hr = src.shape[-2] // 2
            for hlf in range(2):
                rows = pl.ds(hlf * hr + r0, nr)
                out.append(_remote(src.at[layer, rows], dst.at[me, rows], send_sems, recv_sems, 5 * p + 3 + hlf, sib))
            rows = pl.ds(pl.multiple_of(c * hr + r0, SUBLANES_BF16), nr)
            for k, (cx, cy) in enumerate(chips):
                slot = 2 * cx + cy if arriving else me
                out.append(_remote(src.at[layer, rows], dst.at[slot, rows], send_sems, recv_sems, 5 * p + k, (cx, cy, c)))
        return out

    def start(ins, outs, send_sems, recv_sems):
        for cp in copies(ins, outs, send_sems, recv_sems, False):
            cp.start()

    def finish(ins, outs, send_sems, recv_sems):
        for cp in copies(ins, outs, send_sems, recv_sems, True):
            cp.wait_recv()
        for cp in copies(ins, outs, send_sems, recv_sems, False):
            cp.wait_send()

    shapes = [jax.ShapeDtypeStruct((N_SHARDS,) + arrs[a].shape[1:], arrs[a].dtype) for a in touched]
    return _Side(ins_arrs, shapes, 5 * len(pieces), start, finish, aliases), touched


def _forward_side(arrs):
    n = len(arrs)

    def copies(outs, send_sems, recv_sems, arriving):
        x, y, c, chips = _place()
        sib = (x, y, 1 - c)
        out = []
        for a in range(n):
            for k, (cx, cy) in enumerate(chips):
                got = _half(outs[a], (2 * cx + cy,), 1 - c if arriving else c)
                out.append(_remote(got, got, send_sems, recv_sems, 3 * a + k, sib))
        return out

    def start(ins, outs, send_sems, recv_sems):
        for cp in copies(outs, send_sems, recv_sems, False):
            cp.start()

    def finish(ins, outs, send_sems, recv_sems):
        for cp in copies(outs, send_sems, recv_sems, True):
            cp.wait_recv()
        for cp in copies(outs, send_sems, recv_sems, False):
            cp.wait_send()

    return _Side(arrs, [jax.ShapeDtypeStruct(a.shape, a.dtype) for a in arrs], 3 * n, start, finish, {a: a for a in range(n)})


def _sibling_side(arrs):
    n = len(arrs)

    def copies(ins, outs, send_sems, recv_sems):
        x, y, c, _ = _place()
        return [_remote(_half(ins[a], (s,), 1 - c), outs[a].at[s], send_sems, recv_sems, N_SHARDS * a + s, (x, y, 1 - c))
                for a in range(n) for s in range(N_SHARDS)]

    def start(ins, outs, send_sems, recv_sems):
        for cp in copies(ins, outs, send_sems, recv_sems):
            cp.start()

    def finish(ins, outs, send_sems, recv_sems):
        for cp in copies(ins, outs, send_sems, recv_sems):
            cp.wait_recv()
        for cp in copies(ins, outs, send_sems, recv_sems):
            cp.wait_send()

    shapes = [jax.ShapeDtypeStruct((N_SHARDS, a.shape[1] // 2, a.shape[2]), a.dtype) for a in arrs]
    return _Side(arrs, shapes, N_SHARDS * n, start, finish)


def _chip_side(arrs, pieces, prior):
    touched, pos, ins_arrs, aliases = _carried(arrs, pieces, prior)

    def copies(ins, outs, send_sems, recv_sems):
        x, y, c, chips = _place()
        return [_remote(ins[pos[a]].at[2 * cx + cy, pl.ds(r0, nr)], outs[pos[a]].at[k, pl.ds(r0, nr)], send_sems, recv_sems,
                        3 * p + k, (cx, cy, c))
                for p, (a, r0, nr) in enumerate(pieces) for k, (cx, cy) in enumerate(chips)]

    def start(ins, outs, send_sems, recv_sems):
        for cp in copies(ins, outs, send_sems, recv_sems):
            cp.start()

    def finish(ins, outs, send_sems, recv_sems):
        for cp in copies(ins, outs, send_sems, recv_sems):
            cp.wait_recv()
        for cp in copies(ins, outs, send_sems, recv_sems):
            cp.wait_send()

    shapes = [jax.ShapeDtypeStruct((3,) + arrs[a].shape[1:], arrs[a].dtype) for a in touched]
    return _Side(ins_arrs, shapes, 3 * len(pieces), start, finish, aliases), touched


def _sibling_share(arrs):
    n = len(arrs)

    def body(*refs):
        outs, send_sems, recv_sems = refs[n:2 * n], refs[2 * n], refs[2 * n + 1]
        x, y, c, _ = _place()
        sib = (x, y, 1 - c)
        sends = [_remote(outs[a].at[c], outs[a].at[c], send_sems, recv_sems, a, sib) for a in range(n)]
        for cp in sends:
            cp.start()
        for a in range(n):
            _remote(outs[a].at[c], outs[a].at[1 - c], send_sems, recv_sems, a, sib).wait_recv()
        for cp in sends:
            cp.wait_send()

    return pl.pallas_call(
        body, name="rs_share", in_specs=[_HBM] * n, out_specs=[_HBM] * n,
        out_shape=[jax.ShapeDtypeStruct(a.shape, a.dtype) for a in arrs],
        input_output_aliases={a: a for a in range(n)}, scratch_shapes=_sem_pair(n),
    )(*arrs)


def _add_sibling(g, sib_in, place):
    S, rows, cols = g.shape
    hr = rows // 2
    tr = _tile(hr, max(SUBLANES_BF16, TILES["add_bytes"] // (cols * 2)), SUBLANES_BF16)
    nb = hr // tr

    def body(p_ref, a_ref, b_ref, o_ref):
        o_ref[...] = (a_ref[...].astype(F32) + b_ref[...].astype(F32)).astype(o_ref.dtype)

    blk = pl.BlockSpec((None, tr, cols), lambda s, r, p: (s, r, 0))
    return pl.pallas_call(
        body, name="rs_add_sibling",
        grid_spec=pltpu.PrefetchScalarGridSpec(
            num_scalar_prefetch=1, grid=(S, nb),
            in_specs=[pl.BlockSpec((None, tr, cols), lambda s, r, p: (s, p[1] * nb + r, 0)), blk], out_specs=blk),
        out_shape=jax.ShapeDtypeStruct((S, hr, cols), g.dtype),
        compiler_params=_params(("parallel", "parallel"), 16 * _nbytes((tr, cols), F32)),
    )(place, g, sib_in)


def _add_chips(cs, got, place):
    S, r, cols = cs.shape
    tr = _tile(r, max(SUBLANES_BF16, TILES["add_bytes"] // (cols * 2)), SUBLANES_BF16)

    def body(p_ref, a_ref, b_ref, o_ref):
        acc = a_ref[...].astype(F32)
        for k in range(3):
            acc = acc + b_ref[k].astype(F32)
        o_ref[...] = acc

    return pl.pallas_call(
        body, name="rs_add_chips",
        grid_spec=pltpu.PrefetchScalarGridSpec(
            num_scalar_prefetch=1, grid=(r // tr,),
            in_specs=[pl.BlockSpec((None, tr, cols), lambda i, p: (p[0], i, 0)),
                      pl.BlockSpec((3, tr, cols), lambda i, p: (0, i, 0))],
            out_specs=pl.BlockSpec((None, tr, cols), lambda i, p: (p[1], i, 0))),
        out_shape=jax.ShapeDtypeStruct((2, r, cols), F32),
        compiler_params=_params(("parallel",), 24 * _nbytes((tr, cols), F32)),
    )(place, cs, got)


def _reduce_begin(arrs, from_sibling, place):
    return [_add_sibling(g, s, place) for g, s in zip(arrs, from_sibling)]


def _reduce_end(sums, got, place):
    halves = [_add_chips(cs, g, place) for cs, g in zip(sums, got)]
    return [f.reshape(-1, f.shape[-1]) for f in _sibling_share(halves)]


def _all_reduce_small(v):
    R = v.shape[0]

    def body(v_ref, o_ref, buf, send_sems, recv_sems):
        x, y, c, _ = _place()
        me = 4 * x + 2 * y + c
        buf[me] = v_ref[...]
        sends = []
        for k in range(1, 8):
            px = 1 - x if k & 4 else x
            py = 1 - y if k & 2 else y
            pc = 1 - c if k & 1 else c
            sends.append(_remote(v_ref, buf.at[me], send_sems, recv_sems, k - 1, (px, py, pc)))
        for cp in sends:
            cp.start()
        for k in range(1, 8):
            px = 1 - x if k & 4 else x
            py = 1 - y if k & 2 else y
            pc = 1 - c if k & 1 else c
            _remote(v_ref, buf.at[4 * px + 2 * py + pc], send_sems, recv_sems, k - 1, (px, py, pc)).wait_recv()
        for cp in sends:
            cp.wait_send()
        acc = buf[0]
        for d in range(1, 8):
            acc = acc + buf[d]
        o_ref[...] = acc

    vm = pl.BlockSpec(memory_space=pltpu.VMEM)
    return pl.pallas_call(
        body, name="all_reduce_small", in_specs=[vm], out_specs=vm, out_shape=jax.ShapeDtypeStruct(v.shape, F32),
        scratch_shapes=[pltpu.VMEM((8, R, LANES), F32), pltpu.SemaphoreType.DMA((7,)), pltpu.SemaphoreType.DMA((7,))],
    )(v)


def _pad_to(a, n):
    return a if a.shape[0] == n else jnp.pad(a, (0, n - a.shape[0]))


def _piece_len(shape):
    return -(-math.prod(shape) // PACK_ALIGN) * PACK_ALIGN


def _pack(pieces, dtype):
    flat = jnp.concatenate([_pad_to(a.reshape(-1).astype(dtype), _piece_len(a.shape)) for a in pieces])
    return flat.reshape(-1, LANES)


def _unpack(flat, shapes, lead):
    flat = flat.reshape(lead + (-1,))
    out, off = [], 0
    for shp in shapes:
        out.append(flat[..., off:off + math.prod(shp)].reshape(lead + tuple(shp)))
        off += _piece_len(shp)
    return out


def _join(name, a):
    if name in COL_SHARDED:
        return a.transpose(1, 0, 2).reshape(a.shape[1], -1)
    if name in ROW_SHARDED:
        return a.reshape(-1, a.shape[-1])
    return a.transpose(1, 0, 2, 3).reshape(a.shape[1], -1, a.shape[-1])


def _split(name, a):
    if name in COL_SHARDED:
        return a.reshape(a.shape[0], N_SHARDS, -1).transpose(1, 0, 2)
    if name in ROW_SHARDED:
        return a.reshape(N_SHARDS, -1, a.shape[-1])
    return a.reshape(a.shape[0], N_SHARDS, -1, a.shape[-1]).transpose(1, 0, 2, 3)


def _heads_split(w, H, first, second, pad_second):
    K = w.shape[0]
    w3 = w.reshape(K, H, first + second)
    b = w3[:, :, first:]
    if pad_second > second:
        b = jnp.pad(b, ((0, 0), (0, 0), (0, pad_second - second)))
    return jnp.concatenate([w3[:, :, :first].reshape(K, -1), b.reshape(K, -1)], axis=1)


def _heads_merge(w, H, first, second, pad_second):
    K = w.shape[0]
    a = w[:, :H * first].reshape(K, H, first)
    b = w[:, H * first:].reshape(K, H, pad_second)[:, :, :second]
    return jnp.concatenate([a, b], axis=2).reshape(K, -1)


def kernel(x, p, positions, norm_mix_g, w_in, pool_w, pool_scale, q_norm_g, w_uq, kv_norm_g, w_ukv, w_out, norm_ffn_g, w_up, conv_w, conv_b, w_down, norm_ple_g, w_ple, w_ple_gate, final_norm_g, loss_target, m_norm_mix_g, m_w_in, m_pool_w, m_pool_scale, m_q_norm_g, m_w_uq, m_kv_norm_g, m_w_ukv, m_w_out, m_norm_ffn_g, m_w_up, m_conv_w, m_conv_b, m_w_down, m_norm_ple_g, m_w_ple, m_w_ple_gate, m_final_norm_g, v_norm_mix_g, v_w_in, v_pool_w, v_pool_scale, v_q_norm_g, v_w_uq, v_kv_norm_g, v_w_ukv, v_w_out, v_norm_ffn_g, v_w_up, v_conv_w, v_conv_b, v_w_down, v_norm_ple_g, v_w_ple, v_w_ple_gate, v_final_norm_g):
    W = dict(norm_mix_g=norm_mix_g, w_in=w_in, pool_w=pool_w, pool_scale=pool_scale, q_norm_g=q_norm_g, w_uq=w_uq,
             kv_norm_g=kv_norm_g, w_ukv=w_ukv, w_out=w_out, norm_ffn_g=norm_ffn_g, w_up=w_up, conv_w=conv_w, conv_b=conv_b,
             w_down=w_down, norm_ple_g=norm_ple_g, w_ple=w_ple, w_ple_gate=w_ple_gate, final_norm_g=final_norm_g)
    M1 = dict(norm_mix_g=m_norm_mix_g, w_in=m_w_in, pool_w=m_pool_w, pool_scale=m_pool_scale, q_norm_g=m_q_norm_g, w_uq=m_w_uq,
              kv_norm_g=m_kv_norm_g, w_ukv=m_w_ukv, w_out=m_w_out, norm_ffn_g=m_norm_ffn_g, w_up=m_w_up, conv_w=m_conv_w,
              conv_b=m_conv_b, w_down=m_w_down, norm_ple_g=m_norm_ple_g, w_ple=m_w_ple, w_ple_gate=m_w_ple_gate,
              final_norm_g=m_final_norm_g)
    M2 = dict(norm_mix_g=v_norm_mix_g, w_in=v_w_in, pool_w=v_pool_w, pool_scale=v_pool_scale, q_norm_g=v_q_norm_g, w_uq=v_w_uq,
              kv_norm_g=v_kv_norm_g, w_ukv=v_w_ukv, w_out=v_w_out, norm_ffn_g=v_norm_ffn_g, w_up=v_w_up, conv_w=v_conv_w,
              conv_b=v_conv_b, w_down=v_w_down, norm_ple_g=v_norm_ple_g, w_ple=v_w_ple, w_ple_gate=v_w_ple_gate,
              final_norm_g=v_final_norm_g)

    _, T, D = x.shape
    L = p.shape[0]
    P, QL, KL, F = pool_scale.shape[-1], q_norm_g.shape[-1], kv_norm_g.shape[-1], conv_b.shape[-1]
    C = pool_w.shape[-1]
    H = (D - P) // V_DIM
    d_in = P + QL + KL + ROPE_DIM
    dims = dict(P=P, QL=QL, KL=KL, C=C, H=H)
    misc_shapes = [W[n].shape[1:] for n in MISC]
    ns_in, ns_up, ns_conv = w_in.shape[-1], w_up.shape[-1], conv_w.shape[-1]

    xi, yi, ci = lax.axis_index("x"), lax.axis_index("y"), lax.axis_index("c")
    me = 2 * xi + yi
    place = jnp.stack([me, ci]).astype(jnp.int32)

    def all_reduce(parts):
        flat = jnp.concatenate(parts)
        padded = -(-flat.shape[0] // (8 * LANES)) * (8 * LANES)
        return _all_reduce_small(_pad_to(flat, padded).reshape(-1, LANES)).reshape(-1)

    inv_freq = 1.0 / (ROPE_THETA ** (jnp.arange(0, ROPE_DIM, 2, dtype=F32) / ROPE_DIM))
    inv_lane = jnp.concatenate([inv_freq, inv_freq, jnp.zeros((LANES - ROPE_DIM,), F32)]).reshape(1, LANES)
    tabs = _rope_tables(positions.reshape(T, 1).astype(F32), inv_lane)

    local = [W[n].astype(BF16) for n in BIG] + [jnp.stack([_pack([W[n][l] for n in MISC], BF16) for l in range(L)])]
    placed = lax.dynamic_update_slice(jnp.zeros((L, CONV_TAPS, F), F32), conv_w, (0, 0, me * ns_conv))
    conv_full = all_reduce([jnp.where(ci == 0, placed, 0.0).reshape(-1)])[:L * CONV_TAPS * F].reshape(L, CONV_TAPS, F)

    def layout(got, l):
        g = dict(zip(BIG, got[:-1]))
        misc = {n: _join(n, a) for n, a in zip(MISC, _unpack(got[-1], misc_shapes, (N_SHARDS,)))}
        return dict(
            w_in=jnp.concatenate([g["w_in"][sh] for sh in range(N_SHARDS)] + [jnp.zeros((D, LANES - ROPE_DIM), BF16)], axis=1),
            w_out=g["w_out"].reshape(-1, D), w_down=g["w_down"].reshape(-1, D), w_ple_gate=g["w_ple_gate"].reshape(-1, D),
            w_up=g["w_up"], w_ple=misc["w_ple"], pool_w=misc["pool_w"],
            w_uq=_heads_split(misc["w_uq"], H, NOPE_DIM, ROPE_DIM, LANES),
            w_ukv=_heads_split(misc["w_ukv"], H, NOPE_DIM, V_DIM, V_DIM),
            cw8=jnp.concatenate([conv_full[l], conv_b[l][None], jnp.zeros((4, F), F32)], axis=0))

    half_up = (0, N_SHARDS // 2), (N_SHARDS // 2, N_SHARDS // 2)

    h = x[0]
    saved, FW = [], []
    up_at = BIG.index("w_up")
    arriving = _run_side(_forward_side(_run_side(_gather_side(local, 0, _whole(local, True), {})[0], "all_gather")), "gather_forward")
    for l in range(L):
        fw = layout(arriving, l)
        FW.append(fw)
        s = dict(h0=h)
        nxt = {}
        parts = _plan_gather(local, up_at, len(local) - 1) if l + 1 < L else [None] * 4

        def gather(pieces):
            return _gather_side(local, l + 1, pieces, nxt)
        s["n1"] = _rms_fwd(h, norm_mix_g[l], "norm_mix")
        s["u"] = _matmul(s["n1"], fw["w_in"], "nn", F32, "mm_in", tm=512, tn=d_in + LANES - ROPE_DIM)
        s["diff"], s["cqn"], s["ckvn"], s["kr"] = _post_u(s["u"], q_norm_g[l], kv_norm_g[l], tabs, dims)
        s["q"] = _q_rope(_matmul(s["cqn"], fw["w_uq"], "nn", F32, "mm_uq", tn=2 * H * LANES), tabs, dims)
        s["kv"] = _matmul(s["ckvn"], fw["w_ukv"], "nn", BF16, "mm_ukv", tn=2 * H * LANES)
        s["o"], s["lse"] = _ride(lambda sd: _flash_fwd(s["q"], s["kv"], s["kr"], dims, sd), 2, parts[0], gather, nxt)
        s["mix"] = jnp.concatenate([_pool_fwd(s["diff"], fw["pool_w"], pool_scale[l], dims), s["o"]], axis=1)
        s["h1"] = _matmul(s["mix"], fw["w_out"], "nn", F32, "mm_out", res=h, tm=512)
        s["n2"] = _rms_fwd(s["h1"], norm_ffn_g[l], "norm_ffn")
        s["gate"] = _ride(lambda sd: _matmul(s["n2"], fw["w_up"], "nn", BF16, "mm_gate", tm=512, tn=ns_up // 2, b_shards=half_up[0],
                                             side=sd), 1, parts[1], gather, nxt)
        s["up"] = _ride(lambda sd: _matmul(s["n2"], fw["w_up"], "nn", BF16, "mm_up", tm=512, tn=ns_up // 2, b_shards=half_up[1],
                                           side=sd), 1, parts[2], gather, nxt)
        s["a"] = _ffn_fwd(s["gate"], s["up"], fw["cw8"])
        s["h2"] = _ride(lambda sd: _matmul(s["a"], fw["w_down"], "nn", F32, "mm_down", res=s["h1"], tm=512,
                                           tk=F // 2 if F % (2 * LANES) == 0 else F, side=sd), 1, parts[3], gather, nxt)
        s["n3"] = _rms_fwd(s["h2"], norm_ple_g[l], "norm_ple")
        if nxt:
            s["gl"], *arriving = _matmul(s["n3"], fw["w_ple_gate"], "nn", F32, "mm_ple_gate", tm=512,
                                         side=_forward_side([nxt[a] for a in range(len(local))]))
        else:
            s["gl"] = _matmul(s["n3"], fw["w_ple_gate"], "nn", F32, "mm_ple_gate", tm=512)
        s["pe"] = _matmul(p[l, 0], fw["w_ple"], "nn", F32, "mm_ple", tn=D)
        h = _ple_fwd(s["h2"], s["gl"], s["pe"])
        saved.append(s)

    dh, dhb, dg_final, loss_part = _final_loss(h, loss_target[0], final_norm_g)
    loss = lax.psum(loss_part[0, 0], ("x", "y", "c"))

    small = {}
    reduced = [None] * L
    raw = None
    for l in reversed(range(L)):
        fw, s = FW[l], saved[l]
        gw = {}
        got = {}
        dpe, dgl = _ple_bwd(dh, s["gl"], s["pe"])
        gw["w_ple"] = _matmul(p[l, 0], dpe, "tn", BF16, "dw_ple", tm=512)
        gw["w_ple_gate"] = _matmul(s["n3"], dgl, "tn", BF16, "dw_ple_gate")
        dn3 = _matmul(dgl, fw["w_ple_gate"], "nt", F32, "dx_ple_gate", tm=512)
        dh, dhb, small["norm_ple_g", l] = _rms_bwd(dn3, s["h2"], norm_ple_g[l], dh, "norm_ple_bwd")

        tn_down = F // 4 if F % (4 * LANES) == 0 else F
        if raw:
            da, *from_sibling = _matmul(dhb, fw["w_down"], "nt", BF16, "dx_down", tn=tn_down, side=_sibling_side(raw))
            waiting = _reduce_begin(raw, from_sibling, place)
            parts = _plan(waiting, False, up_at)
        else:
            da = _matmul(dhb, fw["w_down"], "nt", BF16, "dx_down", tn=tn_down)
            waiting, parts = None, [None] * 4

        def chips(pieces):
            return _chip_side(waiting, pieces, got)

        gw["w_down"] = _matmul(s["a"], dhb, "tn", BF16, "dw_down")
        dgate, dup, dcw = _ffn_bwd(da, s["gate"], s["up"], fw["cw8"])
        small["conv_w", l], small["conv_b", l] = dcw[:CONV_TAPS], dcw[CONV_TAPS:CONV_TAPS + 1]
        gw["w_up"] = _ride(lambda sd: _matmul(s["n2"], dgate, "tn", BF16, "dw_gate", tn=ns_up // 2,
                                              out_shards=(half_up[0][0], N_SHARDS, ns_up), side=sd), 1, parts[1], chips, got)
        gw["w_up"] = _ride(lambda sd: _matmul(s["n2"], dup, "tn", BF16, "dw_up", tn=ns_up // 2,
                                              out_shards=(half_up[1][0], N_SHARDS, ns_up), carry=gw["w_up"], side=sd), 1, parts[2], chips, got)
        dn2 = _ride(lambda sd: _matmul(dgate, fw["w_up"], "nt", F32, "dx_gate", tm=512, tn=D, tk=ns_up // 2, b_shards=half_up[0],
                                       side=sd), 1, parts[3], chips, got)
        dn2 = _matmul(dup, fw["w_up"], "nt", F32, "dx_up", res=dn2, tm=512, tn=D, tk=ns_up // 2, b_shards=half_up[1])
        dh, dhb, small["norm_ffn_g", l] = _rms_bwd(dn2, s["h1"], norm_ffn_g[l], dh, "norm_ffn_bwd")

        dmix = _matmul(dhb, fw["w_out"], "nt", BF16, "dx_out")
        gw["w_out"] = _matmul(s["mix"], dhb, "tn", BF16, "dw_out")
        ddiff, gw["pool_w"], small["pool_scale", l] = _pool_bwd(dmix, s["diff"], fw["pool_w"], pool_scale[l], dims)
        dq, dkn, dv, dkr = _ride(lambda sd: _flash_bwd(s["q"], s["kv"], s["kr"], s["o"], s["lse"], dmix, dims, sd), 4, parts[0], chips, got)
        if got:
            reduced[l + 1] = _reduce_end(waiting, [got[a] for a in range(len(waiting))], place)
        dqb = _dq_post(dq, tabs, dims)
        dkv = jnp.concatenate([dkn, dv], axis=1)
        gw["w_uq"] = _heads_merge(_matmul(s["cqn"], dqb, "tn", BF16, "dw_uq", tn=2 * H * LANES), H, NOPE_DIM, ROPE_DIM, LANES)
        gw["w_ukv"] = _heads_merge(_matmul(s["ckvn"], dkv, "tn", BF16, "dw_ukv", tn=2 * H * LANES), H, NOPE_DIM, V_DIM, V_DIM)
        dcqn = _matmul(dqb, fw["w_uq"], "nt", F32, "dx_uq")
        dckvn = _matmul(dkv, fw["w_ukv"], "nt", F32, "dx_ukv")
        du, small["q_norm_g", l], small["kv_norm_g", l] = _pre_u_bwd(s["u"], dcqn, dckvn, ddiff, dkr, q_norm_g[l], kv_norm_g[l], tabs, dims)
        gw["w_in"] = _matmul(s["n1"], du, "tn", BF16, "dw_in", tm=512, tn=du.shape[1])[:, :d_in]
        dn1 = _matmul(du, fw["w_in"], "nt", F32, "dx_in", tm=512, tk=du.shape[1])
        dh, dhb, small["norm_mix_g", l] = _rms_bwd(dn1, s["h0"], norm_mix_g[l], dh, "norm_mix_bwd")

        split = {n: _split(n, gw[n]) for n in MISC}
        arrs = [jnp.stack([gw["w_in"][:, sh * ns_in:(sh + 1) * ns_in] for sh in range(N_SHARDS)]),
                gw["w_out"].reshape(N_SHARDS, -1, D), gw["w_up"], gw["w_down"].reshape(N_SHARDS, -1, D),
                gw["w_ple_gate"].reshape(N_SHARDS, -1, D),
                jnp.stack([_pack([split[n][sh] for n in MISC], BF16) for sh in range(N_SHARDS)])]
        raw = arrs
    waiting = _reduce_begin(raw, _run_side(_sibling_side(raw), "rs_sibling"), place)
    reduced[0] = _reduce_end(waiting, _run_side(_chip_side(waiting, _whole(waiting, False), {})[0], "rs_chips"), place)

    grads = {n: jnp.stack([reduced[l][k].reshape(W[n].shape[1:]) for l in range(L)]) for k, n in enumerate(BIG)}
    per_layer = [_unpack(reduced[l][-1], misc_shapes, ()) for l in range(L)]
    for k, n in enumerate(MISC):
        grads[n] = jnp.stack([per_layer[l][k] for l in range(L)])

    small_names = ("norm_mix_g", "pool_scale", "q_norm_g", "kv_norm_g", "norm_ffn_g", "conv_b", "norm_ple_g", "conv_w")
    summed = all_reduce([small[n, l].reshape(-1) for n in small_names for l in range(L)] + [dg_final.reshape(-1)])
    off = 0
    for n in small_names:
        size = CONV_TAPS * F if n == "conv_w" else W[n].shape[-1]
        grads[n] = summed[off:off + L * size].reshape((L, CONV_TAPS, F) if n == "conv_w" else (L, size))
        off += L * size
    grads["final_norm_g"] = summed[off:off + D]
    grads["conv_w"] = lax.dynamic_slice(grads["conv_w"], (0, 0, me * ns_conv), (L, CONV_TAPS, ns_conv))

    deltas, new_m, new_v = {}, {}, {}
    for n in WEIGHTS:
        deltas[n], new_m[n], new_v[n] = _adamw(W[n], grads[n], M1[n], M2[n], "adamw_" + n)

    return (loss, dh[None], *[grads[n] for n in WEIGHTS], *[deltas[n] for n in WEIGHTS],
            *[new_m[n] for n in WEIGHTS], *[new_v[n] for n in WEIGHTS])
```

```python
import functools
import math

import jax
import jax.numpy as jnp
from jax import lax
from jax.experimental import pallas as pl
from jax.experimental.pallas import tpu as pltpu

F32 = jnp.float32
BF16 = jnp.bfloat16

NOPE_DIM = 128
ROPE_DIM = 64
V_DIM = 128
LANES = 128
SUBLANES_BF16 = 16
ROPE_THETA = 10000.0
EPS = 1e-6
POOL_WINDOWS = (2, 4, 8, 16)
POOL_HALO = 16
CONV_TAPS = 3
CONV_HALO = 8
ADAM_LR = 0.001
ADAM_B1 = 0.9
ADAM_B2 = 0.999
ADAM_EPS = 1e-08
ADAM_WD = 0.01
ADAM_STEP = 10
NEG_BIG = -1e30
ATT_SCALE = 1.0 / math.sqrt(NOPE_DIM + ROPE_DIM)
V7X_VMEM_BYTES = 64 * 2 ** 20
N_SHARDS = 4
PACK_ALIGN = 2 * SUBLANES_BF16 * LANES

TILES = dict(row=256, att=512, mm_m=1024, mm_n=1024, mm_k=2048, ffn_row=512, ffn_c=512, add_bytes=1 << 20)

BIG = ("w_in", "w_out", "w_up", "w_down", "w_ple_gate")
MISC = ("w_uq", "w_ukv", "w_ple", "pool_w")
COL_SHARDED = ("w_in", "w_uq", "w_ukv", "w_up", "conv_w", "w_ple")
ROW_SHARDED = ("w_out", "w_down", "w_ple_gate")
WEIGHTS = ("norm_mix_g", "w_in", "pool_w", "pool_scale", "q_norm_g", "w_uq", "kv_norm_g", "w_ukv", "w_out",
           "norm_ffn_g", "w_up", "conv_w", "conv_b", "w_down", "norm_ple_g", "w_ple", "w_ple_gate", "final_norm_g")
MESH = pl.DeviceIdType.MESH


def _nbytes(shape, dtype):
    return math.prod(shape) * jnp.dtype(dtype).itemsize


def _params(sem, need_bytes):
    limit = min(V7X_VMEM_BYTES - (8 << 20), max(32 << 20, int(need_bytes)))
    return pltpu.CompilerParams(dimension_semantics=sem, vmem_limit_bytes=limit)


def _tile(n, want, mult=8):
    if n <= want:
        return n
    for t in range(want - want % mult, 0, -mult):
        if n % t == 0:
            return t
    return n


def _sigmoid(x):
    return 1.0 / (1.0 + jnp.exp(-x))


def _rstd(x):
    return lax.rsqrt(jnp.mean(x * x, axis=-1, keepdims=True) + EPS)


_DOT_DIMS = {"nn": (((1,), (0,)), ((), ())), "nt": (((1,), (1,)), ((), ())), "tn": (((0,), (0,)), ((), ()))}


def _matmul(a, b, mode, out_dtype, name, res=None, tm=None, tn=None, tk=None, b_shards=None, out_shards=None, carry=None,
            side=None):
    if mode == "nn":
        (M, K), N = a.shape, b.shape[-1] * (b_shards[1] if b_shards else 1)
    elif mode == "nt":
        (M, K), N = a.shape, b.shape[-2]
    else:
        (K, M), N = a.shape, b.shape[1]
    per = b.shape[-1] if b_shards else (out_shards[2] if out_shards else None)
    tm = _tile(M, tm or TILES["mm_m"], LANES)
    tn = _tile(per if (per and mode != "nt") else N, tn or TILES["mm_n"], LANES)
    tk = _tile(per if (per and mode == "nt") else K, tk or TILES["mm_k"], LANES)
    nk = K // tk
    has_res = res is not None
    has_carry = carry is not None
    dims = _DOT_DIMS[mode]

    def body(*refs):
        a_ref, b_ref = refs[0], refs[1]
        o_ref = refs[2 + has_res + has_carry]
        part = lax.dot_general(a_ref[...].astype(BF16), b_ref[...].astype(BF16), dims, preferred_element_type=F32)

        def finish(acc):
            if has_res:
                acc = acc + refs[2][...]
            o_ref[...] = acc.astype(o_ref.dtype)

        if nk == 1:
            finish(part)
        else:
            acc_ref = refs[3 + has_res + has_carry]
            k = pl.program_id(2)

            @pl.when(k == 0)
            def _():
                acc_ref[...] = part

            @pl.when(k > 0)
            def _():
                acc_ref[...] += part

            @pl.when(k == nk - 1)
            def _():
                finish(acc_ref[...])

    if mode == "nn":
        a_spec, b_spec = pl.BlockSpec((tm, tk), lambda i, j, k: (i, k)), pl.BlockSpec((tk, tn), lambda i, j, k: (k, j))
    elif mode == "nt":
        a_spec, b_spec = pl.BlockSpec((tm, tk), lambda i, j, k: (i, k)), pl.BlockSpec((tn, tk), lambda i, j, k: (j, k))
    else:
        a_spec, b_spec = pl.BlockSpec((tk, tm), lambda i, j, k: (k, i)), pl.BlockSpec((tk, tn), lambda i, j, k: (k, j))
    o_spec = pl.BlockSpec((tm, tn), lambda i, j, k: (i, j))
    out_shape = jax.ShapeDtypeStruct((M, N), out_dtype)
    if b_shards:
        first = b_shards[0]
        if mode == "nn":
            nps = per // tn
            b_spec = pl.BlockSpec((None, tk, tn), lambda i, j, k: (first + j // nps, k, j % nps))
        else:
            kps = per // tk
            b_spec = pl.BlockSpec((None, tn, tk), lambda i, j, k: (first + k // kps, j, k % kps))
    if out_shards:
        ofirst, nps_o = out_shards[0], per // tn
        o_spec_out = pl.BlockSpec((None, tm, tn), lambda i, j, k: (ofirst + j // nps_o, i, j % nps_o))
        out_shape = jax.ShapeDtypeStruct((out_shards[1], M, per), out_dtype)
    else:
        o_spec_out = o_spec
    in_specs, args = [a_spec, b_spec], [a, b]
    need = 2 * (_nbytes((tm, tk), a.dtype) + _nbytes((tk, tn), b.dtype) + _nbytes((tm, tn), out_dtype)) + 2 * _nbytes((tm, tn), F32)
    if has_res:
        in_specs.append(o_spec)
        args.append(res)
        need += 2 * _nbytes((tm, tn), res.dtype)
    aliases = {}
    if has_carry:
        aliases = {len(args): 0}
        in_specs.append(pl.BlockSpec(memory_space=pl.ANY))
        args.append(carry)
    scratch = [pltpu.VMEM((tm, tn), F32)] if nk > 1 else []
    grid = (M // tm, N // tn, nk)
    params = _params(("parallel", "parallel", "arbitrary") if side is None else ("arbitrary",) * 3, need + (4 << 20))
    if side is None:
        return pl.pallas_call(body, name=name, grid=grid, in_specs=in_specs, out_specs=o_spec_out, out_shape=out_shape,
                              scratch_shapes=scratch, input_output_aliases=aliases, compiler_params=params)(*args)
    body, s_ins, s_in_specs, s_shapes, s_out_specs, s_sems, s_aliases = _attach(side, body, len(args), 1, grid)
    return pl.pallas_call(
        body, name=name, grid=grid, in_specs=in_specs + s_in_specs, out_specs=[o_spec_out] + s_out_specs,
        out_shape=[out_shape] + s_shapes, scratch_shapes=scratch + s_sems, input_output_aliases={**aliases, **s_aliases},
        compiler_params=params,
    )(*args, *s_ins)


def _rms_fwd(x, g, name):
    T, D = x.shape
    tt = _tile(T, TILES["row"])

    def body(x_ref, g_ref, o_ref):
        xv = x_ref[...]
        o_ref[...] = (xv * _rstd(xv) * g_ref[...]).astype(o_ref.dtype)

    row = pl.BlockSpec((tt, D), lambda i: (i, 0))
    return pl.pallas_call(
        body, name=name, grid=(T // tt,), in_specs=[row, pl.BlockSpec((1, D), lambda i: (0, 0))], out_specs=row,
        out_shape=jax.ShapeDtypeStruct((T, D), BF16), compiler_params=_params(("parallel",), 8 * _nbytes((tt, D), F32)),
    )(x, g.reshape(1, D))


def _rms_bwd(dn, x, g, dres, name):
    T, D = x.shape
    tt = _tile(T, TILES["row"])

    def body(dn_ref, x_ref, g_ref, dres_ref, dx_ref, dxb_ref, dg_ref):
        i = pl.program_id(0)
        xv = x_ref[...]
        r = _rstd(xv)
        xh = xv * r
        dnv = dn_ref[...].astype(F32)
        dxh = dnv * g_ref[...]
        tot = dres_ref[...] + r * (dxh - xh * jnp.mean(dxh * xh, axis=-1, keepdims=True))
        dx_ref[...] = tot
        dxb_ref[...] = tot.astype(BF16)
        part = jnp.sum(dnv * xh, axis=0, keepdims=True)

        @pl.when(i == 0)
        def _():
            dg_ref[...] = part

        @pl.when(i > 0)
        def _():
            dg_ref[...] += part

    row = pl.BlockSpec((tt, D), lambda i: (i, 0))
    vec = pl.BlockSpec((1, D), lambda i: (0, 0))
    return pl.pallas_call(
        body, name=name, grid=(T // tt,), in_specs=[row, row, vec, row], out_specs=[row, row, vec],
        out_shape=[jax.ShapeDtypeStruct((T, D), F32), jax.ShapeDtypeStruct((T, D), BF16), jax.ShapeDtypeStruct((1, D), F32)],
        compiler_params=_params(("arbitrary",), 16 * _nbytes((tt, D), F32)),
    )(dn, x, g.reshape(1, D), dres)


def _final_loss(h, target, g):
    T, D = h.shape
    tt = _tile(T, TILES["row"])

    def body(h_ref, t_ref, g_ref, dx_ref, dxb_ref, dg_ref, loss_ref):
        i = pl.program_id(0)
        xv = h_ref[...]
        r = _rstd(xv)
        xh = xv * r
        gv = g_ref[...]
        err = xh * gv - t_ref[...]
        lpart = 0.5 * jnp.sum(jnp.mean(err * err, axis=-1, keepdims=True), axis=0, keepdims=True)
        dy = err * (1.0 / D)
        dxh = dy * gv
        dx = r * (dxh - xh * jnp.mean(dxh * xh, axis=-1, keepdims=True))
        dx_ref[...] = dx
        dxb_ref[...] = dx.astype(BF16)
        gpart = jnp.sum(dy * xh, axis=0, keepdims=True)
        lrow = jnp.broadcast_to(lpart, (1, LANES))

        @pl.when(i == 0)
        def _():
            dg_ref[...] = gpart
            loss_ref[...] = lrow

        @pl.when(i > 0)
        def _():
            dg_ref[...] += gpart
            loss_ref[...] += lrow

    row = pl.BlockSpec((tt, D), lambda i: (i, 0))
    vec = pl.BlockSpec((1, D), lambda i: (0, 0))
    return pl.pallas_call(
        body, name="final_loss", grid=(T // tt,), in_specs=[row, row, vec],
        out_specs=[row, row, vec, pl.BlockSpec((1, LANES), lambda i: (0, 0))],
        out_shape=[jax.ShapeDtypeStruct((T, D), F32), jax.ShapeDtypeStruct((T, D), BF16),
                   jax.ShapeDtypeStruct((1, D), F32), jax.ShapeDtypeStruct((1, LANES), F32)],
        compiler_params=_params(("arbitrary",), 16 * _nbytes((tt, D), F32)),
    )(h, target, g.reshape(1, D))


def _rope_tables(pos_col, inv_lane):
    T = pos_col.shape[0]
    tt = _tile(T, TILES["row"])

    def body(p_ref, f_ref, c_ref, s1_ref, s2_ref):
        ang = p_ref[...] * f_ref[...]
        lane = lax.broadcasted_iota(jnp.int32, ang.shape, 1)
        half = ROPE_DIM // 2
        cs, sn = jnp.cos(ang), jnp.sin(ang)
        c_ref[...] = jnp.where(lane < ROPE_DIM, cs, 0.0)
        s1_ref[...] = jnp.where(lane < half, -sn, 0.0)
        s2_ref[...] = jnp.where((lane >= half) & (lane < ROPE_DIM), sn, 0.0)

    tab = pl.BlockSpec((tt, LANES), lambda i: (i, 0))
    shp = jax.ShapeDtypeStruct((T, LANES), F32)
    return pl.pallas_call(
        body, name="rope_tables", grid=(T // tt,),
        in_specs=[pl.BlockSpec((tt, 1), lambda i: (i, 0)), pl.BlockSpec((1, LANES), lambda i: (0, 0))],
        out_specs=[tab, tab, tab], out_shape=[shp, shp, shp],
        compiler_params=_params(("parallel",), 32 * _nbytes((tt, LANES), F32)),
    )(pos_col, inv_lane)


def _rope(x, c, s1, s2):
    return x * c + pltpu.roll(x, LANES - ROPE_DIM // 2, 1) * s1 + pltpu.roll(x, ROPE_DIM // 2, 1) * s2


def _rope_t(d, c, s1, s2):
    return d * c + pltpu.roll(d * s1, ROPE_DIM // 2, 1) + pltpu.roll(d * s2, LANES - ROPE_DIM // 2, 1)


def _window_sum(xe, w, forward):
    n = xe.shape[0]
    s, sh = xe, 1
    while sh < w:
        s = s + pltpu.roll(s, (n - sh) if forward else sh, 0)
        sh *= 2
    return s


def _post_u(u, gq, gkv, tabs, dims):
    T, Dp = u.shape
    P, QL, KL, C = dims["P"], dims["QL"], dims["KL"], dims["C"]
    tt = _tile(T, TILES["row"], POOL_HALO)
    hb = tt // POOL_HALO

    def body(u_ref, halo_ref, gq_ref, gkv_ref, c_ref, s1_ref, s2_ref, diff_ref, cq_ref, ckv_ref, kr_ref):
        i = pl.program_id(0)
        t = i * tt + lax.broadcasted_iota(jnp.int32, (tt, 1), 0)
        halo = jnp.where(i > 0, halo_ref[...], 0.0)
        for gi, w in enumerate(POOL_WINDOWS):
            cols = slice(gi * C, (gi + 1) * C)
            xg = u_ref[:, cols]
            s = _window_sum(jnp.concatenate([halo[:, cols], xg], axis=0), w, False)[POOL_HALO:]
            cnt = jnp.minimum(t + 1, w).astype(F32)
            diff_ref[:, cols] = (s / cnt - xg).astype(BF16)
        cq = u_ref[:, P:P + QL]
        cq_ref[...] = (cq * _rstd(cq) * gq_ref[...]).astype(BF16)
        ckv = u_ref[:, P + QL:P + QL + KL]
        ckv_ref[...] = (ckv * _rstd(ckv) * gkv_ref[...]).astype(BF16)
        kr_ref[...] = _rope(u_ref[:, P + QL + KL:], c_ref[...], s1_ref[...], s2_ref[...]).astype(BF16)

    def row(w):
        return pl.BlockSpec((tt, w), lambda i: (i, 0))

    def vec(w):
        return pl.BlockSpec((1, w), lambda i: (0, 0))

    return pl.pallas_call(
        body, name="post_u", grid=(T // tt,),
        in_specs=[row(Dp), pl.BlockSpec((POOL_HALO, P), lambda i: (jnp.maximum(i * hb - 1, 0), 0)),
                  vec(QL), vec(KL), row(LANES), row(LANES), row(LANES)],
        out_specs=[row(P), row(QL), row(KL), row(LANES)],
        out_shape=[jax.ShapeDtypeStruct((T, P), BF16), jax.ShapeDtypeStruct((T, QL), BF16),
                   jax.ShapeDtypeStruct((T, KL), BF16), jax.ShapeDtypeStruct((T, LANES), BF16)],
        compiler_params=_params(("parallel",), 10 * _nbytes((tt, Dp), F32)),
    )(u, u, gq.reshape(1, QL), gkv.reshape(1, KL), *tabs)


def _pre_u_bwd(u, d_cqn, d_ckvn, d_diff, dkr, gq, gkv, tabs, dims):
    T, Dp = u.shape
    P, QL, KL, C, H = dims["P"], dims["QL"], dims["KL"], dims["C"], dims["H"]
    tt = _tile(T, TILES["row"], POOL_HALO)
    hb = tt // POOL_HALO
    n_t = T // tt

    def norm_bwd(xv, dn, gv):
        r = _rstd(xv)
        xh = xv * r
        dxh = dn * gv
        return r * (dxh - xh * jnp.mean(dxh * xh, axis=-1, keepdims=True)), jnp.sum(dn * xh, axis=0, keepdims=True)

    def body(u_ref, dcq_ref, dckv_ref, dd_ref, ddn_ref, dkr_ref, gq_ref, gkv_ref, c_ref, s1_ref, s2_ref,
             du_ref, dgq_ref, dgkv_ref):
        i = pl.program_id(0)
        t = i * tt + lax.broadcasted_iota(jnp.int32, (tt, 1), 0)
        nxt = jnp.where(i < n_t - 1, ddn_ref[...].astype(F32), 0.0)
        for gi, w in enumerate(POOL_WINDOWS):
            cols = slice(gi * C, (gi + 1) * C)
            dd = dd_ref[:, cols].astype(F32)
            e = dd / jnp.minimum(t + 1, w).astype(F32)
            s = _window_sum(jnp.concatenate([e, nxt[:, cols] / float(w)], axis=0), w, True)[:tt]
            du_ref[:, cols] = (s - dd).astype(BF16)
        dq, pq = norm_bwd(u_ref[:, P:P + QL], dcq_ref[...], gq_ref[...])
        du_ref[:, P:P + QL] = dq.astype(BF16)
        dkv, pkv = norm_bwd(u_ref[:, P + QL:P + QL + KL], dckv_ref[...], gkv_ref[...])
        du_ref[:, P + QL:P + QL + KL] = dkv.astype(BF16)
        dk = dkr_ref[0]
        for hh in range(1, H):
            dk = dk + dkr_ref[hh]
        du_ref[:, P + QL + KL:] = _rope_t(dk, c_ref[...], s1_ref[...], s2_ref[...]).astype(BF16)

        @pl.when(i == 0)
        def _():
            dgq_ref[...] = pq
            dgkv_ref[...] = pkv

        @pl.when(i > 0)
        def _():
            dgq_ref[...] += pq
            dgkv_ref[...] += pkv

    def row(w):
        return pl.BlockSpec((tt, w), lambda i: (i, 0))

    def vec(w):
        return pl.BlockSpec((1, w), lambda i: (0, 0))

    return pl.pallas_call(
        body, name="pre_u_bwd", grid=(n_t,),
        in_specs=[row(Dp), row(QL), row(KL), row(P),
                  pl.BlockSpec((POOL_HALO, P), lambda i: (jnp.minimum((i + 1) * hb, T // POOL_HALO - 1), 0)),
                  pl.BlockSpec((H, tt, LANES), lambda i: (0, i, 0)), vec(QL), vec(KL), row(LANES), row(LANES), row(LANES)],
        out_specs=[row(Dp), vec(QL), vec(KL)],
        out_shape=[jax.ShapeDtypeStruct((T, Dp), BF16), jax.ShapeDtypeStruct((1, QL), F32), jax.ShapeDtypeStruct((1, KL), F32)],
        compiler_params=_params(("arbitrary",), 12 * _nbytes((tt, Dp), F32)),
    )(u, d_cqn, d_ckvn, d_diff, d_diff, dkr, gq.reshape(1, QL), gkv.reshape(1, KL), *tabs)


def _pool_fwd(diff, pw, ps, dims):
    T, P = diff.shape
    G, C = len(POOL_WINDOWS), dims["C"]
    tt = _tile(T, TILES["row"])

    def body(d_ref, w_ref, s_ref, o_ref):
        for gi in range(G):
            cols = slice(gi * C, (gi + 1) * C)
            y = jnp.dot(d_ref[:, cols], w_ref[gi], preferred_element_type=F32)
            o_ref[:, cols] = (y * s_ref[:, cols]).astype(BF16)

    row = pl.BlockSpec((tt, P), lambda i: (i, 0))
    return pl.pallas_call(
        body, name="pool_fwd", grid=(T // tt,),
        in_specs=[row, pl.BlockSpec((G, C, C), lambda i: (0, 0, 0)), pl.BlockSpec((1, P), lambda i: (0, 0))],
        out_specs=row, out_shape=jax.ShapeDtypeStruct((T, P), BF16),
        compiler_params=_params(("parallel",), 8 * _nbytes((tt, P), F32)),
    )(diff, pw, ps.reshape(1, P))


def _pool_bwd(dmix, diff, pw, ps, dims):
    T, P = diff.shape
    G, C = len(POOL_WINDOWS), dims["C"]
    tt = _tile(T, TILES["row"])

    def body(dy_ref, d_ref, w_ref, s_ref, dd_ref, dw_ref, ds_ref):
        i = pl.program_id(0)

        @pl.when(i == 0)
        def _():
            dw_ref[...] = jnp.zeros_like(dw_ref)
            ds_ref[...] = jnp.zeros_like(ds_ref)

        for gi in range(G):
            cols = slice(gi * C, (gi + 1) * C)
            dy = dy_ref[:, cols].astype(F32)
            d = d_ref[:, cols]
            w = w_ref[gi]
            ypre = jnp.dot(d, w, preferred_element_type=F32)
            ds_ref[:, cols] += jnp.sum(dy * ypre, axis=0, keepdims=True)
            dyp = (dy * s_ref[:, cols]).astype(BF16)
            dd_ref[:, cols] = lax.dot_general(dyp, w, _DOT_DIMS["nt"], preferred_element_type=F32).astype(BF16)
            dw_ref[gi] += lax.dot_general(d, dyp, _DOT_DIMS["tn"], preferred_element_type=F32)

    row = pl.BlockSpec((tt, P), lambda i: (i, 0))
    wsp = pl.BlockSpec((G, C, C), lambda i: (0, 0, 0))
    vec = pl.BlockSpec((1, P), lambda i: (0, 0))
    return pl.pallas_call(
        body, name="pool_bwd", grid=(T // tt,), in_specs=[row, row, wsp, vec], out_specs=[row, wsp, vec],
        out_shape=[jax.ShapeDtypeStruct((T, P), BF16), jax.ShapeDtypeStruct((G, C, C), F32), jax.ShapeDtypeStruct((1, P), F32)],
        compiler_params=_params(("arbitrary",), 10 * _nbytes((tt, P), F32)),
    )(dmix, diff, pw, ps.reshape(1, P))


def _q_rope(qp, tabs, dims):
    T, W = qp.shape
    H = dims["H"]
    tt = _tile(T, TILES["row"])

    def body(q_ref, c_ref, s1_ref, s2_ref, o_ref):
        o_ref[:, :H * LANES] = (q_ref[:, :H * LANES] * ATT_SCALE).astype(BF16)
        c, s1, s2 = c_ref[...], s1_ref[...], s2_ref[...]
        for hh in range(H, 2 * H):
            cols = slice(hh * LANES, (hh + 1) * LANES)
            o_ref[:, cols] = _rope(q_ref[:, cols] * ATT_SCALE, c, s1, s2).astype(BF16)

    row = pl.BlockSpec((tt, W), lambda i: (i, 0))
    tab = pl.BlockSpec((tt, LANES), lambda i: (i, 0))
    return pl.pallas_call(
        body, name="q_rope", grid=(T // tt,), in_specs=[row, tab, tab, tab], out_specs=row,
        out_shape=jax.ShapeDtypeStruct((T, W), BF16), compiler_params=_params(("parallel",), 8 * _nbytes((tt, W), F32)),
    )(qp, *tabs)


def _scores(qn_ref, qr_ref, kn_ref, kr_ref, t, diagonal):
    q = jnp.concatenate([qn_ref[...], qr_ref[...]], axis=1)
    k = jnp.concatenate([kn_ref[...], kr_ref[...]], axis=1)
    s = lax.dot_general(q, k, _DOT_DIMS["nt"], preferred_element_type=F32)
    if diagonal:
        s = jnp.where(lax.broadcasted_iota(jnp.int32, (t, t), 0) >= lax.broadcasted_iota(jnp.int32, (t, t), 1), s, NEG_BIG)
    return q, k, s


class _Side:
    def __init__(self, ins, out_shapes, n_sems, start, finish, aliases=None):
        self.ins, self.out_shapes, self.n_sems, self.start, self.finish = list(ins), list(out_shapes), n_sems, start, finish
        self.aliases = dict(aliases or {})


def _attach(side, body, n_in, n_out, grid):
    if side is None:
        return body, [], [], [], [], [], {}
    n_si, n_so = len(side.ins), len(side.out_shapes)

    def carrying(*refs):
        outs_at = n_in + n_si
        main = refs[:n_in] + refs[outs_at:outs_at + n_out] + refs[outs_at + n_out + n_so:len(refs) - 2]
        parts = (refs[n_in:outs_at], refs[outs_at + n_out:outs_at + n_out + n_so], refs[-2], refs[-1])
        ids = [pl.program_id(d) for d in range(len(grid))]
        first = functools.reduce(lambda u, v: u & v, [i == 0 for i in ids])
        last = functools.reduce(lambda u, v: u & v, [i == g - 1 for i, g in zip(ids, grid)])

        @pl.when(first)
        def _():
            side.start(*parts)

        body(*main)

        @pl.when(last)
        def _():
            side.finish(*parts)

    aliases = {n_in + i: n_out + o for i, o in side.aliases.items()}
    return carrying, side.ins, [_HBM] * n_si, side.out_shapes, [_HBM] * n_so, _sem_pair(side.n_sems), aliases


def _pairs(n, by_query):
    pairs = [(i, j) for i in range(n) for j in range(i + 1)] if by_query else [(i, j) for j in range(n) for i in range(j, n)]
    return jnp.array([p[0] for p in pairs], jnp.int32), jnp.array([p[1] for p in pairs], jnp.int32), len(pairs)


def _flash_fwd(q_att, kv, kr, dims, side=None):
    T = q_att.shape[0]
    H = dims["H"]
    G = 2 if H % 2 == 0 else 1
    t = _tile(T, TILES["att"])
    n = T // t
    it, jt, n_pairs = _pairs(n, True)
    hb = H // G

    def body(it_ref, jt_ref, qn_ref, qr_ref, kn_ref, v_ref, kr_ref, o_ref, lse_ref, m_ref, l_ref, acc_ref):
        step_id = pl.program_id(1)
        i, j = it_ref[step_id], jt_ref[step_id]

        @pl.when(j == 0)
        def _():
            m_ref[...] = jnp.full_like(m_ref, NEG_BIG)
            l_ref[...] = jnp.zeros_like(l_ref)
            acc_ref[...] = jnp.zeros_like(acc_ref)

        def step(diagonal):
            for g in range(G):
                cols = slice(g * LANES, (g + 1) * LANES)
                _, _, s = _scores(qn_ref.at[:, cols], qr_ref.at[:, cols], kn_ref.at[:, cols], kr_ref, t, diagonal)
                m_prev = m_ref[:, cols]
                m_new = jnp.maximum(m_prev, jnp.max(s, axis=1, keepdims=True))
                alpha = jnp.exp(m_prev - m_new)
                p = jnp.exp(s - m_new[:, :1])
                l_ref[:, cols] = alpha * l_ref[:, cols] + jnp.sum(p, axis=1, keepdims=True)
                acc_ref[:, cols] = alpha * acc_ref[:, cols] + jnp.dot(p.astype(BF16), v_ref[:, cols], preferred_element_type=F32)
                m_ref[:, cols] = m_new

        @pl.when(j < i)
        def _():
            step(False)

        @pl.when(j == i)
        def _():
            step(True)
            o_ref[...] = (acc_ref[...] / l_ref[...]).astype(BF16)
            lse_ref[...] = m_ref[...] + jnp.log(l_ref[...])

    blk = (t, G * LANES)
    grid = (hb, n_pairs)
    body, s_ins, s_in_specs, s_shapes, s_out_specs, s_sems, aliases = _attach(side, body, 7, 2, grid)
    return pl.pallas_call(
        body, name="flash_fwd",
        grid_spec=pltpu.PrefetchScalarGridSpec(
            num_scalar_prefetch=2, grid=grid,
            in_specs=[pl.BlockSpec(blk, lambda h, s, it, jt: (it[s], h)), pl.BlockSpec(blk, lambda h, s, it, jt: (it[s], hb + h)),
                      pl.BlockSpec(blk, lambda h, s, it, jt: (jt[s], h)), pl.BlockSpec(blk, lambda h, s, it, jt: (jt[s], hb + h)),
                      pl.BlockSpec((t, LANES), lambda h, s, it, jt: (jt[s], 0))] + s_in_specs,
            out_specs=[pl.BlockSpec(blk, lambda h, s, it, jt: (it[s], h)), pl.BlockSpec(blk, lambda h, s, it, jt: (it[s], h))] + s_out_specs,
            scratch_shapes=[pltpu.VMEM(blk, F32), pltpu.VMEM(blk, F32), pltpu.VMEM(blk, F32), *s_sems]),
        out_shape=[jax.ShapeDtypeStruct((T, H * LANES), BF16), jax.ShapeDtypeStruct((T, H * LANES), F32)] + s_shapes,
        input_output_aliases=aliases,
        compiler_params=_params(("arbitrary", "arbitrary"), 8 * G * _nbytes((t, t), F32) + (8 << 20)),
    )(it, jt, q_att, q_att, kv, kv, kr, *s_ins)


def _flash_bwd(q_att, kv, kr, o, lse, dmix, dims, side=None):
    T = q_att.shape[0]
    H = dims["H"]
    G = 2 if H % 2 == 0 else 1
    ob = dims["P"] // LANES // G
    t = _tile(T, TILES["att"])
    n = T // t
    it, jt, n_pairs = _pairs(n, False)
    hb = H // G

    def body(it_ref, jt_ref, qn_ref, qr_ref, kn_ref, v_ref, kr_ref, o_ref, lse_ref, do_ref,
             dq_ref, dkn_ref, dv_ref, dkr_ref, dk_acc, dv_acc):
        step_id = pl.program_id(1)
        i, j = it_ref[step_id], jt_ref[step_id]

        @pl.when(step_id == 0)
        def _():
            dq_ref[...] = jnp.zeros_like(dq_ref)

        @pl.when(i == j)
        def _():
            dk_acc[...] = jnp.zeros_like(dk_acc)
            dv_acc[...] = jnp.zeros_like(dv_acc)

        def step(diagonal):
            rows = pl.ds(pl.multiple_of(i * t, t), t)
            for g in range(G):
                cols = slice(g * LANES, (g + 1) * LANES)
                q, k, s = _scores(qn_ref.at[:, cols], qr_ref.at[:, cols], kn_ref.at[:, cols], kr_ref, t, diagonal)
                p = jnp.exp(s - lse_ref[:, g * LANES:g * LANES + 1])
                do = do_ref[:, cols]
                delta = jnp.sum(do.astype(F32) * o_ref[:, cols].astype(F32), axis=1, keepdims=True)
                dv_acc[:, cols] += lax.dot_general(p.astype(BF16), do, _DOT_DIMS["tn"], preferred_element_type=F32)
                dp = lax.dot_general(do, v_ref[:, cols], _DOT_DIMS["nt"], preferred_element_type=F32)
                ds = (p * (dp - delta)).astype(BF16)
                dk_acc[g] += lax.dot_general(ds, q, _DOT_DIMS["tn"], preferred_element_type=F32)
                dq_ref[g, rows, :] += jnp.dot(ds, k, preferred_element_type=F32)

        @pl.when(i > j)
        def _():
            step(False)

        @pl.when(i == j)
        def _():
            step(True)

        @pl.when(i == n - 1)
        def _():
            for g in range(G):
                dkn_ref[:, g * LANES:(g + 1) * LANES] = dk_acc[g, :, :LANES].astype(BF16)
                dkr_ref[g] = dk_acc[g, :, LANES:]
            dv_ref[...] = dv_acc[...].astype(BF16)

    blk = (t, G * LANES)
    grid = (hb, n_pairs)
    body, s_ins, s_in_specs, s_shapes, s_out_specs, s_sems, aliases = _attach(side, body, 10, 4, grid)
    return pl.pallas_call(
        body, name="flash_bwd", input_output_aliases=aliases,
        grid_spec=pltpu.PrefetchScalarGridSpec(
            num_scalar_prefetch=2, grid=grid,
            in_specs=[pl.BlockSpec(blk, lambda h, s, it, jt: (it[s], h)), pl.BlockSpec(blk, lambda h, s, it, jt: (it[s], hb + h)),
                      pl.BlockSpec(blk, lambda h, s, it, jt: (jt[s], h)), pl.BlockSpec(blk, lambda h, s, it, jt: (jt[s], hb + h)),
                      pl.BlockSpec((t, LANES), lambda h, s, it, jt: (jt[s], 0)),
                      pl.BlockSpec(blk, lambda h, s, it, jt: (it[s], h)), pl.BlockSpec(blk, lambda h, s, it, jt: (it[s], h)),
                      pl.BlockSpec(blk, lambda h, s, it, jt: (it[s], ob + h))] + s_in_specs,
            out_specs=[pl.BlockSpec((G, T, 2 * LANES), lambda h, s, it, jt: (h, 0, 0)),
                       pl.BlockSpec(blk, lambda h, s, it, jt: (jt[s], h)), pl.BlockSpec(blk, lambda h, s, it, jt: (jt[s], h)),
                       pl.BlockSpec((G, t, LANES), lambda h, s, it, jt: (h, jt[s], 0))] + s_out_specs,
            scratch_shapes=[pltpu.VMEM((G, t, 2 * LANES), F32), pltpu.VMEM(blk, F32), *s_sems]),
        out_shape=[jax.ShapeDtypeStruct((H, T, 2 * LANES), F32), jax.ShapeDtypeStruct((T, H * LANES), BF16),
                   jax.ShapeDtypeStruct((T, H * LANES), BF16), jax.ShapeDtypeStruct((H, T, LANES), F32)] + s_shapes,
        compiler_params=_params(("arbitrary", "arbitrary"),
                                12 * G * _nbytes((t, t), F32) + 2 * G * _nbytes((T, 2 * LANES), F32) + (8 << 20)),
    )(it, jt, q_att, q_att, kv, kv, kr, o, lse, dmix, *s_ins)


def _dq_post(dq, tabs, dims):
    H, T, _ = dq.shape
    tt = _tile(T, TILES["row"])

    def body(dq_ref, c_ref, s1_ref, s2_ref, o_ref):
        c, s1, s2 = c_ref[...], s1_ref[...], s2_ref[...]
        for hh in range(H):
            o_ref[:, hh * LANES:(hh + 1) * LANES] = (dq_ref[hh, :, :LANES] * ATT_SCALE).astype(BF16)
            o_ref[:, (H + hh) * LANES:(H + hh + 1) * LANES] = _rope_t(dq_ref[hh, :, LANES:] * ATT_SCALE, c, s1, s2).astype(BF16)

    tab = pl.BlockSpec((tt, LANES), lambda i: (i, 0))
    return pl.pallas_call(
        body, name="dq_post", grid=(T // tt,),
        in_specs=[pl.BlockSpec((H, tt, 2 * LANES), lambda i: (0, i, 0)), tab, tab, tab],
        out_specs=pl.BlockSpec((tt, 2 * H * LANES), lambda i: (i, 0)),
        out_shape=jax.ShapeDtypeStruct((T, 2 * H * LANES), BF16),
        compiler_params=_params(("parallel",), 8 * _nbytes((tt, 2 * H * LANES), F32)),
    )(dq, *tabs)


def _conv3(ge, cw, n):
    return cw[2:3] * ge + cw[1:2] * pltpu.roll(ge, 1, 0) + cw[0:1] * pltpu.roll(ge, 2, 0) + cw[3:4]


def _ffn_fwd(gate, up, cw8):
    T, F = gate.shape
    tt = _tile(T, TILES["ffn_row"])
    tc = _tile(F, TILES["ffn_c"], LANES)
    hb = tt // CONV_HALO

    def body(g_ref, gp_ref, u_ref, cw_ref, a_ref):
        it = pl.program_id(1)
        prev = jnp.where(it > 0, gp_ref[...].astype(F32), 0.0)
        ge = jnp.concatenate([prev, g_ref[...].astype(F32)], axis=0)
        gc = _conv3(ge, cw_ref[...], tt + CONV_HALO)[CONV_HALO:]
        a_ref[...] = (gc * _sigmoid(gc) * u_ref[...].astype(F32)).astype(BF16)

    blk = pl.BlockSpec((tt, tc), lambda jc, it: (it, jc))
    return pl.pallas_call(
        body, name="ffn_fwd", grid=(F // tc, T // tt),
        in_specs=[blk, pl.BlockSpec((CONV_HALO, tc), lambda jc, it: (jnp.maximum(it * hb - 1, 0), jc)), blk,
                  pl.BlockSpec((8, tc), lambda jc, it: (0, jc))],
        out_specs=blk, out_shape=jax.ShapeDtypeStruct((T, F), BF16),
        compiler_params=_params(("parallel", "parallel"), 16 * _nbytes((tt, tc), F32)),
    )(gate, gate, up, cw8)


def _ffn_bwd(da, gate, up, cw8):
    T, F = gate.shape
    tt = _tile(T, TILES["ffn_row"])
    tc = _tile(F, TILES["ffn_c"], LANES)
    hb = tt // CONV_HALO
    n_t = T // tt
    n = tt + 2 * CONV_HALO

    def body(da_ref, dan_ref, g_ref, gp_ref, gn_ref, u_ref, un_ref, cw_ref, dg_ref, du_ref, dcw_ref):
        it = pl.program_id(1)
        first, last = it == 0, it == n_t - 1
        cw = cw_ref[...]
        zeros = jnp.zeros((CONV_HALO, tc), F32)
        ge = jnp.concatenate([jnp.where(first, 0.0, gp_ref[...].astype(F32)), g_ref[...].astype(F32),
                              gn_ref[...].astype(F32)], axis=0)
        dae = jnp.concatenate([zeros, da_ref[...].astype(F32), jnp.where(last, 0.0, dan_ref[...].astype(F32))], axis=0)
        ue = jnp.concatenate([zeros, u_ref[...].astype(F32), un_ref[...].astype(F32)], axis=0)
        g1, g2 = pltpu.roll(ge, 1, 0), pltpu.roll(ge, 2, 0)
        gc = cw[2:3] * ge + cw[1:2] * g1 + cw[0:1] * g2 + cw[3:4]
        sg = _sigmoid(gc)
        dgc = dae * ue * (sg * (1.0 + gc * (1.0 - sg)))
        du_ref[...] = (dae * gc * sg)[CONV_HALO:CONV_HALO + tt].astype(BF16)
        dgp = cw[2:3] * dgc + cw[1:2] * pltpu.roll(dgc, n - 1, 0) + cw[0:1] * pltpu.roll(dgc, n - 2, 0)
        dg_ref[...] = dgp[CONV_HALO:CONV_HALO + tt].astype(BF16)
        mid = slice(CONV_HALO, CONV_HALO + tt)
        d_mid = dgc[mid]
        part = jnp.concatenate([jnp.sum(d_mid * g2[mid], axis=0, keepdims=True), jnp.sum(d_mid * g1[mid], axis=0, keepdims=True),
                                jnp.sum(d_mid * ge[mid], axis=0, keepdims=True), jnp.sum(d_mid, axis=0, keepdims=True),
                                jnp.zeros((4, tc), F32)], axis=0)

        @pl.when(first)
        def _():
            dcw_ref[...] = part

        @pl.when(it > 0)
        def _():
            dcw_ref[...] += part

    blk = pl.BlockSpec((tt, tc), lambda jc, it: (it, jc))
    prv = pl.BlockSpec((CONV_HALO, tc), lambda jc, it: (jnp.maximum(it * hb - 1, 0), jc))
    nxt = pl.BlockSpec((CONV_HALO, tc), lambda jc, it: (jnp.minimum((it + 1) * hb, T // CONV_HALO - 1), jc))
    cws = pl.BlockSpec((8, tc), lambda jc, it: (0, jc))
    return pl.pallas_call(
        body, name="ffn_bwd", grid=(F // tc, n_t), in_specs=[blk, nxt, blk, prv, nxt, blk, nxt, cws],
        out_specs=[blk, blk, cws],
        out_shape=[jax.ShapeDtypeStruct((T, F), BF16), jax.ShapeDtypeStruct((T, F), BF16), jax.ShapeDtypeStruct((8, F), F32)],
        compiler_params=_params(("parallel", "arbitrary"), 32 * _nbytes((tt, tc), F32)),
    )(da, da, gate, gate, gate, up, up, cw8)


def _ple_fwd(h2, gl, pe):
    T, D = h2.shape
    tt = _tile(T, TILES["row"])

    def body(h_ref, gl_ref, pe_ref, o_ref):
        o_ref[...] = h_ref[...] + pe_ref[...] * _sigmoid(gl_ref[...])

    row = pl.BlockSpec((tt, D), lambda i: (i, 0))
    return pl.pallas_call(
        body, name="ple_fwd", grid=(T // tt,), in_specs=[row, row, row], out_specs=row,
        out_shape=jax.ShapeDtypeStruct((T, D), F32), compiler_params=_params(("parallel",), 12 * _nbytes((tt, D), F32)),
    )(h2, gl, pe)


def _ple_bwd(dh, gl, pe):
    T, D = dh.shape
    tt = _tile(T, TILES["row"])

    def body(dh_ref, gl_ref, pe_ref, dpe_ref, dgl_ref):
        d = dh_ref[...]
        sg = _sigmoid(gl_ref[...])
        dpe_ref[...] = (d * sg).astype(BF16)
        dgl_ref[...] = (d * pe_ref[...] * (sg * (1.0 - sg))).astype(BF16)

    row = pl.BlockSpec((tt, D), lambda i: (i, 0))
    return pl.pallas_call(
        body, name="ple_bwd", grid=(T // tt,), in_specs=[row, row, row], out_specs=[row, row],
        out_shape=[jax.ShapeDtypeStruct((T, D), BF16), jax.ShapeDtypeStruct((T, D), BF16)],
        compiler_params=_params(("parallel",), 12 * _nbytes((tt, D), F32)),
    )(dh, gl, pe)


def _adamw(w, g, m, v, name):
    shape = w.shape
    cols = shape[-1]
    rows = math.prod(shape[:-1]) if len(shape) > 1 else 1
    w2, g2, m2, v2 = (a.reshape(rows, cols) for a in (w, g, m, v))
    tr = _tile(rows, max(8, (1 << 20) // (cols * 4)))
    c1 = 1.0 - ADAM_B1 ** ADAM_STEP
    c2 = 1.0 - ADAM_B2 ** ADAM_STEP

    def body(w_ref, g_ref, m_ref, v_ref, d_ref, mo_ref, vo_ref):
        gv = g_ref[...]
        mn = ADAM_B1 * m_ref[...] + (1.0 - ADAM_B1) * gv
        vn = ADAM_B2 * v_ref[...] + (1.0 - ADAM_B2) * (gv * gv)
        mo_ref[...] = mn
        vo_ref[...] = vn
        d_ref[...] = -ADAM_LR * ((mn / c1) / (jnp.sqrt(vn / c2) + ADAM_EPS) + ADAM_WD * w_ref[...])

    blk = pl.BlockSpec((tr, cols), lambda i: (i, 0))
    shp = jax.ShapeDtypeStruct((rows, cols), F32)
    outs = pl.pallas_call(
        body, name=name, grid=(rows // tr,), in_specs=[blk] * 4, out_specs=[blk] * 3, out_shape=[shp] * 3,
        compiler_params=_params(("parallel",), 16 * _nbytes((tr, cols), F32)),
    )(w2, g2, m2, v2)
    return tuple(o.reshape(shape) for o in outs)


_HBM = pl.BlockSpec(memory_space=pltpu.HBM)


def _place():
    x, y, c = lax.axis_index("x"), lax.axis_index("y"), lax.axis_index("c")
    return x, y, c, [(1 - x, y), (x, 1 - y), (1 - x, 1 - y)]


def _remote(src, dst, send_sems, recv_sems, k, to):
    return pltpu.make_async_remote_copy(src_ref=src, dst_ref=dst, send_sem=send_sems.at[k], recv_sem=recv_sems.at[k],
                                        device_id=to, device_id_type=MESH)


def _half(ref, lead, h):
    hr = ref.shape[-2] // 2
    return ref.at[(*lead, pl.ds(pl.multiple_of(h * hr, SUBLANES_BF16), hr))]


def _sem_pair(n):
    return [pltpu.SemaphoreType.DMA((n,)), pltpu.SemaphoreType.DMA((n,))]


def _run_side(side, name):
    n_in, n_out = len(side.ins), len(side.out_shapes)

    def body(*refs):
        parts = (refs[:n_in], refs[n_in:n_in + n_out]) + tuple(refs[n_in + n_out:])
        side.start(*parts)
        side.finish(*parts)

    return pl.pallas_call(
        body, name=name, in_specs=[_HBM] * n_in, out_specs=[_HBM] * n_out, out_shape=side.out_shapes,
        scratch_shapes=_sem_pair(side.n_sems), input_output_aliases=side.aliases,
    )(*side.ins)


def _whole(arrs, halves):
    return [(a, 0, arr.shape[-2] // (2 if halves else 1)) for a, arr in enumerate(arrs)]


def _plan(arrs, halves, big):
    whole = _whole(arrs, halves)
    q = whole[big][2] // 4
    return [[pc for pc in whole if pc[0] != big] + [(big, 0, q)]] + [[(big, k * q, q)] for k in (1, 2, 3)]


def _plan_gather(arrs, big, small):
    whole = _whole(arrs, True)
    e = whole[big][2] // 8
    return [[pc for pc in whole if pc[0] not in (big, small)], [(big, 0, 3 * e)], [(big, 3 * e, 3 * e)],
            [(big, 6 * e, 2 * e), whole[small]]]


def _ride(fn, n_main, pieces, make, store):
    if pieces is None:
        return fn(None)
    side, touched = make(pieces)
    out = fn(side)
    store.update(zip(touched, out[n_main:]))
    return out[0] if n_main == 1 else out[:n_main]


def _carried(arrs, pieces, prior):
    touched = sorted({a for a, _, _ in pieces})
    pos = {a: i for i, a in enumerate(touched)}
    carried = [a for a in touched if a in prior]
    ins = [arrs[a] for a in touched] + [prior[a] for a in carried]
    return touched, pos, ins, {len(touched) + i: pos[a] for i, a in enumerate(carried)}


def _gather_side(arrs, layer, pieces, prior):
    touched, pos, ins_arrs, aliases = _carried(arrs, pieces, prior)

    def copies(ins, outs, send_sems, recv_sems, arriving):
        x, y, c, chips = _place()
        me, sib = 2 * x + y, (x, y, 1 - c)
        out = []
        for p, (a, r0, nr) in enumerate(pieces):
            src, dst = ins[pos[a]], outs[pos[a]]
            hr = src.shape[-2] // 2
            for hlf in range(2):
                rows = pl.ds(hlf * hr + r0, nr)
                out.append(_remote(src.at[layer, rows], dst.at[me, rows], send_sems, recv_sems, 5 * p + 3 + hlf, sib))
            rows = pl.ds(pl.multiple_of(c * hr + r0, SUBLANES_BF16), nr)
            for k, (cx, cy) in enumerate(chips):
                slot = 2 * cx + cy if arriving else me
                out.append(_remote(src.at[layer, rows], dst.at[slot, rows], send_sems, recv_sems, 5 * p + k, (cx, cy, c)))
        return out

    def start(ins, outs, send_sems, recv_sems):
        for cp in copies(ins, outs, send_sems, recv_sems, False):
            cp.start()

    def finish(ins, outs, send_sems, recv_sems):
        for cp in copies(ins, outs, send_sems, recv_sems, True):
            cp.wait_recv()
        for cp in copies(ins, outs, send_sems, recv_sems, False):
            cp.wait_send()

    shapes = [jax.ShapeDtypeStruct((N_SHARDS,) + arrs[a].shape[1:], arrs[a].dtype) for a in touched]
    return _Side(ins_arrs, shapes, 5 * len(pieces), start, finish, aliases), touched


def _forward_side(arrs):
    n = len(arrs)

    def copies(outs, send_sems, recv_sems, arriving):
        x, y, c, chips = _place()
        sib = (x, y, 1 - c)
        out = []
        for a in range(n):
            for k, (cx, cy) in enumerate(chips):
                got = _half(outs[a], (2 * cx + cy,), 1 - c if arriving else c)
                out.append(_remote(got, got, send_sems, recv_sems, 3 * a + k, sib))
        return out

    def start(ins, outs, send_sems, recv_sems):
        for cp in copies(outs, send_sems, recv_sems, False):
            cp.start()

    def finish(ins, outs, send_sems, recv_sems):
        for cp in copies(outs, send_sems, recv_sems, True):
            cp.wait_recv()
        for cp in copies(outs, send_sems, recv_sems, False):
            cp.wait_send()

    return _Side(arrs, [jax.ShapeDtypeStruct(a.shape, a.dtype) for a in arrs], 3 * n, start, finish, {a: a for a in range(n)})


def _sibling_side(arrs):
    n = len(arrs)

    def copies(ins, outs, send_sems, recv_sems):
        x, y, c, _ = _place()
        return [_remote(_half(ins[a], (s,), 1 - c), outs[a].at[s], send_sems, recv_sems, N_SHARDS * a + s, (x, y, 1 - c))
                for a in range(n) for s in range(N_SHARDS)]

    def start(ins, outs, send_sems, recv_sems):
        for cp in copies(ins, outs, send_sems, recv_sems):
            cp.start()

    def finish(ins, outs, send_sems, recv_sems):
        for cp in copies(ins, outs, send_sems, recv_sems):
            cp.wait_recv()
        for cp in copies(ins, outs, send_sems, recv_sems):
            cp.wait_send()

    shapes = [jax.ShapeDtypeStruct((N_SHARDS, a.shape[1] // 2, a.shape[2]), a.dtype) for a in arrs]
    return _Side(arrs, shapes, N_SHARDS * n, start, finish)


def _chip_side(arrs, pieces, prior):
    touched, pos, ins_arrs, aliases = _carried(arrs, pieces, prior)

    def copies(ins, outs, send_sems, recv_sems):
        x, y, c, chips = _place()
        return [_remote(ins[pos[a]].at[2 * cx + cy, pl.ds(r0, nr)], outs[pos[a]].at[k, pl.ds(r0, nr)], send_sems, recv_sems,
                        3 * p + k, (cx, cy, c))
                for p, (a, r0, nr) in enumerate(pieces) for k, (cx, cy) in enumerate(chips)]

    def start(ins, outs, send_sems, recv_sems):
        for cp in copies(ins, outs, send_sems, recv_sems):
            cp.start()

    def finish(ins, outs, send_sems, recv_sems):
        for cp in copies(ins, outs, send_sems, recv_sems):
            cp.wait_recv()
        for cp in copies(ins, outs, send_sems, recv_sems):
            cp.wait_send()

    shapes = [jax.ShapeDtypeStruct((3,) + arrs[a].shape[1:], arrs[a].dtype) for a in touched]
    return _Side(ins_arrs, shapes, 3 * len(pieces), start, finish, aliases), touched


def _sibling_share(arrs):
    n = len(arrs)

    def body(*refs):
        outs, send_sems, recv_sems = refs[n:2 * n], refs[2 * n], refs[2 * n + 1]
        x, y, c, _ = _place()
        sib = (x, y, 1 - c)
        sends = [_remote(outs[a].at[c], outs[a].at[c], send_sems, recv_sems, a, sib) for a in range(n)]
        for cp in sends:
            cp.start()
        for a in range(n):
            _remote(outs[a].at[c], outs[a].at[1 - c], send_sems, recv_sems, a, sib).wait_recv()
        for cp in sends:
            cp.wait_send()

    return pl.pallas_call(
        body, name="rs_share", in_specs=[_HBM] * n, out_specs=[_HBM] * n,
        out_shape=[jax.ShapeDtypeStruct(a.shape, a.dtype) for a in arrs],
        input_output_aliases={a: a for a in range(n)}, scratch_shapes=_sem_pair(n),
    )(*arrs)


def _add_sibling(g, sib_in, place):
    S, rows, cols = g.shape
    hr = rows // 2
    tr = _tile(hr, max(SUBLANES_BF16, TILES["add_bytes"] // (cols * 2)), SUBLANES_BF16)
    nb = hr // tr

    def body(p_ref, a_ref, b_ref, o_ref):
        o_ref[...] = (a_ref[...].astype(F32) + b_ref[...].astype(F32)).astype(o_ref.dtype)

    blk = pl.BlockSpec((None, tr, cols), lambda s, r, p: (s, r, 0))
    return pl.pallas_call(
        body, name="rs_add_sibling",
        grid_spec=pltpu.PrefetchScalarGridSpec(
            num_scalar_prefetch=1, grid=(S, nb),
            in_specs=[pl.BlockSpec((None, tr, cols), lambda s, r, p: (s, p[1] * nb + r, 0)), blk], out_specs=blk),
        out_shape=jax.ShapeDtypeStruct((S, hr, cols), g.dtype),
        compiler_params=_params(("parallel", "parallel"), 16 * _nbytes((tr, cols), F32)),
    )(place, g, sib_in)


def _add_chips(cs, got, place):
    S, r, cols = cs.shape
    tr = _tile(r, max(SUBLANES_BF16, TILES["add_bytes"] // (cols * 2)), SUBLANES_BF16)

    def body(p_ref, a_ref, b_ref, o_ref):
        acc = a_ref[...].astype(F32)
        for k in range(3):
            acc = acc + b_ref[k].astype(F32)
        o_ref[...] = acc

    return pl.pallas_call(
        body, name="rs_add_chips",
        grid_spec=pltpu.PrefetchScalarGridSpec(
            num_scalar_prefetch=1, grid=(r // tr,),
            in_specs=[pl.BlockSpec((None, tr, cols), lambda i, p: (p[0], i, 0)),
                      pl.BlockSpec((3, tr, cols), lambda i, p: (0, i, 0))],
            out_specs=pl.BlockSpec((None, tr, cols), lambda i, p: (p[1], i, 0))),
        out_shape=jax.ShapeDtypeStruct((2, r, cols), F32),
        compiler_params=_params(("parallel",), 24 * _nbytes((tr, cols), F32)),
    )(place, cs, got)


def _reduce_begin(arrs, from_sibling, place):
    return [_add_sibling(g, s, place) for g, s in zip(arrs, from_sibling)]


def _reduce_end(sums, got, place):
    halves = [_add_chips(cs, g, place) for cs, g in zip(sums, got)]
    return [f.reshape(-1, f.shape[-1]) for f in _sibling_share(halves)]


def _all_reduce_small(v):
    R = v.shape[0]

    def body(v_ref, o_ref, buf, send_sems, recv_sems):
        x, y, c, _ = _place()
        me = 4 * x + 2 * y + c
        buf[me] = v_ref[...]
        sends = []
        for k in range(1, 8):
            px = 1 - x if k & 4 else x
            py = 1 - y if k & 2 else y
            pc = 1 - c if k & 1 else c
            sends.append(_remote(v_ref, buf.at[me], send_sems, recv_sems, k - 1, (px, py, pc)))
        for cp in sends:
            cp.start()
        for k in range(1, 8):
            px = 1 - x if k & 4 else x
            py = 1 - y if k & 2 else y
            pc = 1 - c if k & 1 else c
            _remote(v_ref, buf.at[4 * px + 2 * py + pc], send_sems, recv_sems, k - 1, (px, py, pc)).wait_recv()
        for cp in sends:
            cp.wait_send()
        acc = buf[0]
        for d in range(1, 8):
            acc = acc + buf[d]
        o_ref[...] = acc

    vm = pl.BlockSpec(memory_space=pltpu.VMEM)
    return pl.pallas_call(
        body, name="all_reduce_small", in_specs=[vm], out_specs=vm, out_shape=jax.ShapeDtypeStruct(v.shape, F32),
        scratch_shapes=[pltpu.VMEM((8, R, LANES), F32), pltpu.SemaphoreType.DMA((7,)), pltpu.SemaphoreType.DMA((7,))],
    )(v)


def _pad_to(a, n):
    return a if a.shape[0] == n else jnp.pad(a, (0, n - a.shape[0]))


def _piece_len(shape):
    return -(-math.prod(shape) // PACK_ALIGN) * PACK_ALIGN


def _pack(pieces, dtype):
    flat = jnp.concatenate([_pad_to(a.reshape(-1).astype(dtype), _piece_len(a.shape)) for a in pieces])
    return flat.reshape(-1, LANES)


def _unpack(flat, shapes, lead):
    flat = flat.reshape(lead + (-1,))
    out, off = [], 0
    for shp in shapes:
        out.append(flat[..., off:off + math.prod(shp)].reshape(lead + tuple(shp)))
        off += _piece_len(shp)
    return out


def _join(name, a):
    if name in COL_SHARDED:
        return a.transpose(1, 0, 2).reshape(a.shape[1], -1)
    if name in ROW_SHARDED:
        return a.reshape(-1, a.shape[-1])
    return a.transpose(1, 0, 2, 3).reshape(a.shape[1], -1, a.shape[-1])


def _split(name, a):
    if name in COL_SHARDED:
        return a.reshape(a.shape[0], N_SHARDS, -1).transpose(1, 0, 2)
    if name in ROW_SHARDED:
        return a.reshape(N_SHARDS, -1, a.shape[-1])
    return a.reshape(a.shape[0], N_SHARDS, -1, a.shape[-1]).transpose(1, 0, 2, 3)


def _heads_split(w, H, first, second, pad_second):
    K = w.shape[0]
    w3 = w.reshape(K, H, first + second)
    b = w3[:, :, first:]
    if pad_second > second:
        b = jnp.pad(b, ((0, 0), (0, 0), (0, pad_second - second)))
    return jnp.concatenate([w3[:, :, :first].reshape(K, -1), b.reshape(K, -1)], axis=1)


def _heads_merge(w, H, first, second, pad_second):
    K = w.shape[0]
    a = w[:, :H * first].reshape(K, H, first)
    b = w[:, H * first:].reshape(K, H, pad_second)[:, :, :second]
    return jnp.concatenate([a, b], axis=2).reshape(K, -1)


def kernel(x, p, positions, norm_mix_g, w_in, pool_w, pool_scale, q_norm_g, w_uq, kv_norm_g, w_ukv, w_out, norm_ffn_g, w_up, conv_w, conv_b, w_down, norm_ple_g, w_ple, w_ple_gate, final_norm_g, loss_target, m_norm_mix_g, m_w_in, m_pool_w, m_pool_scale, m_q_norm_g, m_w_uq, m_kv_norm_g, m_w_ukv, m_w_out, m_norm_ffn_g, m_w_up, m_conv_w, m_conv_b, m_w_down, m_norm_ple_g, m_w_ple, m_w_ple_gate, m_final_norm_g, v_norm_mix_g, v_w_in, v_pool_w, v_pool_scale, v_q_norm_g, v_w_uq, v_kv_norm_g, v_w_ukv, v_w_out, v_norm_ffn_g, v_w_up, v_conv_w, v_conv_b, v_w_down, v_norm_ple_g, v_w_ple, v_w_ple_gate, v_final_norm_g):
    W = dict(norm_mix_g=norm_mix_g, w_in=w_in, pool_w=pool_w, pool_scale=pool_scale, q_norm_g=q_norm_g, w_uq=w_uq,
             kv_norm_g=kv_norm_g, w_ukv=w_ukv, w_out=w_out, norm_ffn_g=norm_ffn_g, w_up=w_up, conv_w=conv_w, conv_b=conv_b,
             w_down=w_down, norm_ple_g=norm_ple_g, w_ple=w_ple, w_ple_gate=w_ple_gate, final_norm_g=final_norm_g)
    M1 = dict(norm_mix_g=m_norm_mix_g, w_in=m_w_in, pool_w=m_pool_w, pool_scale=m_pool_scale, q_norm_g=m_q_norm_g, w_uq=m_w_uq,
              kv_norm_g=m_kv_norm_g, w_ukv=m_w_ukv, w_out=m_w_out, norm_ffn_g=m_norm_ffn_g, w_up=m_w_up, conv_w=m_conv_w,
              conv_b=m_conv_b, w_down=m_w_down, norm_ple_g=m_norm_ple_g, w_ple=m_w_ple, w_ple_gate=m_w_ple_gate,
              final_norm_g=m_final_norm_g)
    M2 = dict(norm_mix_g=v_norm_mix_g, w_in=v_w_in, pool_w=v_pool_w, pool_scale=v_pool_scale, q_norm_g=v_q_norm_g, w_uq=v_w_uq,
              kv_norm_g=v_kv_norm_g, w_ukv=v_w_ukv, w_out=v_w_out, norm_ffn_g=v_norm_ffn_g, w_up=v_w_up, conv_w=v_conv_w,
              conv_b=v_conv_b, w_down=v_w_down, norm_ple_g=v_norm_ple_g, w_ple=v_w_ple, w_ple_gate=v_w_ple_gate,
              final_norm_g=v_final_norm_g)

    _, T, D = x.shape
    L = p.shape[0]
    P, QL, KL, F = pool_scale.shape[-1], q_norm_g.shape[-1], kv_norm_g.shape[-1], conv_b.shape[-1]
    C = pool_w.shape[-1]
    H = (D - P) // V_DIM
    d_in = P + QL + KL + ROPE_DIM
    dims = dict(P=P, QL=QL, KL=KL, C=C, H=H)
    misc_shapes = [W[n].shape[1:] for n in MISC]
    ns_in, ns_up, ns_conv = w_in.shape[-1], w_up.shape[-1], conv_w.shape[-1]

    xi, yi, ci = lax.axis_index("x"), lax.axis_index("y"), lax.axis_index("c")
    me = 2 * xi + yi
    place = jnp.stack([me, ci]).astype(jnp.int32)

    def all_reduce(parts):
        flat = jnp.concatenate(parts)
        padded = -(-flat.shape[0] // (8 * LANES)) * (8 * LANES)
        return _all_reduce_small(_pad_to(flat, padded).reshape(-1, LANES)).reshape(-1)

    inv_freq = 1.0 / (ROPE_THETA ** (jnp.arange(0, ROPE_DIM, 2, dtype=F32) / ROPE_DIM))
    inv_lane = jnp.concatenate([inv_freq, inv_freq, jnp.zeros((LANES - ROPE_DIM,), F32)]).reshape(1, LANES)
    tabs = _rope_tables(positions.reshape(T, 1).astype(F32), inv_lane)

    local = [W[n].astype(BF16) for n in BIG] + [jnp.stack([_pack([W[n][l] for n in MISC], BF16) for l in range(L)])]
    placed = lax.dynamic_update_slice(jnp.zeros((L, CONV_TAPS, F), F32), conv_w, (0, 0, me * ns_conv))
    conv_full = all_reduce([jnp.where(ci == 0, placed, 0.0).reshape(-1)])[:L * CONV_TAPS * F].reshape(L, CONV_TAPS, F)

    def layout(got, l):
        g = dict(zip(BIG, got[:-1]))
        misc = {n: _join(n, a) for n, a in zip(MISC, _unpack(got[-1], misc_shapes, (N_SHARDS,)))}
        return dict(
            w_in=jnp.concatenate([g["w_in"][sh] for sh in range(N_SHARDS)] + [jnp.zeros((D, LANES - ROPE_DIM), BF16)], axis=1),
            w_out=g["w_out"].reshape(-1, D), w_down=g["w_down"].reshape(-1, D), w_ple_gate=g["w_ple_gate"].reshape(-1, D),
            w_up=g["w_up"], w_ple=misc["w_ple"], pool_w=misc["pool_w"],
            w_uq=_heads_split(misc["w_uq"], H, NOPE_DIM, ROPE_DIM, LANES),
            w_ukv=_heads_split(misc["w_ukv"], H, NOPE_DIM, V_DIM, V_DIM),
            cw8=jnp.concatenate([conv_full[l], conv_b[l][None], jnp.zeros((4, F), F32)], axis=0))

    half_up = (0, N_SHARDS // 2), (N_SHARDS // 2, N_SHARDS // 2)

    h = x[0]
    saved, FW = [], []
    up_at = BIG.index("w_up")
    arriving = _run_side(_forward_side(_run_side(_gather_side(local, 0, _whole(local, True), {})[0], "all_gather")), "gather_forward")
    for l in range(L):
        fw = layout(arriving, l)
        FW.append(fw)
        s = dict(h0=h)
        nxt = {}
        parts = _plan_gather(local, up_at, len(local) - 1) if l + 1 < L else [None] * 4

        def gather(pieces):
            return _gather_side(local, l + 1, pieces, nxt)
        s["n1"] = _rms_fwd(h, norm_mix_g[l], "norm_mix")
        s["u"] = _matmul(s["n1"], fw["w_in"], "nn", F32, "mm_in", tm=512, tn=d_in + LANES - ROPE_DIM)
        s["diff"], s["cqn"], s["ckvn"], s["kr"] = _post_u(s["u"], q_norm_g[l], kv_norm_g[l], tabs, dims)
        s["q"] = _q_rope(_matmul(s["cqn"], fw["w_uq"], "nn", F32, "mm_uq", tn=2 * H * LANES), tabs, dims)
        s["kv"] = _matmul(s["ckvn"], fw["w_ukv"], "nn", BF16, "mm_ukv", tn=2 * H * LANES)
        s["o"], s["lse"] = _ride(lambda sd: _flash_fwd(s["q"], s["kv"], s["kr"], dims, sd), 2, parts[0], gather, nxt)
        s["mix"] = jnp.concatenate([_pool_fwd(s["diff"], fw["pool_w"], pool_scale[l], dims), s["o"]], axis=1)
        s["h1"] = _matmul(s["mix"], fw["w_out"], "nn", F32, "mm_out", res=h, tm=1024)
        s["n2"] = _rms_fwd(s["h1"], norm_ffn_g[l], "norm_ffn")
        s["gate"] = _ride(lambda sd: _matmul(s["n2"], fw["w_up"], "nn", BF16, "mm_gate", tm=1024, tn=ns_up // 2, b_shards=half_up[0],
                                             side=sd), 1, parts[1], gather, nxt)
        s["up"] = _ride(lambda sd: _matmul(s["n2"], fw["w_up"], "nn", BF16, "mm_up", tm=1024, tn=ns_up // 2, b_shards=half_up[1],
                                           side=sd), 1, parts[2], gather, nxt)
        s["a"] = _ffn_fwd(s["gate"], s["up"], fw["cw8"])
        s["h2"] = _ride(lambda sd: _matmul(s["a"], fw["w_down"], "nn", F32, "mm_down", res=s["h1"], tm=1024, tn=512, tk=F,
                                           side=sd), 1, parts[3], gather, nxt)
        s["n3"] = _rms_fwd(s["h2"], norm_ple_g[l], "norm_ple")
        if nxt:
            s["gl"], *arriving = _matmul(s["n3"], fw["w_ple_gate"], "nn", F32, "mm_ple_gate", tm=1024,
                                         side=_forward_side([nxt[a] for a in range(len(local))]))
        else:
            s["gl"] = _matmul(s["n3"], fw["w_ple_gate"], "nn", F32, "mm_ple_gate", tm=1024)
        s["pe"] = _matmul(p[l, 0], fw["w_ple"], "nn", F32, "mm_ple", tn=D)
        h = _ple_fwd(s["h2"], s["gl"], s["pe"])
        saved.append(s)

    dh, dhb, dg_final, loss_part = _final_loss(h, loss_target[0], final_norm_g)
    loss = lax.psum(loss_part[0, 0], ("x", "y", "c"))

    small = {}
    reduced = [None] * L
    raw = None
    for l in reversed(range(L)):
        fw, s = FW[l], saved[l]
        gw = {}
        got = {}
        dpe, dgl = _ple_bwd(dh, s["gl"], s["pe"])
        gw["w_ple"] = _matmul(p[l, 0], dpe, "tn", BF16, "dw_ple", tm=512)
        gw["w_ple_gate"] = _matmul(s["n3"], dgl, "tn", BF16, "dw_ple_gate", tm=512, tk=T)
        dn3 = _matmul(dgl, fw["w_ple_gate"], "nt", F32, "dx_ple_gate", tm=1024)
        dh, dhb, small["norm_ple_g", l] = _rms_bwd(dn3, s["h2"], norm_ple_g[l], dh, "norm_ple_bwd")

        tn_down = F // 4 if F % (4 * LANES) == 0 else F
        if raw:
            da, *from_sibling = _matmul(dhb, fw["w_down"], "nt", BF16, "dx_down", tn=tn_down, side=_sibling_side(raw))
            waiting = _reduce_begin(raw, from_sibling, place)
            parts = _plan(waiting, False, up_at)
        else:
            da = _matmul(dhb, fw["w_down"], "nt", BF16, "dx_down", tn=tn_down)
            waiting, parts = None, [None] * 4

        def chips(pieces):
            return _chip_side(waiting, pieces, got)

        gw["w_down"] = _matmul(s["a"], dhb, "tn", BF16, "dw_down")
        dgate, dup, dcw = _ffn_bwd(da, s["gate"], s["up"], fw["cw8"])
        small["conv_w", l], small["conv_b", l] = dcw[:CONV_TAPS], dcw[CONV_TAPS:CONV_TAPS + 1]
        gw["w_up"] = _ride(lambda sd: _matmul(s["n2"], dgate, "tn", BF16, "dw_gate", tm=512, tn=ns_up // 2, tk=T,
                                              out_shards=(half_up[0][0], N_SHARDS, ns_up), side=sd), 1, parts[1], chips, got)
        gw["w_up"] = _ride(lambda sd: _matmul(s["n2"], dup, "tn", BF16, "dw_up", tm=512, tn=ns_up // 2, tk=T,
                                              out_shards=(half_up[1][0], N_SHARDS, ns_up), carry=gw["w_up"], side=sd), 1, parts[2], chips, got)
        dn2 = _ride(lambda sd: _matmul(dgate, fw["w_up"], "nt", F32, "dx_gate", tm=1024, tn=D // 2, tk=ns_up, b_shards=half_up[0],
                                       side=sd), 1, parts[3], chips, got)
        dn2 = _matmul(dup, fw["w_up"], "nt", F32, "dx_up", res=dn2, tm=1024, tn=D // 2, tk=ns_up, b_shards=half_up[1])
        dh, dhb, small["norm_ffn_g", l] = _rms_bwd(dn2, s["h1"], norm_ffn_g[l], dh, "norm_ffn_bwd")

        dmix = _matmul(dhb, fw["w_out"], "nt", BF16, "dx_out")
        gw["w_out"] = _matmul(s["mix"], dhb, "tn", BF16, "dw_out", tm=512, tk=T)
        ddiff, gw["pool_w"], small["pool_scale", l] = _pool_bwd(dmix, s["diff"], fw["pool_w"], pool_scale[l], dims)
        dq, dkn, dv, dkr = _ride(lambda sd: _flash_bwd(s["q"], s["kv"], s["kr"], s["o"], s["lse"], dmix, dims, sd), 4, parts[0], chips, got)
        if got:
            reduced[l + 1] = _reduce_end(waiting, [got[a] for a in range(len(waiting))], place)
        dqb = _dq_post(dq, tabs, dims)
        dkv = jnp.concatenate([dkn, dv], axis=1)
        gw["w_uq"] = _heads_merge(_matmul(s["cqn"], dqb, "tn", BF16, "dw_uq", tn=2 * H * LANES), H, NOPE_DIM, ROPE_DIM, LANES)
        gw["w_ukv"] = _heads_merge(_matmul(s["ckvn"], dkv, "tn", BF16, "dw_ukv", tn=2 * H * LANES), H, NOPE_DIM, V_DIM, V_DIM)
        dcqn = _matmul(dqb, fw["w_uq"], "nt", F32, "dx_uq")
        dckvn = _matmul(dkv, fw["w_ukv"], "nt", F32, "dx_ukv")
        du, small["q_norm_g", l], small["kv_norm_g", l] = _pre_u_bwd(s["u"], dcqn, dckvn, ddiff, dkr, q_norm_g[l], kv_norm_g[l], tabs, dims)
        gw["w_in"] = _matmul(s["n1"], du, "tn", BF16, "dw_in", tm=512, tn=du.shape[1])[:, :d_in]
        dn1 = _matmul(du, fw["w_in"], "nt", F32, "dx_in", tm=512, tk=du.shape[1])
        dh, dhb, small["norm_mix_g", l] = _rms_bwd(dn1, s["h0"], norm_mix_g[l], dh, "norm_mix_bwd")

        split = {n: _split(n, gw[n]) for n in MISC}
        arrs = [jnp.stack([gw["w_in"][:, sh * ns_in:(sh + 1) * ns_in] for sh in range(N_SHARDS)]),
                gw["w_out"].reshape(N_SHARDS, -1, D), gw["w_up"], gw["w_down"].reshape(N_SHARDS, -1, D),
                gw["w_ple_gate"].reshape(N_SHARDS, -1, D),
                jnp.stack([_pack([split[n][sh] for n in MISC], BF16) for sh in range(N_SHARDS)])]
        raw = arrs
    waiting = _reduce_begin(raw, _run_side(_sibling_side(raw), "rs_sibling"), place)
    reduced[0] = _reduce_end(waiting, _run_side(_chip_side(waiting, _whole(waiting, False), {})[0], "rs_chips"), place)

    grads = {n: jnp.stack([reduced[l][k].reshape(W[n].shape[1:]) for l in range(L)]) for k, n in enumerate(BIG)}
    per_layer = [_unpack(reduced[l][-1], misc_shapes, ()) for l in range(L)]
    for k, n in enumerate(MISC):
        grads[n] = jnp.stack([per_layer[l][k] for l in range(L)])

    small_names = ("norm_mix_g", "pool_scale", "q_norm_g", "kv_norm_g", "norm_ffn_g", "conv_b", "norm_ple_g", "conv_w")
    summed = all_reduce([small[n, l].reshape(-1) for n in small_names for l in range(L)] + [dg_final.reshape(-1)])
    off = 0
    for n in small_names:
        size = CONV_TAPS * F if n == "conv_w" else W[n].shape[-1]
        grads[n] = summed[off:off + L * size].reshape((L, CONV_TAPS, F) if n == "conv_w" else (L, size))
        off += L * size
    grads["final_norm_g"] = summed[off:off + D]
    grads["conv_w"] = lax.dynamic_slice(grads["conv_w"], (0, 0, me * ns_conv), (L, CONV_TAPS, ns_conv))

    deltas, new_m, new_v = {}, {}, {}
    for n in WEIGHTS:
        deltas[n], new_m[n], new_v[n] = _adamw(W[n], grads[n], M1[n], M2[n], "adamw_" + n)

    return (loss, dh[None], *[grads[n] for n in WEIGHTS], *[deltas[n] for n in WEIGHTS],
            *[new_m[n] for n in WEIGHTS], *[new_v[n] for n in WEIGHTS])
```

```python
import functools
import math

import jax
import jax.numpy as jnp
from jax import lax
from jax.experimental import pallas as pl
from jax.experimental.pallas import tpu as pltpu

F32 = jnp.float32
BF16 = jnp.bfloat16

NOPE_DIM = 128
ROPE_DIM = 64
V_DIM = 128
LANES = 128
SUBLANES_BF16 = 16
ROPE_THETA = 10000.0
EPS = 1e-6
POOL_WINDOWS = (2, 4, 8, 16)
POOL_HALO = 16
CONV_TAPS = 3
CONV_HALO = 8
ADAM_LR = 0.001
ADAM_B1 = 0.9
ADAM_B2 = 0.999
ADAM_EPS = 1e-08
ADAM_WD = 0.01
ADAM_STEP = 10
NEG_BIG = -1e30
ATT_SCALE = 1.0 / math.sqrt(NOPE_DIM + ROPE_DIM)
V7X_VMEM_BYTES = 64 * 2 ** 20
N_SHARDS = 4
PACK_ALIGN = 2 * SUBLANES_BF16 * LANES

TILES = dict(row=256, att=512, mm_m=1024, mm_n=1024, mm_k=2048, ffn_row=512, ffn_c=512, add_bytes=1 << 20)

BIG = ("w_in", "w_out", "w_up", "w_down", "w_ple_gate")
MISC = ("w_uq", "w_ukv", "w_ple", "pool_w")
COL_SHARDED = ("w_in", "w_uq", "w_ukv", "w_up", "conv_w", "w_ple")
ROW_SHARDED = ("w_out", "w_down", "w_ple_gate")
WEIGHTS = ("norm_mix_g", "w_in", "pool_w", "pool_scale", "q_norm_g", "w_uq", "kv_norm_g", "w_ukv", "w_out",
           "norm_ffn_g", "w_up", "conv_w", "conv_b", "w_down", "norm_ple_g", "w_ple", "w_ple_gate", "final_norm_g")
MESH = pl.DeviceIdType.MESH


def _nbytes(shape, dtype):
    return math.prod(shape) * jnp.dtype(dtype).itemsize


def _params(sem, need_bytes):
    limit = min(V7X_VMEM_BYTES - (8 << 20), max(32 << 20, int(need_bytes)))
    return pltpu.CompilerParams(dimension_semantics=sem, vmem_limit_bytes=limit)


def _tile(n, want, mult=8):
    if n <= want:
        return n
    for t in range(want - want % mult, 0, -mult):
        if n % t == 0:
            return t
    return n


def _sigmoid(x):
    return 1.0 / (1.0 + jnp.exp(-x))


def _rstd(x):
    return lax.rsqrt(jnp.mean(x * x, axis=-1, keepdims=True) + EPS)


_DOT_DIMS = {"nn": (((1,), (0,)), ((), ())), "nt": (((1,), (1,)), ((), ())), "tn": (((0,), (0,)), ((), ()))}


def _matmul(a, b, mode, out_dtype, name, res=None, tm=None, tn=None, tk=None, b_shards=None, out_shards=None, carry=None,
            side=None):
    if mode == "nn":
        (M, K), N = a.shape, b.shape[-1] * (b_shards[1] if b_shards else 1)
    elif mode == "nt":
        (M, K), N = a.shape, b.shape[-2]
    else:
        (K, M), N = a.shape, b.shape[1]
    per = b.shape[-1] if b_shards else (out_shards[2] if out_shards else None)
    tm = _tile(M, tm or TILES["mm_m"], LANES)
    tn = _tile(per if (per and mode != "nt") else N, tn or TILES["mm_n"], LANES)
    tk = _tile(per if (per and mode == "nt") else K, tk or TILES["mm_k"], LANES)
    nk = K // tk
    has_res = res is not None
    has_carry = carry is not None
    dims = _DOT_DIMS[mode]

    def body(*refs):
        a_ref, b_ref = refs[0], refs[1]
        o_ref = refs[2 + has_res + has_carry]
        part = lax.dot_general(a_ref[...].astype(BF16), b_ref[...].astype(BF16), dims, preferred_element_type=F32)

        def finish(acc):
            if has_res:
                acc = acc + refs[2][...]
            o_ref[...] = acc.astype(o_ref.dtype)

        if nk == 1:
            finish(part)
        else:
            acc_ref = refs[3 + has_res + has_carry]
            k = pl.program_id(2)

            @pl.when(k == 0)
            def _():
                acc_ref[...] = part

            @pl.when(k > 0)
            def _():
                acc_ref[...] += part

            @pl.when(k == nk - 1)
            def _():
                finish(acc_ref[...])

    if mode == "nn":
        a_spec, b_spec = pl.BlockSpec((tm, tk), lambda i, j, k: (i, k)), pl.BlockSpec((tk, tn), lambda i, j, k: (k, j))
    elif mode == "nt":
        a_spec, b_spec = pl.BlockSpec((tm, tk), lambda i, j, k: (i, k)), pl.BlockSpec((tn, tk), lambda i, j, k: (j, k))
    else:
        a_spec, b_spec = pl.BlockSpec((tk, tm), lambda i, j, k: (k, i)), pl.BlockSpec((tk, tn), lambda i, j, k: (k, j))
    o_spec = pl.BlockSpec((tm, tn), lambda i, j, k: (i, j))
    out_shape = jax.ShapeDtypeStruct((M, N), out_dtype)
    if b_shards:
        first = b_shards[0]
        if mode == "nn":
            nps = per // tn
            b_spec = pl.BlockSpec((None, tk, tn), lambda i, j, k: (first + j // nps, k, j % nps))
        else:
            kps = per // tk
            b_spec = pl.BlockSpec((None, tn, tk), lambda i, j, k: (first + k // kps, j, k % kps))
    if out_shards:
        ofirst, nps_o = out_shards[0], per // tn
        o_spec_out = pl.BlockSpec((None, tm, tn), lambda i, j, k: (ofirst + j // nps_o, i, j % nps_o))
        out_shape = jax.ShapeDtypeStruct((out_shards[1], M, per), out_dtype)
    else:
        o_spec_out = o_spec
    in_specs, args = [a_spec, b_spec], [a, b]
    need = 2 * (_nbytes((tm, tk), a.dtype) + _nbytes((tk, tn), b.dtype) + _nbytes((tm, tn), out_dtype)) + 2 * _nbytes((tm, tn), F32)
    if has_res:
        in_specs.append(o_spec)
        args.append(res)
        need += 2 * _nbytes((tm, tn), res.dtype)
    aliases = {}
    if has_carry:
        aliases = {len(args): 0}
        in_specs.append(pl.BlockSpec(memory_space=pl.ANY))
        args.append(carry)
    scratch = [pltpu.VMEM((tm, tn), F32)] if nk > 1 else []
    grid = (M // tm, N // tn, nk)
    params = _params(("parallel", "parallel", "arbitrary") if side is None else ("arbitrary",) * 3, need + (4 << 20))
    if side is None:
        return pl.pallas_call(body, name=name, grid=grid, in_specs=in_specs, out_specs=o_spec_out, out_shape=out_shape,
                              scratch_shapes=scratch, input_output_aliases=aliases, compiler_params=params)(*args)
    body, s_ins, s_in_specs, s_shapes, s_out_specs, s_sems, s_aliases = _attach(side, body, len(args), 1, grid)
    return pl.pallas_call(
        body, name=name, grid=grid, in_specs=in_specs + s_in_specs, out_specs=[o_spec_out] + s_out_specs,
        out_shape=[out_shape] + s_shapes, scratch_shapes=scratch + s_sems, input_output_aliases={**aliases, **s_aliases},
        compiler_params=params,
    )(*args, *s_ins)


def _rms_fwd(x, g, name):
    T, D = x.shape
    tt = _tile(T, TILES["row"])

    def body(x_ref, g_ref, o_ref):
        xv = x_ref[...]
        o_ref[...] = (xv * _rstd(xv) * g_ref[...]).astype(o_ref.dtype)

    row = pl.BlockSpec((tt, D), lambda i: (i, 0))
    return pl.pallas_call(
        body, name=name, grid=(T // tt,), in_specs=[row, pl.BlockSpec((1, D), lambda i: (0, 0))], out_specs=row,
        out_shape=jax.ShapeDtypeStruct((T, D), BF16), compiler_params=_params(("parallel",), 8 * _nbytes((tt, D), F32)),
    )(x, g.reshape(1, D))


def _rms_bwd(dn, x, g, dres, name):
    T, D = x.shape
    tt = _tile(T, TILES["row"])

    def body(dn_ref, x_ref, g_ref, dres_ref, dx_ref, dxb_ref, dg_ref):
        i = pl.program_id(0)
        xv = x_ref[...]
        r = _rstd(xv)
        xh = xv * r
        dnv = dn_ref[...].astype(F32)
        dxh = dnv * g_ref[...]
        tot = dres_ref[...] + r * (dxh - xh * jnp.mean(dxh * xh, axis=-1, keepdims=True))
        dx_ref[...] = tot
        dxb_ref[...] = tot.astype(BF16)
        part = jnp.sum(dnv * xh, axis=0, keepdims=True)

        @pl.when(i == 0)
        def _():
            dg_ref[...] = part

        @pl.when(i > 0)
        def _():
            dg_ref[...] += part

    row = pl.BlockSpec((tt, D), lambda i: (i, 0))
    vec = pl.BlockSpec((1, D), lambda i: (0, 0))
    return pl.pallas_call(
        body, name=name, grid=(T // tt,), in_specs=[row, row, vec, row], out_specs=[row, row, vec],
        out_shape=[jax.ShapeDtypeStruct((T, D), F32), jax.ShapeDtypeStruct((T, D), BF16), jax.ShapeDtypeStruct((1, D), F32)],
        compiler_params=_params(("arbitrary",), 16 * _nbytes((tt, D), F32)),
    )(dn, x, g.reshape(1, D), dres)


def _final_loss(h, target, g):
    T, D = h.shape
    tt = _tile(T, TILES["row"])

    def body(h_ref, t_ref, g_ref, dx_ref, dxb_ref, dg_ref, loss_ref):
        i = pl.program_id(0)
        xv = h_ref[...]
        r = _rstd(xv)
        xh = xv * r
        gv = g_ref[...]
        err = xh * gv - t_ref[...]
        lpart = 0.5 * jnp.sum(jnp.mean(err * err, axis=-1, keepdims=True), axis=0, keepdims=True)
        dy = err * (1.0 / D)
        dxh = dy * gv
        dx = r * (dxh - xh * jnp.mean(dxh * xh, axis=-1, keepdims=True))
        dx_ref[...] = dx
        dxb_ref[...] = dx.astype(BF16)
        gpart = jnp.sum(dy * xh, axis=0, keepdims=True)
        lrow = jnp.broadcast_to(lpart, (1, LANES))

        @pl.when(i == 0)
        def _():
            dg_ref[...] = gpart
            loss_ref[...] = lrow

        @pl.when(i > 0)
        def _():
            dg_ref[...] += gpart
            loss_ref[...] += lrow

    row = pl.BlockSpec((tt, D), lambda i: (i, 0))
    vec = pl.BlockSpec((1, D), lambda i: (0, 0))
    return pl.pallas_call(
        body, name="final_loss", grid=(T // tt,), in_specs=[row, row, vec],
        out_specs=[row, row, vec, pl.BlockSpec((1, LANES), lambda i: (0, 0))],
        out_shape=[jax.ShapeDtypeStruct((T, D), F32), jax.ShapeDtypeStruct((T, D), BF16),
                   jax.ShapeDtypeStruct((1, D), F32), jax.ShapeDtypeStruct((1, LANES), F32)],
        compiler_params=_params(("arbitrary",), 16 * _nbytes((tt, D), F32)),
    )(h, target, g.reshape(1, D))


def _rope_tables(pos_col, inv_lane):
    T = pos_col.shape[0]
    tt = _tile(T, TILES["row"])

    def body(p_ref, f_ref, c_ref, s1_ref, s2_ref):
        ang = p_ref[...] * f_ref[...]
        lane = lax.broadcasted_iota(jnp.int32, ang.shape, 1)
        half = ROPE_DIM // 2
        cs, sn = jnp.cos(ang), jnp.sin(ang)
        c_ref[...] = jnp.where(lane < ROPE_DIM, cs, 0.0)
        s1_ref[...] = jnp.where(lane < half, -sn, 0.0)
        s2_ref[...] = jnp.where((lane >= half) & (lane < ROPE_DIM), sn, 0.0)

    tab = pl.BlockSpec((tt, LANES), lambda i: (i, 0))
    shp = jax.ShapeDtypeStruct((T, LANES), F32)
    return pl.pallas_call(
        body, name="rope_tables", grid=(T // tt,),
        in_specs=[pl.BlockSpec((tt, 1), lambda i: (i, 0)), pl.BlockSpec((1, LANES), lambda i: (0, 0))],
        out_specs=[tab, tab, tab], out_shape=[shp, shp, shp],
        compiler_params=_params(("parallel",), 32 * _nbytes((tt, LANES), F32)),
    )(pos_col, inv_lane)


def _rope(x, c, s1, s2):
    return x * c + pltpu.roll(x, LANES - ROPE_DIM // 2, 1) * s1 + pltpu.roll(x, ROPE_DIM // 2, 1) * s2


def _rope_t(d, c, s1, s2):
    return d * c + pltpu.roll(d * s1, ROPE_DIM // 2, 1) + pltpu.roll(d * s2, LANES - ROPE_DIM // 2, 1)


def _window_sum(xe, w, forward):
    n = xe.shape[0]
    s, sh = xe, 1
    while sh < w:
        s = s + pltpu.roll(s, (n - sh) if forward else sh, 0)
        sh *= 2
    return s


def _post_u(u, gq, gkv, tabs, dims):
    T, Dp = u.shape
    P, QL, KL, C = dims["P"], dims["QL"], dims["KL"], dims["C"]
    tt = _tile(T, TILES["row"], POOL_HALO)
    hb = tt // POOL_HALO

    def body(u_ref, halo_ref, gq_ref, gkv_ref, c_ref, s1_ref, s2_ref, diff_ref, cq_ref, ckv_ref, kr_ref):
        i = pl.program_id(0)
        t = i * tt + lax.broadcasted_iota(jnp.int32, (tt, 1), 0)
        halo = jnp.where(i > 0, halo_ref[...], 0.0)
        for gi, w in enumerate(POOL_WINDOWS):
            cols = slice(gi * C, (gi + 1) * C)
            xg = u_ref[:, cols]
            s = _window_sum(jnp.concatenate([halo[:, cols], xg], axis=0), w, False)[POOL_HALO:]
            cnt = jnp.minimum(t + 1, w).astype(F32)
            diff_ref[:, cols] = (s / cnt - xg).astype(BF16)
        cq = u_ref[:, P:P + QL]
        cq_ref[...] = (cq * _rstd(cq) * gq_ref[...]).astype(BF16)
        ckv = u_ref[:, P + QL:P + QL + KL]
        ckv_ref[...] = (ckv * _rstd(ckv) * gkv_ref[...]).astype(BF16)
        kr_ref[...] = _rope(u_ref[:, P + QL + KL:], c_ref[...], s1_ref[...], s2_ref[...]).astype(BF16)

    def row(w):
        return pl.BlockSpec((tt, w), lambda i: (i, 0))

    def vec(w):
        return pl.BlockSpec((1, w), lambda i: (0, 0))

    return pl.pallas_call(
        body, name="post_u", grid=(T // tt,),
        in_specs=[row(Dp), pl.BlockSpec((POOL_HALO, P), lambda i: (jnp.maximum(i * hb - 1, 0), 0)),
                  vec(QL), vec(KL), row(LANES), row(LANES), row(LANES)],
        out_specs=[row(P), row(QL), row(KL), row(LANES)],
        out_shape=[jax.ShapeDtypeStruct((T, P), BF16), jax.ShapeDtypeStruct((T, QL), BF16),
                   jax.ShapeDtypeStruct((T, KL), BF16), jax.ShapeDtypeStruct((T, LANES), BF16)],
        compiler_params=_params(("parallel",), 10 * _nbytes((tt, Dp), F32)),
    )(u, u, gq.reshape(1, QL), gkv.reshape(1, KL), *tabs)


def _pre_u_bwd(u, d_cqn, d_ckvn, d_diff, dkr, gq, gkv, tabs, dims):
    T, Dp = u.shape
    P, QL, KL, C, H = dims["P"], dims["QL"], dims["KL"], dims["C"], dims["H"]
    tt = _tile(T, TILES["row"], POOL_HALO)
    hb = tt // POOL_HALO
    n_t = T // tt

    def norm_bwd(xv, dn, gv):
        r = _rstd(xv)
        xh = xv * r
        dxh = dn * gv
        return r * (dxh - xh * jnp.mean(dxh * xh, axis=-1, keepdims=True)), jnp.sum(dn * xh, axis=0, keepdims=True)

    def body(u_ref, dcq_ref, dckv_ref, dd_ref, ddn_ref, dkr_ref, gq_ref, gkv_ref, c_ref, s1_ref, s2_ref,
             du_ref, dgq_ref, dgkv_ref):
        i = pl.program_id(0)
        t = i * tt + lax.broadcasted_iota(jnp.int32, (tt, 1), 0)
        nxt = jnp.where(i < n_t - 1, ddn_ref[...].astype(F32), 0.0)
        for gi, w in enumerate(POOL_WINDOWS):
            cols = slice(gi * C, (gi + 1) * C)
            dd = dd_ref[:, cols].astype(F32)
            e = dd / jnp.minimum(t + 1, w).astype(F32)
            s = _window_sum(jnp.concatenate([e, nxt[:, cols] / float(w)], axis=0), w, True)[:tt]
            du_ref[:, cols] = (s - dd).astype(BF16)
        dq, pq = norm_bwd(u_ref[:, P:P + QL], dcq_ref[...], gq_ref[...])
        du_ref[:, P:P + QL] = dq.astype(BF16)
        dkv, pkv = norm_bwd(u_ref[:, P + QL:P + QL + KL], dckv_ref[...], gkv_ref[...])
        du_ref[:, P + QL:P + QL + KL] = dkv.astype(BF16)
        dk = dkr_ref[0]
        for hh in range(1, H):
            dk = dk + dkr_ref[hh]
        du_ref[:, P + QL + KL:] = _rope_t(dk, c_ref[...], s1_ref[...], s2_ref[...]).astype(BF16)

        @pl.when(i == 0)
        def _():
            dgq_ref[...] = pq
            dgkv_ref[...] = pkv

        @pl.when(i > 0)
        def _():
            dgq_ref[...] += pq
            dgkv_ref[...] += pkv

    def row(w):
        return pl.BlockSpec((tt, w), lambda i: (i, 0))

    def vec(w):
        return pl.BlockSpec((1, w), lambda i: (0, 0))

    return pl.pallas_call(
        body, name="pre_u_bwd", grid=(n_t,),
        in_specs=[row(Dp), row(QL), row(KL), row(P),
                  pl.BlockSpec((POOL_HALO, P), lambda i: (jnp.minimum((i + 1) * hb, T // POOL_HALO - 1), 0)),
                  pl.BlockSpec((H, tt, LANES), lambda i: (0, i, 0)), vec(QL), vec(KL), row(LANES), row(LANES), row(LANES)],
        out_specs=[row(Dp), vec(QL), vec(KL)],
        out_shape=[jax.ShapeDtypeStruct((T, Dp), BF16), jax.ShapeDtypeStruct((1, QL), F32), jax.ShapeDtypeStruct((1, KL), F32)],
        compiler_params=_params(("arbitrary",), 12 * _nbytes((tt, Dp), F32)),
    )(u, d_cqn, d_ckvn, d_diff, d_diff, dkr, gq.reshape(1, QL), gkv.reshape(1, KL), *tabs)


def _pool_fwd(diff, pw, ps, dims):
    T, P = diff.shape
    G, C = len(POOL_WINDOWS), dims["C"]
    tt = _tile(T, TILES["row"])

    def body(d_ref, w_ref, s_ref, o_ref):
        for gi in range(G):
            cols = slice(gi * C, (gi + 1) * C)
            y = jnp.dot(d_ref[:, cols], w_ref[gi], preferred_element_type=F32)
            o_ref[:, cols] = (y * s_ref[:, cols]).astype(BF16)

    row = pl.BlockSpec((tt, P), lambda i: (i, 0))
    return pl.pallas_call(
        body, name="pool_fwd", grid=(T // tt,),
        in_specs=[row, pl.BlockSpec((G, C, C), lambda i: (0, 0, 0)), pl.BlockSpec((1, P), lambda i: (0, 0))],
        out_specs=row, out_shape=jax.ShapeDtypeStruct((T, P), BF16),
        compiler_params=_params(("parallel",), 8 * _nbytes((tt, P), F32)),
    )(diff, pw, ps.reshape(1, P))


def _pool_bwd(dmix, diff, pw, ps, dims):
    T, P = diff.shape
    G, C = len(POOL_WINDOWS), dims["C"]
    tt = _tile(T, TILES["row"])

    def body(dy_ref, d_ref, w_ref, s_ref, dd_ref, dw_ref, ds_ref):
        i = pl.program_id(0)

        @pl.when(i == 0)
        def _():
            dw_ref[...] = jnp.zeros_like(dw_ref)
            ds_ref[...] = jnp.zeros_like(ds_ref)

        for gi in range(G):
            cols = slice(gi * C, (gi + 1) * C)
            dy = dy_ref[:, cols].astype(F32)
            d = d_ref[:, cols]
            w = w_ref[gi]
            ypre = jnp.dot(d, w, preferred_element_type=F32)
            ds_ref[:, cols] += jnp.sum(dy * ypre, axis=0, keepdims=True)
            dyp = (dy * s_ref[:, cols]).astype(BF16)
            dd_ref[:, cols] = lax.dot_general(dyp, w, _DOT_DIMS["nt"], preferred_element_type=F32).astype(BF16)
            dw_ref[gi] += lax.dot_general(d, dyp, _DOT_DIMS["tn"], preferred_element_type=F32)

    row = pl.BlockSpec((tt, P), lambda i: (i, 0))
    wsp = pl.BlockSpec((G, C, C), lambda i: (0, 0, 0))
    vec = pl.BlockSpec((1, P), lambda i: (0, 0))
    return pl.pallas_call(
        body, name="pool_bwd", grid=(T // tt,), in_specs=[row, row, wsp, vec], out_specs=[row, wsp, vec],
        out_shape=[jax.ShapeDtypeStruct((T, P), BF16), jax.ShapeDtypeStruct((G, C, C), F32), jax.ShapeDtypeStruct((1, P), F32)],
        compiler_params=_params(("arbitrary",), 10 * _nbytes((tt, P), F32)),
    )(dmix, diff, pw, ps.reshape(1, P))


def _q_rope(qp, tabs, dims):
    T, W = qp.shape
    H = dims["H"]
    tt = _tile(T, TILES["row"])

    def body(q_ref, c_ref, s1_ref, s2_ref, o_ref):
        o_ref[:, :H * LANES] = (q_ref[:, :H * LANES] * ATT_SCALE).astype(BF16)
        c, s1, s2 = c_ref[...], s1_ref[...], s2_ref[...]
        for hh in range(H, 2 * H):
            cols = slice(hh * LANES, (hh + 1) * LANES)
            o_ref[:, cols] = _rope(q_ref[:, cols] * ATT_SCALE, c, s1, s2).astype(BF16)

    row = pl.BlockSpec((tt, W), lambda i: (i, 0))
    tab = pl.BlockSpec((tt, LANES), lambda i: (i, 0))
    return pl.pallas_call(
        body, name="q_rope", grid=(T // tt,), in_specs=[row, tab, tab, tab], out_specs=row,
        out_shape=jax.ShapeDtypeStruct((T, W), BF16), compiler_params=_params(("parallel",), 8 * _nbytes((tt, W), F32)),
    )(qp, *tabs)


def _scores(qn_ref, qr_ref, kn_ref, kr_ref, t, diagonal):
    q = jnp.concatenate([qn_ref[...], qr_ref[...]], axis=1)
    k = jnp.concatenate([kn_ref[...], kr_ref[...]], axis=1)
    s = lax.dot_general(q, k, _DOT_DIMS["nt"], preferred_element_type=F32)
    if diagonal:
        s = jnp.where(lax.broadcasted_iota(jnp.int32, (t, t), 0) >= lax.broadcasted_iota(jnp.int32, (t, t), 1), s, NEG_BIG)
    return q, k, s


class _Side:
    def __init__(self, ins, out_shapes, n_sems, start, finish, aliases=None):
        self.ins, self.out_shapes, self.n_sems, self.start, self.finish = list(ins), list(out_shapes), n_sems, start, finish
        self.aliases = dict(aliases or {})


def _attach(side, body, n_in, n_out, grid):
    if side is None:
        return body, [], [], [], [], [], {}
    n_si, n_so = len(side.ins), len(side.out_shapes)

    def carrying(*refs):
        outs_at = n_in + n_si
        main = refs[:n_in] + refs[outs_at:outs_at + n_out] + refs[outs_at + n_out + n_so:len(refs) - 2]
        parts = (refs[n_in:outs_at], refs[outs_at + n_out:outs_at + n_out + n_so], refs[-2], refs[-1])
        ids = [pl.program_id(d) for d in range(len(grid))]
        first = functools.reduce(lambda u, v: u & v, [i == 0 for i in ids])
        last = functools.reduce(lambda u, v: u & v, [i == g - 1 for i, g in zip(ids, grid)])

        @pl.when(first)
        def _():
            side.start(*parts)

        body(*main)

        @pl.when(last)
        def _():
            side.finish(*parts)

    aliases = {n_in + i: n_out + o for i, o in side.aliases.items()}
    return carrying, side.ins, [_HBM] * n_si, side.out_shapes, [_HBM] * n_so, _sem_pair(side.n_sems), aliases


def _pairs(n, by_query):
    pairs = [(i, j) for i in range(n) for j in range(i + 1)] if by_query else [(i, j) for j in range(n) for i in range(j, n)]
    return jnp.array([p[0] for p in pairs], jnp.int32), jnp.array([p[1] for p in pairs], jnp.int32), len(pairs)


def _flash_fwd(q_att, kv, kr, dims, side=None):
    T = q_att.shape[0]
    H = dims["H"]
    G = 2 if H % 2 == 0 else 1
    t = _tile(T, TILES["att"])
    n = T // t
    it, jt, n_pairs = _pairs(n, True)
    hb = H // G

    def body(it_ref, jt_ref, qn_ref, qr_ref, kn_ref, v_ref, kr_ref, o_ref, lse_ref, m_ref, l_ref, acc_ref):
        step_id = pl.program_id(1)
        i, j = it_ref[step_id], jt_ref[step_id]

        @pl.when(j == 0)
        def _():
            m_ref[...] = jnp.full_like(m_ref, NEG_BIG)
            l_ref[...] = jnp.zeros_like(l_ref)
            acc_ref[...] = jnp.zeros_like(acc_ref)

        def step(diagonal):
            for g in range(G):
                cols = slice(g * LANES, (g + 1) * LANES)
                _, _, s = _scores(qn_ref.at[:, cols], qr_ref.at[:, cols], kn_ref.at[:, cols], kr_ref, t, diagonal)
                m_prev = m_ref[:, cols]
                m_new = jnp.maximum(m_prev, jnp.max(s, axis=1, keepdims=True))
                alpha = jnp.exp(m_prev - m_new)
                p = jnp.exp(s - m_new[:, :1])
                l_ref[:, cols] = alpha * l_ref[:, cols] + jnp.sum(p, axis=1, keepdims=True)
                acc_ref[:, cols] = alpha * acc_ref[:, cols] + jnp.dot(p.astype(BF16), v_ref[:, cols], preferred_element_type=F32)
                m_ref[:, cols] = m_new

        @pl.when(j < i)
        def _():
            step(False)

        @pl.when(j == i)
        def _():
            step(True)
            o_ref[...] = (acc_ref[...] / l_ref[...]).astype(BF16)
            lse_ref[...] = m_ref[...] + jnp.log(l_ref[...])

    blk = (t, G * LANES)
    grid = (hb, n_pairs)
    body, s_ins, s_in_specs, s_shapes, s_out_specs, s_sems, aliases = _attach(side, body, 7, 2, grid)
    return pl.pallas_call(
        body, name="flash_fwd",
        grid_spec=pltpu.PrefetchScalarGridSpec(
            num_scalar_prefetch=2, grid=grid,
            in_specs=[pl.BlockSpec(blk, lambda h, s, it, jt: (it[s], h)), pl.BlockSpec(blk, lambda h, s, it, jt: (it[s], hb + h)),
                      pl.BlockSpec(blk, lambda h, s, it, jt: (jt[s], h)), pl.BlockSpec(blk, lambda h, s, it, jt: (jt[s], hb + h)),
                      pl.BlockSpec((t, LANES), lambda h, s, it, jt: (jt[s], 0))] + s_in_specs,
            out_specs=[pl.BlockSpec(blk, lambda h, s, it, jt: (it[s], h)), pl.BlockSpec(blk, lambda h, s, it, jt: (it[s], h))] + s_out_specs,
            scratch_shapes=[pltpu.VMEM(blk, F32), pltpu.VMEM(blk, F32), pltpu.VMEM(blk, F32), *s_sems]),
        out_shape=[jax.ShapeDtypeStruct((T, H * LANES), BF16), jax.ShapeDtypeStruct((T, H * LANES), F32)] + s_shapes,
        input_output_aliases=aliases,
        compiler_params=_params(("arbitrary", "arbitrary"), 8 * G * _nbytes((t, t), F32) + (8 << 20)),
    )(it, jt, q_att, q_att, kv, kv, kr, *s_ins)


def _flash_bwd(q_att, kv, kr, o, lse, dmix, dims, side=None):
    T = q_att.shape[0]
    H = dims["H"]
    G = 2 if H % 2 == 0 else 1
    ob = dims["P"] // LANES // G
    t = _tile(T, TILES["att"])
    n = T // t
    it, jt, n_pairs = _pairs(n, False)
    hb = H // G

    def body(it_ref, jt_ref, qn_ref, qr_ref, kn_ref, v_ref, kr_ref, o_ref, lse_ref, do_ref,
             dq_ref, dkn_ref, dv_ref, dkr_ref, dk_acc, dv_acc):
        step_id = pl.program_id(1)
        i, j = it_ref[step_id], jt_ref[step_id]

        @pl.when(step_id == 0)
        def _():
            dq_ref[...] = jnp.zeros_like(dq_ref)

        @pl.when(i == j)
        def _():
            dk_acc[...] = jnp.zeros_like(dk_acc)
            dv_acc[...] = jnp.zeros_like(dv_acc)

        def step(diagonal):
            rows = pl.ds(pl.multiple_of(i * t, t), t)
            for g in range(G):
                cols = slice(g * LANES, (g + 1) * LANES)
                q, k, s = _scores(qn_ref.at[:, cols], qr_ref.at[:, cols], kn_ref.at[:, cols], kr_ref, t, diagonal)
                p = jnp.exp(s - lse_ref[:, g * LANES:g * LANES + 1])
                do = do_ref[:, cols]
                delta = jnp.sum(do.astype(F32) * o_ref[:, cols].astype(F32), axis=1, keepdims=True)
                dv_acc[:, cols] += lax.dot_general(p.astype(BF16), do, _DOT_DIMS["tn"], preferred_element_type=F32)
                dp = lax.dot_general(do, v_ref[:, cols], _DOT_DIMS["nt"], preferred_element_type=F32)
                ds = (p * (dp - delta)).astype(BF16)
                dk_acc[g] += lax.dot_general(ds, q, _DOT_DIMS["tn"], preferred_element_type=F32)
                dq_ref[g, rows, :] += jnp.dot(ds, k, preferred_element_type=F32)

        @pl.when(i > j)
        def _():
            step(False)

        @pl.when(i == j)
        def _():
            step(True)

        @pl.when(i == n - 1)
        def _():
            for g in range(G):
                dkn_ref[:, g * LANES:(g + 1) * LANES] = dk_acc[g, :, :LANES].astype(BF16)
                dkr_ref[g] = dk_acc[g, :, LANES:]
            dv_ref[...] = dv_acc[...].astype(BF16)

    blk = (t, G * LANES)
    grid = (hb, n_pairs)
    body, s_ins, s_in_specs, s_shapes, s_out_specs, s_sems, aliases = _attach(side, body, 10, 4, grid)
    return pl.pallas_call(
        body, name="flash_bwd", input_output_aliases=aliases,
        grid_spec=pltpu.PrefetchScalarGridSpec(
            num_scalar_prefetch=2, grid=grid,
            in_specs=[pl.BlockSpec(blk, lambda h, s, it, jt: (it[s], h)), pl.BlockSpec(blk, lambda h, s, it, jt: (it[s], hb + h)),
                      pl.BlockSpec(blk, lambda h, s, it, jt: (jt[s], h)), pl.BlockSpec(blk, lambda h, s, it, jt: (jt[s], hb + h)),
                      pl.BlockSpec((t, LANES), lambda h, s, it, jt: (jt[s], 0)),
                      pl.BlockSpec(blk, lambda h, s, it, jt: (it[s], h)), pl.BlockSpec(blk, lambda h, s, it, jt: (it[s], h)),
                      pl.BlockSpec(blk, lambda h, s, it, jt: (it[s], ob + h))] + s_in_specs,
            out_specs=[pl.BlockSpec((G, T, 2 * LANES), lambda h, s, it, jt: (h, 0, 0)),
                       pl.BlockSpec(blk, lambda h, s, it, jt: (jt[s], h)), pl.BlockSpec(blk, lambda h, s, it, jt: (jt[s], h)),
                       pl.BlockSpec((G, t, LANES), lambda h, s, it, jt: (h, jt[s], 0))] + s_out_specs,
            scratch_shapes=[pltpu.VMEM((G, t, 2 * LANES), F32), pltpu.VMEM(blk, F32), *s_sems]),
        out_shape=[jax.ShapeDtypeStruct((H, T, 2 * LANES), F32), jax.ShapeDtypeStruct((T, H * LANES), BF16),
                   jax.ShapeDtypeStruct((T, H * LANES), BF16), jax.ShapeDtypeStruct((H, T, LANES), F32)] + s_shapes,
        compiler_params=_params(("arbitrary", "arbitrary"),
                                12 * G * _nbytes((t, t), F32) + 2 * G * _nbytes((T, 2 * LANES), F32) + (8 << 20)),
    )(it, jt, q_att, q_att, kv, kv, kr, o, lse, dmix, *s_ins)


def _dq_post(dq, tabs, dims):
    H, T, _ = dq.shape
    tt = _tile(T, TILES["row"])

    def body(dq_ref, c_ref, s1_ref, s2_ref, o_ref):
        c, s1, s2 = c_ref[...], s1_ref[...], s2_ref[...]
        for hh in range(H):
            o_ref[:, hh * LANES:(hh + 1) * LANES] = (dq_ref[hh, :, :LANES] * ATT_SCALE).astype(BF16)
            o_ref[:, (H + hh) * LANES:(H + hh + 1) * LANES] = _rope_t(dq_ref[hh, :, LANES:] * ATT_SCALE, c, s1, s2).astype(BF16)

    tab = pl.BlockSpec((tt, LANES), lambda i: (i, 0))
    return pl.pallas_call(
        body, name="dq_post", grid=(T // tt,),
        in_specs=[pl.BlockSpec((H, tt, 2 * LANES), lambda i: (0, i, 0)), tab, tab, tab],
        out_specs=pl.BlockSpec((tt, 2 * H * LANES), lambda i: (i, 0)),
        out_shape=jax.ShapeDtypeStruct((T, 2 * H * LANES), BF16),
        compiler_params=_params(("parallel",), 8 * _nbytes((tt, 2 * H * LANES), F32)),
    )(dq, *tabs)


def _conv3(ge, cw, n):
    return cw[2:3] * ge + cw[1:2] * pltpu.roll(ge, 1, 0) + cw[0:1] * pltpu.roll(ge, 2, 0) + cw[3:4]


def _ffn_fwd(gate, up, cw8):
    T, F = gate.shape
    tt = _tile(T, TILES["ffn_row"])
    tc = _tile(F, TILES["ffn_c"], LANES)
    hb = tt // CONV_HALO

    def body(g_ref, gp_ref, u_ref, cw_ref, a_ref):
        it = pl.program_id(1)
        prev = jnp.where(it > 0, gp_ref[...].astype(F32), 0.0)
        ge = jnp.concatenate([prev, g_ref[...].astype(F32)], axis=0)
        gc = _conv3(ge, cw_ref[...], tt + CONV_HALO)[CONV_HALO:]
        a_ref[...] = (gc * _sigmoid(gc) * u_ref[...].astype(F32)).astype(BF16)

    blk = pl.BlockSpec((tt, tc), lambda jc, it: (it, jc))
    return pl.pallas_call(
        body, name="ffn_fwd", grid=(F // tc, T // tt),
        in_specs=[blk, pl.BlockSpec((CONV_HALO, tc), lambda jc, it: (jnp.maximum(it * hb - 1, 0), jc)), blk,
                  pl.BlockSpec((8, tc), lambda jc, it: (0, jc))],
        out_specs=blk, out_shape=jax.ShapeDtypeStruct((T, F), BF16),
        compiler_params=_params(("parallel", "parallel"), 16 * _nbytes((tt, tc), F32)),
    )(gate, gate, up, cw8)


def _ffn_bwd(da, gate, up, cw8):
    T, F = gate.shape
    tt = _tile(T, TILES["ffn_row"])
    tc = _tile(F, TILES["ffn_c"], LANES)
    hb = tt // CONV_HALO
    n_t = T // tt
    n = tt + 2 * CONV_HALO

    def body(da_ref, dan_ref, g_ref, gp_ref, gn_ref, u_ref, un_ref, cw_ref, dg_ref, du_ref, dcw_ref):
        it = pl.program_id(1)
        first, last = it == 0, it == n_t - 1
        cw = cw_ref[...]
        zeros = jnp.zeros((CONV_HALO, tc), F32)
        ge = jnp.concatenate([jnp.where(first, 0.0, gp_ref[...].astype(F32)), g_ref[...].astype(F32),
                              gn_ref[...].astype(F32)], axis=0)
        dae = jnp.concatenate([zeros, da_ref[...].astype(F32), jnp.where(last, 0.0, dan_ref[...].astype(F32))], axis=0)
        ue = jnp.concatenate([zeros, u_ref[...].astype(F32), un_ref[...].astype(F32)], axis=0)
        g1, g2 = pltpu.roll(ge, 1, 0), pltpu.roll(ge, 2, 0)
        gc = cw[2:3] * ge + cw[1:2] * g1 + cw[0:1] * g2 + cw[3:4]
        sg = _sigmoid(gc)
        dgc = dae * ue * (sg * (1.0 + gc * (1.0 - sg)))
        du_ref[...] = (dae * gc * sg)[CONV_HALO:CONV_HALO + tt].astype(BF16)
        dgp = cw[2:3] * dgc + cw[1:2] * pltpu.roll(dgc, n - 1, 0) + cw[0:1] * pltpu.roll(dgc, n - 2, 0)
        dg_ref[...] = dgp[CONV_HALO:CONV_HALO + tt].astype(BF16)
        mid = slice(CONV_HALO, CONV_HALO + tt)
        d_mid = dgc[mid]
        part = jnp.concatenate([jnp.sum(d_mid * g2[mid], axis=0, keepdims=True), jnp.sum(d_mid * g1[mid], axis=0, keepdims=True),
                                jnp.sum(d_mid * ge[mid], axis=0, keepdims=True), jnp.sum(d_mid, axis=0, keepdims=True),
                                jnp.zeros((4, tc), F32)], axis=0)

        @pl.when(first)
        def _():
            dcw_ref[...] = part

        @pl.when(it > 0)
        def _():
            dcw_ref[...] += part

    blk = pl.BlockSpec((tt, tc), lambda jc, it: (it, jc))
    prv = pl.BlockSpec((CONV_HALO, tc), lambda jc, it: (jnp.maximum(it * hb - 1, 0), jc))
    nxt = pl.BlockSpec((CONV_HALO, tc), lambda jc, it: (jnp.minimum((it + 1) * hb, T // CONV_HALO - 1), jc))
    cws = pl.BlockSpec((8, tc), lambda jc, it: (0, jc))
    return pl.pallas_call(
        body, name="ffn_bwd", grid=(F // tc, n_t), in_specs=[blk, nxt, blk, prv, nxt, blk, nxt, cws],
        out_specs=[blk, blk, cws],
        out_shape=[jax.ShapeDtypeStruct((T, F), BF16), jax.ShapeDtypeStruct((T, F), BF16), jax.ShapeDtypeStruct((8, F), F32)],
        compiler_params=_params(("parallel", "arbitrary"), 32 * _nbytes((tt, tc), F32)),
    )(da, da, gate, gate, gate, up, up, cw8)


def _ple_fwd(h2, gl, pe):
    T, D = h2.shape
    tt = _tile(T, TILES["row"])

    def body(h_ref, gl_ref, pe_ref, o_ref):
        o_ref[...] = h_ref[...] + pe_ref[...] * _sigmoid(gl_ref[...])

    row = pl.BlockSpec((tt, D), lambda i: (i, 0))
    return pl.pallas_call(
        body, name="ple_fwd", grid=(T // tt,), in_specs=[row, row, row], out_specs=row,
        out_shape=jax.ShapeDtypeStruct((T, D), F32), compiler_params=_params(("parallel",), 12 * _nbytes((tt, D), F32)),
    )(h2, gl, pe)


def _ple_bwd(dh, gl, pe):
    T, D = dh.shape
    tt = _tile(T, TILES["row"])

    def body(dh_ref, gl_ref, pe_ref, dpe_ref, dgl_ref):
        d = dh_ref[...]
        sg = _sigmoid(gl_ref[...])
        dpe_ref[...] = (d * sg).astype(BF16)
        dgl_ref[...] = (d * pe_ref[...] * (sg * (1.0 - sg))).astype(BF16)

    row = pl.BlockSpec((tt, D), lambda i: (i, 0))
    return pl.pallas_call(
        body, name="ple_bwd", grid=(T // tt,), in_specs=[row, row, row], out_specs=[row, row],
        out_shape=[jax.ShapeDtypeStruct((T, D), BF16), jax.ShapeDtypeStruct((T, D), BF16)],
        compiler_params=_params(("parallel",), 12 * _nbytes((tt, D), F32)),
    )(dh, gl, pe)


def _adamw(w, g, m, v, name):
    shape = w.shape
    cols = shape[-1]
    rows = math.prod(shape[:-1]) if len(shape) > 1 else 1
    w2, g2, m2, v2 = (a.reshape(rows, cols) for a in (w, g, m, v))
    tr = _tile(rows, max(8, (1 << 20) // (cols * 4)))
    c1 = 1.0 - ADAM_B1 ** ADAM_STEP
    c2 = 1.0 - ADAM_B2 ** ADAM_STEP

    def body(w_ref, g_ref, m_ref, v_ref, d_ref, mo_ref, vo_ref):
        gv = g_ref[...]
        mn = ADAM_B1 * m_ref[...] + (1.0 - ADAM_B1) * gv
        vn = ADAM_B2 * v_ref[...] + (1.0 - ADAM_B2) * (gv * gv)
        mo_ref[...] = mn
        vo_ref[...] = vn
        d_ref[...] = -ADAM_LR * ((mn / c1) / (jnp.sqrt(vn / c2) + ADAM_EPS) + ADAM_WD * w_ref[...])

    blk = pl.BlockSpec((tr, cols), lambda i: (i, 0))
    shp = jax.ShapeDtypeStruct((rows, cols), F32)
    outs = pl.pallas_call(
        body, name=name, grid=(rows // tr,), in_specs=[blk] * 4, out_specs=[blk] * 3, out_shape=[shp] * 3,
        compiler_params=_params(("parallel",), 16 * _nbytes((tr, cols), F32)),
    )(w2, g2, m2, v2)
    return tuple(o.reshape(shape) for o in outs)


_HBM = pl.BlockSpec(memory_space=pltpu.HBM)


def _place():
    x, y, c = lax.axis_index("x"), lax.axis_index("y"), lax.axis_index("c")
    return x, y, c, [(1 - x, y), (x, 1 - y), (1 - x, 1 - y)]


def _remote(src, dst, send_sems, recv_sems, k, to):
    return pltpu.make_async_remote_copy(src_ref=src, dst_ref=dst, send_sem=send_sems.at[k], recv_sem=recv_sems.at[k],
                                        device_id=to, device_id_type=MESH)


def _half(ref, lead, h):
    hr = ref.shape[-2] // 2
    return ref.at[(*lead, pl.ds(pl.multiple_of(h * hr, SUBLANES_BF16), hr))]


def _sem_pair(n):
    return [pltpu.SemaphoreType.DMA((n,)), pltpu.SemaphoreType.DMA((n,))]


def _run_side(side, name):
    n_in, n_out = len(side.ins), len(side.out_shapes)

    def body(*refs):
        parts = (refs[:n_in], refs[n_in:n_in + n_out]) + tuple(refs[n_in + n_out:])
        side.start(*parts)
        side.finish(*parts)

    return pl.pallas_call(
        body, name=name, in_specs=[_HBM] * n_in, out_specs=[_HBM] * n_out, out_shape=side.out_shapes,
        scratch_shapes=_sem_pair(side.n_sems), input_output_aliases=side.aliases,
    )(*side.ins)


def _whole(arrs, halves):
    return [(a, 0, arr.shape[-2] // (2 if halves else 1)) for a, arr in enumerate(arrs)]


def _plan(arrs, halves, big):
    whole = _whole(arrs, halves)
    q = whole[big][2] // 4
    return [[pc for pc in whole if pc[0] != big] + [(big, 0, q)]] + [[(big, k * q, q)] for k in (1, 2, 3)]


def _plan_gather(arrs, big, small):
    whole = _whole(arrs, True)
    e = whole[big][2] // 8
    return [[pc for pc in whole if pc[0] not in (big, small)], [(big, 0, 3 * e)], [(big, 3 * e, 3 * e)],
            [(big, 6 * e, 2 * e), whole[small]]]


def _ride(fn, n_main, pieces, make, store):
    if pieces is None:
        return fn(None)
    side, touched = make(pieces)
    out = fn(side)
    store.update(zip(touched, out[n_main:]))
    return out[0] if n_main == 1 else out[:n_main]


def _carried(arrs, pieces, prior):
    touched = sorted({a for a, _, _ in pieces})
    pos = {a: i for i, a in enumerate(touched)}
    carried = [a for a in touched if a in prior]
    ins = [arrs[a] for a in touched] + [prior[a] for a in carried]
    return touched, pos, ins, {len(touched) + i: pos[a] for i, a in enumerate(carried)}


def _gather_side(arrs, layer, pieces, prior):
    touched, pos, ins_arrs, aliases = _carried(arrs, pieces, prior)

    def copies(ins, outs, send_sems, recv_sems, arriving):
        x, y, c, chips = _place()
        me, sib = 2 * x + y, (x, y, 1 - c)
        out = []
        for p, (a, r0, nr) in enumerate(pieces):
            src, dst = ins[pos[a]], outs[pos[a]]
            hr = src.shape[-2] // 2
            for hlf in range(2):
                rows = pl.ds(hlf * hr + r0, nr)
                out.append(_remote(src.at[layer, rows], dst.at[me, rows], send_sems, recv_sems, 5 * p + 3 + hlf, sib))
            rows = pl.ds(pl.multiple_of(c * hr + r0, SUBLANES_BF16), nr)
            for k, (cx, cy) in enumerate(chips):
                slot = 2 * cx + cy if arriving else me
                out.append(_remote(src.at[layer, rows], dst.at[slot, rows], send_sems, recv_sems, 5 * p + k, (cx, cy, c)))
        return out

    def start(ins, outs, send_sems, recv_sems):
        for cp in copies(ins, outs, send_sems, recv_sems, False):
            cp.start()

    def finish(ins, outs, send_sems, recv_sems):
        for cp in copies(ins, outs, send_sems, recv_sems, True):
            cp.wait_recv()
        for cp in copies(ins, outs, send_sems, recv_sems, False):
            cp.wait_send()

    shapes = [jax.ShapeDtypeStruct((N_SHARDS,) + arrs[a].shape[1:], arrs[a].dtype) for a in touched]
    return _Side(ins_arrs, shapes, 5 * len(pieces), start, finish, aliases), touched


def _forward_side(arrs):
    n = len(arrs)

    def copies(outs, send_sems, recv_sems, arriving):
        x, y, c, chips = _place()
        sib = (x, y, 1 - c)
        out = []
        for a in range(n):
            for k, (cx, cy) in enumerate(chips):
                got = _half(outs[a], (2 * cx + cy,), 1 - c if arriving else c)
                out.append(_remote(got, got, send_sems, recv_sems, 3 * a + k, sib))
        return out

    def start(ins, outs, send_sems, recv_sems):
        for cp in copies(outs, send_sems, recv_sems, False):
            cp.start()

    def finish(ins, outs, send_sems, recv_sems):
        for cp in copies(outs, send_sems, recv_sems, True):
            cp.wait_recv()
        for cp in copies(outs, send_sems, recv_sems, False):
            cp.wait_send()

    return _Side(arrs, [jax.ShapeDtypeStruct(a.shape, a.dtype) for a in arrs], 3 * n, start, finish, {a: a for a in range(n)})


def _sibling_side(arrs):
    n = len(arrs)

    def copies(ins, outs, send_sems, recv_sems):
        x, y, c, _ = _place()
        return [_remote(_half(ins[a], (s,), 1 - c), outs[a].at[s], send_sems, recv_sems, N_SHARDS * a + s, (x, y, 1 - c))
                for a in range(n) for s in range(N_SHARDS)]

    def start(ins, outs, send_sems, recv_sems):
        for cp in copies(ins, outs, send_sems, recv_sems):
            cp.start()

    def finish(ins, outs, send_sems, recv_sems):
        for cp in copies(ins, outs, send_sems, recv_sems):
            cp.wait_recv()
        for cp in copies(ins, outs, send_sems, recv_sems):
            cp.wait_send()

    shapes = [jax.ShapeDtypeStruct((N_SHARDS, a.shape[1] // 2, a.shape[2]), a.dtype) for a in arrs]
    return _Side(arrs, shapes, N_SHARDS * n, start, finish)


def _chip_side(arrs, pieces, prior):
    touched, pos, ins_arrs, aliases = _carried(arrs, pieces, prior)

    def copies(ins, outs, send_sems, recv_sems):
        x, y, c, chips = _place()
        return [_remote(ins[pos[a]].at[2 * cx + cy, pl.ds(r0, nr)], outs[pos[a]].at[k, pl.ds(r0, nr)], send_sems, recv_sems,
                        3 * p + k, (cx, cy, c))
                for p, (a, r0, nr) in enumerate(pieces) for k, (cx, cy) in enumerate(chips)]

    def start(ins, outs, send_sems, recv_sems):
        for cp in copies(ins, outs, send_sems, recv_sems):
            cp.start()

    def finish(ins, outs, send_sems, recv_sems):
        for cp in copies(ins, outs, send_sems, recv_sems):
            cp.wait_recv()
        for cp in copies(ins, outs, send_sems, recv_sems):
            cp.wait_send()

    shapes = [jax.ShapeDtypeStruct((3,) + arrs[a].shape[1:], arrs[a].dtype) for a in touched]
    return _Side(ins_arrs, shapes, 3 * len(pieces), start, finish, aliases), touched


def _share_side(arrs):
    n = len(arrs)

    def copies(outs, send_sems, recv_sems, arriving):
        x, y, c, _ = _place()
        return [_remote(outs[a].at[c], outs[a].at[1 - c if arriving else c], send_sems, recv_sems, a, (x, y, 1 - c)) for a in range(n)]

    def start(ins, outs, send_sems, recv_sems):
        for cp in copies(outs, send_sems, recv_sems, False):
            cp.start()

    def finish(ins, outs, send_sems, recv_sems):
        for cp in copies(outs, send_sems, recv_sems, True):
            cp.wait_recv()
        for cp in copies(outs, send_sems, recv_sems, False):
            cp.wait_send()

    return _Side(arrs, [jax.ShapeDtypeStruct(a.shape, a.dtype) for a in arrs], n, start, finish, {a: a for a in range(n)})


def _add_sibling(g, sib_in, place):
    S, rows, cols = g.shape
    hr = rows // 2
    tr = _tile(hr, max(SUBLANES_BF16, TILES["add_bytes"] // (cols * 2)), SUBLANES_BF16)
    nb = hr // tr

    def body(p_ref, a_ref, b_ref, o_ref):
        o_ref[...] = (a_ref[...].astype(F32) + b_ref[...].astype(F32)).astype(o_ref.dtype)

    blk = pl.BlockSpec((None, tr, cols), lambda s, r, p: (s, r, 0))
    return pl.pallas_call(
        body, name="rs_add_sibling",
        grid_spec=pltpu.PrefetchScalarGridSpec(
            num_scalar_prefetch=1, grid=(S, nb),
            in_specs=[pl.BlockSpec((None, tr, cols), lambda s, r, p: (s, p[1] * nb + r, 0)), blk], out_specs=blk),
        out_shape=jax.ShapeDtypeStruct((S, hr, cols), g.dtype),
        compiler_params=_params(("parallel", "parallel"), 16 * _nbytes((tr, cols), F32)),
    )(place, g, sib_in)


def _add_chips(cs, got, place):
    S, r, cols = cs.shape
    tr = _tile(r, max(SUBLANES_BF16, TILES["add_bytes"] // (cols * 2)), SUBLANES_BF16)

    def body(p_ref, a_ref, b_ref, o_ref):
        acc = a_ref[...].astype(F32)
        for k in range(3):
            acc = acc + b_ref[k].astype(F32)
        o_ref[...] = acc

    return pl.pallas_call(
        body, name="rs_add_chips",
        grid_spec=pltpu.PrefetchScalarGridSpec(
            num_scalar_prefetch=1, grid=(r // tr,),
            in_specs=[pl.BlockSpec((None, tr, cols), lambda i, p: (p[0], i, 0)),
                      pl.BlockSpec((3, tr, cols), lambda i, p: (0, i, 0))],
            out_specs=pl.BlockSpec((None, tr, cols), lambda i, p: (p[1], i, 0))),
        out_shape=jax.ShapeDtypeStruct((2, r, cols), F32),
        compiler_params=_params(("parallel",), 24 * _nbytes((tr, cols), F32)),
    )(place, cs, got)


def _reduce_begin(arrs, from_sibling, place):
    return [_add_sibling(g, s, place) for g, s in zip(arrs, from_sibling)]


def _reduce_end(sums, got, place):
    return _share_side([_add_chips(cs, g, place) for cs, g in zip(sums, got)])


def _shards(full):
    return [f.reshape(-1, f.shape[-1]) for f in full]


def _all_reduce_small(v):
    R = v.shape[0]

    def body(v_ref, o_ref, buf, send_sems, recv_sems):
        x, y, c, _ = _place()
        me = 4 * x + 2 * y + c
        buf[me] = v_ref[...]
        sends = []
        for k in range(1, 8):
            px = 1 - x if k & 4 else x
            py = 1 - y if k & 2 else y
            pc = 1 - c if k & 1 else c
            sends.append(_remote(v_ref, buf.at[me], send_sems, recv_sems, k - 1, (px, py, pc)))
        for cp in sends:
            cp.start()
        for k in range(1, 8):
            px = 1 - x if k & 4 else x
            py = 1 - y if k & 2 else y
            pc = 1 - c if k & 1 else c
            _remote(v_ref, buf.at[4 * px + 2 * py + pc], send_sems, recv_sems, k - 1, (px, py, pc)).wait_recv()
        for cp in sends:
            cp.wait_send()
        acc = buf[0]
        for d in range(1, 8):
            acc = acc + buf[d]
        o_ref[...] = acc

    vm = pl.BlockSpec(memory_space=pltpu.VMEM)
    return pl.pallas_call(
        body, name="all_reduce_small", in_specs=[vm], out_specs=vm, out_shape=jax.ShapeDtypeStruct(v.shape, F32),
        scratch_shapes=[pltpu.VMEM((8, R, LANES), F32), pltpu.SemaphoreType.DMA((7,)), pltpu.SemaphoreType.DMA((7,))],
    )(v)


def _pad_to(a, n):
    return a if a.shape[0] == n else jnp.pad(a, (0, n - a.shape[0]))


def _piece_len(shape):
    return -(-math.prod(shape) // PACK_ALIGN) * PACK_ALIGN


def _pack(pieces, dtype):
    flat = jnp.concatenate([_pad_to(a.reshape(-1).astype(dtype), _piece_len(a.shape)) for a in pieces])
    return flat.reshape(-1, LANES)


def _unpack(flat, shapes, lead):
    flat = flat.reshape(lead + (-1,))
    out, off = [], 0
    for shp in shapes:
        out.append(flat[..., off:off + math.prod(shp)].reshape(lead + tuple(shp)))
        off += _piece_len(shp)
    return out


def _join(name, a):
    if name in COL_SHARDED:
        return a.transpose(1, 0, 2).reshape(a.shape[1], -1)
    if name in ROW_SHARDED:
        return a.reshape(-1, a.shape[-1])
    return a.transpose(1, 0, 2, 3).reshape(a.shape[1], -1, a.shape[-1])


def _split(name, a):
    if name in COL_SHARDED:
        return a.reshape(a.shape[0], N_SHARDS, -1).transpose(1, 0, 2)
    if name in ROW_SHARDED:
        return a.reshape(N_SHARDS, -1, a.shape[-1])
    return a.reshape(a.shape[0], N_SHARDS, -1, a.shape[-1]).transpose(1, 0, 2, 3)


def _heads_split(w, H, first, second, pad_second):
    K = w.shape[0]
    w3 = w.reshape(K, H, first + second)
    b = w3[:, :, first:]
    if pad_second > second:
        b = jnp.pad(b, ((0, 0), (0, 0), (0, pad_second - second)))
    return jnp.concatenate([w3[:, :, :first].reshape(K, -1), b.reshape(K, -1)], axis=1)


def _heads_merge(w, H, first, second, pad_second):
    K = w.shape[0]
    a = w[:, :H * first].reshape(K, H, first)
    b = w[:, H * first:].reshape(K, H, pad_second)[:, :, :second]
    return jnp.concatenate([a, b], axis=2).reshape(K, -1)


def kernel(x, p, positions, norm_mix_g, w_in, pool_w, pool_scale, q_norm_g, w_uq, kv_norm_g, w_ukv, w_out, norm_ffn_g, w_up, conv_w, conv_b, w_down, norm_ple_g, w_ple, w_ple_gate, final_norm_g, loss_target, m_norm_mix_g, m_w_in, m_pool_w, m_pool_scale, m_q_norm_g, m_w_uq, m_kv_norm_g, m_w_ukv, m_w_out, m_norm_ffn_g, m_w_up, m_conv_w, m_conv_b, m_w_down, m_norm_ple_g, m_w_ple, m_w_ple_gate, m_final_norm_g, v_norm_mix_g, v_w_in, v_pool_w, v_pool_scale, v_q_norm_g, v_w_uq, v_kv_norm_g, v_w_ukv, v_w_out, v_norm_ffn_g, v_w_up, v_conv_w, v_conv_b, v_w_down, v_norm_ple_g, v_w_ple, v_w_ple_gate, v_final_norm_g):
    W = dict(norm_mix_g=norm_mix_g, w_in=w_in, pool_w=pool_w, pool_scale=pool_scale, q_norm_g=q_norm_g, w_uq=w_uq,
             kv_norm_g=kv_norm_g, w_ukv=w_ukv, w_out=w_out, norm_ffn_g=norm_ffn_g, w_up=w_up, conv_w=conv_w, conv_b=conv_b,
             w_down=w_down, norm_ple_g=norm_ple_g, w_ple=w_ple, w_ple_gate=w_ple_gate, final_norm_g=final_norm_g)
    M1 = dict(norm_mix_g=m_norm_mix_g, w_in=m_w_in, pool_w=m_pool_w, pool_scale=m_pool_scale, q_norm_g=m_q_norm_g, w_uq=m_w_uq,
              kv_norm_g=m_kv_norm_g, w_ukv=m_w_ukv, w_out=m_w_out, norm_ffn_g=m_norm_ffn_g, w_up=m_w_up, conv_w=m_conv_w,
              conv_b=m_conv_b, w_down=m_w_down, norm_ple_g=m_norm_ple_g, w_ple=m_w_ple, w_ple_gate=m_w_ple_gate,
              final_norm_g=m_final_norm_g)
    M2 = dict(norm_mix_g=v_norm_mix_g, w_in=v_w_in, pool_w=v_pool_w, pool_scale=v_pool_scale, q_norm_g=v_q_norm_g, w_uq=v_w_uq,
              kv_norm_g=v_kv_norm_g, w_ukv=v_w_ukv, w_out=v_w_out, norm_ffn_g=v_norm_ffn_g, w_up=v_w_up, conv_w=v_conv_w,
              conv_b=v_conv_b, w_down=v_w_down, norm_ple_g=v_norm_ple_g, w_ple=v_w_ple, w_ple_gate=v_w_ple_gate,
              final_norm_g=v_final_norm_g)

    _, T, D = x.shape
    L = p.shape[0]
    P, QL, KL, F = pool_scale.shape[-1], q_norm_g.shape[-1], kv_norm_g.shape[-1], conv_b.shape[-1]
    C = pool_w.shape[-1]
    H = (D - P) // V_DIM
    d_in = P + QL + KL + ROPE_DIM
    dims = dict(P=P, QL=QL, KL=KL, C=C, H=H)
    misc_shapes = [W[n].shape[1:] for n in MISC]
    ns_in, ns_up, ns_conv = w_in.shape[-1], w_up.shape[-1], conv_w.shape[-1]

    xi, yi, ci = lax.axis_index("x"), lax.axis_index("y"), lax.axis_index("c")
    me = 2 * xi + yi
    place = jnp.stack([me, ci]).astype(jnp.int32)

    def all_reduce(parts):
        flat = jnp.concatenate(parts)
        padded = -(-flat.shape[0] // (8 * LANES)) * (8 * LANES)
        return _all_reduce_small(_pad_to(flat, padded).reshape(-1, LANES)).reshape(-1)

    inv_freq = 1.0 / (ROPE_THETA ** (jnp.arange(0, ROPE_DIM, 2, dtype=F32) / ROPE_DIM))
    inv_lane = jnp.concatenate([inv_freq, inv_freq, jnp.zeros((LANES - ROPE_DIM,), F32)]).reshape(1, LANES)
    tabs = _rope_tables(positions.reshape(T, 1).astype(F32), inv_lane)

    local = [W[n].astype(BF16) for n in BIG] + [jnp.stack([_pack([W[n][l] for n in MISC], BF16) for l in range(L)])]
    placed = lax.dynamic_update_slice(jnp.zeros((L, CONV_TAPS, F), F32), conv_w, (0, 0, me * ns_conv))
    conv_full = all_reduce([jnp.where(ci == 0, placed, 0.0).reshape(-1)])[:L * CONV_TAPS * F].reshape(L, CONV_TAPS, F)

    def layout(got, l):
        g = dict(zip(BIG, got[:-1]))
        misc = {n: _join(n, a) for n, a in zip(MISC, _unpack(got[-1], misc_shapes, (N_SHARDS,)))}
        return dict(
            w_in=jnp.concatenate([g["w_in"][sh] for sh in range(N_SHARDS)] + [jnp.zeros((D, LANES - ROPE_DIM), BF16)], axis=1),
            w_out=g["w_out"].reshape(-1, D), w_down=g["w_down"].reshape(-1, D), w_ple_gate=g["w_ple_gate"].reshape(-1, D),
            w_up=g["w_up"], w_ple=misc["w_ple"], pool_w=misc["pool_w"],
            w_uq=_heads_split(misc["w_uq"], H, NOPE_DIM, ROPE_DIM, LANES),
            w_ukv=_heads_split(misc["w_ukv"], H, NOPE_DIM, V_DIM, V_DIM),
            cw8=jnp.concatenate([conv_full[l], conv_b[l][None], jnp.zeros((4, F), F32)], axis=0))

    half_up = (0, N_SHARDS // 2), (N_SHARDS // 2, N_SHARDS // 2)

    h = x[0]
    saved, FW = [], []
    up_at = BIG.index("w_up")
    arriving = _run_side(_forward_side(_run_side(_gather_side(local, 0, _whole(local, True), {})[0], "all_gather")), "gather_forward")
    for l in range(L):
        fw = layout(arriving, l)
        FW.append(fw)
        s = dict(h0=h)
        nxt = {}
        parts = _plan_gather(local, up_at, len(local) - 1) if l + 1 < L else [None] * 4

        def gather(pieces):
            return _gather_side(local, l + 1, pieces, nxt)
        s["n1"] = _rms_fwd(h, norm_mix_g[l], "norm_mix")
        s["u"] = _matmul(s["n1"], fw["w_in"], "nn", F32, "mm_in", tm=512, tn=d_in + LANES - ROPE_DIM)
        s["diff"], s["cqn"], s["ckvn"], s["kr"] = _post_u(s["u"], q_norm_g[l], kv_norm_g[l], tabs, dims)
        s["q"] = _q_rope(_matmul(s["cqn"], fw["w_uq"], "nn", F32, "mm_uq", tn=2 * H * LANES), tabs, dims)
        s["kv"] = _matmul(s["ckvn"], fw["w_ukv"], "nn", BF16, "mm_ukv", tn=2 * H * LANES)
        s["o"], s["lse"] = _ride(lambda sd: _flash_fwd(s["q"], s["kv"], s["kr"], dims, sd), 2, parts[0], gather, nxt)
        s["mix"] = jnp.concatenate([_pool_fwd(s["diff"], fw["pool_w"], pool_scale[l], dims), s["o"]], axis=1)
        s["h1"] = _matmul(s["mix"], fw["w_out"], "nn", F32, "mm_out", res=h, tm=1024)
        s["n2"] = _rms_fwd(s["h1"], norm_ffn_g[l], "norm_ffn")
        s["gate"] = _ride(lambda sd: _matmul(s["n2"], fw["w_up"], "nn", BF16, "mm_gate", tm=1024, tn=ns_up // 2, b_shards=half_up[0],
                                             side=sd), 1, parts[1], gather, nxt)
        s["up"] = _ride(lambda sd: _matmul(s["n2"], fw["w_up"], "nn", BF16, "mm_up", tm=1024, tn=ns_up // 2, b_shards=half_up[1],
                                           side=sd), 1, parts[2], gather, nxt)
        s["a"] = _ffn_fwd(s["gate"], s["up"], fw["cw8"])
        s["h2"] = _ride(lambda sd: _matmul(s["a"], fw["w_down"], "nn", F32, "mm_down", res=s["h1"], tm=1024, tn=512, tk=F,
                                           side=sd), 1, parts[3], gather, nxt)
        s["n3"] = _rms_fwd(s["h2"], norm_ple_g[l], "norm_ple")
        if nxt:
            s["gl"], *arriving = _matmul(s["n3"], fw["w_ple_gate"], "nn", F32, "mm_ple_gate", tm=1024,
                                         side=_forward_side([nxt[a] for a in range(len(local))]))
        else:
            s["gl"] = _matmul(s["n3"], fw["w_ple_gate"], "nn", F32, "mm_ple_gate", tm=1024)
        s["pe"] = _matmul(p[l, 0], fw["w_ple"], "nn", F32, "mm_ple", tn=D)
        h = _ple_fwd(s["h2"], s["gl"], s["pe"])
        saved.append(s)

    dh, dhb, dg_final, loss_part = _final_loss(h, loss_target[0], final_norm_g)
    loss = lax.psum(loss_part[0, 0], ("x", "y", "c"))

    small = {}
    reduced = [None] * L
    raw = None
    for l in reversed(range(L)):
        fw, s = FW[l], saved[l]
        gw = {}
        got = {}
        dpe, dgl = _ple_bwd(dh, s["gl"], s["pe"])
        gw["w_ple"] = _matmul(p[l, 0], dpe, "tn", BF16, "dw_ple", tm=512)
        gw["w_ple_gate"] = _matmul(s["n3"], dgl, "tn", BF16, "dw_ple_gate", tm=512, tk=T)
        dn3 = _matmul(dgl, fw["w_ple_gate"], "nt", F32, "dx_ple_gate", tm=1024)
        dh, dhb, small["norm_ple_g", l] = _rms_bwd(dn3, s["h2"], norm_ple_g[l], dh, "norm_ple_bwd")

        tn_down = F // 4 if F % (4 * LANES) == 0 else F
        if raw:
            da, *from_sibling = _matmul(dhb, fw["w_down"], "nt", BF16, "dx_down", tn=tn_down, side=_sibling_side(raw))
            waiting = _reduce_begin(raw, from_sibling, place)
            parts = _plan(waiting, False, up_at)
        else:
            da = _matmul(dhb, fw["w_down"], "nt", BF16, "dx_down", tn=tn_down)
            waiting, parts = None, [None] * 4

        def chips(pieces):
            return _chip_side(waiting, pieces, got)

        gw["w_down"] = _matmul(s["a"], dhb, "tn", BF16, "dw_down", tm=F // 4 if F % (4 * LANES) == 0 else None)
        dgate, dup, dcw = _ffn_bwd(da, s["gate"], s["up"], fw["cw8"])
        small["conv_w", l], small["conv_b", l] = dcw[:CONV_TAPS], dcw[CONV_TAPS:CONV_TAPS + 1]
        gw["w_up"] = _ride(lambda sd: _matmul(s["n2"], dgate, "tn", BF16, "dw_gate", tm=512, tn=ns_up // 2, tk=T,
                                              out_shards=(half_up[0][0], N_SHARDS, ns_up), side=sd), 1, parts[1], chips, got)
        gw["w_up"] = _ride(lambda sd: _matmul(s["n2"], dup, "tn", BF16, "dw_up", tm=512, tn=ns_up // 2, tk=T,
                                              out_shards=(half_up[1][0], N_SHARDS, ns_up), carry=gw["w_up"], side=sd), 1, parts[2], chips, got)
        dn2 = _ride(lambda sd: _matmul(dgate, fw["w_up"], "nt", F32, "dx_gate", tm=1024, tn=D // 2, tk=ns_up, b_shards=half_up[0],
                                       side=sd), 1, parts[3], chips, got)
        dn2 = _matmul(dup, fw["w_up"], "nt", F32, "dx_up", res=dn2, tm=1024, tn=D // 2, tk=ns_up, b_shards=half_up[1])
        dh, dhb, small["norm_ffn_g", l] = _rms_bwd(dn2, s["h1"], norm_ffn_g[l], dh, "norm_ffn_bwd")

        dmix = _matmul(dhb, fw["w_out"], "nt", BF16, "dx_out")
        gw["w_out"] = _matmul(s["mix"], dhb, "tn", BF16, "dw_out", tm=512, tk=T)
        ddiff, gw["pool_w"], small["pool_scale", l] = _pool_bwd(dmix, s["diff"], fw["pool_w"], pool_scale[l], dims)
        dq, dkn, dv, dkr = _ride(lambda sd: _flash_bwd(s["q"], s["kv"], s["kr"], s["o"], s["lse"], dmix, dims, sd), 4, parts[0], chips, got)
        sharing = _reduce_end(waiting, [got[a] for a in range(len(waiting))], place) if got else None
        dqb = _dq_post(dq, tabs, dims)
        dkv = jnp.concatenate([dkn, dv], axis=1)
        gw["w_uq"] = _heads_merge(_matmul(s["cqn"], dqb, "tn", BF16, "dw_uq", tn=2 * H * LANES), H, NOPE_DIM, ROPE_DIM, LANES)
        gw["w_ukv"] = _heads_merge(_matmul(s["ckvn"], dkv, "tn", BF16, "dw_ukv", tn=2 * H * LANES), H, NOPE_DIM, V_DIM, V_DIM)
        dcqn = _matmul(dqb, fw["w_uq"], "nt", F32, "dx_uq")
        dckvn = _matmul(dkv, fw["w_ukv"], "nt", F32, "dx_ukv")
        du, small["q_norm_g", l], small["kv_norm_g", l] = _pre_u_bwd(s["u"], dcqn, dckvn, ddiff, dkr, q_norm_g[l], kv_norm_g[l], tabs, dims)
        gw["w_in"] = _matmul(s["n1"], du, "tn", BF16, "dw_in", tm=512, tn=du.shape[1])[:, :d_in]
        if sharing:
            dn1, *full = _matmul(du, fw["w_in"], "nt", F32, "dx_in", tm=1024, tk=du.shape[1], side=sharing)
            reduced[l + 1] = _shards(full)
        else:
            dn1 = _matmul(du, fw["w_in"], "nt", F32, "dx_in", tm=1024, tk=du.shape[1])
        dh, dhb, small["norm_mix_g", l] = _rms_bwd(dn1, s["h0"], norm_mix_g[l], dh, "norm_mix_bwd")

        split = {n: _split(n, gw[n]) for n in MISC}
        arrs = [jnp.stack([gw["w_in"][:, sh * ns_in:(sh + 1) * ns_in] for sh in range(N_SHARDS)]),
                gw["w_out"].reshape(N_SHARDS, -1, D), gw["w_up"], gw["w_down"].reshape(N_SHARDS, -1, D),
                gw["w_ple_gate"].reshape(N_SHARDS, -1, D),
                jnp.stack([_pack([split[n][sh] for n in MISC], BF16) for sh in range(N_SHARDS)])]
        raw = arrs
    waiting = _reduce_begin(raw, _run_side(_sibling_side(raw), "rs_sibling"), place)
    reduced[0] = _shards(_run_side(_reduce_end(waiting, _run_side(_chip_side(waiting, _whole(waiting, False), {})[0], "rs_chips"), place),
                                   "rs_share"))

    grads = {n: jnp.stack([reduced[l][k].reshape(W[n].shape[1:]) for l in range(L)]) for k, n in enumerate(BIG)}
    per_layer = [_unpack(reduced[l][-1], misc_shapes, ()) for l in range(L)]
    for k, n in enumerate(MISC):
        grads[n] = jnp.stack([per_layer[l][k] for l in range(L)])

    small_names = ("norm_mix_g", "pool_scale", "q_norm_g", "kv_norm_g", "norm_ffn_g", "conv_b", "norm_ple_g", "conv_w")
    summed = all_reduce([small[n, l].reshape(-1) for n in small_names for l in range(L)] + [dg_final.reshape(-1)])
    off = 0
    for n in small_names:
        size = CONV_TAPS * F if n == "conv_w" else W[n].shape[-1]
        grads[n] = summed[off:off + L * size].reshape((L, CONV_TAPS, F) if n == "conv_w" else (L, size))
        off += L * size
    grads["final_norm_g"] = summed[off:off + D]
    grads["conv_w"] = lax.dynamic_slice(grads["conv_w"], (0, 0, me * ns_conv), (L, CONV_TAPS, ns_conv))

    deltas, new_m, new_v = {}, {}, {}
    for n in WEIGHTS:
        deltas[n], new_m[n], new_v[n] = _adamw(W[n], grads[n], M1[n], M2[n], "adamw_" + n)

    return (loss, dh[None], *[grads[n] for n in WEIGHTS], *[deltas[n] for n in WEIGHTS],
            *[new_m[n] for n in WEIGHTS], *[new_v[n] for n in WEIGHTS])
```

```python
import functools
import math

import jax
import jax.numpy as jnp
from jax import lax
from jax.experimental import pallas as pl
from jax.experimental.pallas import tpu as pltpu

F32 = jnp.float32
BF16 = jnp.bfloat16

NOPE_DIM = 128
ROPE_DIM = 64
V_DIM = 128
LANES = 128
SUBLANES_BF16 = 16
ROPE_THETA = 10000.0
EPS = 1e-6
POOL_WINDOWS = (2, 4, 8, 16)
POOL_HALO = 16
CONV_TAPS = 3
CONV_HALO = 8
ADAM_LR = 0.001
ADAM_B1 = 0.9
ADAM_B2 = 0.999
ADAM_EPS = 1e-08
ADAM_WD = 0.01
ADAM_STEP = 10
NEG_BIG = -1e30
ATT_SCALE = 1.0 / math.sqrt(NOPE_DIM + ROPE_DIM)
V7X_VMEM_BYTES = 64 * 2 ** 20
N_SHARDS = 4
PACK_ALIGN = 2 * SUBLANES_BF16 * LANES

TILES = dict(row=256, att=512, mm_m=1024, mm_n=1024, mm_k=2048, ffn_row=512, ffn_c=512, add_bytes=3 << 20)

BIG = ("w_in", "w_out", "w_up", "w_down", "w_ple_gate")
MISC = ("w_uq", "w_ukv", "w_ple", "pool_w")
COL_SHARDED = ("w_in", "w_uq", "w_ukv", "w_up", "conv_w", "w_ple")
ROW_SHARDED = ("w_out", "w_down", "w_ple_gate")
WEIGHTS = ("norm_mix_g", "w_in", "pool_w", "pool_scale", "q_norm_g", "w_uq", "kv_norm_g", "w_ukv", "w_out",
           "norm_ffn_g", "w_up", "conv_w", "conv_b", "w_down", "norm_ple_g", "w_ple", "w_ple_gate", "final_norm_g")
MESH = pl.DeviceIdType.MESH


def _nbytes(shape, dtype):
    return math.prod(shape) * jnp.dtype(dtype).itemsize


def _params(sem, need_bytes):
    limit = min(V7X_VMEM_BYTES - (8 << 20), max(32 << 20, int(need_bytes)))
    return pltpu.CompilerParams(dimension_semantics=sem, vmem_limit_bytes=limit)


def _tile(n, want, mult=8):
    if n <= want:
        return n
    for t in range(want - want % mult, 0, -mult):
        if n % t == 0:
            return t
    return n


def _sigmoid(x):
    return 1.0 / (1.0 + jnp.exp(-x))


def _rstd(x):
    return lax.rsqrt(jnp.mean(x * x, axis=-1, keepdims=True) + EPS)


_DOT_DIMS = {"nn": (((1,), (0,)), ((), ())), "nt": (((1,), (1,)), ((), ())), "tn": (((0,), (0,)), ((), ()))}


def _matmul(a, b, mode, out_dtype, name, res=None, tm=None, tn=None, tk=None, b_shards=None, out_shards=None, carry=None,
            side=None):
    if mode == "nn":
        (M, K), N = a.shape, b.shape[-1] * (b_shards[1] if b_shards else 1)
    elif mode == "nt":
        (M, K), N = a.shape, b.shape[-2]
    else:
        (K, M), N = a.shape, b.shape[1]
    per = b.shape[-1] if b_shards else (out_shards[2] if out_shards else None)
    tm = _tile(M, tm or TILES["mm_m"], LANES)
    tn = _tile(per if (per and mode != "nt") else N, tn or TILES["mm_n"], LANES)
    tk = _tile(per if (per and mode == "nt") else K, tk or TILES["mm_k"], LANES)
    nk = K // tk
    has_res = res is not None
    has_carry = carry is not None
    dims = _DOT_DIMS[mode]

    def body(*refs):
        a_ref, b_ref = refs[0], refs[1]
        o_ref = refs[2 + has_res + has_carry]
        part = lax.dot_general(a_ref[...].astype(BF16), b_ref[...].astype(BF16), dims, preferred_element_type=F32)

        def finish(acc):
            if has_res:
                acc = acc + refs[2][...]
            o_ref[...] = acc.astype(o_ref.dtype)

        if nk == 1:
            finish(part)
        else:
            acc_ref = refs[3 + has_res + has_carry]
            k = pl.program_id(2)

            @pl.when(k == 0)
            def _():
                acc_ref[...] = part

            @pl.when(k > 0)
            def _():
                acc_ref[...] += part

            @pl.when(k == nk - 1)
            def _():
                finish(acc_ref[...])

    if mode == "nn":
        a_spec, b_spec = pl.BlockSpec((tm, tk), lambda i, j, k: (i, k)), pl.BlockSpec((tk, tn), lambda i, j, k: (k, j))
    elif mode == "nt":
        a_spec, b_spec = pl.BlockSpec((tm, tk), lambda i, j, k: (i, k)), pl.BlockSpec((tn, tk), lambda i, j, k: (j, k))
    else:
        a_spec, b_spec = pl.BlockSpec((tk, tm), lambda i, j, k: (k, i)), pl.BlockSpec((tk, tn), lambda i, j, k: (k, j))
    o_spec = pl.BlockSpec((tm, tn), lambda i, j, k: (i, j))
    out_shape = jax.ShapeDtypeStruct((M, N), out_dtype)
    if b_shards:
        first = b_shards[0]
        if mode == "nn":
            nps = per // tn
            b_spec = pl.BlockSpec((None, tk, tn), lambda i, j, k: (first + j // nps, k, j % nps))
        else:
            kps = per // tk
            b_spec = pl.BlockSpec((None, tn, tk), lambda i, j, k: (first + k // kps, j, k % kps))
    if out_shards:
        ofirst, nps_o = out_shards[0], per // tn
        o_spec_out = pl.BlockSpec((None, tm, tn), lambda i, j, k: (ofirst + j // nps_o, i, j % nps_o))
        out_shape = jax.ShapeDtypeStruct((out_shards[1], M, per), out_dtype)
    else:
        o_spec_out = o_spec
    in_specs, args = [a_spec, b_spec], [a, b]
    need = 2 * (_nbytes((tm, tk), a.dtype) + _nbytes((tk, tn), b.dtype) + _nbytes((tm, tn), out_dtype)) + 2 * _nbytes((tm, tn), F32)
    if has_res:
        in_specs.append(o_spec)
        args.append(res)
        need += 2 * _nbytes((tm, tn), res.dtype)
    aliases = {}
    if has_carry:
        aliases = {len(args): 0}
        in_specs.append(pl.BlockSpec(memory_space=pl.ANY))
        args.append(carry)
    scratch = [pltpu.VMEM((tm, tn), F32)] if nk > 1 else []
    grid = (M // tm, N // tn, nk)
    params = _params(("parallel", "parallel", "arbitrary") if side is None else ("arbitrary",) * 3, need + (4 << 20))
    if side is None:
        return pl.pallas_call(body, name=name, grid=grid, in_specs=in_specs, out_specs=o_spec_out, out_shape=out_shape,
                              scratch_shapes=scratch, input_output_aliases=aliases, compiler_params=params)(*args)
    body, s_ins, s_in_specs, s_shapes, s_out_specs, s_sems, s_aliases = _attach(side, body, len(args), 1, grid)
    return pl.pallas_call(
        body, name=name, grid=grid, in_specs=in_specs + s_in_specs, out_specs=[o_spec_out] + s_out_specs,
        out_shape=[out_shape] + s_shapes, scratch_shapes=scratch + s_sems, input_output_aliases={**aliases, **s_aliases},
        compiler_params=params,
    )(*args, *s_ins)


def _rms_fwd(x, g, name):
    T, D = x.shape
    tt = _tile(T, TILES["row"])

    def body(x_ref, g_ref, o_ref):
        xv = x_ref[...]
        o_ref[...] = (xv * _rstd(xv) * g_ref[...]).astype(o_ref.dtype)

    row = pl.BlockSpec((tt, D), lambda i: (i, 0))
    return pl.pallas_call(
        body, name=name, grid=(T // tt,), in_specs=[row, pl.BlockSpec((1, D), lambda i: (0, 0))], out_specs=row,
        out_shape=jax.ShapeDtypeStruct((T, D), BF16), compiler_params=_params(("parallel",), 8 * _nbytes((tt, D), F32)),
    )(x, g.reshape(1, D))


def _rms_bwd(dn, x, g, dres, name):
    T, D = x.shape
    tt = _tile(T, TILES["row"])

    def body(dn_ref, x_ref, g_ref, dres_ref, dx_ref, dxb_ref, dg_ref):
        i = pl.program_id(0)
        xv = x_ref[...]
        r = _rstd(xv)
        xh = xv * r
        dnv = dn_ref[...].astype(F32)
        dxh = dnv * g_ref[...]
        tot = dres_ref[...] + r * (dxh - xh * jnp.mean(dxh * xh, axis=-1, keepdims=True))
        dx_ref[...] = tot
        dxb_ref[...] = tot.astype(BF16)
        part = jnp.sum(dnv * xh, axis=0, keepdims=True)

        @pl.when(i == 0)
        def _():
            dg_ref[...] = part

        @pl.when(i > 0)
        def _():
            dg_ref[...] += part

    row = pl.BlockSpec((tt, D), lambda i: (i, 0))
    vec = pl.BlockSpec((1, D), lambda i: (0, 0))
    return pl.pallas_call(
        body, name=name, grid=(T // tt,), in_specs=[row, row, vec, row], out_specs=[row, row, vec],
        out_shape=[jax.ShapeDtypeStruct((T, D), F32), jax.ShapeDtypeStruct((T, D), BF16), jax.ShapeDtypeStruct((1, D), F32)],
        compiler_params=_params(("arbitrary",), 16 * _nbytes((tt, D), F32)),
    )(dn, x, g.reshape(1, D), dres)


def _final_loss(h, target, g):
    T, D = h.shape
    tt = _tile(T, TILES["row"])

    def body(h_ref, t_ref, g_ref, dx_ref, dxb_ref, dg_ref, loss_ref):
        i = pl.program_id(0)
        xv = h_ref[...]
        r = _rstd(xv)
        xh = xv * r
        gv = g_ref[...]
        err = xh * gv - t_ref[...]
        lpart = 0.5 * jnp.sum(jnp.mean(err * err, axis=-1, keepdims=True), axis=0, keepdims=True)
        dy = err * (1.0 / D)
        dxh = dy * gv
        dx = r * (dxh - xh * jnp.mean(dxh * xh, axis=-1, keepdims=True))
        dx_ref[...] = dx
        dxb_ref[...] = dx.astype(BF16)
        gpart = jnp.sum(dy * xh, axis=0, keepdims=True)
        lrow = jnp.broadcast_to(lpart, (1, LANES))

        @pl.when(i == 0)
        def _():
            dg_ref[...] = gpart
            loss_ref[...] = lrow

        @pl.when(i > 0)
        def _():
            dg_ref[...] += gpart
            loss_ref[...] += lrow

    row = pl.BlockSpec((tt, D), lambda i: (i, 0))
    vec = pl.BlockSpec((1, D), lambda i: (0, 0))
    return pl.pallas_call(
        body, name="final_loss", grid=(T // tt,), in_specs=[row, row, vec],
        out_specs=[row, row, vec, pl.BlockSpec((1, LANES), lambda i: (0, 0))],
        out_shape=[jax.ShapeDtypeStruct((T, D), F32), jax.ShapeDtypeStruct((T, D), BF16),
                   jax.ShapeDtypeStruct((1, D), F32), jax.ShapeDtypeStruct((1, LANES), F32)],
        compiler_params=_params(("arbitrary",), 16 * _nbytes((tt, D), F32)),
    )(h, target, g.reshape(1, D))


def _rope_tables(pos_col, inv_lane):
    T = pos_col.shape[0]
    tt = _tile(T, TILES["row"])

    def body(p_ref, f_ref, c_ref, s1_ref, s2_ref):
        ang = p_ref[...] * f_ref[...]
        lane = lax.broadcasted_iota(jnp.int32, ang.shape, 1)
        half = ROPE_DIM // 2
        cs, sn = jnp.cos(ang), jnp.sin(ang)
        c_ref[...] = jnp.where(lane < ROPE_DIM, cs, 0.0)
        s1_ref[...] = jnp.where(lane < half, -sn, 0.0)
        s2_ref[...] = jnp.where((lane >= half) & (lane < ROPE_DIM), sn, 0.0)

    tab = pl.BlockSpec((tt, LANES), lambda i: (i, 0))
    shp = jax.ShapeDtypeStruct((T, LANES), F32)
    return pl.pallas_call(
        body, name="rope_tables", grid=(T // tt,),
        in_specs=[pl.BlockSpec((tt, 1), lambda i: (i, 0)), pl.BlockSpec((1, LANES), lambda i: (0, 0))],
        out_specs=[tab, tab, tab], out_shape=[shp, shp, shp],
        compiler_params=_params(("parallel",), 32 * _nbytes((tt, LANES), F32)),
    )(pos_col, inv_lane)


def _rope(x, c, s1, s2):
    return x * c + pltpu.roll(x, LANES - ROPE_DIM // 2, 1) * s1 + pltpu.roll(x, ROPE_DIM // 2, 1) * s2


def _rope_t(d, c, s1, s2):
    return d * c + pltpu.roll(d * s1, ROPE_DIM // 2, 1) + pltpu.roll(d * s2, LANES - ROPE_DIM // 2, 1)


def _window_sum(xe, w, forward):
    n = xe.shape[0]
    s, sh = xe, 1
    while sh < w:
        s = s + pltpu.roll(s, (n - sh) if forward else sh, 0)
        sh *= 2
    return s


def _post_u(u, gq, gkv, tabs, dims):
    T, Dp = u.shape
    P, QL, KL, C = dims["P"], dims["QL"], dims["KL"], dims["C"]
    tt = _tile(T, TILES["row"], POOL_HALO)
    hb = tt // POOL_HALO

    def body(u_ref, halo_ref, gq_ref, gkv_ref, c_ref, s1_ref, s2_ref, diff_ref, cq_ref, ckv_ref, kr_ref):
        i = pl.program_id(0)
        t = i * tt + lax.broadcasted_iota(jnp.int32, (tt, 1), 0)
        halo = jnp.where(i > 0, halo_ref[...], 0.0)
        for gi, w in enumerate(POOL_WINDOWS):
            cols = slice(gi * C, (gi + 1) * C)
            xg = u_ref[:, cols]
            s = _window_sum(jnp.concatenate([halo[:, cols], xg], axis=0), w, False)[POOL_HALO:]
            cnt = jnp.minimum(t + 1, w).astype(F32)
            diff_ref[:, cols] = (s / cnt - xg).astype(BF16)
        cq = u_ref[:, P:P + QL]
        cq_ref[...] = (cq * _rstd(cq) * gq_ref[...]).astype(BF16)
        ckv = u_ref[:, P + QL:P + QL + KL]
        ckv_ref[...] = (ckv * _rstd(ckv) * gkv_ref[...]).astype(BF16)
        kr_ref[...] = _rope(u_ref[:, P + QL + KL:], c_ref[...], s1_ref[...], s2_ref[...]).astype(BF16)

    def row(w):
        return pl.BlockSpec((tt, w), lambda i: (i, 0))

    def vec(w):
        return pl.BlockSpec((1, w), lambda i: (0, 0))

    return pl.pallas_call(
        body, name="post_u", grid=(T // tt,),
        in_specs=[row(Dp), pl.BlockSpec((POOL_HALO, P), lambda i: (jnp.maximum(i * hb - 1, 0), 0)),
                  vec(QL), vec(KL), row(LANES), row(LANES), row(LANES)],
        out_specs=[row(P), row(QL), row(KL), row(LANES)],
        out_shape=[jax.ShapeDtypeStruct((T, P), BF16), jax.ShapeDtypeStruct((T, QL), BF16),
                   jax.ShapeDtypeStruct((T, KL), BF16), jax.ShapeDtypeStruct((T, LANES), BF16)],
        compiler_params=_params(("parallel",), 10 * _nbytes((tt, Dp), F32)),
    )(u, u, gq.reshape(1, QL), gkv.reshape(1, KL), *tabs)


def _pre_u_bwd(u, d_cqn, d_ckvn, d_diff, dkr, gq, gkv, tabs, dims):
    T, Dp = u.shape
    P, QL, KL, C, H = dims["P"], dims["QL"], dims["KL"], dims["C"], dims["H"]
    tt = _tile(T, TILES["row"], POOL_HALO)
    hb = tt // POOL_HALO
    n_t = T // tt

    def norm_bwd(xv, dn, gv):
        r = _rstd(xv)
        xh = xv * r
        dxh = dn * gv
        return r * (dxh - xh * jnp.mean(dxh * xh, axis=-1, keepdims=True)), jnp.sum(dn * xh, axis=0, keepdims=True)

    def body(u_ref, dcq_ref, dckv_ref, dd_ref, ddn_ref, dkr_ref, gq_ref, gkv_ref, c_ref, s1_ref, s2_ref,
             du_ref, dgq_ref, dgkv_ref):
        i = pl.program_id(0)
        t = i * tt + lax.broadcasted_iota(jnp.int32, (tt, 1), 0)
        nxt = jnp.where(i < n_t - 1, ddn_ref[...].astype(F32), 0.0)
        for gi, w in enumerate(POOL_WINDOWS):
            cols = slice(gi * C, (gi + 1) * C)
            dd = dd_ref[:, cols].astype(F32)
            e = dd / jnp.minimum(t + 1, w).astype(F32)
            s = _window_sum(jnp.concatenate([e, nxt[:, cols] / float(w)], axis=0), w, True)[:tt]
            du_ref[:, cols] = (s - dd).astype(BF16)
        dq, pq = norm_bwd(u_ref[:, P:P + QL], dcq_ref[...], gq_ref[...])
        du_ref[:, P:P + QL] = dq.astype(BF16)
        dkv, pkv = norm_bwd(u_ref[:, P + QL:P + QL + KL], dckv_ref[...], gkv_ref[...])
        du_ref[:, P + QL:P + QL + KL] = dkv.astype(BF16)
        dk = dkr_ref[0]
        for hh in range(1, H):
            dk = dk + dkr_ref[hh]
        du_ref[:, P + QL + KL:] = _rope_t(dk, c_ref[...], s1_ref[...], s2_ref[...]).astype(BF16)

        @pl.when(i == 0)
        def _():
            dgq_ref[...] = pq
            dgkv_ref[...] = pkv

        @pl.when(i > 0)
        def _():
            dgq_ref[...] += pq
            dgkv_ref[...] += pkv

    def row(w):
        return pl.BlockSpec((tt, w), lambda i: (i, 0))

    def vec(w):
        return pl.BlockSpec((1, w), lambda i: (0, 0))

    return pl.pallas_call(
        body, name="pre_u_bwd", grid=(n_t,),
        in_specs=[row(Dp), row(QL), row(KL), row(P),
                  pl.BlockSpec((POOL_HALO, P), lambda i: (jnp.minimum((i + 1) * hb, T // POOL_HALO - 1), 0)),
                  pl.BlockSpec((H, tt, LANES), lambda i: (0, i, 0)), vec(QL), vec(KL), row(LANES), row(LANES), row(LANES)],
        out_specs=[row(Dp), vec(QL), vec(KL)],
        out_shape=[jax.ShapeDtypeStruct((T, Dp), BF16), jax.ShapeDtypeStruct((1, QL), F32), jax.ShapeDtypeStruct((1, KL), F32)],
        compiler_params=_params(("arbitrary",), 12 * _nbytes((tt, Dp), F32)),
    )(u, d_cqn, d_ckvn, d_diff, d_diff, dkr, gq.reshape(1, QL), gkv.reshape(1, KL), *tabs)


def _pool_fwd(diff, pw, ps, dims):
    T, P = diff.shape
    G, C = len(POOL_WINDOWS), dims["C"]
    tt = _tile(T, TILES["row"])

    def body(d_ref, w_ref, s_ref, o_ref):
        for gi in range(G):
            cols = slice(gi * C, (gi + 1) * C)
            y = jnp.dot(d_ref[:, cols], w_ref[gi], preferred_element_type=F32)
            o_ref[:, cols] = (y * s_ref[:, cols]).astype(BF16)

    row = pl.BlockSpec((tt, P), lambda i: (i, 0))
    return pl.pallas_call(
        body, name="pool_fwd", grid=(T // tt,),
        in_specs=[row, pl.BlockSpec((G, C, C), lambda i: (0, 0, 0)), pl.BlockSpec((1, P), lambda i: (0, 0))],
        out_specs=row, out_shape=jax.ShapeDtypeStruct((T, P), BF16),
        compiler_params=_params(("parallel",), 8 * _nbytes((tt, P), F32)),
    )(diff, pw, ps.reshape(1, P))


def _pool_bwd(dmix, diff, pw, ps, dims):
    T, P = diff.shape
    G, C = len(POOL_WINDOWS), dims["C"]
    tt = _tile(T, TILES["row"])

    def body(dy_ref, d_ref, w_ref, s_ref, dd_ref, dw_ref, ds_ref):
        i = pl.program_id(0)

        @pl.when(i == 0)
        def _():
            dw_ref[...] = jnp.zeros_like(dw_ref)
            ds_ref[...] = jnp.zeros_like(ds_ref)

        for gi in range(G):
            cols = slice(gi * C, (gi + 1) * C)
            dy = dy_ref[:, cols].astype(F32)
            d = d_ref[:, cols]
            w = w_ref[gi]
            ypre = jnp.dot(d, w, preferred_element_type=F32)
            ds_ref[:, cols] += jnp.sum(dy * ypre, axis=0, keepdims=True)
            dyp = (dy * s_ref[:, cols]).astype(BF16)
            dd_ref[:, cols] = lax.dot_general(dyp, w, _DOT_DIMS["nt"], preferred_element_type=F32).astype(BF16)
            dw_ref[gi] += lax.dot_general(d, dyp, _DOT_DIMS["tn"], preferred_element_type=F32)

    row = pl.BlockSpec((tt, P), lambda i: (i, 0))
    wsp = pl.BlockSpec((G, C, C), lambda i: (0, 0, 0))
    vec = pl.BlockSpec((1, P), lambda i: (0, 0))
    return pl.pallas_call(
        body, name="pool_bwd", grid=(T // tt,), in_specs=[row, row, wsp, vec], out_specs=[row, wsp, vec],
        out_shape=[jax.ShapeDtypeStruct((T, P), BF16), jax.ShapeDtypeStruct((G, C, C), F32), jax.ShapeDtypeStruct((1, P), F32)],
        compiler_params=_params(("arbitrary",), 10 * _nbytes((tt, P), F32)),
    )(dmix, diff, pw, ps.reshape(1, P))


def _q_rope(qp, tabs, dims):
    T, W = qp.shape
    H = dims["H"]
    tt = _tile(T, TILES["row"])

    def body(q_ref, c_ref, s1_ref, s2_ref, o_ref):
        o_ref[:, :H * LANES] = (q_ref[:, :H * LANES] * ATT_SCALE).astype(BF16)
        c, s1, s2 = c_ref[...], s1_ref[...], s2_ref[...]
        for hh in range(H, 2 * H):
            cols = slice(hh * LANES, (hh + 1) * LANES)
            o_ref[:, cols] = _rope(q_ref[:, cols] * ATT_SCALE, c, s1, s2).astype(BF16)

    row = pl.BlockSpec((tt, W), lambda i: (i, 0))
    tab = pl.BlockSpec((tt, LANES), lambda i: (i, 0))
    return pl.pallas_call(
        body, name="q_rope", grid=(T // tt,), in_specs=[row, tab, tab, tab], out_specs=row,
        out_shape=jax.ShapeDtypeStruct((T, W), BF16), compiler_params=_params(("parallel",), 8 * _nbytes((tt, W), F32)),
    )(qp, *tabs)


def _scores(qn_ref, qr_ref, kn_ref, kr_ref, t, diagonal):
    q = jnp.concatenate([qn_ref[...], qr_ref[...]], axis=1)
    k = jnp.concatenate([kn_ref[...], kr_ref[...]], axis=1)
    s = lax.dot_general(q, k, _DOT_DIMS["nt"], preferred_element_type=F32)
    if diagonal:
        s = jnp.where(lax.broadcasted_iota(jnp.int32, (t, t), 0) >= lax.broadcasted_iota(jnp.int32, (t, t), 1), s, NEG_BIG)
    return q, k, s


class _Side:
    def __init__(self, ins, out_shapes, n_sems, start, finish, aliases=None):
        self.ins, self.out_shapes, self.n_sems, self.start, self.finish = list(ins), list(out_shapes), n_sems, start, finish
        self.aliases = dict(aliases or {})


def _attach(side, body, n_in, n_out, grid):
    if side is None:
        return body, [], [], [], [], [], {}
    n_si, n_so = len(side.ins), len(side.out_shapes)

    def carrying(*refs):
        outs_at = n_in + n_si
        main = refs[:n_in] + refs[outs_at:outs_at + n_out] + refs[outs_at + n_out + n_so:len(refs) - 2]
        parts = (refs[n_in:outs_at], refs[outs_at + n_out:outs_at + n_out + n_so], refs[-2], refs[-1])
        ids = [pl.program_id(d) for d in range(len(grid))]
        first = functools.reduce(lambda u, v: u & v, [i == 0 for i in ids])
        last = functools.reduce(lambda u, v: u & v, [i == g - 1 for i, g in zip(ids, grid)])

        @pl.when(first)
        def _():
            side.start(*parts)

        body(*main)

        @pl.when(last)
        def _():
            side.finish(*parts)

    aliases = {n_in + i: n_out + o for i, o in side.aliases.items()}
    return carrying, side.ins, [_HBM] * n_si, side.out_shapes, [_HBM] * n_so, _sem_pair(side.n_sems), aliases


def _pairs(n, by_query):
    pairs = [(i, j) for i in range(n) for j in range(i + 1)] if by_query else [(i, j) for j in range(n) for i in range(j, n)]
    return jnp.array([p[0] for p in pairs], jnp.int32), jnp.array([p[1] for p in pairs], jnp.int32), len(pairs)


def _flash_fwd(q_att, kv, kr, dims, side=None):
    T = q_att.shape[0]
    H = dims["H"]
    G = 2 if H % 2 == 0 else 1
    t = _tile(T, TILES["att"])
    n = T // t
    it, jt, n_pairs = _pairs(n, True)
    hb = H // G

    def body(it_ref, jt_ref, qn_ref, qr_ref, kn_ref, v_ref, kr_ref, o_ref, lse_ref, m_ref, l_ref, acc_ref):
        step_id = pl.program_id(1)
        i, j = it_ref[step_id], jt_ref[step_id]

        @pl.when(j == 0)
        def _():
            m_ref[...] = jnp.full_like(m_ref, NEG_BIG)
            l_ref[...] = jnp.zeros_like(l_ref)
            acc_ref[...] = jnp.zeros_like(acc_ref)

        def step(diagonal):
            for g in range(G):
                cols = slice(g * LANES, (g + 1) * LANES)
                _, _, s = _scores(qn_ref.at[:, cols], qr_ref.at[:, cols], kn_ref.at[:, cols], kr_ref, t, diagonal)
                m_prev = m_ref[:, cols]
                m_new = jnp.maximum(m_prev, jnp.max(s, axis=1, keepdims=True))
                alpha = jnp.exp(m_prev - m_new)
                p = jnp.exp(s - m_new[:, :1])
                l_ref[:, cols] = alpha * l_ref[:, cols] + jnp.sum(p, axis=1, keepdims=True)
                acc_ref[:, cols] = alpha * acc_ref[:, cols] + jnp.dot(p.astype(BF16), v_ref[:, cols], preferred_element_type=F32)
                m_ref[:, cols] = m_new

        @pl.when(j < i)
        def _():
            step(False)

        @pl.when(j == i)
        def _():
            step(True)
            o_ref[...] = (acc_ref[...] / l_ref[...]).astype(BF16)
            lse_ref[...] = m_ref[...] + jnp.log(l_ref[...])

    blk = (t, G * LANES)
    grid = (hb, n_pairs)
    body, s_ins, s_in_specs, s_shapes, s_out_specs, s_sems, aliases = _attach(side, body, 7, 2, grid)
    return pl.pallas_call(
        body, name="flash_fwd",
        grid_spec=pltpu.PrefetchScalarGridSpec(
            num_scalar_prefetch=2, grid=grid,
            in_specs=[pl.BlockSpec(blk, lambda h, s, it, jt: (it[s], h)), pl.BlockSpec(blk, lambda h, s, it, jt: (it[s], hb + h)),
                      pl.BlockSpec(blk, lambda h, s, it, jt: (jt[s], h)), pl.BlockSpec(blk, lambda h, s, it, jt: (jt[s], hb + h)),
                      pl.BlockSpec((t, LANES), lambda h, s, it, jt: (jt[s], 0))] + s_in_specs,
            out_specs=[pl.BlockSpec(blk, lambda h, s, it, jt: (it[s], h)), pl.BlockSpec(blk, lambda h, s, it, jt: (it[s], h))] + s_out_specs,
            scratch_shapes=[pltpu.VMEM(blk, F32), pltpu.VMEM(blk, F32), pltpu.VMEM(blk, F32), *s_sems]),
        out_shape=[jax.ShapeDtypeStruct((T, H * LANES), BF16), jax.ShapeDtypeStruct((T, H * LANES), F32)] + s_shapes,
        input_output_aliases=aliases,
        compiler_params=_params(("arbitrary", "arbitrary"), 8 * G * _nbytes((t, t), F32) + (8 << 20)),
    )(it, jt, q_att, q_att, kv, kv, kr, *s_ins)


def _flash_bwd(q_att, kv, kr, o, lse, dmix, dims, side=None):
    T = q_att.shape[0]
    H = dims["H"]
    G = 2 if H % 2 == 0 else 1
    ob = dims["P"] // LANES // G
    t = _tile(T, TILES["att"])
    n = T // t
    it, jt, n_pairs = _pairs(n, False)
    hb = H // G

    def body(it_ref, jt_ref, qn_ref, qr_ref, kn_ref, v_ref, kr_ref, o_ref, lse_ref, do_ref,
             dq_ref, dkn_ref, dv_ref, dkr_ref, dk_acc, dv_acc):
        step_id = pl.program_id(1)
        i, j = it_ref[step_id], jt_ref[step_id]

        @pl.when(step_id == 0)
        def _():
            dq_ref[...] = jnp.zeros_like(dq_ref)

        @pl.when(i == j)
        def _():
            dk_acc[...] = jnp.zeros_like(dk_acc)
            dv_acc[...] = jnp.zeros_like(dv_acc)

        def step(diagonal):
            rows = pl.ds(pl.multiple_of(i * t, t), t)
            for g in range(G):
                cols = slice(g * LANES, (g + 1) * LANES)
                q, k, s = _scores(qn_ref.at[:, cols], qr_ref.at[:, cols], kn_ref.at[:, cols], kr_ref, t, diagonal)
                p = jnp.exp(s - lse_ref[:, g * LANES:g * LANES + 1])
                do = do_ref[:, cols]
                delta = jnp.sum(do.astype(F32) * o_ref[:, cols].astype(F32), axis=1, keepdims=True)
                dv_acc[:, cols] += lax.dot_general(p.astype(BF16), do, _DOT_DIMS["tn"], preferred_element_type=F32)
                dp = lax.dot_general(do, v_ref[:, cols], _DOT_DIMS["nt"], preferred_element_type=F32)
                ds = (p * (dp - delta)).astype(BF16)
                dk_acc[g] += lax.dot_general(ds, q, _DOT_DIMS["tn"], preferred_element_type=F32)
                dq_ref[g, rows, :] += jnp.dot(ds, k, preferred_element_type=F32)

        @pl.when(i > j)
        def _():
            step(False)

        @pl.when(i == j)
        def _():
            step(True)

        @pl.when(i == n - 1)
        def _():
            for g in range(G):
                dkn_ref[:, g * LANES:(g + 1) * LANES] = dk_acc[g, :, :LANES].astype(BF16)
                dkr_ref[g] = dk_acc[g, :, LANES:]
            dv_ref[...] = dv_acc[...].astype(BF16)

    blk = (t, G * LANES)
    grid = (hb, n_pairs)
    body, s_ins, s_in_specs, s_shapes, s_out_specs, s_sems, aliases = _attach(side, body, 10, 4, grid)
    return pl.pallas_call(
        body, name="flash_bwd", input_output_aliases=aliases,
        grid_spec=pltpu.PrefetchScalarGridSpec(
            num_scalar_prefetch=2, grid=grid,
            in_specs=[pl.BlockSpec(blk, lambda h, s, it, jt: (it[s], h)), pl.BlockSpec(blk, lambda h, s, it, jt: (it[s], hb + h)),
                      pl.BlockSpec(blk, lambda h, s, it, jt: (jt[s], h)), pl.BlockSpec(blk, lambda h, s, it, jt: (jt[s], hb + h)),
                      pl.BlockSpec((t, LANES), lambda h, s, it, jt: (jt[s], 0)),
                      pl.BlockSpec(blk, lambda h, s, it, jt: (it[s], h)), pl.BlockSpec(blk, lambda h, s, it, jt: (it[s], h)),
                      pl.BlockSpec(blk, lambda h, s, it, jt: (it[s], ob + h))] + s_in_specs,
            out_specs=[pl.BlockSpec((G, T, 2 * LANES), lambda h, s, it, jt: (h, 0, 0)),
                       pl.BlockSpec(blk, lambda h, s, it, jt: (jt[s], h)), pl.BlockSpec(blk, lambda h, s, it, jt: (jt[s], h)),
                       pl.BlockSpec((G, t, LANES), lambda h, s, it, jt: (h, jt[s], 0))] + s_out_specs,
            scratch_shapes=[pltpu.VMEM((G, t, 2 * LANES), F32), pltpu.VMEM(blk, F32), *s_sems]),
        out_shape=[jax.ShapeDtypeStruct((H, T, 2 * LANES), F32), jax.ShapeDtypeStruct((T, H * LANES), BF16),
                   jax.ShapeDtypeStruct((T, H * LANES), BF16), jax.ShapeDtypeStruct((H, T, LANES), F32)] + s_shapes,
        compiler_params=_params(("arbitrary", "arbitrary"),
                                12 * G * _nbytes((t, t), F32) + 2 * G * _nbytes((T, 2 * LANES), F32) + (8 << 20)),
    )(it, jt, q_att, q_att, kv, kv, kr, o, lse, dmix, *s_ins)


def _dq_post(dq, tabs, dims):
    H, T, _ = dq.shape
    tt = _tile(T, TILES["row"])

    def body(dq_ref, c_ref, s1_ref, s2_ref, o_ref):
        c, s1, s2 = c_ref[...], s1_ref[...], s2_ref[...]
        for hh in range(H):
            o_ref[:, hh * LANES:(hh + 1) * LANES] = (dq_ref[hh, :, :LANES] * ATT_SCALE).astype(BF16)
            o_ref[:, (H + hh) * LANES:(H + hh + 1) * LANES] = _rope_t(dq_ref[hh, :, LANES:] * ATT_SCALE, c, s1, s2).astype(BF16)

    tab = pl.BlockSpec((tt, LANES), lambda i: (i, 0))
    return pl.pallas_call(
        body, name="dq_post", grid=(T // tt,),
        in_specs=[pl.BlockSpec((H, tt, 2 * LANES), lambda i: (0, i, 0)), tab, tab, tab],
        out_specs=pl.BlockSpec((tt, 2 * H * LANES), lambda i: (i, 0)),
        out_shape=jax.ShapeDtypeStruct((T, 2 * H * LANES), BF16),
        compiler_params=_params(("parallel",), 8 * _nbytes((tt, 2 * H * LANES), F32)),
    )(dq, *tabs)


def _conv3(ge, cw, n):
    return cw[2:3] * ge + cw[1:2] * pltpu.roll(ge, 1, 0) + cw[0:1] * pltpu.roll(ge, 2, 0) + cw[3:4]


def _ffn_fwd(gate, up, cw8):
    T, F = gate.shape
    tt = _tile(T, TILES["ffn_row"])
    tc = _tile(F, TILES["ffn_c"], LANES)
    hb = tt // CONV_HALO

    def body(g_ref, gp_ref, u_ref, cw_ref, a_ref):
        it = pl.program_id(1)
        prev = jnp.where(it > 0, gp_ref[...].astype(F32), 0.0)
        ge = jnp.concatenate([prev, g_ref[...].astype(F32)], axis=0)
        gc = _conv3(ge, cw_ref[...], tt + CONV_HALO)[CONV_HALO:]
        a_ref[...] = (gc * _sigmoid(gc) * u_ref[...].astype(F32)).astype(BF16)

    blk = pl.BlockSpec((tt, tc), lambda jc, it: (it, jc))
    return pl.pallas_call(
        body, name="ffn_fwd", grid=(F // tc, T // tt),
        in_specs=[blk, pl.BlockSpec((CONV_HALO, tc), lambda jc, it: (jnp.maximum(it * hb - 1, 0), jc)), blk,
                  pl.BlockSpec((8, tc), lambda jc, it: (0, jc))],
        out_specs=blk, out_shape=jax.ShapeDtypeStruct((T, F), BF16),
        compiler_params=_params(("parallel", "parallel"), 16 * _nbytes((tt, tc), F32)),
    )(gate, gate, up, cw8)


def _ffn_bwd(da, gate, up, cw8):
    T, F = gate.shape
    tt = _tile(T, TILES["ffn_row"])
    tc = _tile(F, TILES["ffn_c"], LANES)
    hb = tt // CONV_HALO
    n_t = T // tt
    n = tt + 2 * CONV_HALO

    def body(da_ref, dan_ref, g_ref, gp_ref, gn_ref, u_ref, un_ref, cw_ref, dg_ref, du_ref, dcw_ref):
        it = pl.program_id(1)
        first, last = it == 0, it == n_t - 1
        cw = cw_ref[...]
        zeros = jnp.zeros((CONV_HALO, tc), F32)
        ge = jnp.concatenate([jnp.where(first, 0.0, gp_ref[...].astype(F32)), g_ref[...].astype(F32),
                              gn_ref[...].astype(F32)], axis=0)
        dae = jnp.concatenate([zeros, da_ref[...].astype(F32), jnp.where(last, 0.0, dan_ref[...].astype(F32))], axis=0)
        ue = jnp.concatenate([zeros, u_ref[...].astype(F32), un_ref[...].astype(F32)], axis=0)
        g1, g2 = pltpu.roll(ge, 1, 0), pltpu.roll(ge, 2, 0)
        gc = cw[2:3] * ge + cw[1:2] * g1 + cw[0:1] * g2 + cw[3:4]
        sg = _sigmoid(gc)
        dgc = dae * ue * (sg * (1.0 + gc * (1.0 - sg)))
        du_ref[...] = (dae * gc * sg)[CONV_HALO:CONV_HALO + tt].astype(BF16)
        dgp = cw[2:3] * dgc + cw[1:2] * pltpu.roll(dgc, n - 1, 0) + cw[0:1] * pltpu.roll(dgc, n - 2, 0)
        dg_ref[...] = dgp[CONV_HALO:CONV_HALO + tt].astype(BF16)
        mid = slice(CONV_HALO, CONV_HALO + tt)
        d_mid = dgc[mid]
        part = jnp.concatenate([jnp.sum(d_mid * g2[mid], axis=0, keepdims=True), jnp.sum(d_mid * g1[mid], axis=0, keepdims=True),
                                jnp.sum(d_mid * ge[mid], axis=0, keepdims=True), jnp.sum(d_mid, axis=0, keepdims=True),
                                jnp.zeros((4, tc), F32)], axis=0)

        @pl.when(first)
        def _():
            dcw_ref[...] = part

        @pl.when(it > 0)
        def _():
            dcw_ref[...] += part

    blk = pl.BlockSpec((tt, tc), lambda jc, it: (it, jc))
    prv = pl.BlockSpec((CONV_HALO, tc), lambda jc, it: (jnp.maximum(it * hb - 1, 0), jc))
    nxt = pl.BlockSpec((CONV_HALO, tc), lambda jc, it: (jnp.minimum((it + 1) * hb, T // CONV_HALO - 1), jc))
    cws = pl.BlockSpec((8, tc), lambda jc, it: (0, jc))
    return pl.pallas_call(
        body, name="ffn_bwd", grid=(F // tc, n_t), in_specs=[blk, nxt, blk, prv, nxt, blk, nxt, cws],
        out_specs=[blk, blk, cws],
        out_shape=[jax.ShapeDtypeStruct((T, F), BF16), jax.ShapeDtypeStruct((T, F), BF16), jax.ShapeDtypeStruct((8, F), F32)],
        compiler_params=_params(("parallel", "arbitrary"), 32 * _nbytes((tt, tc), F32)),
    )(da, da, gate, gate, gate, up, up, cw8)


def _ple_fwd(h2, gl, pe):
    T, D = h2.shape
    tt = _tile(T, TILES["row"])

    def body(h_ref, gl_ref, pe_ref, o_ref):
        o_ref[...] = h_ref[...] + pe_ref[...] * _sigmoid(gl_ref[...])

    row = pl.BlockSpec((tt, D), lambda i: (i, 0))
    return pl.pallas_call(
        body, name="ple_fwd", grid=(T // tt,), in_specs=[row, row, row], out_specs=row,
        out_shape=jax.ShapeDtypeStruct((T, D), F32), compiler_params=_params(("parallel",), 12 * _nbytes((tt, D), F32)),
    )(h2, gl, pe)


def _ple_bwd(dh, gl, pe):
    T, D = dh.shape
    tt = _tile(T, TILES["row"])

    def body(dh_ref, gl_ref, pe_ref, dpe_ref, dgl_ref):
        d = dh_ref[...]
        sg = _sigmoid(gl_ref[...])
        dpe_ref[...] = (d * sg).astype(BF16)
        dgl_ref[...] = (d * pe_ref[...] * (sg * (1.0 - sg))).astype(BF16)

    row = pl.BlockSpec((tt, D), lambda i: (i, 0))
    return pl.pallas_call(
        body, name="ple_bwd", grid=(T // tt,), in_specs=[row, row, row], out_specs=[row, row],
        out_shape=[jax.ShapeDtypeStruct((T, D), BF16), jax.ShapeDtypeStruct((T, D), BF16)],
        compiler_params=_params(("parallel",), 12 * _nbytes((tt, D), F32)),
    )(dh, gl, pe)


def _adamw(w, g, m, v, name):
    shape = w.shape
    cols = shape[-1]
    rows = math.prod(shape[:-1]) if len(shape) > 1 else 1
    w2, g2, m2, v2 = (a.reshape(rows, cols) for a in (w, g, m, v))
    tr = _tile(rows, max(8, (1 << 20) // (cols * 4)))
    c1 = 1.0 - ADAM_B1 ** ADAM_STEP
    c2 = 1.0 - ADAM_B2 ** ADAM_STEP

    def body(w_ref, g_ref, m_ref, v_ref, d_ref, mo_ref, vo_ref):
        gv = g_ref[...]
        mn = ADAM_B1 * m_ref[...] + (1.0 - ADAM_B1) * gv
        vn = ADAM_B2 * v_ref[...] + (1.0 - ADAM_B2) * (gv * gv)
        mo_ref[...] = mn
        vo_ref[...] = vn
        d_ref[...] = -ADAM_LR * ((mn / c1) / (jnp.sqrt(vn / c2) + ADAM_EPS) + ADAM_WD * w_ref[...])

    blk = pl.BlockSpec((tr, cols), lambda i: (i, 0))
    shp = jax.ShapeDtypeStruct((rows, cols), F32)
    outs = pl.pallas_call(
        body, name=name, grid=(rows // tr,), in_specs=[blk] * 4, out_specs=[blk] * 3, out_shape=[shp] * 3,
        compiler_params=_params(("parallel",), 16 * _nbytes((tr, cols), F32)),
    )(w2, g2, m2, v2)
    return tuple(o.reshape(shape) for o in outs)


_HBM = pl.BlockSpec(memory_space=pltpu.HBM)


def _place():
    x, y, c = lax.axis_index("x"), lax.axis_index("y"), lax.axis_index("c")
    return x, y, c, [(1 - x, y), (x, 1 - y), (1 - x, 1 - y)]


def _remote(src, dst, send_sems, recv_sems, k, to):
    return pltpu.make_async_remote_copy(src_ref=src, dst_ref=dst, send_sem=send_sems.at[k], recv_sem=recv_sems.at[k],
                                        device_id=to, device_id_type=MESH)


def _half(ref, lead, h):
    hr = ref.shape[-2] // 2
    return ref.at[(*lead, pl.ds(pl.multiple_of(h * hr, SUBLANES_BF16), hr))]


def _sem_pair(n):
    return [pltpu.SemaphoreType.DMA((n,)), pltpu.SemaphoreType.DMA((n,))]


def _run_side(side, name):
    n_in, n_out = len(side.ins), len(side.out_shapes)

    def body(*refs):
        parts = (refs[:n_in], refs[n_in:n_in + n_out]) + tuple(refs[n_in + n_out:])
        side.start(*parts)
        side.finish(*parts)

    return pl.pallas_call(
        body, name=name, in_specs=[_HBM] * n_in, out_specs=[_HBM] * n_out, out_shape=side.out_shapes,
        scratch_shapes=_sem_pair(side.n_sems), input_output_aliases=side.aliases,
    )(*side.ins)


def _whole(arrs, halves):
    return [(a, 0, arr.shape[-2] // (2 if halves else 1)) for a, arr in enumerate(arrs)]


def _plan(arrs, halves, big):
    whole = _whole(arrs, halves)
    q = whole[big][2] // 4
    return [[pc for pc in whole if pc[0] != big] + [(big, 0, q)]] + [[(big, k * q, q)] for k in (1, 2, 3)]


def _plan_gather(arrs, big, small):
    whole = _whole(arrs, True)
    e = whole[big][2] // 8
    return [[pc for pc in whole if pc[0] not in (big, small)], [(big, 0, 3 * e)], [(big, 3 * e, 3 * e)],
            [(big, 6 * e, 2 * e), whole[small]]]


def _ride(fn, n_main, pieces, make, store):
    if pieces is None:
        return fn(None)
    side, touched = make(pieces)
    out = fn(side)
    store.update(zip(touched, out[n_main:]))
    return out[0] if n_main == 1 else out[:n_main]


def _carried(arrs, pieces, prior):
    touched = sorted({a for a, _, _ in pieces})
    pos = {a: i for i, a in enumerate(touched)}
    carried = [a for a in touched if a in prior]
    ins = [arrs[a] for a in touched] + [prior[a] for a in carried]
    return touched, pos, ins, {len(touched) + i: pos[a] for i, a in enumerate(carried)}


def _gather_side(arrs, layer, pieces, prior):
    touched, pos, ins_arrs, aliases = _carried(arrs, pieces, prior)

    def copies(ins, outs, send_sems, recv_sems, arriving):
        x, y, c, chips = _place()
        me, sib = 2 * x + y, (x, y, 1 - c)
        out = []
        for p, (a, r0, nr) in enumerate(pieces):
            src, dst = ins[pos[a]], outs[pos[a]]
            hr = src.shape[-2] // 2
            for hlf in range(2):
                rows = pl.ds(hlf * hr + r0, nr)
                out.append(_remote(src.at[layer, rows], dst.at[me, rows], send_sems, recv_sems, 5 * p + 3 + hlf, sib))
            rows = pl.ds(pl.multiple_of(c * hr + r0, SUBLANES_BF16), nr)
            for k, (cx, cy) in enumerate(chips):
                slot = 2 * cx + cy if arriving else me
                out.append(_remote(src.at[layer, rows], dst.at[slot, rows], send_sems, recv_sems, 5 * p + k, (cx, cy, c)))
        return out

    def start(ins, outs, send_sems, recv_sems):
        for cp in copies(ins, outs, send_sems, recv_sems, False):
            cp.start()

    def finish(ins, outs, send_sems, recv_sems):
        for cp in copies(ins, outs, send_sems, recv_sems, True):
            cp.wait_recv()
        for cp in copies(ins, outs, send_sems, recv_sems, False):
            cp.wait_send()

    shapes = [jax.ShapeDtypeStruct((N_SHARDS,) + arrs[a].shape[1:], arrs[a].dtype) for a in touched]
    return _Side(ins_arrs, shapes, 5 * len(pieces), start, finish, aliases), touched


def _forward_side(arrs):
    n = len(arrs)

    def copies(outs, send_sems, recv_sems, arriving):
        x, y, c, chips = _place()
        sib = (x, y, 1 - c)
        out = []
        for a in range(n):
            for k, (cx, cy) in enumerate(chips):
                got = _half(outs[a], (2 * cx + cy,), 1 - c if arriving else c)
                out.append(_remote(got, got, send_sems, recv_sems, 3 * a + k, sib))
        return out

    def start(ins, outs, send_sems, recv_sems):
        for cp in copies(outs, send_sems, recv_sems, False):
            cp.start()

    def finish(ins, outs, send_sems, recv_sems):
        for cp in copies(outs, send_sems, recv_sems, True):
            cp.wait_recv()
        for cp in copies(outs, send_sems, recv_sems, False):
            cp.wait_send()

    return _Side(arrs, [jax.ShapeDtypeStruct(a.shape, a.dtype) for a in arrs], 3 * n, start, finish, {a: a for a in range(n)})


def _sibling_side(arrs):
    n = len(arrs)

    def copies(ins, outs, send_sems, recv_sems):
        x, y, c, _ = _place()
        return [_remote(_half(ins[a], (s,), 1 - c), outs[a].at[s], send_sems, recv_sems, N_SHARDS * a + s, (x, y, 1 - c))
                for a in range(n) for s in range(N_SHARDS)]

    def start(ins, outs, send_sems, recv_sems):
        for cp in copies(ins, outs, send_sems, recv_sems):
            cp.start()

    def finish(ins, outs, send_sems, recv_sems):
        for cp in copies(ins, outs, send_sems, recv_sems):
            cp.wait_recv()
        for cp in copies(ins, outs, send_sems, recv_sems):
            cp.wait_send()

    shapes = [jax.ShapeDtypeStruct((N_SHARDS, a.shape[1] // 2, a.shape[2]), a.dtype) for a in arrs]
    return _Side(arrs, shapes, N_SHARDS * n, start, finish)


def _chip_side(arrs, pieces, prior):
    touched, pos, ins_arrs, aliases = _carried(arrs, pieces, prior)

    def copies(ins, outs, send_sems, recv_sems):
        x, y, c, chips = _place()
        return [_remote(ins[pos[a]].at[2 * cx + cy, pl.ds(r0, nr)], outs[pos[a]].at[k, pl.ds(r0, nr)], send_sems, recv_sems,
                        3 * p + k, (cx, cy, c))
                for p, (a, r0, nr) in enumerate(pieces) for k, (cx, cy) in enumerate(chips)]

    def start(ins, outs, send_sems, recv_sems):
        for cp in copies(ins, outs, send_sems, recv_sems):
            cp.start()

    def finish(ins, outs, send_sems, recv_sems):
        for cp in copies(ins, outs, send_sems, recv_sems):
            cp.wait_recv()
        for cp in copies(ins, outs, send_sems, recv_sems):
            cp.wait_send()

    shapes = [jax.ShapeDtypeStruct((3,) + arrs[a].shape[1:], arrs[a].dtype) for a in touched]
    return _Side(ins_arrs, shapes, 3 * len(pieces), start, finish, aliases), touched


def _share_side(arrs):
    n = len(arrs)

    def copies(outs, send_sems, recv_sems, arriving):
        x, y, c, _ = _place()
        return [_remote(outs[a].at[c], outs[a].at[1 - c if arriving else c], send_sems, recv_sems, a, (x, y, 1 - c)) for a in range(n)]

    def start(ins, outs, send_sems, recv_sems):
        for cp in copies(outs, send_sems, recv_sems, False):
            cp.start()

    def finish(ins, outs, send_sems, recv_sems):
        for cp in copies(outs, send_sems, recv_sems, True):
            cp.wait_recv()
        for cp in copies(outs, send_sems, recv_sems, False):
            cp.wait_send()

    return _Side(arrs, [jax.ShapeDtypeStruct(a.shape, a.dtype) for a in arrs], n, start, finish, {a: a for a in range(n)})


def _add_sibling(g, sib_in, place):
    S, rows, cols = g.shape
    hr = rows // 2
    tr = _tile(hr, max(SUBLANES_BF16, TILES["add_bytes"] // (cols * 2)), SUBLANES_BF16)
    nb = hr // tr

    def body(p_ref, a_ref, b_ref, o_ref):
        o_ref[...] = (a_ref[...].astype(F32) + b_ref[...].astype(F32)).astype(o_ref.dtype)

    blk = pl.BlockSpec((None, tr, cols), lambda s, r, p: (s, r, 0))
    return pl.pallas_call(
        body, name="rs_add_sibling",
        grid_spec=pltpu.PrefetchScalarGridSpec(
            num_scalar_prefetch=1, grid=(S, nb),
            in_specs=[pl.BlockSpec((None, tr, cols), lambda s, r, p: (s, p[1] * nb + r, 0)), blk], out_specs=blk),
        out_shape=jax.ShapeDtypeStruct((S, hr, cols), g.dtype),
        compiler_params=_params(("parallel", "parallel"), 16 * _nbytes((tr, cols), F32)),
    )(place, g, sib_in)


def _add_chips(cs, got, place):
    S, r, cols = cs.shape
    tr = _tile(r, max(SUBLANES_BF16, TILES["add_bytes"] // (cols * 2)), SUBLANES_BF16)

    def body(p_ref, a_ref, b_ref, o_ref):
        acc = a_ref[...].astype(F32)
        for k in range(3):
            acc = acc + b_ref[k].astype(F32)
        o_ref[...] = acc

    return pl.pallas_call(
        body, name="rs_add_chips",
        grid_spec=pltpu.PrefetchScalarGridSpec(
            num_scalar_prefetch=1, grid=(r // tr,),
            in_specs=[pl.BlockSpec((None, tr, cols), lambda i, p: (p[0], i, 0)),
                      pl.BlockSpec((3, tr, cols), lambda i, p: (0, i, 0))],
            out_specs=pl.BlockSpec((None, tr, cols), lambda i, p: (p[1], i, 0))),
        out_shape=jax.ShapeDtypeStruct((2, r, cols), F32),
        compiler_params=_params(("parallel",), 24 * _nbytes((tr, cols), F32)),
    )(place, cs, got)


def _reduce_begin(arrs, from_sibling, place):
    return [_add_sibling(g, s, place) for g, s in zip(arrs, from_sibling)]


def _reduce_end(sums, got, place):
    return _share_side([_add_chips(cs, g, place) for cs, g in zip(sums, got)])


def _shards(full):
    return [f.reshape(-1, f.shape[-1]) for f in full]


def _all_reduce_small(v):
    R = v.shape[0]

    def body(v_ref, o_ref, buf, send_sems, recv_sems):
        x, y, c, _ = _place()
        me = 4 * x + 2 * y + c
        buf[me] = v_ref[...]
        sends = []
        for k in range(1, 8):
            px = 1 - x if k & 4 else x
            py = 1 - y if k & 2 else y
            pc = 1 - c if k & 1 else c
            sends.append(_remote(v_ref, buf.at[me], send_sems, recv_sems, k - 1, (px, py, pc)))
        for cp in sends:
            cp.start()
        for k in range(1, 8):
            px = 1 - x if k & 4 else x
            py = 1 - y if k & 2 else y
            pc = 1 - c if k & 1 else c
            _remote(v_ref, buf.at[4 * px + 2 * py + pc], send_sems, recv_sems, k - 1, (px, py, pc)).wait_recv()
        for cp in sends:
            cp.wait_send()
        acc = buf[0]
        for d in range(1, 8):
            acc = acc + buf[d]
        o_ref[...] = acc

    vm = pl.BlockSpec(memory_space=pltpu.VMEM)
    return pl.pallas_call(
        body, name="all_reduce_small", in_specs=[vm], out_specs=vm, out_shape=jax.ShapeDtypeStruct(v.shape, F32),
        scratch_shapes=[pltpu.VMEM((8, R, LANES), F32), pltpu.SemaphoreType.DMA((7,)), pltpu.SemaphoreType.DMA((7,))],
    )(v)


def _pad_to(a, n):
    return a if a.shape[0] == n else jnp.pad(a, (0, n - a.shape[0]))


def _piece_len(shape):
    return -(-math.prod(shape) // PACK_ALIGN) * PACK_ALIGN


def _pack(pieces, dtype):
    flat = jnp.concatenate([_pad_to(a.reshape(-1).astype(dtype), _piece_len(a.shape)) for a in pieces])
    return flat.reshape(-1, LANES)


def _unpack(flat, shapes, lead):
    flat = flat.reshape(lead + (-1,))
    out, off = [], 0
    for shp in shapes:
        out.append(flat[..., off:off + math.prod(shp)].reshape(lead + tuple(shp)))
        off += _piece_len(shp)
    return out


def _join(name, a):
    if name in COL_SHARDED:
        return a.transpose(1, 0, 2).reshape(a.shape[1], -1)
    if name in ROW_SHARDED:
        return a.reshape(-1, a.shape[-1])
    return a.transpose(1, 0, 2, 3).reshape(a.shape[1], -1, a.shape[-1])


def _split(name, a):
    if name in COL_SHARDED:
        return a.reshape(a.shape[0], N_SHARDS, -1).transpose(1, 0, 2)
    if name in ROW_SHARDED:
        return a.reshape(N_SHARDS, -1, a.shape[-1])
    return a.reshape(a.shape[0], N_SHARDS, -1, a.shape[-1]).transpose(1, 0, 2, 3)


def _heads_split(w, H, first, second, pad_second):
    K = w.shape[0]
    w3 = w.reshape(K, H, first + second)
    b = w3[:, :, first:]
    if pad_second > second:
        b = jnp.pad(b, ((0, 0), (0, 0), (0, pad_second - second)))
    return jnp.concatenate([w3[:, :, :first].reshape(K, -1), b.reshape(K, -1)], axis=1)


def _heads_merge(w, H, first, second, pad_second):
    K = w.shape[0]
    a = w[:, :H * first].reshape(K, H, first)
    b = w[:, H * first:].reshape(K, H, pad_second)[:, :, :second]
    return jnp.concatenate([a, b], axis=2).reshape(K, -1)


def kernel(x, p, positions, norm_mix_g, w_in, pool_w, pool_scale, q_norm_g, w_uq, kv_norm_g, w_ukv, w_out, norm_ffn_g, w_up, conv_w, conv_b, w_down, norm_ple_g, w_ple, w_ple_gate, final_norm_g, loss_target, m_norm_mix_g, m_w_in, m_pool_w, m_pool_scale, m_q_norm_g, m_w_uq, m_kv_norm_g, m_w_ukv, m_w_out, m_norm_ffn_g, m_w_up, m_conv_w, m_conv_b, m_w_down, m_norm_ple_g, m_w_ple, m_w_ple_gate, m_final_norm_g, v_norm_mix_g, v_w_in, v_pool_w, v_pool_scale, v_q_norm_g, v_w_uq, v_kv_norm_g, v_w_ukv, v_w_out, v_norm_ffn_g, v_w_up, v_conv_w, v_conv_b, v_w_down, v_norm_ple_g, v_w_ple, v_w_ple_gate, v_final_norm_g):
    W = dict(norm_mix_g=norm_mix_g, w_in=w_in, pool_w=pool_w, pool_scale=pool_scale, q_norm_g=q_norm_g, w_uq=w_uq,
             kv_norm_g=kv_norm_g, w_ukv=w_ukv, w_out=w_out, norm_ffn_g=norm_ffn_g, w_up=w_up, conv_w=conv_w, conv_b=conv_b,
             w_down=w_down, norm_ple_g=norm_ple_g, w_ple=w_ple, w_ple_gate=w_ple_gate, final_norm_g=final_norm_g)
    M1 = dict(norm_mix_g=m_norm_mix_g, w_in=m_w_in, pool_w=m_pool_w, pool_scale=m_pool_scale, q_norm_g=m_q_norm_g, w_uq=m_w_uq,
              kv_norm_g=m_kv_norm_g, w_ukv=m_w_ukv, w_out=m_w_out, norm_ffn_g=m_norm_ffn_g, w_up=m_w_up, conv_w=m_conv_w,
              conv_b=m_conv_b, w_down=m_w_down, norm_ple_g=m_norm_ple_g, w_ple=m_w_ple, w_ple_gate=m_w_ple_gate,
              final_norm_g=m_final_norm_g)
    M2 = dict(norm_mix_g=v_norm_mix_g, w_in=v_w_in, pool_w=v_pool_w, pool_scale=v_pool_scale, q_norm_g=v_q_norm_g, w_uq=v_w_uq,
              kv_norm_g=v_kv_norm_g, w_ukv=v_w_ukv, w_out=v_w_out, norm_ffn_g=v_norm_ffn_g, w_up=v_w_up, conv_w=v_conv_w,
              conv_b=v_conv_b, w_down=v_w_down, norm_ple_g=v_norm_ple_g, w_ple=v_w_ple, w_ple_gate=v_w_ple_gate,
              final_norm_g=v_final_norm_g)

    _, T, D = x.shape
    L = p.shape[0]
    P, QL, KL, F = pool_scale.shape[-1], q_norm_g.shape[-1], kv_norm_g.shape[-1], conv_b.shape[-1]
    C = pool_w.shape[-1]
    H = (D - P) // V_DIM
    d_in = P + QL + KL + ROPE_DIM
    dims = dict(P=P, QL=QL, KL=KL, C=C, H=H)
    misc_shapes = [W[n].shape[1:] for n in MISC]
    ns_in, ns_up, ns_conv = w_in.shape[-1], w_up.shape[-1], conv_w.shape[-1]

    xi, yi, ci = lax.axis_index("x"), lax.axis_index("y"), lax.axis_index("c")
    me = 2 * xi + yi
    place = jnp.stack([me, ci]).astype(jnp.int32)

    def all_reduce(parts):
        flat = jnp.concatenate(parts)
        padded = -(-flat.shape[0] // (8 * LANES)) * (8 * LANES)
        return _all_reduce_small(_pad_to(flat, padded).reshape(-1, LANES)).reshape(-1)

    inv_freq = 1.0 / (ROPE_THETA ** (jnp.arange(0, ROPE_DIM, 2, dtype=F32) / ROPE_DIM))
    inv_lane = jnp.concatenate([inv_freq, inv_freq, jnp.zeros((LANES - ROPE_DIM,), F32)]).reshape(1, LANES)
    tabs = _rope_tables(positions.reshape(T, 1).astype(F32), inv_lane)

    local = [W[n].astype(BF16) for n in BIG] + [jnp.stack([_pack([W[n][l] for n in MISC], BF16) for l in range(L)])]
    placed = lax.dynamic_update_slice(jnp.zeros((L, CONV_TAPS, F), F32), conv_w, (0, 0, me * ns_conv))
    conv_full = all_reduce([jnp.where(ci == 0, placed, 0.0).reshape(-1)])[:L * CONV_TAPS * F].reshape(L, CONV_TAPS, F)

    def layout(got, l):
        g = dict(zip(BIG, got[:-1]))
        misc = {n: _join(n, a) for n, a in zip(MISC, _unpack(got[-1], misc_shapes, (N_SHARDS,)))}
        return dict(
            w_in=jnp.concatenate([g["w_in"][sh] for sh in range(N_SHARDS)] + [jnp.zeros((D, LANES - ROPE_DIM), BF16)], axis=1),
            w_out=g["w_out"].reshape(-1, D), w_down=g["w_down"].reshape(-1, D), w_ple_gate=g["w_ple_gate"].reshape(-1, D),
            w_up=g["w_up"], w_ple=misc["w_ple"], pool_w=misc["pool_w"],
            w_uq=_heads_split(misc["w_uq"], H, NOPE_DIM, ROPE_DIM, LANES),
            w_ukv=_heads_split(misc["w_ukv"], H, NOPE_DIM, V_DIM, V_DIM),
            cw8=jnp.concatenate([conv_full[l], conv_b[l][None], jnp.zeros((4, F), F32)], axis=0))

    half_up = (0, N_SHARDS // 2), (N_SHARDS // 2, N_SHARDS // 2)

    h = x[0]
    saved, FW = [], []
    up_at = BIG.index("w_up")
    arriving = _run_side(_forward_side(_run_side(_gather_side(local, 0, _whole(local, True), {})[0], "all_gather")), "gather_forward")
    for l in range(L):
        fw = layout(arriving, l)
        FW.append(fw)
        s = dict(h0=h)
        nxt = {}
        parts = _plan_gather(local, up_at, len(local) - 1) if l + 1 < L else [None] * 4

        def gather(pieces):
            return _gather_side(local, l + 1, pieces, nxt)
        s["n1"] = _rms_fwd(h, norm_mix_g[l], "norm_mix")
        s["u"] = _matmul(s["n1"], fw["w_in"], "nn", F32, "mm_in", tm=512, tn=d_in + LANES - ROPE_DIM)
        s["diff"], s["cqn"], s["ckvn"], s["kr"] = _post_u(s["u"], q_norm_g[l], kv_norm_g[l], tabs, dims)
        s["q"] = _q_rope(_matmul(s["cqn"], fw["w_uq"], "nn", F32, "mm_uq", tn=2 * H * LANES), tabs, dims)
        s["kv"] = _matmul(s["ckvn"], fw["w_ukv"], "nn", BF16, "mm_ukv", tn=2 * H * LANES)
        s["o"], s["lse"] = _ride(lambda sd: _flash_fwd(s["q"], s["kv"], s["kr"], dims, sd), 2, parts[0], gather, nxt)
        s["mix"] = jnp.concatenate([_pool_fwd(s["diff"], fw["pool_w"], pool_scale[l], dims), s["o"]], axis=1)
        s["h1"] = _matmul(s["mix"], fw["w_out"], "nn", F32, "mm_out", res=h, tm=1024)
        s["n2"] = _rms_fwd(s["h1"], norm_ffn_g[l], "norm_ffn")
        s["gate"] = _ride(lambda sd: _matmul(s["n2"], fw["w_up"], "nn", BF16, "mm_gate", tm=1024, tn=ns_up // 2, b_shards=half_up[0],
                                             side=sd), 1, parts[1], gather, nxt)
        s["up"] = _ride(lambda sd: _matmul(s["n2"], fw["w_up"], "nn", BF16, "mm_up", tm=1024, tn=ns_up // 2, b_shards=half_up[1],
                                           side=sd), 1, parts[2], gather, nxt)
        s["a"] = _ffn_fwd(s["gate"], s["up"], fw["cw8"])
        s["h2"] = _ride(lambda sd: _matmul(s["a"], fw["w_down"], "nn", F32, "mm_down", res=s["h1"], tm=1024, tn=512, tk=F,
                                           side=sd), 1, parts[3], gather, nxt)
        s["n3"] = _rms_fwd(s["h2"], norm_ple_g[l], "norm_ple")
        if nxt:
            s["gl"], *arriving = _matmul(s["n3"], fw["w_ple_gate"], "nn", F32, "mm_ple_gate", tm=1024,
                                         side=_forward_side([nxt[a] for a in range(len(local))]))
        else:
            s["gl"] = _matmul(s["n3"], fw["w_ple_gate"], "nn", F32, "mm_ple_gate", tm=1024)
        s["pe"] = _matmul(p[l, 0], fw["w_ple"], "nn", F32, "mm_ple", tn=D)
        h = _ple_fwd(s["h2"], s["gl"], s["pe"])
        saved.append(s)

    dh, dhb, dg_final, loss_part = _final_loss(h, loss_target[0], final_norm_g)
    loss = lax.psum(loss_part[0, 0], ("x", "y", "c"))

    small = {}
    reduced = [None] * L
    raw = None
    for l in reversed(range(L)):
        fw, s = FW[l], saved[l]
        gw = {}
        got = {}
        dpe, dgl = _ple_bwd(dh, s["gl"], s["pe"])
        gw["w_ple"] = _matmul(p[l, 0], dpe, "tn", BF16, "dw_ple", tm=512)
        gw["w_ple_gate"] = _matmul(s["n3"], dgl, "tn", BF16, "dw_ple_gate", tm=512, tk=T)
        dn3 = _matmul(dgl, fw["w_ple_gate"], "nt", F32, "dx_ple_gate", tm=1024)
        dh, dhb, small["norm_ple_g", l] = _rms_bwd(dn3, s["h2"], norm_ple_g[l], dh, "norm_ple_bwd")

        tn_down = F // 4 if F % (4 * LANES) == 0 else F
        if raw:
            da, *from_sibling = _matmul(dhb, fw["w_down"], "nt", BF16, "dx_down", tn=tn_down, side=_sibling_side(raw))
            waiting = _reduce_begin(raw, from_sibling, place)
            parts = _plan(waiting, False, up_at)
        else:
            da = _matmul(dhb, fw["w_down"], "nt", BF16, "dx_down", tn=tn_down)
            waiting, parts = None, [None] * 4

        def chips(pieces):
            return _chip_side(waiting, pieces, got)

        gw["w_down"] = _matmul(s["a"], dhb, "tn", BF16, "dw_down", tm=F // 4 if F % (4 * LANES) == 0 else None)
        dgate, dup, dcw = _ffn_bwd(da, s["gate"], s["up"], fw["cw8"])
        small["conv_w", l], small["conv_b", l] = dcw[:CONV_TAPS], dcw[CONV_TAPS:CONV_TAPS + 1]
        gw["w_up"] = _ride(lambda sd: _matmul(s["n2"], dgate, "tn", BF16, "dw_gate", tm=512, tn=ns_up // 2, tk=T,
                                              out_shards=(half_up[0][0], N_SHARDS, ns_up), side=sd), 1, parts[1], chips, got)
        gw["w_up"] = _ride(lambda sd: _matmul(s["n2"], dup, "tn", BF16, "dw_up", tm=512, tn=ns_up // 2, tk=T,
                                              out_shards=(half_up[1][0], N_SHARDS, ns_up), carry=gw["w_up"], side=sd), 1, parts[2], chips, got)
        dn2 = _ride(lambda sd: _matmul(dgate, fw["w_up"], "nt", F32, "dx_gate", tm=1024, tn=D // 2, tk=ns_up, b_shards=half_up[0],
                                       side=sd), 1, parts[3], chips, got)
        dn2 = _matmul(dup, fw["w_up"], "nt", F32, "dx_up", res=dn2, tm=1024, tn=D // 2, tk=ns_up, b_shards=half_up[1])
        dh, dhb, small["norm_ffn_g", l] = _rms_bwd(dn2, s["h1"], norm_ffn_g[l], dh, "norm_ffn_bwd")

        dmix = _matmul(dhb, fw["w_out"], "nt", BF16, "dx_out")
        gw["w_out"] = _matmul(s["mix"], dhb, "tn", BF16, "dw_out", tm=512, tk=T)
        ddiff, gw["pool_w"], small["pool_scale", l] = _pool_bwd(dmix, s["diff"], fw["pool_w"], pool_scale[l], dims)
        dq, dkn, dv, dkr = _ride(lambda sd: _flash_bwd(s["q"], s["kv"], s["kr"], s["o"], s["lse"], dmix, dims, sd), 4, parts[0], chips, got)
        sharing = _reduce_end(waiting, [got[a] for a in range(len(waiting))], place) if got else None
        dqb = _dq_post(dq, tabs, dims)
        dkv = jnp.concatenate([dkn, dv], axis=1)
        gw["w_uq"] = _heads_merge(_matmul(s["cqn"], dqb, "tn", BF16, "dw_uq", tn=2 * H * LANES), H, NOPE_DIM, ROPE_DIM, LANES)
        gw["w_ukv"] = _heads_merge(_matmul(s["ckvn"], dkv, "tn", BF16, "dw_ukv", tn=2 * H * LANES), H, NOPE_DIM, V_DIM, V_DIM)
        dcqn = _matmul(dqb, fw["w_uq"], "nt", F32, "dx_uq")
        dckvn = _matmul(dkv, fw["w_ukv"], "nt", F32, "dx_ukv")
        du, small["q_norm_g", l], small["kv_norm_g", l] = _pre_u_bwd(s["u"], dcqn, dckvn, ddiff, dkr, q_norm_g[l], kv_norm_g[l], tabs, dims)
        gw["w_in"] = _matmul(s["n1"], du, "tn", BF16, "dw_in", tm=512, tn=du.shape[1])[:, :d_in]
        if sharing:
            dn1, *full = _matmul(du, fw["w_in"], "nt", F32, "dx_in", tm=1024, tk=du.shape[1], side=sharing)
            reduced[l + 1] = _shards(full)
        else:
            dn1 = _matmul(du, fw["w_in"], "nt", F32, "dx_in", tm=1024, tk=du.shape[1])
        dh, dhb, small["norm_mix_g", l] = _rms_bwd(dn1, s["h0"], norm_mix_g[l], dh, "norm_mix_bwd")

        split = {n: _split(n, gw[n]) for n in MISC}
        arrs = [jnp.stack([gw["w_in"][:, sh * ns_in:(sh + 1) * ns_in] for sh in range(N_SHARDS)]),
                gw["w_out"].reshape(N_SHARDS, -1, D), gw["w_up"], gw["w_down"].reshape(N_SHARDS, -1, D),
                gw["w_ple_gate"].reshape(N_SHARDS, -1, D),
                jnp.stack([_pack([split[n][sh] for n in MISC], BF16) for sh in range(N_SHARDS)])]
        raw = arrs
    waiting = _reduce_begin(raw, _run_side(_sibling_side(raw), "rs_sibling"), place)
    reduced[0] = _shards(_run_side(_reduce_end(waiting, _run_side(_chip_side(waiting, _whole(waiting, False), {})[0], "rs_chips"), place),
                                   "rs_share"))

    grads = {n: jnp.stack([reduced[l][k].reshape(W[n].shape[1:]) for l in range(L)]) for k, n in enumerate(BIG)}
    per_layer = [_unpack(reduced[l][-1], misc_shapes, ()) for l in range(L)]
    for k, n in enumerate(MISC):
        grads[n] = jnp.stack([per_layer[l][k] for l in range(L)])

    small_names = ("norm_mix_g", "pool_scale", "q_norm_g", "kv_norm_g", "norm_ffn_g", "conv_b", "norm_ple_g", "conv_w")
    summed = all_reduce([small[n, l].reshape(-1) for n in small_names for l in range(L)] + [dg_final.reshape(-1)])
    off = 0
    for n in small_names:
        size = CONV_TAPS * F if n == "conv_w" else W[n].shape[-1]
        grads[n] = summed[off:off + L * size].reshape((L, CONV_TAPS, F) if n == "conv_w" else (L, size))
        off += L * size
    grads["final_norm_g"] = summed[off:off + D]
    grads["conv_w"] = lax.dynamic_slice(grads["conv_w"], (0, 0, me * ns_conv), (L, CONV_TAPS, ns_conv))

    deltas, new_m, new_v = {}, {}, {}
    for n in WEIGHTS:
        deltas[n], new_m[n], new_v[n] = _adamw(W[n], grads[n], M1[n], M2[n], "adamw_" + n)

    return (loss, dh[None], *[grads[n] for n in WEIGHTS], *[deltas[n] for n in WEIGHTS],
            *[new_m[n] for n in WEIGHTS], *[new_v[n] for n in WEIGHTS])
```
